```python
import jax, jax.numpy as jnp
from jax import lax
import numpy as np

D_MODEL = 1024
BATCH = 2
SEQ = 8192
DEPTH = 4

N_MIXERS = 3
N_LAYERS_A = len(range(0, DEPTH, N_MIXERS))
N_LAYERS_B = len(range(1, DEPTH, N_MIXERS))
N_LAYERS_C = len(range(2, DEPTH, N_MIXERS))
RMS_EPS = 1e-6
N_MOD = 6
FFN_HIDDEN = -(-8 * D_MODEL // (3 * 256)) * 256

NSA_HEAD_DIM = 64
NSA_HEADS = D_MODEL // NSA_HEAD_DIM
NSA_KV_GROUPS = 4
NSA_HEADS_PER_GROUP = NSA_HEADS // NSA_KV_GROUPS
CMP_LEN = 32
CMP_STRIDE = 16
CMP_HIDDEN = 2 * NSA_HEAD_DIM
SEL_LEN = 64
SEL_TOPK = 16
WINDOW = 512
NSA_Q_BLOCK = SEL_LEN
NSA_SCALE = NSA_HEAD_DIM ** -0.5
SEL_BIG = 1e9
NSA_IN = NSA_HEADS * NSA_HEAD_DIM + 6 * NSA_KV_GROUPS * NSA_HEAD_DIM + 3 * NSA_HEADS

RWKV_HEAD_DIM = 64
RWKV_HEADS = D_MODEL // RWKV_HEAD_DIM
DECAY_LORA = 64
AAA_LORA = 64
GATE_LORA = 128
RWKV_GN_EPS = 64e-5

POOL_WINDOWS = (2, 4, 8, 16)
POOL_GROUPS = len(POOL_WINDOWS)
POOL_GROUP_DIM = D_MODEL // POOL_GROUPS

kernel_name = "hybrid_nsa_rwkv7_pool_adaln_trunk"


def rms_norm(x, w):
    xf = x.astype(jnp.float32)
    y = xf * lax.rsqrt(jnp.mean(xf * xf, axis=-1, keepdims=True) + RMS_EPS)
    return y.astype(x.dtype) * w


def modulate(h, shift, scale):
    return h * (1 + scale[:, None, :]) + shift[:, None, :]


def masked_softmax(s, mask):
    s = jnp.where(mask, s.astype(jnp.float32), -jnp.inf)
    m = jnp.max(s, axis=-1, keepdims=True)
    m = jnp.where(jnp.isfinite(m), m, 0.0)
    e = jnp.where(mask, jnp.exp(s - m), 0.0)
    return e / jnp.maximum(jnp.sum(e, axis=-1, keepdims=True), 1e-30)


def swiglu(h, w_gate, w_up, w_down):
    return (jax.nn.silu(h @ w_gate) * (h @ w_up)) @ w_down


def nsa_mixer(h, w_in, cmp_pos, cmp_w1, cmp_w2, qk_norm, w_out):
    B, S, _ = h.shape
    H, G, HG, DK = NSA_HEADS, NSA_KV_GROUPS, NSA_HEADS_PER_GROUP, NSA_HEAD_DIM
    n_cmp = (S - CMP_LEN) // CMP_STRIDE + 1
    n_sel = S // SEL_LEN
    topk = min(SEL_TOPK, n_sel)
    qd, kd = H * DK, G * DK
    proj = h @ w_in
    q, k_c, v_c, k_s, v_s, k_w, v_w, gates = jnp.split(proj, [qd + i * kd for i in range(7)], axis=-1)

    def head_rms(z, w):
        zf = z.astype(jnp.float32)
        return (zf * lax.rsqrt(jnp.mean(zf * zf, axis=-1, keepdims=True) + RMS_EPS)).astype(z.dtype) * w

    q = head_rms(q.reshape(B, S, H, DK), qk_norm[0]).reshape(B, S, G, HG, DK).transpose(0, 2, 3, 1, 4)

    blk_idx = jnp.arange(n_cmp)[:, None] * CMP_STRIDE + jnp.arange(CMP_LEN)[None, :]

    def compress(kv, pos, w1, w2):
        blocks = kv.reshape(B, S, G, DK)[:, blk_idx] + pos[:, None, :]
        blocks = blocks.transpose(0, 3, 1, 2, 4).reshape(B, G, n_cmp, CMP_LEN * DK)
        return jax.nn.gelu(blocks @ w1) @ w2

    kc = head_rms(compress(k_c, cmp_pos[0], cmp_w1[0], cmp_w2[0]), qk_norm[1])
    vc = compress(v_c, cmp_pos[1], cmp_w1[1], cmp_w2[1])
    cmp_end = jnp.arange(n_cmp) * CMP_STRIDE + CMP_LEN - 1

    ks = head_rms(k_s.reshape(B, S, G, DK), qk_norm[2]).transpose(0, 2, 1, 3).reshape(B, G, n_sel, SEL_LEN, DK)
    vs = v_s.reshape(B, S, G, DK).transpose(0, 2, 1, 3).reshape(B, G, n_sel, SEL_LEN, DK)
    cmp_start = jnp.arange(n_cmp)[:, None] * CMP_STRIDE
    sel_start = jnp.arange(n_sel) * SEL_LEN
    overlap = ((cmp_start < sel_start[None, :] + SEL_LEN) & (cmp_start + CMP_LEN > sel_start[None, :])).astype(jnp.float32)

    pad = ((0, 0), (0, 0), (WINDOW, 0), (0, 0))
    kw = jnp.pad(head_rms(k_w.reshape(B, S, G, DK), qk_norm[3]).transpose(0, 2, 1, 3), pad)
    vw = jnp.pad(v_w.reshape(B, S, G, DK).transpose(0, 2, 1, 3), pad)

    gates = jax.nn.sigmoid(gates.astype(jnp.float32)).reshape(B, S, 3, G, HG).transpose(2, 0, 3, 4, 1)
    slopes = jnp.exp2(-8.0 * (jnp.arange(H, dtype=jnp.float32) + 1) / H).reshape(G, HG)
    b_ix = jnp.arange(B)[:, None, None, None]
    g_ix = jnp.arange(G)[None, :, None, None]
    j_sel = jnp.arange(n_sel)

    def block(qb):
        q0 = qb * NSA_Q_BLOCK
        t = q0 + jnp.arange(NSA_Q_BLOCK)
        qq = lax.dynamic_slice_in_dim(q, q0, NSA_Q_BLOCK, axis=3)
        dist_c = t[:, None] - cmp_end[None, :]
        s_c = jnp.einsum('bghqd,bgnd->bghqn', qq, kc) * NSA_SCALE - slopes[:, :, None, None] * dist_c
        p_c = masked_softmax(s_c, dist_c >= 0)
        o_c = jnp.einsum('bghqn,bgnd->bghqd', p_c, vc)
        imp = jnp.einsum('bghqn,nj->bgqj', p_c, overlap)
        valid = sel_start[None, :] <= t[:, None]
        forced = (j_sel == 0) | (j_sel == qb) | (j_sel == qb - 1)
        score = jnp.where(valid, jnp.where(forced, SEL_BIG, imp), -SEL_BIG)
        vals, idx = lax.top_k(score, topk)
        sel_ok = vals > -SEL_BIG / 2
        kg = ks[b_ix, g_ix, idx]
        vg = vs[b_ix, g_ix, idx]
        pos = idx[..., None] * SEL_LEN + jnp.arange(SEL_LEN)
        dist_s = t[None, None, :, None, None] - pos
        mask_s = sel_ok[..., None] & (dist_s >= 0)
        s_s = jnp.einsum('bghqd,bgqnld->bghqnl', qq, kg) * NSA_SCALE - slopes[None, :, :, None, None, None] * dist_s[:, :, None]
        p_s = masked_softmax(s_s.reshape(B, G, HG, NSA_Q_BLOCK, topk * SEL_LEN),
                             mask_s[:, :, None].reshape(B, G, 1, NSA_Q_BLOCK, topk * SEL_LEN))
        o_s = jnp.einsum('bghqnl,bgqnld->bghqd', p_s.reshape(B, G, HG, NSA_Q_BLOCK, topk, SEL_LEN), vg)
        kwin = lax.dynamic_slice_in_dim(kw, q0, WINDOW + NSA_Q_BLOCK, axis=2)
        vwin = lax.dynamic_slice_in_dim(vw, q0, WINDOW + NSA_Q_BLOCK, axis=2)
        p_k = q0 - WINDOW + jnp.arange(WINDOW + NSA_Q_BLOCK)
        dist_w = t[:, None] - p_k[None, :]
        mask_w = (dist_w >= 0) & (dist_w < WINDOW) & (p_k >= 0)[None, :]
        s_w = jnp.einsum('bghqd,bgkd->bghqk', qq, kwin) * NSA_SCALE - slopes[:, :, None, None] * dist_w
        o_w = jnp.einsum('bghqk,bgkd->bghqd', masked_softmax(s_w, mask_w), vwin)
        gb = lax.dynamic_slice_in_dim(gates, q0, NSA_Q_BLOCK, axis=4)[..., None]
        o = gb[0] * o_c + gb[1] * o_s + gb[2] * o_w
        return o.transpose(0, 3, 1, 2, 4).reshape(B, NSA_Q_BLOCK, H * DK)

    o = lax.map(block, jnp.arange(S // NSA_Q_BLOCK))
    o = o.transpose(1, 0, 2, 3).reshape(B, S, H * DK)
    return o @ w_out


def rwkv7_mixer(h, mu, w_rkv, w0, wd_a, wd_b, a0, wa_a, wa_b, wg_a, wg_b, k_k, k_a, r_k, ln_w, ln_b, w_out):
    B, S, D = h.shape
    H, N = RWKV_HEADS, RWKV_HEAD_DIM
    f32 = jnp.float32
    xx = jnp.pad(h, ((0, 0), (1, 0), (0, 0)))[:, :-1] - h
    xr, xw, xk, xv, xa, xg = [h + xx * mu[i] for i in range(6)]
    r = xr @ w_rkv[0]
    k = xk @ w_rkv[1]
    v = xv @ w_rkv[2]
    w_log = -jax.nn.softplus(-(w0 + jnp.tanh(xw @ wd_a) @ wd_b)) - 0.5
    decay = jnp.exp(-jnp.exp(w_log.astype(f32)))
    a = jax.nn.sigmoid((a0 + (xa @ wa_a) @ wa_b).astype(f32))
    g = jax.nn.sigmoid(xg @ wg_a) @ wg_b

    def heads(z):
        return z.reshape(B, S, H, N).astype(f32)

    r_h, v_h, w_h, a_h, k_h = heads(r), heads(v), heads(decay), heads(a), heads(k)
    kk = k_h * k_k.reshape(H, N).astype(f32)
    kk = kk * lax.rsqrt(jnp.maximum(jnp.sum(kk * kk, axis=-1, keepdims=True), 1e-24))
    k_h = k_h * (1 + (a_h - 1) * k_a.reshape(H, N).astype(f32))

    def step(state, inp):
        r_t, w_t, k_t, v_t, kk_t, a_t = inp
        sa = jnp.einsum('bhij,bhj->bhi', state, -kk_t)
        state = (state * w_t[:, :, None, :] + sa[..., None] * (kk_t * a_t)[:, :, None, :]
                 + v_t[..., None] * k_t[:, :, None, :])
        return state, jnp.einsum('bhij,bhj->bhi', state, r_t)

    xs = tuple(z.transpose(1, 0, 2, 3) for z in (r_h, w_h, k_h, v_h, kk, a_h))
    _, y = lax.scan(step, jnp.zeros((B, H, N, N), f32), xs)
    y = y.transpose(1, 0, 2, 3)
    mean = jnp.mean(y, axis=-1, keepdims=True)
    var = jnp.mean(jnp.square(y - mean), axis=-1, keepdims=True)
    y = ((y - mean) * lax.rsqrt(var + RWKV_GN_EPS)).reshape(B, S, D) * ln_w + ln_b
    y = y + (jnp.sum(r_h * k_h * r_k.astype(f32), axis=-1, keepdims=True) * v_h).reshape(B, S, D)
    return (y * g) @ w_out


def pool_mixer(h, w_grp, b, scale):
    B, S, D = h.shape
    hf = h.astype(jnp.float32)
    cs = jnp.pad(jnp.cumsum(hf, axis=1), ((0, 0), (1, 0), (0, 0)))
    t1 = jnp.arange(1, S + 1)
    groups = []
    for gi, win in enumerate(POOL_WINDOWS):
        csg = cs[:, :, gi * POOL_GROUP_DIM:(gi + 1) * POOL_GROUP_DIM]
        lo = jnp.maximum(t1 - win, 0)
        groups.append((csg[:, 1:] - csg[:, lo]) / (t1 - lo).astype(jnp.float32)[None, :, None])
    pooled = jnp.concatenate(groups, axis=-1) - hf
    y = jnp.einsum('bsgi,gio->bsgo', pooled.reshape(B, S, POOL_GROUPS, POOL_GROUP_DIM), w_grp).reshape(B, S, D) + b
    return y * scale


def setup_inputs(seed: int = 0) -> dict:
    key = jax.random.key(seed)
    ks = iter(jax.random.split(key, 40))
    D, F = D_MODEL, FFN_HIDDEN

    def nrm(shape, scale):
        return jax.random.normal(next(ks), shape, jnp.float32) * scale

    return {
        "x": nrm((BATCH, SEQ, D), 1.0),
        "c": nrm((BATCH, D), 1.0),
        "ada_w": nrm((DEPTH, D, N_MOD * D), 0.5 * D ** -0.5),
        "ada_b": nrm((DEPTH, N_MOD * D), 0.02),
        "norm_mix_w": 1.0 + nrm((DEPTH, D), 0.02),
        "norm_ffn_w": 1.0 + nrm((DEPTH, D), 0.02),
        "ffn_w_gate": nrm((DEPTH, D, F), D ** -0.5),
        "ffn_w_up": nrm((DEPTH, D, F), D ** -0.5),
        "ffn_w_down": nrm((DEPTH, F, D), F ** -0.5),
        "nsa_w_in": nrm((N_LAYERS_A, D, NSA_IN), D ** -0.5),
        "nsa_cmp_pos": nrm((N_LAYERS_A, 2, CMP_LEN, NSA_HEAD_DIM), 0.1),
        "nsa_cmp_w1": nrm((N_LAYERS_A, 2, CMP_LEN * NSA_HEAD_DIM, CMP_HIDDEN), (CMP_LEN * NSA_HEAD_DIM) ** -0.5),
        "nsa_cmp_w2": nrm((N_LAYERS_A, 2, CMP_HIDDEN, NSA_HEAD_DIM), CMP_HIDDEN ** -0.5),
        "nsa_qk_norm": 1.0 + nrm((N_LAYERS_A, 4, NSA_HEAD_DIM), 0.02),
        "nsa_w_out": nrm((N_LAYERS_A, NSA_HEADS * NSA_HEAD_DIM, D), D ** -0.5),
        "rwkv_mu": jax.random.uniform(next(ks), (N_LAYERS_B, 6, D), jnp.float32),
        "rwkv_w_rkv": nrm((N_LAYERS_B, 3, D, D), D ** -0.5),
        "rwkv_w0": nrm((N_LAYERS_B, D), 0.5),
        "rwkv_wd_a": nrm((N_LAYERS_B, D, DECAY_LORA), D ** -0.5),
        "rwkv_wd_b": nrm((N_LAYERS_B, DECAY_LORA, D), 0.5 * DECAY_LORA ** -0.5),
        "rwkv_a0": nrm((N_LAYERS_B, D), 0.5),
        "rwkv_wa_a": nrm((N_LAYERS_B, D, AAA_LORA), D ** -0.5),
        "rwkv_wa_b": nrm((N_LAYERS_B, AAA_LORA, D), 0.5 * AAA_LORA ** -0.5),
        "rwkv_wg_a": nrm((N_LAYERS_B, D, GATE_LORA), D ** -0.5),
        "rwkv_wg_b": nrm((N_LAYERS_B, GATE_LORA, D), GATE_LORA ** -0.5),
        "rwkv_k_k": 0.85 + nrm((N_LAYERS_B, D), 0.02),
        "rwkv_k_a": 1.0 + nrm((N_LAYERS_B, D), 0.02),
        "rwkv_r_k": nrm((N_LAYERS_B, RWKV_HEADS, RWKV_HEAD_DIM), 0.1),
        "rwkv_ln_w": 1.0 + nrm((N_LAYERS_B, D), 0.02),
        "rwkv_ln_b": nrm((N_LAYERS_B, D), 0.02),
        "rwkv_w_out": nrm((N_LAYERS_B, D, D), D ** -0.5),
        "pool_w": nrm((N_LAYERS_C, POOL_GROUPS, POOL_GROUP_DIM, POOL_GROUP_DIM), POOL_GROUP_DIM ** -0.5),
        "pool_b": nrm((N_LAYERS_C, D), 0.02),
        "pool_scale": 0.5 + nrm((N_LAYERS_C, D), 0.1),
    }


def reference(x, c, ada_w, ada_b, norm_mix_w, norm_ffn_w, ffn_w_gate, ffn_w_up, ffn_w_down,
              nsa_w_in, nsa_cmp_pos, nsa_cmp_w1, nsa_cmp_w2, nsa_qk_norm, nsa_w_out,
              rwkv_mu, rwkv_w_rkv, rwkv_w0, rwkv_wd_a, rwkv_wd_b, rwkv_a0, rwkv_wa_a, rwkv_wa_b,
              rwkv_wg_a, rwkv_wg_b, rwkv_k_k, rwkv_k_a, rwkv_r_k, rwkv_ln_w, rwkv_ln_b, rwkv_w_out,
              pool_w, pool_b, pool_scale):
    c_act = jax.nn.silu(c)
    for i in range(DEPTH):
        mod = c_act @ ada_w[i] + ada_b[i]
        sh_m, sc_m, g_m, sh_f, sc_f, g_f = jnp.split(mod, N_MOD, axis=-1)
        h = modulate(rms_norm(x, norm_mix_w[i]), sh_m, sc_m)
        kind, j = i % N_MIXERS, i // N_MIXERS
        if kind == 0:
            y = nsa_mixer(h, nsa_w_in[j], nsa_cmp_pos[j], nsa_cmp_w1[j], nsa_cmp_w2[j], nsa_qk_norm[j], nsa_w_out[j])
        elif kind == 1:
            y = rwkv7_mixer(h, rwkv_mu[j], rwkv_w_rkv[j], rwkv_w0[j], rwkv_wd_a[j], rwkv_wd_b[j], rwkv_a0[j],
                            rwkv_wa_a[j], rwkv_wa_b[j], rwkv_wg_a[j], rwkv_wg_b[j], rwkv_k_k[j], rwkv_k_a[j],
                            rwkv_r_k[j], rwkv_ln_w[j], rwkv_ln_b[j], rwkv_w_out[j])
        else:
            y = pool_mixer(h, pool_w[j], pool_b[j], pool_scale[j])
        x = x + (g_m[:, None, :] * y).astype(x.dtype)
        h = modulate(rms_norm(x, norm_ffn_w[i]), sh_f, sc_f)
        x = x + (g_f[:, None, :] * swiglu(h, ffn_w_gate[i], ffn_w_up[i], ffn_w_down[i])).astype(x.dtype)
    return x
```

```python
import functools
import math

import jax
import jax.numpy as jnp
from jax import lax
from jax.experimental import pallas as pl
from jax.experimental.pallas import tpu as pltpu

F32 = jnp.float32
BF16 = jnp.bfloat16

V7X_LANES = 128
V7X_VMEM_LIMIT_BYTES = 56 * 1024 * 1024

RMS_EPS = 1e-6
N_MOD = 6
N_MIXERS = 3

NSA_HEAD_DIM = 64
NSA_KV_GROUPS = 4
NSA_HEADS_PER_GROUP = 4
NSA_HEADS = NSA_KV_GROUPS * NSA_HEADS_PER_GROUP
CMP_LEN = 32
CMP_STRIDE = 16
SEL_LEN = 64
SEL_TOPK = 16
WINDOW = 512
NSA_SCALE = NSA_HEAD_DIM ** -0.5
SEL_BIG = 1e9
MASKED_SCORE = -1e30

RWKV_HEAD_DIM = 64
RWKV_GN_EPS = 64e-5
RWKV_CHUNK = 64

POOL_WINDOWS = (2, 4, 8, 16)
POOL_HALO = 16

ROW_TILE = 512


def _params(*sem):
    return pltpu.CompilerParams(dimension_semantics=sem, vmem_limit_bytes=V7X_VMEM_LIMIT_BYTES)


def _sigmoid(z):
    return 1.0 / (1.0 + jnp.exp(-z))


def _normmod(x, nw, shift, scale):
    ms = jnp.mean(x * x, axis=-1, keepdims=True)
    return (x * lax.rsqrt(ms + RMS_EPS) * nw) * (1.0 + scale) + shift


def _dot(a, b):
    return jnp.dot(a.astype(BF16), b.astype(BF16), preferred_element_type=F32)


def _dot_nt(a, b):
    return lax.dot_general(a.astype(BF16), b.astype(BF16), (((1,), (1,)), ((), ())), preferred_element_type=F32)


def _dot_tn(a, b):
    return lax.dot_general(a.astype(BF16), b.astype(BF16), (((0,), (0,)), ((), ())), preferred_element_type=F32)


def _split_bf16(x, parts):
    out = []
    for _ in range(parts):
        p = x.astype(BF16)
        out.append(p)
        x = x - p.astype(F32)
    return out


def _mod_kernel(c_ref, w_ref, b_ref, o_ref):
    c = c_ref[...]
    ca = c * _sigmoid(c)
    o_ref[0] = _dot(ca, w_ref[0]) + b_ref[0]


def _ada_mod(c, ada_w, ada_b):
    depth, d, n = ada_w.shape
    b = c.shape[0]
    rows = 8
    tn = 1536
    assert b <= rows and n % tn == 0
    c_pad = jnp.zeros((rows, d), F32).at[:b].set(c)
    out = pl.pallas_call(
        _mod_kernel,
        out_shape=jax.ShapeDtypeStruct((depth, rows, n), F32),
        grid=(depth, n // tn),
        in_specs=[
            pl.BlockSpec((rows, d), lambda i, j: (0, 0)),
            pl.BlockSpec((1, d, tn), lambda i, j: (i, 0, j)),
            pl.BlockSpec((1, 1, tn), lambda i, j: (i, 0, j)),
        ],
        out_specs=pl.BlockSpec((1, rows, tn), lambda i, j: (i, 0, j)),
        compiler_params=_params("parallel", "parallel"),
        name="ada_mod",
    )(c_pad, ada_w, ada_b.reshape(depth, 1, n))
    return out[:, :b]


def _ffn_kernel(nj, x_ref, nw_ref, sh_ref, sc_ref, g_ref, wg_ref, wu_ref, wd_ref, o_ref, h_scr, acc_scr):
    j = pl.program_id(1)

    @pl.when(j == 0)
    def _():
        h_scr[...] = _normmod(x_ref[...], nw_ref[...], sh_ref[0], sc_ref[0]).astype(BF16)
        acc_scr[...] = jnp.zeros_like(acc_scr)

    h = h_scr[...]
    g = jnp.dot(h, wg_ref[...], preferred_element_type=F32)
    u = jnp.dot(h, wu_ref[...], preferred_element_type=F32)
    a = g * _sigmoid(g) * u
    acc_scr[...] += jnp.dot(a.astype(BF16), wd_ref[...], preferred_element_type=F32)

    @pl.when(j == nj - 1)
    def _():
        o_ref[...] = x_ref[...] + g_ref[0] * acc_scr[...]


def _ffn(x2, nw, shift, scale, gate, wg, wu, wd, seq):
    t, d = x2.shape
    f = wg.shape[1]
    tm = ROW_TILE
    fc = f // 2
    assert t % tm == 0 and seq % tm == 0 and fc % V7X_LANES == 0
    tpb = seq // tm
    nj = f // fc
    mod_spec = pl.BlockSpec((1, 1, d), lambda i, j: (i // tpb, 0, 0))
    return pl.pallas_call(
        functools.partial(_ffn_kernel, nj),
        out_shape=jax.ShapeDtypeStruct((t, d), F32),
        grid=(t // tm, nj),
        in_specs=[
            pl.BlockSpec((tm, d), lambda i, j: (i, 0)),
            pl.BlockSpec((1, d), lambda i, j: (0, 0)),
            mod_spec, mod_spec, mod_spec,
            pl.BlockSpec((d, fc), lambda i, j: (0, j)),
            pl.BlockSpec((d, fc), lambda i, j: (0, j)),
            pl.BlockSpec((fc, d), lambda i, j: (j, 0)),
        ],
        out_specs=pl.BlockSpec((tm, d), lambda i, j: (i, 0)),
        scratch_shapes=[pltpu.VMEM((tm, d), BF16), pltpu.VMEM((tm, d), F32)],
        compiler_params=_params("parallel", "arbitrary"),
        name="ffn",
    )(x2, nw.reshape(1, d), shift[:, None], scale[:, None], gate[:, None],
      wg.astype(BF16), wu.astype(BF16), wd.astype(BF16))


def _outproj_kernel(x_ref, z_ref, g_ref, w_ref, o_ref):
    o_ref[...] = x_ref[...] + g_ref[0] * _dot(z_ref[...], w_ref[...])


def _outproj(x2, z2, gate, w, seq):
    t, d = x2.shape
    k = z2.shape[1]
    tm = ROW_TILE
    tpb = seq // tm
    return pl.pallas_call(
        _outproj_kernel,
        out_shape=jax.ShapeDtypeStruct((t, d), F32),
        grid=(t // tm,),
        in_specs=[
            pl.BlockSpec((tm, d), lambda i: (i, 0)),
            pl.BlockSpec((tm, k), lambda i: (i, 0)),
            pl.BlockSpec((1, 1, d), lambda i: (i // tpb, 0, 0)),
            pl.BlockSpec((k, d), lambda i: (0, 0)),
        ],
        out_specs=pl.BlockSpec((tm, d), lambda i: (i, 0)),
        compiler_params=_params("parallel"),
        name="outproj",
    )(x2, z2, gate[:, None], w.astype(BF16))


def _pool_kernel(tm, gd, x_ref, nw_ref, sh_ref, sc_ref, g_ref, pw_ref, pb_ref, ps_ref, o_ref, ext_scr):
    s = pl.program_id(1)
    x = x_ref[0]
    h = _normmod(x, nw_ref[...], sh_ref[0], sc_ref[0])

    @pl.when(s == 0)
    def _():
        ext_scr[0:POOL_HALO, :] = jnp.zeros((POOL_HALO, x.shape[1]), F32)

    ext_scr[POOL_HALO:POOL_HALO + tm, :] = h
    row = s * tm + lax.broadcasted_iota(jnp.int32, (tm, 1), 0)
    ys = []
    for gi, win in enumerate(POOL_WINDOWS):
        lanes = slice(gi * gd, (gi + 1) * gd)
        hg = h[:, lanes]
        acc = hg
        for k in range(1, win):
            acc = acc + ext_scr[POOL_HALO - k:POOL_HALO - k + tm, lanes]
        cnt = jnp.minimum(row + 1, win).astype(F32)
        ys.append(_dot(acc / cnt - hg, pw_ref[gi]))
    y = (jnp.concatenate(ys, axis=-1) + pb_ref[...]) * ps_ref[...]
    o_ref[0] = x + g_ref[0] * y
    ext_scr[0:POOL_HALO, :] = ext_scr[tm:tm + POOL_HALO, :]


def _pool_layer(x, nw, shift, scale, gate, pw, pb, ps):
    b, s, d = x.shape
    tm = ROW_TILE
    gd = d // len(POOL_WINDOWS)
    assert s % tm == 0 and max(POOL_WINDOWS) <= POOL_HALO
    mod_spec = pl.BlockSpec((1, 1, d), lambda i, j: (i, 0, 0))
    vec_spec = pl.BlockSpec((1, d), lambda i, j: (0, 0))
    return pl.pallas_call(
        functools.partial(_pool_kernel, tm, gd),
        out_shape=jax.ShapeDtypeStruct((b, s, d), F32),
        grid=(b, s // tm),
        in_specs=[
            pl.BlockSpec((1, tm, d), lambda i, j: (i, j, 0)),
            vec_spec, mod_spec, mod_spec, mod_spec,
            pl.BlockSpec(pw.shape, lambda i, j: (0, 0, 0)),
            vec_spec, vec_spec,
        ],
        out_specs=pl.BlockSpec((1, tm, d), lambda i, j: (i, j, 0)),
        scratch_shapes=[pltpu.VMEM((tm + POOL_HALO, d), F32)],
        compiler_params=_params("arbitrary", "arbitrary"),
        name="pool_mixer",
    )(x, nw.reshape(1, d), shift[:, None], scale[:, None], gate[:, None],
      pw.astype(BF16), pb.reshape(1, d), ps.reshape(1, d))


def _rwkv_proj_kernel(tm, x_ref, nw_ref, sh_ref, sc_ref, mu_ref, wr_ref, wk_ref, wv_ref,
                      w0_ref, wda_ref, wdb_ref, a0_ref, waa_ref, wab_ref, wga_ref, wgb_ref,
                      r_ref, k_ref, v_ref, lw_ref, a_ref, g_ref, ext_scr):
    s = pl.program_id(1)
    h = _normmod(x_ref[0], nw_ref[...], sh_ref[0], sc_ref[0])

    @pl.when(s == 0)
    def _():
        ext_scr[0:8, :] = jnp.zeros((8, h.shape[1]), F32)

    ext_scr[8:8 + tm, :] = h
    xx = ext_scr[7:7 + tm, :] - h

    def mix(i):
        return h + xx * mu_ref[i:i + 1, :]

    r_ref[0] = _dot(mix(0), wr_ref[...])
    k_ref[0] = _dot(mix(2), wk_ref[...])
    v_ref[0] = _dot(mix(3), wv_ref[...])
    dw = w0_ref[...] + _dot(jnp.tanh(_dot(mix(1), wda_ref[...])), wdb_ref[...])
    softplus_neg = jnp.maximum(-dw, 0.0) + jnp.log(1.0 + jnp.exp(-jnp.abs(dw)))
    lw_ref[0] = -jnp.exp(-softplus_neg - 0.5)
    a_ref[0] = _sigmoid(a0_ref[...] + _dot(_dot(mix(4), waa_ref[...]), wab_ref[...]))
    g_ref[0] = _dot(_sigmoid(_dot(mix(5), wga_ref[...])), wgb_ref[...])
    ext_scr[0:8, :] = ext_scr[tm:tm + 8, :]


def _rwkv_scan_kernel(ct, r_ref, k_ref, v_ref, lw_ref, a_ref, g_ref, kk_ref, ka_ref, rk_ref, lnw_ref, lnb_ref,
                      z_ref, s_scr):
    c, n = RWKV_CHUNK, RWKV_HEAD_DIM
    w = 2 * n
    shift = int(math.log2(n))

    @pl.when(pl.program_id(2) == 0)
    def _():
        s_scr[...] = jnp.zeros_like(s_scr)

    r, k, v, lw, a = r_ref[0], k_ref[0], v_ref[0], lw_ref[0], a_ref[0]
    head0 = lax.broadcasted_iota(jnp.int32, (1, w), 1) < n
    ri = lax.broadcasted_iota(jnp.int32, (w, w), 0)
    ci = lax.broadcasted_iota(jnp.int32, (w, w), 1)
    same_head = jnp.where((ri >> shift) == (ci >> shift), 1.0, 0.0).astype(BF16)
    strict, incl, eye = ri > ci, ri >= ci, ri == ci
    rt = lax.broadcasted_iota(jnp.int32, (ct, ct), 0)
    cc = lax.broadcasted_iota(jnp.int32, (ct, ct), 1)
    cshift = int(math.log2(c))
    chunk_tri = jnp.where(((rt >> cshift) == (cc >> cshift)) & (rt >= cc), 1.0, 0.0).astype(BF16)

    def head_sum(xf):
        return sum(jnp.dot(p, same_head, preferred_element_type=F32) for p in _split_bf16(xf, 2))

    kk = k * kk_ref[...]
    kkn = kk * lax.rsqrt(jnp.maximum(head_sum(kk * kk), 1e-24))
    k2 = k * (1.0 + (a - 1.0) * ka_ref[...])
    beta = kkn * a
    cum = sum(jnp.dot(chunk_tri, p, preferred_element_type=F32) for p in _split_bf16(lw, 3))
    at_all = -kkn * jnp.exp(cum - lw)
    rt_all = r * jnp.exp(cum)
    einv = jnp.exp(-cum)
    kb_all = beta * einv
    kq_all = k2 * einv

    def stack(xc):
        return jnp.concatenate([jnp.where(head0, xc, 0.0), jnp.where(head0, 0.0, xc)], axis=0)

    state = s_scr[...]
    ys = []
    for idx in range(ct // c):
        sl = slice(idx * c, (idx + 1) * c)
        pc = jnp.exp(cum[(idx + 1) * c - 1:(idx + 1) * c, :])
        at, rt_, kb, kq, vs = stack(at_all[sl]), stack(rt_all[sl]), stack(kb_all[sl]), stack(kq_all[sl]), stack(v[sl])
        kbe, kqe = kb * pc, kq * pc
        a_m = jnp.where(strict, _dot_nt(at, kb), 0.0)
        b_m = jnp.where(strict, _dot_nt(at, kq), 0.0)
        ar_m = jnp.where(incl, _dot_nt(rt_, kb), 0.0)
        br_m = jnp.where(incl, _dot_nt(rt_, kq), 0.0)
        tinv = jnp.where(eye, 1.0, 0.0) + a_m
        pw = a_m
        for _ in range(cshift - 1):
            pw = _dot(pw, pw)
            tinv = tinv + _dot(pw, tinv)
        w1 = _dot(tinv, at)
        w2 = _dot(tinv, _dot(b_m, vs))
        g_m = rt_ + _dot(ar_m, w1)
        y0 = _dot(ar_m, w2) + _dot(br_m, vs)
        m_m = jnp.where(eye, jnp.broadcast_to(pc, (w, w)), 0.0) + _dot_tn(w1, kbe)
        n_m = _dot_tn(w2, kbe) + _dot_tn(vs, kqe)
        yst = _dot_nt(g_m, state) + y0
        state = _dot(state, m_m) + n_m
        ys.append(yst[:c] + yst[c:])
    s_scr[...] = state
    y = jnp.concatenate(ys, axis=0)

    mean = head_sum(y) * (1.0 / n)
    dlt = y - mean
    var = head_sum(dlt * dlt) * (1.0 / n)
    yn = dlt * lax.rsqrt(var + RWKV_GN_EPS) * lnw_ref[...] + lnb_ref[...]
    bonus = head_sum(r * k2 * rk_ref[...]) * v
    z_ref[0] = ((yn + bonus) * g_ref[0]).astype(z_ref.dtype)


def _rwkv_layer(x, nw, shift, scale, gate, mu, w_rkv, w0, wd_a, wd_b, a0, wa_a, wa_b, wg_a, wg_b,
                k_k, k_a, r_k, ln_w, ln_b, w_out):
    b, s, d = x.shape
    tm = ROW_TILE
    assert s % tm == 0
    mod_spec = pl.BlockSpec((1, 1, d), lambda i, j: (i, 0, 0))
    vec_spec = pl.BlockSpec((1, d), lambda i, j: (0, 0))
    tok_spec = pl.BlockSpec((1, tm, d), lambda i, j: (i, j, 0))

    def full(arr):
        return pl.BlockSpec(arr.shape, lambda i, j: (0,) * arr.ndim)

    weights = [w_rkv[0].astype(BF16), w_rkv[1].astype(BF16), w_rkv[2].astype(BF16),
               w0.reshape(1, d), wd_a.astype(BF16), wd_b.astype(BF16),
               a0.reshape(1, d), wa_a.astype(BF16), wa_b.astype(BF16),
               wg_a.astype(BF16), wg_b.astype(BF16)]
    tok_shape = jax.ShapeDtypeStruct((b, s, d), F32)
    r, k, v, lw, a, g = pl.pallas_call(
        functools.partial(_rwkv_proj_kernel, tm),
        out_shape=[tok_shape] * 6,
        grid=(b, s // tm),
        in_specs=[tok_spec, vec_spec, mod_spec, mod_spec, full(mu)] + [full(wt) for wt in weights],
        out_specs=[tok_spec] * 6,
        scratch_shapes=[pltpu.VMEM((tm + 8, d), F32)],
        compiler_params=_params("arbitrary", "arbitrary"),
        name="rwkv_proj",
    )(x, nw.reshape(1, d), shift[:, None], scale[:, None], mu, *weights)

    ct = 256
    pair = 2 * RWKV_HEAD_DIM
    assert s % ct == 0 and d % pair == 0 and pair == V7X_LANES
    seq_spec = pl.BlockSpec((1, ct, pair), lambda i, p, j: (i, j, p))
    par_spec = pl.BlockSpec((1, pair), lambda i, p, j: (0, p))
    z = pl.pallas_call(
        functools.partial(_rwkv_scan_kernel, ct),
        out_shape=jax.ShapeDtypeStruct((b, s, d), BF16),
        grid=(b, d // pair, s // ct),
        in_specs=[seq_spec] * 6 + [par_spec] * 5,
        out_specs=seq_spec,
        scratch_shapes=[pltpu.VMEM((pair, pair), F32)],
        compiler_params=_params("parallel", "parallel", "arbitrary"),
        name="rwkv_scan",
    )(r, k, v, lw, a, g, k_k.reshape(1, d), k_a.reshape(1, d), r_k.reshape(1, d), ln_w.reshape(1, d), ln_b.reshape(1, d))
    return _outproj(x.reshape(b * s, d), z.reshape(b * s, d), gate, w_out, s).reshape(b, s, d)


NSA_Q_TILE = 128
NSA_KV_TILE = 512
NSA_PROJ_PAD = 128


def _head_rms(xf, p_ref, pt_ref, wvec):
    sums = sum(jnp.dot(part, p_ref[...], preferred_element_type=F32) for part in _split_bf16(xf * xf, 2))
    inv = lax.rsqrt(sums * (1.0 / NSA_HEAD_DIM) + RMS_EPS)
    inv_full = sum(jnp.dot(part, pt_ref[...], preferred_element_type=F32) for part in _split_bf16(inv, 3))
    return xf * inv_full * wvec


def _nsa_proj_kernel(qd, kd, x_ref, nw_ref, sh_ref, sc_ref, w_ref, pq_ref, pqt_ref, pk_ref, pkt_ref,
                     nq_ref, nks_ref, nkw_ref,
                     q_ref, kc_ref, vc_ref, ks_ref, vs_ref, kw_ref, vw_ref, gt_ref):
    h = _normmod(x_ref[...], nw_ref[...], sh_ref[0], sc_ref[0])
    proj = _dot(h, w_ref[...])
    o = qd
    q_ref[...] = (_head_rms(proj[:, :qd], pq_ref, pqt_ref, nq_ref[...]) * NSA_SCALE).astype(BF16)
    kc_ref[...] = proj[:, o:o + kd]
    vc_ref[...] = proj[:, o + kd:o + 2 * kd]
    ks_ref[...] = _head_rms(proj[:, o + 2 * kd:o + 3 * kd], pk_ref, pkt_ref, nks_ref[...]).astype(BF16)
    vs_ref[...] = proj[:, o + 3 * kd:o + 4 * kd].astype(BF16)
    kw_ref[...] = _head_rms(proj[:, o + 4 * kd:o + 5 * kd], pk_ref, pkt_ref, nkw_ref[...]).astype(BF16)
    vw_ref[...] = proj[:, o + 5 * kd:o + 6 * kd].astype(BF16)
    gt_ref[...] = _sigmoid(proj[:, o + 6 * kd:o + 6 * kd + NSA_PROJ_PAD])


def _gelu_tanh(x):
    return 0.5 * x * (1.0 + jnp.tanh(math.sqrt(2.0 / math.pi) * (x + 0.044715 * (x * x * x))))


def _nsa_compress_kernel(half, kc_ref, vc_ref, pos_ref, w1_ref, w2_ref, nk_ref, kco_ref, vco_ref):
    nsub = kc_ref.shape[2]

    def comp(x, i):
        ya = _dot(x + pos_ref[2 * i:2 * i + 1, :], w1_ref[i, 0:half, :])
        yb = _dot(x + pos_ref[2 * i + 1:2 * i + 2, :], w1_ref[i, half:2 * half, :])
        hid = ya + pltpu.roll(yb, nsub - 1, 0)
        return _dot(_gelu_tanh(hid), w2_ref[i])

    kcm = comp(kc_ref[0, 0], 0)
    ms = jnp.mean(kcm * kcm, axis=-1, keepdims=True)
    kco_ref[0, 0] = (kcm * lax.rsqrt(ms + RMS_EPS) * nk_ref[...]).astype(BF16)
    vco_ref[0, 0] = comp(vc_ref[0, 0], 1).astype(BF16)


def _nsa_cmp_kernel(qt, nc, nsel, q_ref, kc_ref, vc_ref, g_ref, sl_ref, oc_ref, sel_ref):
    hg = NSA_HEADS_PER_GROUP
    q0 = pl.program_id(2) * qt
    q = q_ref[0, 0].reshape(hg * qt, NSA_HEAD_DIM)
    s3 = _dot_nt(q, kc_ref[0, 0]).reshape(hg, qt, nc)
    t = q0 + lax.broadcasted_iota(jnp.int32, (qt, nc), 0)
    n = lax.broadcasted_iota(jnp.int32, (qt, nc), 1)
    dist = t - (n * CMP_STRIDE + CMP_LEN - 1)
    mask = (dist >= 0)[None]
    slope3 = sl_ref[0].reshape(hg, qt, 1)
    sm = jnp.where(mask, s3 - slope3 * dist.astype(F32)[None], MASKED_SCORE)
    m = jnp.max(sm, axis=-1, keepdims=True)
    e = jnp.where(mask, jnp.exp(sm - m), 0.0)
    p = e / jnp.maximum(jnp.sum(e, axis=-1, keepdims=True), 1e-30)
    oc = _dot(p.reshape(hg * qt, nc), vc_ref[0, 0])
    gate_c = g_ref[0, 0].reshape(hg * qt, 3)[:, 0:1]
    oc_ref[0, 0] = (oc * gate_c).reshape(hg, qt, NSA_HEAD_DIM)

    psum = p[0] + p[1] + p[2] + p[3]
    jo = lax.broadcasted_iota(jnp.int32, (nsel, nc), 0)
    no = lax.broadcasted_iota(jnp.int32, (nsel, nc), 1)
    ratio = SEL_LEN // CMP_STRIDE
    first = jo * ratio - (CMP_LEN // CMP_STRIDE - 1)
    overlap_t = jnp.where((no >= first) & (no < (jo + 1) * ratio) & (no < nc - 1), 1.0, 0.0).astype(BF16)
    imp_t = sum(lax.dot_general(overlap_t, part, (((1,), (1,)), ((), ())), preferred_element_type=F32)
                for part in _split_bf16(psum, 3))

    j = lax.broadcasted_iota(jnp.int32, (nsel, qt), 0)
    tt = q0 + lax.broadcasted_iota(jnp.int32, (nsel, qt), 1)
    valid = j * SEL_LEN <= tt
    qb = tt >> int(math.log2(SEL_LEN))
    forced = (j == 0) | (j == qb) | (j == qb - 1)
    score = jnp.where(valid, jnp.where(forced, SEL_BIG, imp_t), -SEL_BIG)
    jf = j.astype(F32)
    sel = jnp.zeros((nsel, qt), F32)
    for _ in range(min(SEL_TOPK, nsel)):
        mx = jnp.max(score, axis=0, keepdims=True)
        jmin = jnp.min(jnp.where(score == mx, jf, float(nsel)), axis=0, keepdims=True)
        pick = jf == jmin
        sel = jnp.where(pick, 1.0, sel)
        score = jnp.where(pick, -3e38, score)
    sel = jnp.where(valid, sel, 0.0)
    sel_ref[0, 0] = sel.T.astype(BF16)


def _nsa_attn_kernel(qt, kt, nsel, q_ref, ks_ref, vs_ref, kw_ref, vw_ref, sel_ref, g_ref, sl_ref, oc_ref, o_ref):
    hg, dk = NSA_HEADS_PER_GROUP, NSA_HEAD_DIM
    rows = hg * qt
    q0 = pl.program_id(2) * qt
    q = q_ref[0, 0].reshape(rows, dk)
    slope3 = sl_ref[0].reshape(hg, qt, 1)
    sel = sel_ref[0, 0]
    lsel = int(math.log2(SEL_LEN))

    t_k = q0 + lax.broadcasted_iota(jnp.int32, (qt, kt), 0)
    c_k = lax.broadcasted_iota(jnp.int32, (qt, kt), 1)
    blk_of_col = lax.broadcasted_iota(jnp.int32, (nsel, kt), 0) - (lax.broadcasted_iota(jnp.int32, (nsel, kt), 1) >> lsel)

    def sel_step(i, carry):
        m, l, acc = carry
        k0 = pl.multiple_of(i * kt, kt)
        s3 = _dot_nt(q, ks_ref[0, 0, pl.ds(k0, kt), :]).reshape(hg, qt, kt)
        expand = jnp.where(blk_of_col == (k0 >> lsel), 1.0, 0.0).astype(BF16)
        chosen = jnp.dot(sel, expand, preferred_element_type=F32)
        dist = t_k - (k0 + c_k)
        mask = ((chosen > 0.5) & (dist >= 0))[None]
        sm = jnp.where(mask, s3 - slope3 * dist.astype(F32)[None], MASKED_SCORE).reshape(rows, kt)
        m_new = jnp.maximum(m, jnp.max(sm, axis=-1, keepdims=True))
        alpha = jnp.exp(m - m_new)
        p = jnp.exp(sm - m_new)
        l = alpha * l + jnp.sum(p, axis=-1, keepdims=True)
        acc = alpha * acc + _dot(p, vs_ref[0, 0, pl.ds(k0, kt), :])
        return m_new, l, acc

    n_tiles = (q0 + qt - 1) // kt + 1
    init = (jnp.full((rows, 1), MASKED_SCORE, F32), jnp.zeros((rows, 1), F32), jnp.zeros((rows, dk), F32))
    _, l_s, acc_s = lax.fori_loop(0, n_tiles, sel_step, init)
    o_s = acc_s / jnp.maximum(l_s, 1e-30)

    wk = WINDOW + qt
    ws = pl.multiple_of(jnp.maximum(q0 - WINDOW, 0), qt)
    s3 = _dot_nt(q, kw_ref[0, 0, pl.ds(ws, wk), :]).reshape(hg, qt, wk)
    dist = (q0 + lax.broadcasted_iota(jnp.int32, (qt, wk), 0)) - (ws + lax.broadcasted_iota(jnp.int32, (qt, wk), 1))
    mask = ((dist >= 0) & (dist < WINDOW))[None]
    sm = jnp.where(mask, s3 - slope3 * dist.astype(F32)[None], MASKED_SCORE).reshape(rows, wk)
    e = jnp.exp(sm - jnp.max(sm, axis=-1, keepdims=True))
    o_w =_dot(e, vw_ref[0, 0, pl.ds(ws, wk), :]) / jnp.maximum(jnp.sum(e, axis=-1, keepdims=True), 1e-30)

    g = g_ref[0, 0].reshape(rows, 3)
    o = oc_ref[0, 0].reshape(rows, dk) + g[:, 1:2] * o_s + g[:, 2:3] * o_w
    o_ref[0, 0] = o.reshape(hg, qt, dk).astype(o_ref.dtype)


def _nsa_layer(x, nw, shift, scale, gate, w_in, cmp_pos, cmp_w1, cmp_w2, qk_norm, w_out):
    b, s, d = x.shape
    t = b * s
    g_, hg, dk = NSA_KV_GROUPS, NSA_HEADS_PER_GROUP, NSA_HEAD_DIM
    qd, kd = NSA_HEADS * dk, g_ * dk
    tm = ROW_TILE
    tpb = s // tm
    n_in = w_in.shape[1]
    n_pad = qd + 6 * kd + NSA_PROJ_PAD
    assert n_in <= n_pad and s % tm == 0
    w_pad = jnp.zeros((d, n_pad), BF16).at[:, :n_in].set(w_in.astype(BF16))

    def head_onehot(width):
        lane_head = jnp.arange(width)[:, None] // dk
        return (lane_head == jnp.arange(V7X_LANES)[None, :]).astype(BF16)

    pq, pk = head_onehot(qd), head_onehot(kd)

    def tile_w(wv, reps):
        return jnp.tile(wv, reps).reshape(1, reps * dk)

    def full2(arr):
        return pl.BlockSpec(arr.shape, lambda i: (0, 0))

    consts = [pq, pq.T, pk, pk.T, tile_w(qk_norm[0], NSA_HEADS), tile_w(qk_norm[2], g_), tile_w(qk_norm[3], g_)]
    row = lambda n: pl.BlockSpec((tm, n), lambda i: (i, 0))
    mod_spec = pl.BlockSpec((1, 1, d), lambda i: (i // tpb, 0, 0))
    outs = pl.pallas_call(
        functools.partial(_nsa_proj_kernel, qd, kd),
        out_shape=[jax.ShapeDtypeStruct((t, qd), BF16), jax.ShapeDtypeStruct((t, kd), F32), jax.ShapeDtypeStruct((t, kd), F32)]
        + [jax.ShapeDtypeStruct((t, kd), BF16)] * 4 + [jax.ShapeDtypeStruct((t, NSA_PROJ_PAD), F32)],
        grid=(t // tm,),
        in_specs=[row(d), pl.BlockSpec((1, d), lambda i: (0, 0)), mod_spec, mod_spec, full2(w_pad)] + [full2(cst) for cst in consts],
        out_specs=[row(qd), row(kd), row(kd), row(kd), row(kd), row(kd), row(kd), row(NSA_PROJ_PAD)],
        compiler_params=_params("parallel"),
        name="nsa_proj",
    )(x.reshape(t, d), nw.reshape(1, d), shift[:, None], scale[:, None], w_pad, *consts)
    qn, kc, vc, ksn, vs, kwn, vw, gates = outs

    q5 = qn.reshape(b, s, g_, hg, dk).transpose(0, 2, 3, 1, 4)
    per_group = lambda arr: arr.reshape(b, s, g_, dk).transpose(0, 2, 1, 3)
    ks4, vs4, kw4, vw4 = per_group(ksn), per_group(vs), per_group(kwn), per_group(vw)
    nsub = s // CMP_STRIDE
    half = CMP_STRIDE * dk
    kc3 = per_group(kc).reshape(b, g_, nsub, half)
    vc3 = per_group(vc).reshape(b, g_, nsub, half)
    gt5 = gates[:, :3 * NSA_HEADS].reshape(b, s, 3, g_, hg).transpose(0, 3, 4, 1, 2)

    pos = jnp.stack([cmp_pos[0, :CMP_STRIDE].reshape(half), cmp_pos[0, CMP_STRIDE:].reshape(half),
                     cmp_pos[1, :CMP_STRIDE].reshape(half), cmp_pos[1, CMP_STRIDE:].reshape(half)])
    w1b, w2b = cmp_w1.astype(BF16), cmp_w2.astype(BF16)
    cmp_in = pl.BlockSpec((1, 1, nsub, half), lambda i, j: (i, j, 0, 0))
    cmp_out = pl.BlockSpec((1, 1, nsub, dk), lambda i, j: (i, j, 0, 0))
    kcc, vcc = pl.pallas_call(
        functools.partial(_nsa_compress_kernel, half),
        out_shape=[jax.ShapeDtypeStruct((b, g_, nsub, dk), BF16)] * 2,
        grid=(b, g_),
        in_specs=[cmp_in, cmp_in, pl.BlockSpec(pos.shape, lambda i, j: (0, 0)),
                  pl.BlockSpec(w1b.shape, lambda i, j: (0, 0, 0)), pl.BlockSpec(w2b.shape, lambda i, j: (0, 0, 0)),
                  pl.BlockSpec((1, dk), lambda i, j: (0, 0))],
        out_specs=[cmp_out, cmp_out],
        compiler_params=_params("parallel", "parallel"),
        name="nsa_compress",
    )(kc3, vc3, pos, w1b, w2b, qk_norm[1].reshape(1, dk))

    qt = NSA_Q_TILE
    nsel = s // SEL_LEN
    kt = min(NSA_KV_TILE, s)
    assert s % qt == 0 and s % kt == 0 and s >= WINDOW + qt and kt % qt == 0
    slopes = jnp.exp2(-8.0 * (jnp.arange(NSA_HEADS, dtype=F32) + 1) / NSA_HEADS).reshape(g_, hg)
    slope_rows = jnp.repeat(slopes, qt, axis=1).reshape(g_, hg * qt, 1)
    q_spec = pl.BlockSpec((1, 1, hg, qt, dk), lambda i, j, k: (i, j, 0, k, 0))
    gate_spec = pl.BlockSpec((1, 1, hg, qt, 3), lambda i, j, k: (i, j, 0, k, 0))
    slope_spec = pl.BlockSpec((1, hg * qt, 1), lambda i, j, k: (j, 0, 0))
    sel_spec = pl.BlockSpec((1, 1, qt, nsel), lambda i, j, k: (i, j, k, 0))
    seq_kv = lambda n: pl.BlockSpec((1, 1, n, dk), lambda i, j, k: (i, j, 0, 0))
    oc, sel = pl.pallas_call(
        functools.partial(_nsa_cmp_kernel, qt, nsub, nsel),
        out_shape=[jax.ShapeDtypeStruct((b, g_, hg, s, dk), F32), jax.ShapeDtypeStruct((b, g_, s, nsel), BF16)],
        grid=(b, g_, s // qt),
        in_specs=[q_spec, seq_kv(nsub), seq_kv(nsub), gate_spec, slope_spec],
        out_specs=[q_spec, sel_spec],
        compiler_params=_params("parallel", "parallel", "parallel"),
        name="nsa_cmp_select",
    )(q5, kcc, vcc, gt5, slope_rows)

    o5 = pl.pallas_call(
        functools.partial(_nsa_attn_kernel, qt, kt, nsel),
        out_shape=jax.ShapeDtypeStruct((b, g_, hg, s, dk), BF16),
        grid=(b, g_, s // qt),
        in_specs=[q_spec, seq_kv(s), seq_kv(s), seq_kv(s), seq_kv(s), sel_spec, gate_spec, slope_spec, q_spec],
        out_specs=q_spec,
        compiler_params=_params("parallel", "parallel", "arbitrary"),
        name="nsa_attention",
    )(q5, ks4, vs4, kw4, vw4, sel, gt5, slope_rows, oc)

    o2 = o5.transpose(0, 3, 1, 2, 4).reshape(t, qd)
    return _outproj(x.reshape(t, d), o2, gate, w_out, s).reshape(b, s, d)


def kernel(x, c, ada_w, ada_b, norm_mix_w, norm_ffn_w, ffn_w_gate, ffn_w_up, ffn_w_down, nsa_w_in, nsa_cmp_pos, nsa_cmp_w1, nsa_cmp_w2, nsa_qk_norm, nsa_w_out, rwkv_mu, rwkv_w_rkv, rwkv_w0, rwkv_wd_a, rwkv_wd_b, rwkv_a0, rwkv_wa_a, rwkv_wa_b, rwkv_wg_a, rwkv_wg_b, rwkv_k_k, rwkv_k_a, rwkv_r_k, rwkv_ln_w, rwkv_ln_b, rwkv_w_out, pool_w, pool_b, pool_scale):
    b, s, d = x.shape
    depth = ada_w.shape[0]
    mod = _ada_mod(c, ada_w, ada_b)
    for i in range(depth):
        sh_m, sc_m, g_m, sh_f, sc_f, g_f = jnp.split(mod[i], N_MOD, axis=-1)
        kind, j = i % N_MIXERS, i // N_MIXERS
        if kind == 0:
            x = _nsa_layer(x, norm_mix_w[i], sh_m, sc_m, g_m, nsa_w_in[j], nsa_cmp_pos[j], nsa_cmp_w1[j], nsa_cmp_w2[j],
                           nsa_qk_norm[j], nsa_w_out[j])
        elif kind == 1:
            x = _rwkv_layer(x, norm_mix_w[i], sh_m, sc_m, g_m, rwkv_mu[j], rwkv_w_rkv[j], rwkv_w0[j], rwkv_wd_a[j],
                            rwkv_wd_b[j], rwkv_a0[j], rwkv_wa_a[j], rwkv_wa_b[j], rwkv_wg_a[j], rwkv_wg_b[j],
                            rwkv_k_k[j], rwkv_k_a[j], rwkv_r_k[j], rwkv_ln_w[j], rwkv_ln_b[j], rwkv_w_out[j])
        elif kind == 2:
            x = _pool_layer(x, norm_mix_w[i], sh_m, sc_m, g_m, pool_w[j], pool_b[j], pool_scale[j])
        x = _ffn(x.reshape(b * s, d), norm_ffn_w[i], sh_f, sc_f, g_f, ffn_w_gate[i], ffn_w_up[i], ffn_w_down[i], s).reshape(b, s, d)
    return x
```

```python
import functools
import math

import jax
import jax.numpy as jnp
from jax import lax
from jax.experimental import pallas as pl
from jax.experimental.pallas import tpu as pltpu

F32 = jnp.float32
BF16 = jnp.bfloat16

V7X_LANES = 128
V7X_VMEM_LIMIT_BYTES = 56 * 1024 * 1024

RMS_EPS = 1e-6
N_MOD = 6
N_MIXERS = 3

NSA_HEAD_DIM = 64
NSA_KV_GROUPS = 4
NSA_HEADS_PER_GROUP = 4
NSA_HEADS = NSA_KV_GROUPS * NSA_HEADS_PER_GROUP
CMP_LEN = 32
CMP_STRIDE = 16
SEL_LEN = 64
SEL_TOPK = 16
WINDOW = 512
NSA_SCALE = NSA_HEAD_DIM ** -0.5
SEL_BIG = 1e9
MASKED_SCORE = -1e30

RWKV_HEAD_DIM = 64
RWKV_GN_EPS = 64e-5
RWKV_CHUNK = 64

POOL_WINDOWS = (2, 4, 8, 16)
POOL_HALO = 16

ROW_TILE = 512


def _params(*sem):
    return pltpu.CompilerParams(dimension_semantics=sem, vmem_limit_bytes=V7X_VMEM_LIMIT_BYTES)


def _sigmoid(z):
    return 1.0 / (1.0 + jnp.exp(-z))


def _normmod(x, nw, shift, scale):
    ms = jnp.mean(x * x, axis=-1, keepdims=True)
    return (x * lax.rsqrt(ms + RMS_EPS) * nw) * (1.0 + scale) + shift


def _dot(a, b):
    return jnp.dot(a.astype(BF16), b.astype(BF16), preferred_element_type=F32)


def _dot_nt(a, b):
    return lax.dot_general(a.astype(BF16), b.astype(BF16), (((1,), (1,)), ((), ())), preferred_element_type=F32)


def _dot_tn(a, b):
    return lax.dot_general(a.astype(BF16), b.astype(BF16), (((0,), (0,)), ((), ())), preferred_element_type=F32)


def _split_bf16(x, parts):
    out = []
    for _ in range(parts):
        p = x.astype(BF16)
        out.append(p)
        x = x - p.astype(F32)
    return out


def _mod_kernel(c_ref, w_ref, b_ref, o_ref):
    c = c_ref[...]
    ca = c * _sigmoid(c)
    o_ref[0] = _dot(ca, w_ref[0]) + b_ref[0]


def _ada_mod(c, ada_w, ada_b):
    depth, d, n = ada_w.shape
    b = c.shape[0]
    rows = 8
    tn = 1536
    assert b <= rows and n % tn == 0
    c_pad = jnp.zeros((rows, d), F32).at[:b].set(c)
    out = pl.pallas_call(
        _mod_kernel,
        out_shape=jax.ShapeDtypeStruct((depth, rows, n), F32),
        grid=(depth, n // tn),
        in_specs=[
            pl.BlockSpec((rows, d), lambda i, j: (0, 0)),
            pl.BlockSpec((1, d, tn), lambda i, j: (i, 0, j)),
            pl.BlockSpec((1, 1, tn), lambda i, j: (i, 0, j)),
        ],
        out_specs=pl.BlockSpec((1, rows, tn), lambda i, j: (i, 0, j)),
        compiler_params=_params("parallel", "parallel"),
        name="ada_mod",
    )(c_pad, ada_w, ada_b.reshape(depth, 1, n))
    return out[:, :b]


def _ffn_kernel(nj, x_ref, nw_ref, sh_ref, sc_ref, g_ref, wg_ref, wu_ref, wd_ref, o_ref, h_scr, acc_scr):
    j = pl.program_id(1)

    @pl.when(j == 0)
    def _():
        h_scr[...] = _normmod(x_ref[...], nw_ref[...], sh_ref[0], sc_ref[0]).astype(BF16)
        acc_scr[...] = jnp.zeros_like(acc_scr)

    h = h_scr[...]
    g = jnp.dot(h, wg_ref[...], preferred_element_type=F32)
    u = jnp.dot(h, wu_ref[...], preferred_element_type=F32)
    a = g * _sigmoid(g) * u
    acc_scr[...] += jnp.dot(a.astype(BF16), wd_ref[...], preferred_element_type=F32)

    @pl.when(j == nj - 1)
    def _():
        o_ref[...] = x_ref[...] + g_ref[0] * acc_scr[...]


def _ffn(x2, nw, shift, scale, gate, wg, wu, wd, seq):
    t, d = x2.shape
    f = wg.shape[1]
    tm = ROW_TILE
    fc = f // 2
    assert t % tm == 0 and seq % tm == 0 and fc % V7X_LANES == 0
    tpb = seq // tm
    nj = f // fc
    mod_spec = pl.BlockSpec((1, 1, d), lambda i, j: (i // tpb, 0, 0))
    return pl.pallas_call(
        functools.partial(_ffn_kernel, nj),
        out_shape=jax.ShapeDtypeStruct((t, d), F32),
        grid=(t // tm, nj),
        in_specs=[
            pl.BlockSpec((tm, d), lambda i, j: (i, 0)),
            pl.BlockSpec((1, d), lambda i, j: (0, 0)),
            mod_spec, mod_spec, mod_spec,
            pl.BlockSpec((d, fc), lambda i, j: (0, j)),
            pl.BlockSpec((d, fc), lambda i, j: (0, j)),
            pl.BlockSpec((fc, d), lambda i, j: (j, 0)),
        ],
        out_specs=pl.BlockSpec((tm, d), lambda i, j: (i, 0)),
        scratch_shapes=[pltpu.VMEM((tm, d), BF16), pltpu.VMEM((tm, d), F32)],
        compiler_params=_params("parallel", "arbitrary"),
        name="ffn",
    )(x2, nw.reshape(1, d), shift[:, None], scale[:, None], gate[:, None],
      wg.astype(BF16), wu.astype(BF16), wd.astype(BF16))


def _outproj_kernel(x_ref, z_ref, g_ref, w_ref, o_ref):
    o_ref[...] = x_ref[...] + g_ref[0] * _dot(z_ref[...], w_ref[...])


def _outproj(x2, z2, gate, w, seq):
    t, d = x2.shape
    k = z2.shape[1]
    tm = ROW_TILE
    tpb = seq // tm
    return pl.pallas_call(
        _outproj_kernel,
        out_shape=jax.ShapeDtypeStruct((t, d), F32),
        grid=(t // tm,),
        in_specs=[
            pl.BlockSpec((tm, d), lambda i: (i, 0)),
            pl.BlockSpec((tm, k), lambda i: (i, 0)),
            pl.BlockSpec((1, 1, d), lambda i: (i // tpb, 0, 0)),
            pl.BlockSpec((k, d), lambda i: (0, 0)),
        ],
        out_specs=pl.BlockSpec((tm, d), lambda i: (i, 0)),
        compiler_params=_params("parallel"),
        name="outproj",
    )(x2, z2, gate[:, None], w.astype(BF16))


def _pool_kernel(tm, gd, x_ref, nw_ref, sh_ref, sc_ref, g_ref, pw_ref, pb_ref, ps_ref, o_ref, ext_scr):
    s = pl.program_id(1)
    x = x_ref[0]
    h = _normmod(x, nw_ref[...], sh_ref[0], sc_ref[0])

    @pl.when(s == 0)
    def _():
        ext_scr[0:POOL_HALO, :] = jnp.zeros((POOL_HALO, x.shape[1]), F32)

    ext_scr[POOL_HALO:POOL_HALO + tm, :] = h
    row = s * tm + lax.broadcasted_iota(jnp.int32, (tm, 1), 0)
    ys = []
    for gi, win in enumerate(POOL_WINDOWS):
        lanes = slice(gi * gd, (gi + 1) * gd)
        hg = h[:, lanes]
        acc = hg
        for k in range(1, win):
            acc = acc + ext_scr[POOL_HALO - k:POOL_HALO - k + tm, lanes]
        cnt = jnp.minimum(row + 1, win).astype(F32)
        ys.append(_dot(acc / cnt - hg, pw_ref[gi]))
    y = (jnp.concatenate(ys, axis=-1) + pb_ref[...]) * ps_ref[...]
    o_ref[0] = x + g_ref[0] * y
    ext_scr[0:POOL_HALO, :] = ext_scr[tm:tm + POOL_HALO, :]


def _pool_layer(x, nw, shift, scale, gate, pw, pb, ps):
    b, s, d = x.shape
    tm = ROW_TILE
    gd = d // len(POOL_WINDOWS)
    assert s % tm == 0 and max(POOL_WINDOWS) <= POOL_HALO
    mod_spec = pl.BlockSpec((1, 1, d), lambda i, j: (i, 0, 0))
    vec_spec = pl.BlockSpec((1, d), lambda i, j: (0, 0))
    return pl.pallas_call(
        functools.partial(_pool_kernel, tm, gd),
        out_shape=jax.ShapeDtypeStruct((b, s, d), F32),
        grid=(b, s // tm),
        in_specs=[
            pl.BlockSpec((1, tm, d), lambda i, j: (i, j, 0)),
            vec_spec, mod_spec, mod_spec, mod_spec,
            pl.BlockSpec(pw.shape, lambda i, j: (0, 0, 0)),
            vec_spec, vec_spec,
        ],
        out_specs=pl.BlockSpec((1, tm, d), lambda i, j: (i, j, 0)),
        scratch_shapes=[pltpu.VMEM((tm + POOL_HALO, d), F32)],
        compiler_params=_params("arbitrary", "arbitrary"),
        name="pool_mixer",
    )(x, nw.reshape(1, d), shift[:, None], scale[:, None], gate[:, None],
      pw.astype(BF16), pb.reshape(1, d), ps.reshape(1, d))


def _rwkv_proj_kernel(tm, x_ref, nw_ref, sh_ref, sc_ref, mu_ref, wr_ref, wk_ref, wv_ref,
                      w0_ref, wda_ref, wdb_ref, a0_ref, waa_ref, wab_ref, wga_ref, wgb_ref,
                      r_ref, k_ref, v_ref, lw_ref, a_ref, g_ref, ext_scr):
    s = pl.program_id(1)
    h = _normmod(x_ref[0], nw_ref[...], sh_ref[0], sc_ref[0])

    @pl.when(s == 0)
    def _():
        ext_scr[0:8, :] = jnp.zeros((8, h.shape[1]), F32)

    ext_scr[8:8 + tm, :] = h
    xx = ext_scr[7:7 + tm, :] - h

    def mix(i):
        return h + xx * mu_ref[i:i + 1, :]

    r_ref[0] = _dot(mix(0), wr_ref[...])
    k_ref[0] = _dot(mix(2), wk_ref[...])
    v_ref[0] = _dot(mix(3), wv_ref[...])
    dw = w0_ref[...] + _dot(jnp.tanh(_dot(mix(1), wda_ref[...])), wdb_ref[...])
    softplus_neg = jnp.maximum(-dw, 0.0) + jnp.log(1.0 + jnp.exp(-jnp.abs(dw)))
    lw_ref[0] = -jnp.exp(-softplus_neg - 0.5)
    a_ref[0] = _sigmoid(a0_ref[...] + _dot(_dot(mix(4), waa_ref[...]), wab_ref[...]))
    g_ref[0] = _dot(_sigmoid(_dot(mix(5), wga_ref[...])), wgb_ref[...])
    ext_scr[0:8, :] = ext_scr[tm:tm + 8, :]


def _rwkv_scan_kernel(ct, r_ref, k_ref, v_ref, lw_ref, a_ref, g_ref, kk_ref, ka_ref, rk_ref, lnw_ref, lnb_ref,
                      z_ref, s_scr):
    c, n = RWKV_CHUNK, RWKV_HEAD_DIM
    w = 2 * n
    shift = int(math.log2(n))

    @pl.when(pl.program_id(2) == 0)
    def _():
        s_scr[...] = jnp.zeros_like(s_scr)

    r, k, v, lw, a = r_ref[0], k_ref[0], v_ref[0], lw_ref[0], a_ref[0]
    head0 = lax.broadcasted_iota(jnp.int32, (1, w), 1) < n
    ri = lax.broadcasted_iota(jnp.int32, (w, w), 0)
    ci = lax.broadcasted_iota(jnp.int32, (w, w), 1)
    same_head = jnp.where((ri >> shift) == (ci >> shift), 1.0, 0.0).astype(BF16)
    strict, incl, eye = ri > ci, ri >= ci, ri == ci
    rt = lax.broadcasted_iota(jnp.int32, (ct, ct), 0)
    cc = lax.broadcasted_iota(jnp.int32, (ct, ct), 1)
    cshift = int(math.log2(c))
    chunk_tri = jnp.where(((rt >> cshift) == (cc >> cshift)) & (rt >= cc), 1.0, 0.0).astype(BF16)

    def head_sum(xf):
        return sum(jnp.dot(p, same_head, preferred_element_type=F32) for p in _split_bf16(xf, 2))

    kk = k * kk_ref[...]
    kkn = kk * lax.rsqrt(jnp.maximum(head_sum(kk * kk), 1e-24))
    k2 = k * (1.0 + (a - 1.0) * ka_ref[...])
    beta = kkn * a
    cum = sum(jnp.dot(chunk_tri, p, preferred_element_type=F32) for p in _split_bf16(lw, 3))
    at_all = -kkn * jnp.exp(cum - lw)
    rt_all = r * jnp.exp(cum)
    einv = jnp.exp(-cum)
    kb_all = beta * einv
    kq_all = k2 * einv

    def stack(xc):
        return jnp.concatenate([jnp.where(head0, xc, 0.0), jnp.where(head0, 0.0, xc)], axis=0)

    state = s_scr[...]
    ys = []
    for idx in range(ct // c):
        sl = slice(idx * c, (idx + 1) * c)
        pc = jnp.exp(cum[(idx + 1) * c - 1:(idx + 1) * c, :])
        at, rt_, kb, kq, vs = stack(at_all[sl]), stack(rt_all[sl]), stack(kb_all[sl]), stack(kq_all[sl]), stack(v[sl])
        kbe, kqe = kb * pc, kq * pc
        a_m = jnp.where(strict, _dot_nt(at, kb), 0.0)
        b_m = jnp.where(strict, _dot_nt(at, kq), 0.0)
        ar_m = jnp.where(incl, _dot_nt(rt_, kb), 0.0)
        br_m = jnp.where(incl, _dot_nt(rt_, kq), 0.0)
        tinv = jnp.where(eye, 1.0, 0.0) + a_m
        pw = a_m
        for _ in range(cshift - 1):
            pw = _dot(pw, pw)
            tinv = tinv + _dot(pw, tinv)
        w1 = _dot(tinv, at)
        w2 = _dot(tinv, _dot(b_m, vs))
        g_m = rt_ + _dot(ar_m, w1)
        y0 = _dot(ar_m, w2) + _dot(br_m, vs)
        m_m = jnp.where(eye, jnp.broadcast_to(pc, (w, w)), 0.0) + _dot_tn(w1, kbe)
        n_m = _dot_tn(w2, kbe) + _dot_tn(vs, kqe)
        yst = _dot_nt(g_m, state) + y0
        state = _dot(state, m_m) + n_m
        ys.append(yst[:c] + yst[c:])
    s_scr[...] = state
    y = jnp.concatenate(ys, axis=0)

    mean = head_sum(y) * (1.0 / n)
    dlt = y - mean
    var = head_sum(dlt * dlt) * (1.0 / n)
    yn = dlt * lax.rsqrt(var + RWKV_GN_EPS) * lnw_ref[...] + lnb_ref[...]
    bonus = head_sum(r * k2 * rk_ref[...]) * v
    z_ref[0] = ((yn + bonus) * g_ref[0]).astype(z_ref.dtype)


def _rwkv_layer(x, nw, shift, scale, gate, mu, w_rkv, w0, wd_a, wd_b, a0, wa_a, wa_b, wg_a, wg_b,
                k_k, k_a, r_k, ln_w, ln_b, w_out):
    b, s, d = x.shape
    tm = ROW_TILE
    assert s % tm == 0
    mod_spec = pl.BlockSpec((1, 1, d), lambda i, j: (i, 0, 0))
    vec_spec = pl.BlockSpec((1, d), lambda i, j: (0, 0))
    tok_spec = pl.BlockSpec((1, tm, d), lambda i, j: (i, j, 0))

    def full(arr):
        return pl.BlockSpec(arr.shape, lambda i, j: (0,) * arr.ndim)

    weights = [w_rkv[0].astype(BF16), w_rkv[1].astype(BF16), w_rkv[2].astype(BF16),
               w0.reshape(1, d), wd_a.astype(BF16), wd_b.astype(BF16),
               a0.reshape(1, d), wa_a.astype(BF16), wa_b.astype(BF16),
               wg_a.astype(BF16), wg_b.astype(BF16)]
    tok_shape = jax.ShapeDtypeStruct((b, s, d), F32)
    r, k, v, lw, a, g = pl.pallas_call(
        functools.partial(_rwkv_proj_kernel, tm),
        out_shape=[tok_shape] * 6,
        grid=(b, s // tm),
        in_specs=[tok_spec, vec_spec, mod_spec, mod_spec, full(mu)] + [full(wt) for wt in weights],
        out_specs=[tok_spec] * 6,
        scratch_shapes=[pltpu.VMEM((tm + 8, d), F32)],
        compiler_params=_params("arbitrary", "arbitrary"),
        name="rwkv_proj",
    )(x, nw.reshape(1, d), shift[:, None], scale[:, None], mu, *weights)

    ct = 256
    pair = 2 * RWKV_HEAD_DIM
    assert s % ct == 0 and d % pair == 0 and pair == V7X_LANES
    seq_spec = pl.BlockSpec((1, ct, pair), lambda i, p, j: (i, j, p))
    par_spec = pl.BlockSpec((1, pair), lambda i, p, j: (0, p))
    z = pl.pallas_call(
        functools.partial(_rwkv_scan_kernel, ct),
        out_shape=jax.ShapeDtypeStruct((b, s, d), BF16),
        grid=(b, d // pair, s // ct),
        in_specs=[seq_spec] * 6 + [par_spec] * 5,
        out_specs=seq_spec,
        scratch_shapes=[pltpu.VMEM((pair, pair), F32)],
        compiler_params=_params("parallel", "parallel", "arbitrary"),
        name="rwkv_scan",
    )(r, k, v, lw, a, g, k_k.reshape(1, d), k_a.reshape(1, d), r_k.reshape(1, d), ln_w.reshape(1, d), ln_b.reshape(1, d))
    return _outproj(x.reshape(b * s, d), z.reshape(b * s, d), gate, w_out, s).reshape(b, s, d)


NSA_Q_TILE = 128
NSA_KV_TILE = 512
NSA_PROJ_PAD = 128
NSA_QK_WIDTH = 128


def _head_rms(xf, p_ref, pt_ref, wvec):
    sums = sum(jnp.dot(part, p_ref[...], preferred_element_type=F32) for part in _split_bf16(xf * xf, 2))
    inv = lax.rsqrt(sums * (1.0 / NSA_HEAD_DIM) + RMS_EPS)
    inv_full = sum(jnp.dot(part, pt_ref[...], preferred_element_type=F32) for part in _split_bf16(inv, 3))
    return xf * inv_full * wvec


def _nsa_proj_kernel(qd, kd, x_ref, nw_ref, sh_ref, sc_ref, w_ref, pq_ref, pqt_ref, pk_ref, pkt_ref,
                     nq_ref, nks_ref, nkw_ref,
                     q_ref, kc_ref, vc_ref, ks_ref, vs_ref, kw_ref, vw_ref, gt_ref):
    h = _normmod(x_ref[...], nw_ref[...], sh_ref[0], sc_ref[0])
    proj = _dot(h, w_ref[...])
    o = qd
    q_ref[...] = (_head_rms(proj[:, :qd], pq_ref, pqt_ref, nq_ref[...]) * NSA_SCALE).astype(BF16)
    kc_ref[...] = proj[:, o:o + kd]
    vc_ref[...] = proj[:, o + kd:o + 2 * kd]
    ks_ref[...] = _head_rms(proj[:, o + 2 * kd:o + 3 * kd], pk_ref, pkt_ref, nks_ref[...]).astype(BF16)
    vs_ref[...] = proj[:, o + 3 * kd:o + 4 * kd].astype(BF16)
    kw_ref[...] = _head_rms(proj[:, o + 4 * kd:o + 5 * kd], pk_ref, pkt_ref, nkw_ref[...]).astype(BF16)
    vw_ref[...] = proj[:, o + 5 * kd:o + 6 * kd].astype(BF16)
    gt_ref[...] = _sigmoid(proj[:, o + 6 * kd:o + 6 * kd + NSA_PROJ_PAD])


def _gelu_tanh(x):
    return 0.5 * x * (1.0 + jnp.tanh(math.sqrt(2.0 / math.pi) * (x + 0.044715 * (x * x * x))))


def _nsa_compress_kernel(half, kc_ref, vc_ref, pos_ref, w1_ref, w2_ref, nk_ref, kco_ref, vco_ref):
    nsub = kc_ref.shape[2]

    def comp(x, i):
        ya = _dot(x + pos_ref[2 * i:2 * i + 1, :], w1_ref[i, 0:half, :])
        yb = _dot(x + pos_ref[2 * i + 1:2 * i + 2, :], w1_ref[i, half:2 * half, :])
        hid = ya + pltpu.roll(yb, nsub - 1, 0)
        return _dot(_gelu_tanh(hid), w2_ref[i])

    kcm = comp(kc_ref[0, 0], 0)
    ms = jnp.mean(kcm * kcm, axis=-1, keepdims=True)
    kco_ref[0, 0] = (kcm * lax.rsqrt(ms + RMS_EPS) * nk_ref[...]).astype(BF16)
    vco_ref[0, 0] = comp(vc_ref[0, 0], 1).astype(BF16)


def _nsa_cmp_kernel(qt, nc, nsel, q_ref, kc_ref, vc_ref, g_ref, oc_ref, sel_ref):
    hg = NSA_HEADS_PER_GROUP
    q0 = pl.program_id(2) * qt
    q = q_ref[0, 0].reshape(hg * qt, NSA_QK_WIDTH)
    s3 = _dot_nt(q, kc_ref[0, 0]).reshape(hg, qt, nc)
    t = q0 + lax.broadcasted_iota(jnp.int32, (qt, nc), 0)
    n = lax.broadcasted_iota(jnp.int32, (qt, nc), 1)
    mask = (t >= n * CMP_STRIDE + CMP_LEN - 1)[None]
    sm = jnp.where(mask, s3, MASKED_SCORE)
    m = jnp.max(sm, axis=-1, keepdims=True)
    e = jnp.where(mask, jnp.exp(sm - m), 0.0)
    p = e / jnp.maximum(jnp.sum(e, axis=-1, keepdims=True), 1e-30)
    oc = _dot(p.reshape(hg * qt, nc), vc_ref[0, 0])
    gate_c = g_ref[0, 0].reshape(hg * qt, 3)[:, 0:1]
    oc_ref[0, 0] = (oc * gate_c).reshape(hg, qt, NSA_HEAD_DIM)

    psum = p[0] + p[1] + p[2] + p[3]
    jo = lax.broadcasted_iota(jnp.int32, (nsel, nc), 0)
    no = lax.broadcasted_iota(jnp.int32, (nsel, nc), 1)
    ratio = SEL_LEN // CMP_STRIDE
    first = jo * ratio - (CMP_LEN // CMP_STRIDE - 1)
    overlap_t = jnp.where((no >= first) & (no < (jo + 1) * ratio) & (no < nc - 1), 1.0, 0.0).astype(BF16)
    imp_t = sum(lax.dot_general(overlap_t, part, (((1,), (1,)), ((), ())), preferred_element_type=F32)
                for part in _split_bf16(psum, 3))

    j = lax.broadcasted_iota(jnp.int32, (nsel, qt), 0)
    tt = q0 + lax.broadcasted_iota(jnp.int32, (nsel, qt), 1)
    valid = j * SEL_LEN <= tt
    qb = tt >> int(math.log2(SEL_LEN))
    forced = (j == 0) | (j == qb) | (j == qb - 1)
    score = jnp.where(valid, jnp.where(forced, SEL_BIG, imp_t), -SEL_BIG)
    jf = j.astype(F32)
    sel = jnp.zeros((nsel, qt), F32)
    for _ in range(min(SEL_TOPK, nsel)):
        mx = jnp.max(score, axis=0, keepdims=True)
        jmin = jnp.min(jnp.where(score == mx, jf, float(nsel)), axis=0, keepdims=True)
        pick = jf == jmin
        sel = jnp.where(pick, 1.0, sel)
        score = jnp.where(pick, -3e38, score)
    sel = jnp.where(valid, sel, 0.0)
    sel_ref[0, 0] = sel.T.astype(BF16)


def _nsa_attn_kernel(qt, kt, nsel, nkt, nq, tiles_ref, cnt_ref,
                     q_ref, ks_ref, vs_ref, kw_ref, vw_ref, sel_ref, g_ref, oc_ref, o_ref):
    hg, dk = NSA_HEADS_PER_GROUP, NSA_HEAD_DIM
    rows = hg * qt
    step = (pl.program_id(0) * NSA_KV_GROUPS + pl.program_id(1)) * nq + pl.program_id(2)
    q0 = pl.program_id(2) * qt
    q = q_ref[0, 0].reshape(rows, NSA_QK_WIDTH)
    sel = sel_ref[0, 0]
    lsel = int(math.log2(SEL_LEN))

    t_k = q0 + lax.broadcasted_iota(jnp.int32, (qt, kt), 0)
    c_k = lax.broadcasted_iota(jnp.int32, (qt, kt), 1)
    blk_of_col = lax.broadcasted_iota(jnp.int32, (nsel, kt), 0) - (lax.broadcasted_iota(jnp.int32, (nsel, kt), 1) >> lsel)

    def sel_step(i, carry):
        m, l, acc = carry
        k0 = pl.multiple_of(tiles_ref[step * nkt + i] * kt, kt)
        s3 = _dot_nt(q, ks_ref[0, 0, pl.ds(k0, kt), :]).reshape(hg, qt, kt)
        expand = jnp.where(blk_of_col == (k0 >> lsel), 1.0, 0.0).astype(BF16)
        chosen = jnp.dot(sel, expand, preferred_element_type=F32)
        mask = ((chosen > 0.5) & (k0 + c_k <= t_k))[None]
        sm = jnp.where(mask, s3, MASKED_SCORE).reshape(rows, kt)
        m_new = jnp.maximum(m, jnp.max(sm, axis=-1, keepdims=True))
        alpha = jnp.exp(m - m_new)
        p = jnp.exp(sm - m_new)
        l = alpha * l + jnp.sum(p, axis=-1, keepdims=True)
        acc = alpha * acc + _dot(p, vs_ref[0, 0, pl.ds(k0, kt), :])
        return m_new, l, acc

    init = (jnp.full((rows, 1), MASKED_SCORE, F32), jnp.zeros((rows, 1), F32), jnp.zeros((rows, dk), F32))
    _, l_s, acc_s = lax.fori_loop(0, cnt_ref[step], sel_step, init)
    o_s = acc_s / jnp.maximum(l_s, 1e-30)

    wk = WINDOW + qt
    ws = pl.multiple_of(jnp.maximum(q0 - WINDOW, 0), qt)
    s3 = _dot_nt(q, kw_ref[0, 0, pl.ds(ws, wk), :]).reshape(hg, qt, wk)
    dist = (q0 + lax.broadcasted_iota(jnp.int32, (qt, wk), 0)) - (ws + lax.broadcasted_iota(jnp.int32, (qt, wk), 1))
    mask = ((dist >= 0) & (dist < WINDOW))[None]
    sm = jnp.where(mask, s3, MASKED_SCORE).reshape(rows, wk)
    e = jnp.exp(sm - jnp.max(sm, axis=-1, keepdims=True))
    o_w =_dot(e, vw_ref[0, 0, pl.ds(ws, wk), :]) / jnp.maximum(jnp.sum(e, axis=-1, keepdims=True), 1e-30)

    g = g_ref[0, 0].reshape(rows, 3)
    o = oc_ref[0, 0].reshape(rows, dk) + g[:, 1:2] * o_s + g[:, 2:3] * o_w
    o_ref[0, 0] = o.reshape(hg, qt, dk).astype(o_ref.dtype)


def _nsa_layer(x, nw, shift, scale, gate, w_in, cmp_pos, cmp_w1, cmp_w2, qk_norm, w_out):
    b, s, d = x.shape
    t = b * s
    g_, hg, dk = NSA_KV_GROUPS, NSA_HEADS_PER_GROUP, NSA_HEAD_DIM
    qd, kd = NSA_HEADS * dk, g_ * dk
    tm = ROW_TILE
    tpb = s // tm
    n_in = w_in.shape[1]
    n_pad = qd + 6 * kd + NSA_PROJ_PAD
    assert n_in <= n_pad and s % tm == 0
    w_pad = jnp.zeros((d, n_pad), BF16).at[:, :n_in].set(w_in.astype(BF16))

    def head_onehot(width):
        lane_head = jnp.arange(width)[:, None] // dk
        return (lane_head == jnp.arange(V7X_LANES)[None, :]).astype(BF16)

    pq, pk = head_onehot(qd), head_onehot(kd)

    def tile_w(wv, reps):
        return jnp.tile(wv, reps).reshape(1, reps * dk)

    def full2(arr):
        return pl.BlockSpec(arr.shape, lambda i: (0, 0))

    consts = [pq, pq.T, pk, pk.T, tile_w(qk_norm[0], NSA_HEADS), tile_w(qk_norm[2], g_), tile_w(qk_norm[3], g_)]
    row = lambda n: pl.BlockSpec((tm, n), lambda i: (i, 0))
    mod_spec = pl.BlockSpec((1, 1, d), lambda i: (i // tpb, 0, 0))
    outs = pl.pallas_call(
        functools.partial(_nsa_proj_kernel, qd, kd),
        out_shape=[jax.ShapeDtypeStruct((t, qd), BF16), jax.ShapeDtypeStruct((t, kd), F32), jax.ShapeDtypeStruct((t, kd), F32)]
        + [jax.ShapeDtypeStruct((t, kd), BF16)] * 4 + [jax.ShapeDtypeStruct((t, NSA_PROJ_PAD), F32)],
        grid=(t // tm,),
        in_specs=[row(d), pl.BlockSpec((1, d), lambda i: (0, 0)), mod_spec, mod_spec, full2(w_pad)] + [full2(cst) for cst in consts],
        out_specs=[row(qd), row(kd), row(kd), row(kd), row(kd), row(kd), row(kd), row(NSA_PROJ_PAD)],
        compiler_params=_params("parallel"),
        name="nsa_proj",
    )(x.reshape(t, d), nw.reshape(1, d), shift[:, None], scale[:, None], w_pad, *consts)
    qn, kc, vc, ksn, vs, kwn, vw, gates = outs

    q5 = qn.reshape(b, s, g_, hg, dk).transpose(0, 2, 3, 1, 4)
    per_group = lambda arr: arr.reshape(b, s, g_, dk).transpose(0, 2, 1, 3)
    ks4, vs4, kw4, vw4 = per_group(ksn), per_group(vs), per_group(kwn), per_group(vw)
    nsub = s // CMP_STRIDE
    half = CMP_STRIDE * dk
    kc3 = per_group(kc).reshape(b, g_, nsub, half)
    vc3 = per_group(vc).reshape(b, g_, nsub, half)
    gt5 = gates[:, :3 * NSA_HEADS].reshape(b, s, 3, g_, hg).transpose(0, 3, 4, 1, 2)

    pos = jnp.stack([cmp_pos[0, :CMP_STRIDE].reshape(half), cmp_pos[0, CMP_STRIDE:].reshape(half),
                     cmp_pos[1, :CMP_STRIDE].reshape(half), cmp_pos[1, CMP_STRIDE:].reshape(half)])
    w1b, w2b = cmp_w1.astype(BF16), cmp_w2.astype(BF16)
    cmp_in = pl.BlockSpec((1, 1, nsub, half), lambda i, j: (i, j, 0, 0))
    cmp_out = pl.BlockSpec((1, 1, nsub, dk), lambda i, j: (i, j, 0, 0))
    kcc, vcc = pl.pallas_call(
        functools.partial(_nsa_compress_kernel, half),
        out_shape=[jax.ShapeDtypeStruct((b, g_, nsub, dk), BF16)] * 2,
        grid=(b, g_),
        in_specs=[cmp_in, cmp_in, pl.BlockSpec(pos.shape, lambda i, j: (0, 0)),
                  pl.BlockSpec(w1b.shape, lambda i, j: (0, 0, 0)), pl.BlockSpec(w2b.shape, lambda i, j: (0, 0, 0)),
                  pl.BlockSpec((1, dk), lambda i, j: (0, 0))],
        out_specs=[cmp_out, cmp_out],
        compiler_params=_params("parallel", "parallel"),
        name="nsa_compress",
    )(kc3, vc3, pos, w1b, w2b, qk_norm[1].reshape(1, dk))

    qt = NSA_Q_TILE
    nq = s // qt
    nsel = s // SEL_LEN
    kt = min(NSA_KV_TILE, s)
    nkt = s // kt
    assert s % qt == 0 and s % kt == 0 and s >= WINDOW + qt and kt % qt == 0

    slopes = jnp.exp2(-8.0 * (jnp.arange(NSA_HEADS, dtype=F32) + 1) / NSA_HEADS)
    slope_parts = jnp.stack(_split_bf16(slopes, 3), axis=-1)
    q_feat = jnp.concatenate([slope_parts, slope_parts], axis=-1).reshape(1, g_, hg, 1, 6)

    def with_lanes(arr, feat):
        feat = jnp.broadcast_to(feat.astype(BF16), arr.shape[:-1] + (feat.shape[-1],))
        pad = jnp.zeros(arr.shape[:-1] + (NSA_QK_WIDTH - dk - feat.shape[-1],), BF16)
        return jnp.concatenate([arr, feat, pad], axis=-1)

    def pos_feat(pos):
        hi = ((pos >> 6) << 6).astype(F32)
        lo = (pos & 63).astype(F32)
        return jnp.stack([hi, hi, hi, lo, lo, lo], axis=-1)

    q5p = with_lanes(q5, q_feat)
    tok_feat = pos_feat(jnp.arange(s, dtype=jnp.int32))
    ks4p, kw4p = with_lanes(ks4, tok_feat), with_lanes(kw4, tok_feat)
    kccp = with_lanes(kcc, pos_feat(jnp.arange(nsub, dtype=jnp.int32) * CMP_STRIDE + CMP_LEN - 1))

    q_spec = pl.BlockSpec((1, 1, hg, qt, NSA_QK_WIDTH), lambda i, j, k, *_: (i, j, 0, k, 0))
    o_spec = pl.BlockSpec((1, 1, hg, qt, dk), lambda i, j, k, *_: (i, j, 0, k, 0))
    gate_spec = pl.BlockSpec((1, 1, hg, qt, 3), lambda i, j, k, *_: (i, j, 0, k, 0))
    sel_spec = pl.BlockSpec((1, 1, qt, nsel), lambda i, j, k, *_: (i, j, k, 0))
    seq_kv = lambda n, wd: pl.BlockSpec((1, 1, n, wd), lambda i, j, k, *_: (i, j, 0, 0))
    oc, sel = pl.pallas_call(
        functools.partial(_nsa_cmp_kernel, qt, nsub, nsel),
        out_shape=[jax.ShapeDtypeStruct((b, g_, hg, s, dk), F32), jax.ShapeDtypeStruct((b, g_, s, nsel), BF16)],
        grid=(b, g_, nq),
        in_specs=[q_spec, seq_kv(nsub, NSA_QK_WIDTH), seq_kv(nsub, dk), gate_spec],
        out_specs=[o_spec, sel_spec],
        compiler_params=_params("parallel", "parallel", "parallel"),
        name="nsa_cmp_select",
    )(q5p, kccp, vcc, gt5)

    active = sel.reshape(b, g_, nq, qt, nkt, kt // SEL_LEN).max(axis=(3, 5)) > 0
    order = jnp.sort(jnp.where(active, 0, nkt) + jnp.arange(nkt, dtype=jnp.int32), axis=-1)
    tiles = (order % nkt).astype(jnp.int32).reshape(-1)
    counts = active.sum(axis=-1).astype(jnp.int32).reshape(-1)

    o5 = pl.pallas_call(
        functools.partial(_nsa_attn_kernel, qt, kt, nsel, nkt, nq),
        out_shape=jax.ShapeDtypeStruct((b, g_, hg, s, dk), BF16),
        grid_spec=pltpu.PrefetchScalarGridSpec(
            num_scalar_prefetch=2,
            grid=(b, g_, nq),
            in_specs=[q_spec, seq_kv(s, NSA_QK_WIDTH), seq_kv(s, dk), seq_kv(s, NSA_QK_WIDTH), seq_kv(s, dk),
                      sel_spec, gate_spec, o_spec],
            out_specs=o_spec,
        ),
        compiler_params=_params("parallel", "parallel", "arbitrary"),
        name="nsa_attention",
    )(tiles, counts, q5p, ks4p, vs4, kw4p, vw4, sel, gt5, oc)

    o2 = o5.transpose(0, 3, 1, 2, 4).reshape(t, qd)
    return _outproj(x.reshape(t, d), o2, gate, w_out, s).reshape(b, s, d)


def kernel(x, c, ada_w, ada_b, norm_mix_w, norm_ffn_w, ffn_w_gate, ffn_w_up, ffn_w_down, nsa_w_in, nsa_cmp_pos, nsa_cmp_w1, nsa_cmp_w2, nsa_qk_norm, nsa_w_out, rwkv_mu, rwkv_w_rkv, rwkv_w0, rwkv_wd_a, rwkv_wd_b, rwkv_a0, rwkv_wa_a, rwkv_wa_b, rwkv_wg_a, rwkv_wg_b, rwkv_k_k, rwkv_k_a, rwkv_r_k, rwkv_ln_w, rwkv_ln_b, rwkv_w_out, pool_w, pool_b, pool_scale):
    b, s, d = x.shape
    depth = ada_w.shape[0]
    mod = _ada_mod(c, ada_w, ada_b)
    for i in range(depth):
        sh_m, sc_m, g_m, sh_f, sc_f, g_f = jnp.split(mod[i], N_MOD, axis=-1)
        kind, j = i % N_MIXERS, i // N_MIXERS
        if kind == 0:
            x = _nsa_layer(x, norm_mix_w[i], sh_m, sc_m, g_m, nsa_w_in[j], nsa_cmp_pos[j], nsa_cmp_w1[j], nsa_cmp_w2[j],
                           nsa_qk_norm[j], nsa_w_out[j])
        elif kind == 1:
            x = _rwkv_layer(x, norm_mix_w[i], sh_m, sc_m, g_m, rwkv_mu[j], rwkv_w_rkv[j], rwkv_w0[j], rwkv_wd_a[j],
                            rwkv_wd_b[j], rwkv_a0[j], rwkv_wa_a[j], rwkv_wa_b[j], rwkv_wg_a[j], rwkv_wg_b[j],
                            rwkv_k_k[j], rwkv_k_a[j], rwkv_r_k[j], rwkv_ln_w[j], rwkv_ln_b[j], rwkv_w_out[j])
        elif kind == 2:
            x = _pool_layer(x, norm_mix_w[i], sh_m, sc_m, g_m, pool_w[j], pool_b[j], pool_scale[j])
        x = _ffn(x.reshape(b * s, d), norm_ffn_w[i], sh_f, sc_f, g_f, ffn_w_gate[i], ffn_w_up[i], ffn_w_down[i], s).reshape(b, s, d)
    return x
```

```python
import functools
import math

import jax
import jax.numpy as jnp
from jax import lax
from jax.experimental import pallas as pl
from jax.experimental.pallas import tpu as pltpu

F32 = jnp.float32
BF16 = jnp.bfloat16

V7X_LANES = 128
V7X_VMEM_LIMIT_BYTES = 56 * 1024 * 1024

RMS_EPS = 1e-6
N_MOD = 6
N_MIXERS = 3

NSA_HEAD_DIM = 64
NSA_KV_GROUPS = 4
NSA_HEADS_PER_GROUP = 4
NSA_HEADS = NSA_KV_GROUPS * NSA_HEADS_PER_GROUP
CMP_LEN = 32
CMP_STRIDE = 16
SEL_LEN = 64
SEL_TOPK = 16
WINDOW = 512
NSA_SCALE = NSA_HEAD_DIM ** -0.5
SEL_BIG = 1e9
MASKED_SCORE = -1e30

RWKV_HEAD_DIM = 64
RWKV_GN_EPS = 64e-5
RWKV_CHUNK = 64

POOL_WINDOWS = (2, 4, 8, 16)
POOL_HALO = 16

ROW_TILE = 512


def _params(*sem):
    return pltpu.CompilerParams(dimension_semantics=sem, vmem_limit_bytes=V7X_VMEM_LIMIT_BYTES)


def _sigmoid(z):
    return 1.0 / (1.0 + jnp.exp(-z))


def _normmod(x, nw, shift, scale):
    ms = jnp.mean(x * x, axis=-1, keepdims=True)
    return (x * lax.rsqrt(ms + RMS_EPS) * nw) * (1.0 + scale) + shift


def _dot(a, b):
    return jnp.dot(a.astype(BF16), b.astype(BF16), preferred_element_type=F32)


def _dot_nt(a, b):
    return lax.dot_general(a.astype(BF16), b.astype(BF16), (((1,), (1,)), ((), ())), preferred_element_type=F32)


def _dot_tn(a, b):
    return lax.dot_general(a.astype(BF16), b.astype(BF16), (((0,), (0,)), ((), ())), preferred_element_type=F32)


def _split_bf16(x, parts):
    out = []
    for _ in range(parts):
        p = x.astype(BF16)
        out.append(p)
        x = x - p.astype(F32)
    return out


def _mod_kernel(c_ref, w_ref, b_ref, o_ref):
    c = c_ref[...]
    ca = c * _sigmoid(c)
    o_ref[0] = _dot(ca, w_ref[0]) + b_ref[0]


def _ada_mod(c, ada_w, ada_b):
    depth, d, n = ada_w.shape
    b = c.shape[0]
    rows = 8
    tn = 1536
    assert b <= rows and n % tn == 0
    c_pad = jnp.zeros((rows, d), F32).at[:b].set(c)
    out = pl.pallas_call(
        _mod_kernel,
        out_shape=jax.ShapeDtypeStruct((depth, rows, n), F32),
        grid=(depth, n // tn),
        in_specs=[
            pl.BlockSpec((rows, d), lambda i, j: (0, 0)),
            pl.BlockSpec((1, d, tn), lambda i, j: (i, 0, j)),
            pl.BlockSpec((1, 1, tn), lambda i, j: (i, 0, j)),
        ],
        out_specs=pl.BlockSpec((1, rows, tn), lambda i, j: (i, 0, j)),
        compiler_params=_params("parallel", "parallel"),
        name="ada_mod",
    )(c_pad, ada_w, ada_b.reshape(depth, 1, n))
    return out[:, :b]


def _ffn_kernel(nj, x_ref, nw_ref, sh_ref, sc_ref, g_ref, wg_ref, wu_ref, wd_ref, o_ref, h_scr, acc_scr):
    j = pl.program_id(1)

    @pl.when(j == 0)
    def _():
        h_scr[...] = _normmod(x_ref[...], nw_ref[...], sh_ref[0], sc_ref[0]).astype(BF16)
        acc_scr[...] = jnp.zeros_like(acc_scr)

    h = h_scr[...]
    g = jnp.dot(h, wg_ref[...], preferred_element_type=F32)
    u = jnp.dot(h, wu_ref[...], preferred_element_type=F32)
    a = g * _sigmoid(g) * u
    acc_scr[...] += jnp.dot(a.astype(BF16), wd_ref[...], preferred_element_type=F32)

    @pl.when(j == nj - 1)
    def _():
        o_ref[...] = x_ref[...] + g_ref[0] * acc_scr[...]


def _ffn(x2, nw, shift, scale, gate, wg, wu, wd, seq):
    t, d = x2.shape
    f = wg.shape[1]
    tm = ROW_TILE
    fc = f // 2
    assert t % tm == 0 and seq % tm == 0 and fc % V7X_LANES == 0
    tpb = seq // tm
    nj = f // fc
    mod_spec = pl.BlockSpec((1, 1, d), lambda i, j: (i // tpb, 0, 0))
    return pl.pallas_call(
        functools.partial(_ffn_kernel, nj),
        out_shape=jax.ShapeDtypeStruct((t, d), F32),
        grid=(t // tm, nj),
        in_specs=[
            pl.BlockSpec((tm, d), lambda i, j: (i, 0)),
            pl.BlockSpec((1, d), lambda i, j: (0, 0)),
            mod_spec, mod_spec, mod_spec,
            pl.BlockSpec((d, fc), lambda i, j: (0, j)),
            pl.BlockSpec((d, fc), lambda i, j: (0, j)),
            pl.BlockSpec((fc, d), lambda i, j: (j, 0)),
        ],
        out_specs=pl.BlockSpec((tm, d), lambda i, j: (i, 0)),
        scratch_shapes=[pltpu.VMEM((tm, d), BF16), pltpu.VMEM((tm, d), F32)],
        compiler_params=_params("parallel", "arbitrary"),
        name="ffn",
    )(x2, nw.reshape(1, d), shift[:, None], scale[:, None], gate[:, None],
      wg.astype(BF16), wu.astype(BF16), wd.astype(BF16))


def _outproj_kernel(x_ref, z_ref, g_ref, w_ref, o_ref):
    o_ref[...] = x_ref[...] + g_ref[0] * _dot(z_ref[...], w_ref[...])


def _outproj(x2, z2, gate, w, seq):
    t, d = x2.shape
    k = z2.shape[1]
    tm = ROW_TILE
    tpb = seq // tm
    return pl.pallas_call(
        _outproj_kernel,
        out_shape=jax.ShapeDtypeStruct((t, d), F32),
        grid=(t // tm,),
        in_specs=[
            pl.BlockSpec((tm, d), lambda i: (i, 0)),
            pl.BlockSpec((tm, k), lambda i: (i, 0)),
            pl.BlockSpec((1, 1, d), lambda i: (i // tpb, 0, 0)),
            pl.BlockSpec((k, d), lambda i: (0, 0)),
        ],
        out_specs=pl.BlockSpec((tm, d), lambda i: (i, 0)),
        compiler_params=_params("parallel"),
        name="outproj",
    )(x2, z2, gate[:, None], w.astype(BF16))


def _pool_kernel(tm, gd, x_ref, nw_ref, sh_ref, sc_ref, g_ref, pw_ref, pb_ref, ps_ref, o_ref, ext_scr):
    s = pl.program_id(1)
    x = x_ref[0]
    h = _normmod(x, nw_ref[...], sh_ref[0], sc_ref[0])

    @pl.when(s == 0)
    def _():
        ext_scr[0:POOL_HALO, :] = jnp.zeros((POOL_HALO, x.shape[1]), F32)

    ext_scr[POOL_HALO:POOL_HALO + tm, :] = h
    row = s * tm + lax.broadcasted_iota(jnp.int32, (tm, 1), 0)
    ys = []
    for gi, win in enumerate(POOL_WINDOWS):
        lanes = slice(gi * gd, (gi + 1) * gd)
        hg = h[:, lanes]
        acc = hg
        for k in range(1, win):
            acc = acc + ext_scr[POOL_HALO - k:POOL_HALO - k + tm, lanes]
        cnt = jnp.minimum(row + 1, win).astype(F32)
        ys.append(_dot(acc / cnt - hg, pw_ref[gi]))
    y = (jnp.concatenate(ys, axis=-1) + pb_ref[...]) * ps_ref[...]
    o_ref[0] = x + g_ref[0] * y
    ext_scr[0:POOL_HALO, :] = ext_scr[tm:tm + POOL_HALO, :]


def _pool_layer(x, nw, shift, scale, gate, pw, pb, ps):
    b, s, d = x.shape
    tm = ROW_TILE
    gd = d // len(POOL_WINDOWS)
    assert s % tm == 0 and max(POOL_WINDOWS) <= POOL_HALO
    mod_spec = pl.BlockSpec((1, 1, d), lambda i, j: (i, 0, 0))
    vec_spec = pl.BlockSpec((1, d), lambda i, j: (0, 0))
    return pl.pallas_call(
        functools.partial(_pool_kernel, tm, gd),
        out_shape=jax.ShapeDtypeStruct((b, s, d), F32),
        grid=(b, s // tm),
        in_specs=[
            pl.BlockSpec((1, tm, d), lambda i, j: (i, j, 0)),
            vec_spec, mod_spec, mod_spec, mod_spec,
            pl.BlockSpec(pw.shape, lambda i, j: (0, 0, 0)),
            vec_spec, vec_spec,
        ],
        out_specs=pl.BlockSpec((1, tm, d), lambda i, j: (i, j, 0)),
        scratch_shapes=[pltpu.VMEM((tm + POOL_HALO, d), F32)],
        compiler_params=_params("arbitrary", "arbitrary"),
        name="pool_mixer",
    )(x, nw.reshape(1, d), shift[:, None], scale[:, None], gate[:, None],
      pw.astype(BF16), pb.reshape(1, d), ps.reshape(1, d))


def _rwkv_proj_kernel(tm, x_ref, nw_ref, sh_ref, sc_ref, mu_ref, wr_ref, wk_ref, wv_ref,
                      w0_ref, wda_ref, wdb_ref, a0_ref, waa_ref, wab_ref, wga_ref, wgb_ref,
                      r_ref, k_ref, v_ref, lw_ref, a_ref, g_ref, ext_scr):
    s = pl.program_id(1)
    h = _normmod(x_ref[0], nw_ref[...], sh_ref[0], sc_ref[0])

    @pl.when(s == 0)
    def _():
        ext_scr[0:8, :] = jnp.zeros((8, h.shape[1]), F32)

    ext_scr[8:8 + tm, :] = h
    xx = ext_scr[7:7 + tm, :] - h

    def mix(i):
        return h + xx * mu_ref[i:i + 1, :]

    r_ref[0] = _dot(mix(0), wr_ref[...])
    k_ref[0] = _dot(mix(2), wk_ref[...])
    v_ref[0] = _dot(mix(3), wv_ref[...])
    dw = w0_ref[...] + _dot(jnp.tanh(_dot(mix(1), wda_ref[...])), wdb_ref[...])
    softplus_neg = jnp.maximum(-dw, 0.0) + jnp.log(1.0 + jnp.exp(-jnp.abs(dw)))
    lw_ref[0] = -jnp.exp(-softplus_neg - 0.5)
    a_ref[0] = _sigmoid(a0_ref[...] + _dot(_dot(mix(4), waa_ref[...]), wab_ref[...]))
    g_ref[0] = _dot(_sigmoid(_dot(mix(5), wga_ref[...])), wgb_ref[...])
    ext_scr[0:8, :] = ext_scr[tm:tm + 8, :]


def _rwkv_scan_kernel(ct, r_ref, k_ref, v_ref, lw_ref, a_ref, g_ref, kk_ref, ka_ref, rk_ref, lnw_ref, lnb_ref,
                      z_ref, s_scr):
    c, n = RWKV_CHUNK, RWKV_HEAD_DIM
    w = 2 * n
    shift = int(math.log2(n))

    @pl.when(pl.program_id(2) == 0)
    def _():
        s_scr[...] = jnp.zeros_like(s_scr)

    r, k, v, lw, a = r_ref[0], k_ref[0], v_ref[0], lw_ref[0], a_ref[0]
    head0 = lax.broadcasted_iota(jnp.int32, (1, w), 1) < n
    ri = lax.broadcasted_iota(jnp.int32, (w, w), 0)
    ci = lax.broadcasted_iota(jnp.int32, (w, w), 1)
    same_head = jnp.where((ri >> shift) == (ci >> shift), 1.0, 0.0).astype(BF16)
    strict, incl, eye = ri > ci, ri >= ci, ri == ci
    rt = lax.broadcasted_iota(jnp.int32, (ct, ct), 0)
    cc = lax.broadcasted_iota(jnp.int32, (ct, ct), 1)
    cshift = int(math.log2(c))
    chunk_tri = jnp.where(((rt >> cshift) == (cc >> cshift)) & (rt >= cc), 1.0, 0.0).astype(BF16)

    def head_sum(xf):
        return sum(jnp.dot(p, same_head, preferred_element_type=F32) for p in _split_bf16(xf, 2))

    kk = k * kk_ref[...]
    kkn = kk * lax.rsqrt(jnp.maximum(head_sum(kk * kk), 1e-24))
    k2 = k * (1.0 + (a - 1.0) * ka_ref[...])
    beta = kkn * a
    cum = sum(jnp.dot(chunk_tri, p, preferred_element_type=F32) for p in _split_bf16(lw, 3))
    at_all = -kkn * jnp.exp(cum - lw)
    rt_all = r * jnp.exp(cum)
    einv = jnp.exp(-cum)
    kb_all = beta * einv
    kq_all = k2 * einv

    nb = ct // c

    def stack(xf):
        x3 = xf.reshape(nb, c, w)
        return jnp.concatenate([jnp.where(head0, x3, 0.0), jnp.where(head0, 0.0, x3)], axis=1)

    def bmm(x, y):
        return jnp.einsum('bij,bjk->bik', x.astype(BF16), y.astype(BF16), preferred_element_type=F32)

    def bmm_nt(x, y):
        return jnp.einsum('bik,bjk->bij', x.astype(BF16), y.astype(BF16), preferred_element_type=F32)

    at, rt_, kb, kq, vs = stack(at_all), stack(rt_all), stack(kb_all), stack(kq_all), stack(v)
    pc = jnp.exp(cum.reshape(nb, c, w)[:, c - 1:c, :])
    kbe, kqe = kb * pc, kq * pc
    tt = bmm_nt(jnp.concatenate([at, rt_], axis=1), jnp.concatenate([kb, kq], axis=1))
    a_m = jnp.where(strict, tt[:, :w, :w], 0.0)
    b_m = jnp.where(strict, tt[:, :w, w:], 0.0)
    ar_m = jnp.where(incl, tt[:, w:, :w], 0.0)
    br_m = jnp.where(incl, tt[:, w:, w:], 0.0)
    rsum = jnp.where(eye, 1.0, 0.0) + a_m
    pw = bmm(a_m, a_m)
    for _ in range(cshift - 2):
        both = bmm(pw, jnp.concatenate([rsum, pw], axis=2))
        rsum = rsum + both[:, :, :w]
        pw = both[:, :, w:]
    tinv = rsum + bmm(pw, rsum)
    w12 = bmm(tinv, jnp.concatenate([at, bmm(b_m, vs)], axis=2))
    xmat = jnp.concatenate([w12, jnp.concatenate([jnp.zeros_like(vs), vs], axis=2)], axis=1)
    kbe_t = jnp.stack([kbe[i].T for i in range(nb)])
    kqe_t = jnp.stack([kqe[i].T for i in range(nb)])
    lhs = jnp.concatenate([jnp.concatenate([ar_m, br_m], axis=2), jnp.concatenate([kbe_t, kqe_t], axis=2)], axis=1)
    res = bmm(lhs, xmat)
    g_m = rt_ + res[:, :w, :w]
    y0 = res[:, :w, w:]
    mt = jnp.where(eye, jnp.broadcast_to(pc, (nb, w, w)), 0.0) + res[:, w:, :w]
    nt = res[:, w:, w:]

    state_t = s_scr[...]
    ys = []
    for idx in range(nb):
        step = _dot(jnp.concatenate([g_m[idx], mt[idx]], axis=0), state_t)
        yst = step[:w] + y0[idx]
        state_t = step[w:] + nt[idx]
        ys.append(yst[:c] + yst[c:])
    s_scr[...] = state_t
    y = jnp.concatenate(ys, axis=0)

    mean = head_sum(y) * (1.0 / n)
    dlt = y - mean
    var = head_sum(dlt * dlt) * (1.0 / n)
    yn = dlt * lax.rsqrt(var + RWKV_GN_EPS) * lnw_ref[...] + lnb_ref[...]
    bonus = head_sum(r * k2 * rk_ref[...]) * v
    z_ref[0] = ((yn + bonus) * g_ref[0]).astype(z_ref.dtype)


def _rwkv_layer(x, nw, shift, scale, gate, mu, w_rkv, w0, wd_a, wd_b, a0, wa_a, wa_b, wg_a, wg_b,
                k_k, k_a, r_k, ln_w, ln_b, w_out):
    b, s, d = x.shape
    tm = ROW_TILE
    assert s % tm == 0
    mod_spec = pl.BlockSpec((1, 1, d), lambda i, j: (i, 0, 0))
    vec_spec = pl.BlockSpec((1, d), lambda i, j: (0, 0))
    tok_spec = pl.BlockSpec((1, tm, d), lambda i, j: (i, j, 0))

    def full(arr):
        return pl.BlockSpec(arr.shape, lambda i, j: (0,) * arr.ndim)

    weights = [w_rkv[0].astype(BF16), w_rkv[1].astype(BF16), w_rkv[2].astype(BF16),
               w0.reshape(1, d), wd_a.astype(BF16), wd_b.astype(BF16),
               a0.reshape(1, d), wa_a.astype(BF16), wa_b.astype(BF16),
               wg_a.astype(BF16), wg_b.astype(BF16)]
    tok_shape = jax.ShapeDtypeStruct((b, s, d), F32)
    r, k, v, lw, a, g = pl.pallas_call(
        functools.partial(_rwkv_proj_kernel, tm),
        out_shape=[tok_shape] * 6,
        grid=(b, s // tm),
        in_specs=[tok_spec, vec_spec, mod_spec, mod_spec, full(mu)] + [full(wt) for wt in weights],
        out_specs=[tok_spec] * 6,
        scratch_shapes=[pltpu.VMEM((tm + 8, d), F32)],
        compiler_params=_params("arbitrary", "arbitrary"),
        name="rwkv_proj",
    )(x, nw.reshape(1, d), shift[:, None], scale[:, None], mu, *weights)

    ct = 512
    pair = 2 * RWKV_HEAD_DIM
    assert s % ct == 0 and d % pair == 0 and pair == V7X_LANES
    seq_spec = pl.BlockSpec((1, ct, pair), lambda i, p, j: (i, j, p))
    par_spec = pl.BlockSpec((1, pair), lambda i, p, j: (0, p))
    z = pl.pallas_call(
        functools.partial(_rwkv_scan_kernel, ct),
        out_shape=jax.ShapeDtypeStruct((b, s, d), BF16),
        grid=(b, d // pair, s // ct),
        in_specs=[seq_spec] * 6 + [par_spec] * 5,
        out_specs=seq_spec,
        scratch_shapes=[pltpu.VMEM((pair, pair), F32)],
        compiler_params=_params("parallel", "parallel", "arbitrary"),
        name="rwkv_scan",
    )(r, k, v, lw, a, g, k_k.reshape(1, d), k_a.reshape(1, d), r_k.reshape(1, d), ln_w.reshape(1, d), ln_b.reshape(1, d))
    return _outproj(x.reshape(b * s, d), z.reshape(b * s, d), gate, w_out, s).reshape(b, s, d)


NSA_Q_TILE = 128
NSA_KV_TILE = 512
NSA_PROJ_PAD = 128
NSA_QK_WIDTH = 128


def _head_rms(xf, p_ref, pt_ref, wvec):
    sums = sum(jnp.dot(part, p_ref[...], preferred_element_type=F32) for part in _split_bf16(xf * xf, 2))
    inv = lax.rsqrt(sums * (1.0 / NSA_HEAD_DIM) + RMS_EPS)
    inv_full = sum(jnp.dot(part, pt_ref[...], preferred_element_type=F32) for part in _split_bf16(inv, 3))
    return xf * inv_full * wvec


def _pos_lanes(pos):
    lane = lax.broadcasted_iota(jnp.int32, (1, NSA_QK_WIDTH), 1)
    hi = ((pos >> 6) << 6).astype(F32)
    lo = (pos & 63).astype(F32)
    d = NSA_HEAD_DIM
    return jnp.where((lane >= d) & (lane < d + 3), hi, jnp.where((lane >= d + 3) & (lane < d + 6), lo, 0.0))


def _nsa_proj_kernel(tm, qd, kd, x_ref, nw_ref, sh_ref, sc_ref, w_ref, pq_ref, pqt_ref, pk_ref, pkt_ref,
                     nq_ref, nks_ref, nkw_ref, qf_ref,
                     q_ref, ks_ref, vs_ref, kw_ref, vw_ref, kc_ref, vc_ref, gt_ref):
    hg, dk = NSA_HEADS_PER_GROUP, NSA_HEAD_DIM
    h = _normmod(x_ref[0], nw_ref[...], sh_ref[0], sc_ref[0])
    proj = _dot(h, w_ref[...])
    low = lax.broadcasted_iota(jnp.int32, (1, NSA_QK_WIDTH), 1) < dk
    key_lanes = _pos_lanes(pl.program_id(1) * tm + lax.broadcasted_iota(jnp.int32, (tm, 1), 0))

    def heads(x, nheads):
        for pair in range(nheads // 2):
            xp = x[:, pair * 2 * dk:(pair + 1) * 2 * dk]
            yield 2 * pair, xp
            yield 2 * pair + 1, pltpu.roll(xp, dk, 1)

    qn = _head_rms(proj[:, :qd], pq_ref, pqt_ref, nq_ref[...]) * NSA_SCALE
    for hd, xs in heads(qn, NSA_HEADS):
        q_ref[0, hd // hg, hd % hg] = jnp.where(low, xs, qf_ref[hd:hd + 1, :]).astype(BF16)
    o = qd
    kc_ref[0] = proj[:, o:o + kd]
    vc_ref[0] = proj[:, o + kd:o + 2 * kd]
    for g, xs in heads(_head_rms(proj[:, o + 2 * kd:o + 3 * kd], pk_ref, pkt_ref, nks_ref[...]), NSA_KV_GROUPS):
        ks_ref[0, g] = jnp.where(low, xs, key_lanes).astype(BF16)
    for g, xs in heads(proj[:, o + 3 * kd:o + 4 * kd], NSA_KV_GROUPS):
        vs_ref[0, g] = xs.astype(BF16)
    for g, xs in heads(_head_rms(proj[:, o + 4 * kd:o + 5 * kd], pk_ref, pkt_ref, nkw_ref[...]), NSA_KV_GROUPS):
        kw_ref[0, g] = jnp.where(low, xs, key_lanes).astype(BF16)
    for g, xs in heads(proj[:, o + 5 * kd:o + 6 * kd], NSA_KV_GROUPS):
        vw_ref[0, g] = xs.astype(BF16)
    o += 6 * kd
    for g in range(NSA_KV_GROUPS):
        gt_ref[0, g] = _sigmoid(proj[:, o + g * NSA_PROJ_PAD:o + (g + 1) * NSA_PROJ_PAD])


def _gelu_tanh(x):
    return 0.5 * x * (1.0 + jnp.tanh(math.sqrt(2.0 / math.pi) * (x + 0.044715 * (x * x * x))))


def _nsa_compress_kernel(half, kc_ref, vc_ref, pos_ref, w1_ref, w2_ref, nk_ref, kco_ref, vco_ref):
    nsub = kc_ref.shape[2]

    def comp(x, i):
        ya = _dot(x + pos_ref[2 * i:2 * i + 1, :], w1_ref[i, 0:half, :])
        yb = _dot(x + pos_ref[2 * i + 1:2 * i + 2, :], w1_ref[i, half:2 * half, :])
        hid = ya + pltpu.roll(yb, nsub - 1, 0)
        return _dot(_gelu_tanh(hid), w2_ref[i])

    kcm = comp(kc_ref[0, 0], 0)
    ms = jnp.sum(kcm * kcm, axis=-1, keepdims=True) * (1.0 / NSA_HEAD_DIM)
    block_end = lax.broadcasted_iota(jnp.int32, (nsub, 1), 0) * CMP_STRIDE + CMP_LEN - 1
    kco_ref[0, 0] = (kcm * lax.rsqrt(ms + RMS_EPS) * nk_ref[...] + _pos_lanes(block_end)).astype(BF16)
    vco_ref[0, 0] = comp(vc_ref[0, 0], 1).astype(BF16)


def _pack_heads(per_head):
    low = lax.broadcasted_iota(jnp.int32, (1, NSA_QK_WIDTH), 1) < NSA_HEAD_DIM
    pairs = [jnp.where(low, per_head[i], pltpu.roll(per_head[i + 1], NSA_HEAD_DIM, 1))
             for i in range(0, len(per_head), 2)]
    return jnp.concatenate(pairs, axis=-1)


def _nsa_cmp_kernel(qt, nc, nsel, q_ref, kc_ref, vc_ref, g_ref, oc_ref, sel_ref):
    hg = NSA_HEADS_PER_GROUP
    q0 = pl.program_id(2) * qt
    q = q_ref[0, 0].reshape(hg * qt, NSA_QK_WIDTH)
    s3 = _dot_nt(q, kc_ref[0, 0]).reshape(hg, qt, nc)
    t = q0 + lax.broadcasted_iota(jnp.int32, (qt, nc), 0)
    n = lax.broadcasted_iota(jnp.int32, (qt, nc), 1)
    mask = (t >= n * CMP_STRIDE + CMP_LEN - 1)[None]
    sm = jnp.where(mask, s3, MASKED_SCORE)
    m = jnp.max(sm, axis=-1, keepdims=True)
    e = jnp.where(mask, jnp.exp(sm - m), 0.0)
    p = e / jnp.maximum(jnp.sum(e, axis=-1, keepdims=True), 1e-30)
    oc = _dot(p.reshape(hg * qt, nc), vc_ref[0, 0])
    g = g_ref[0, 0]
    oc_ref[0] = _pack_heads([oc[hd * qt:(hd + 1) * qt] * g[:, hd:hd + 1] for hd in range(hg)])

    psum = p[0] + p[1] + p[2] + p[3]
    jo = lax.broadcasted_iota(jnp.int32, (nsel, nc), 0)
    no = lax.broadcasted_iota(jnp.int32, (nsel, nc), 1)
    ratio = SEL_LEN // CMP_STRIDE
    first = jo * ratio - (CMP_LEN // CMP_STRIDE - 1)
    overlap_t = jnp.where((no >= first) & (no < (jo + 1) * ratio) & (no < nc - 1), 1.0, 0.0).astype(BF16)
    imp_t = sum(lax.dot_general(overlap_t, part, (((1,), (1,)), ((), ())), preferred_element_type=F32)
                for part in _split_bf16(psum, 3))

    j = lax.broadcasted_iota(jnp.int32, (nsel, qt), 0)
    tt = q0 + lax.broadcasted_iota(jnp.int32, (nsel, qt), 1)
    valid = j * SEL_LEN <= tt
    qb = tt >> int(math.log2(SEL_LEN))
    forced = (j == 0) | (j == qb) | (j == qb - 1)
    score = jnp.where(valid, jnp.where(forced, SEL_BIG, imp_t), -SEL_BIG)
    jf = j.astype(F32)
    sel = jnp.zeros((nsel, qt), F32)
    for _ in range(min(SEL_TOPK, nsel)):
        mx = jnp.max(score, axis=0, keepdims=True)
        jmin = jnp.min(jnp.where(score == mx, jf, float(nsel)), axis=0, keepdims=True)
        pick = jf == jmin
        sel = jnp.where(pick, 1.0, sel)
        score = jnp.where(pick, -3e38, score)
    sel = jnp.where(valid, sel, 0.0)
    sel_ref[0, 0] = sel.T.astype(BF16)


def _nsa_attn_kernel(qt, kt, nsel, nkt, nq, tiles_ref, cnt_ref,
                     q_ref, ks_ref, vs_ref, kw_ref, vw_ref, sel_ref, g_ref, oc_ref, o_ref):
    hg, dk = NSA_HEADS_PER_GROUP, NSA_HEAD_DIM
    rows = hg * qt
    step = (pl.program_id(0) * NSA_KV_GROUPS + pl.program_id(1)) * nq + pl.program_id(2)
    q0 = pl.program_id(2) * qt
    q = q_ref[0, 0].reshape(rows, NSA_QK_WIDTH)
    sel = sel_ref[0, 0]
    lsel = int(math.log2(SEL_LEN))

    t_k = q0 + lax.broadcasted_iota(jnp.int32, (qt, kt), 0)
    c_k = lax.broadcasted_iota(jnp.int32, (qt, kt), 1)
    blk_of_col = lax.broadcasted_iota(jnp.int32, (nsel, kt), 0) - (lax.broadcasted_iota(jnp.int32, (nsel, kt), 1) >> lsel)

    def sel_step(i, carry):
        m, l, acc = carry
        k0 = pl.multiple_of(tiles_ref[step * nkt + i] * kt, kt)
        s3 = _dot_nt(q, ks_ref[0, 0, pl.ds(k0, kt), :]).reshape(hg, qt, kt)
        expand = jnp.where(blk_of_col == (k0 >> lsel), 1.0, 0.0).astype(BF16)
        chosen = jnp.dot(sel, expand, preferred_element_type=F32)
        mask = ((chosen > 0.5) & (k0 + c_k <= t_k))[None]
        sm = jnp.where(mask, s3, MASKED_SCORE).reshape(rows, kt)
        m_new = jnp.maximum(m, jnp.max(sm, axis=-1, keepdims=True))
        alpha = jnp.exp(m - m_new)
        p = jnp.exp(sm - m_new)
        l = alpha * l + jnp.sum(p, axis=-1, keepdims=True)
        acc = alpha * acc + _dot(p, vs_ref[0, 0, pl.ds(k0, kt), :])
        return m_new, l, acc

    init = (jnp.full((rows, 1), MASKED_SCORE, F32), jnp.zeros((rows, 1), F32), jnp.zeros((rows, NSA_QK_WIDTH), F32))
    _, l_s, acc_s = lax.fori_loop(0, cnt_ref[step], sel_step, init)
    o_s = acc_s / jnp.maximum(l_s, 1e-30)

    wk = WINDOW + qt
    ws = pl.multiple_of(jnp.maximum(q0 - WINDOW, 0), qt)
    s3 = _dot_nt(q, kw_ref[0, 0, pl.ds(ws, wk), :]).reshape(hg, qt, wk)
    dist = (q0 + lax.broadcasted_iota(jnp.int32, (qt, wk), 0)) - (ws + lax.broadcasted_iota(jnp.int32, (qt, wk), 1))
    mask = ((dist >= 0) & (dist < WINDOW))[None]
    sm = jnp.where(mask, s3, MASKED_SCORE).reshape(rows, wk)
    e = jnp.exp(sm - jnp.max(sm, axis=-1, keepdims=True))
    o_w =_dot(e, vw_ref[0, 0, pl.ds(ws, wk), :]) / jnp.maximum(jnp.sum(e, axis=-1, keepdims=True), 1e-30)

    g = g_ref[0, 0]
    per_head = [g[:, hg + hd:hg + hd + 1] * o_s[hd * qt:(hd + 1) * qt]
                + g[:, 2 * hg + hd:2 * hg + hd + 1] * o_w[hd * qt:(hd + 1) * qt] for hd in range(hg)]
    o_ref[0] = (oc_ref[0] + _pack_heads(per_head)).astype(o_ref.dtype)


def _nsa_layer(x, nw, shift, scale, gate, w_in, cmp_pos, cmp_w1, cmp_w2, qk_norm, w_out):
    b, s, d = x.shape
    t = b * s
    g_, hg, dk = NSA_KV_GROUPS, NSA_HEADS_PER_GROUP, NSA_HEAD_DIM
    qd, kd = NSA_HEADS * dk, g_ * dk
    tm = ROW_TILE
    wl = NSA_QK_WIDTH
    n_main = qd + 6 * kd
    assert w_in.shape[1] == n_main + 3 * NSA_HEADS and s % tm == 0
    gate_cols = w_in[:, n_main:].reshape(d, 3, g_, hg).transpose(0, 2, 1, 3).reshape(d, g_, 3 * hg)
    gate_cols = jnp.pad(gate_cols, ((0, 0), (0, 0), (0, NSA_PROJ_PAD - 3 * hg))).reshape(d, g_ * NSA_PROJ_PAD)
    w_pad = jnp.concatenate([w_in[:, :n_main], gate_cols], axis=1).astype(BF16)

    def head_onehot(width):
        lane_head = jnp.arange(width)[:, None] // dk
        return (lane_head == jnp.arange(V7X_LANES)[None, :]).astype(BF16)

    pq, pk = head_onehot(qd), head_onehot(kd)

    def tile_w(wv, reps):
        return jnp.tile(wv, reps).reshape(1, reps * dk)

    slopes = jnp.exp2(-8.0 * (jnp.arange(NSA_HEADS, dtype=F32) + 1) / NSA_HEADS)
    slope_parts = jnp.stack([p.astype(F32) for p in _split_bf16(slopes, 3)], axis=-1)
    q_lanes = jnp.zeros((NSA_HEADS, wl), F32).at[:, dk:dk + 6].set(jnp.concatenate([slope_parts, slope_parts], axis=-1))

    consts = [pq, pq.T, pk, pk.T, tile_w(qk_norm[0], NSA_HEADS), tile_w(qk_norm[2], g_), tile_w(qk_norm[3], g_), q_lanes]
    full2 = lambda arr: pl.BlockSpec(arr.shape, lambda i, j: (0, 0))
    mod_spec = pl.BlockSpec((1, 1, d), lambda i, j: (i, 0, 0))
    grp = lambda dt: jax.ShapeDtypeStruct((b, g_, s, wl), dt)
    grp_spec = pl.BlockSpec((1, g_, tm, wl), lambda i, j: (i, 0, j, 0))
    tokf = jax.ShapeDtypeStruct((b, s, kd), F32)
    tokf_spec = pl.BlockSpec((1, tm, kd), lambda i, j: (i, j, 0))
    q5, ks4, vs4, kw4, vw4, kc, vc, gates = pl.pallas_call(
        functools.partial(_nsa_proj_kernel, tm, qd, kd),
        out_shape=[jax.ShapeDtypeStruct((b, g_, hg, s, wl), BF16), grp(BF16), grp(BF16), grp(BF16), grp(BF16),
                   tokf, tokf, grp(F32)],
        grid=(b, s // tm),
        in_specs=[pl.BlockSpec((1, tm, d), lambda i, j: (i, j, 0)), pl.BlockSpec((1, d), lambda i, j: (0, 0)),
                  mod_spec, mod_spec, full2(w_pad)] + [full2(cst) for cst in consts],
        out_specs=[pl.BlockSpec((1, g_, hg, tm, wl), lambda i, j: (i, 0, 0, j, 0)),
                   grp_spec, grp_spec, grp_spec, grp_spec, tokf_spec, tokf_spec, grp_spec],
        compiler_params=_params("parallel", "parallel"),
        name="nsa_proj",
    )(x, nw.reshape(1, d), shift[:, None], scale[:, None], w_pad, *consts)

    nsub = s // CMP_STRIDE
    half = CMP_STRIDE * dk
    sub_rows = lambda arr: arr.reshape(b, s, g_, dk).transpose(0, 2, 1, 3).reshape(b, g_, nsub, half)
    kc3, vc3 = sub_rows(kc), sub_rows(vc)

    pos = jnp.stack([cmp_pos[0, :CMP_STRIDE].reshape(half), cmp_pos[0, CMP_STRIDE:].reshape(half),
                     cmp_pos[1, :CMP_STRIDE].reshape(half), cmp_pos[1, CMP_STRIDE:].reshape(half)])
    w1b = cmp_w1.astype(BF16)
    w2b = jnp.pad(cmp_w2, ((0, 0), (0, 0), (0, wl - dk))).astype(BF16)
    nk_lanes = jnp.pad(qk_norm[1], (0, wl - dk)).reshape(1, wl)
    cmp_in = pl.BlockSpec((1, 1, nsub, half), lambda i, j: (i, j, 0, 0))
    cmp_out = pl.BlockSpec((1, 1, nsub, wl), lambda i, j: (i, j, 0, 0))
    kcc, vcc = pl.pallas_call(
        functools.partial(_nsa_compress_kernel, half),
        out_shape=[jax.ShapeDtypeStruct((b, g_, nsub, wl), BF16)] * 2,
        grid=(b, g_),
        in_specs=[cmp_in, cmp_in, pl.BlockSpec(pos.shape, lambda i, j: (0, 0)),
                  pl.BlockSpec(w1b.shape, lambda i, j: (0, 0, 0)), pl.BlockSpec(w2b.shape, lambda i, j: (0, 0, 0)),
                  pl.BlockSpec((1, wl), lambda i, j: (0, 0))],
        out_specs=[cmp_out, cmp_out],
        compiler_params=_params("parallel", "parallel"),
        name="nsa_compress",
    )(kc3, vc3, pos, w1b, w2b, nk_lanes)

    qt = NSA_Q_TILE
    nq = s // qt
    nsel = s // SEL_LEN
    kt = min(NSA_KV_TILE, s)
    nkt = s // kt
    assert s % qt == 0 and s % kt == 0 and s >= WINDOW + qt and kt % qt == 0

    q_spec = pl.BlockSpec((1, 1, hg, qt, wl), lambda i, j, k, *_: (i, j, 0, k, 0))
    o_spec = pl.BlockSpec((1, qt, hg * dk), lambda i, j, k, *_: (i, k, j))
    gate_spec = pl.BlockSpec((1, 1, qt, wl), lambda i, j, k, *_: (i, j, k, 0))
    sel_spec = pl.BlockSpec((1, 1, qt, nsel), lambda i, j, k, *_: (i, j, k, 0))
    seq_kv = lambda n: pl.BlockSpec((1, 1, n, wl), lambda i, j, k, *_: (i, j, 0, 0))
    oc, sel = pl.pallas_call(
        functools.partial(_nsa_cmp_kernel, qt, nsub, nsel),
        out_shape=[jax.ShapeDtypeStruct((b, s, qd), F32), jax.ShapeDtypeStruct((b, g_, s, nsel), BF16)],
        grid=(b, g_, nq),
        in_specs=[q_spec, seq_kv(nsub), seq_kv(nsub), gate_spec],
        out_specs=[o_spec, sel_spec],
        compiler_params=_params("parallel", "parallel", "parallel"),
        name="nsa_cmp_select",
    )(q5, kcc, vcc, gates)

    active = sel.reshape(b, g_, nq, qt, nkt, kt // SEL_LEN).max(axis=(3, 5)) > 0
    order = jnp.sort(jnp.where(active, 0, nkt) + jnp.arange(nkt, dtype=jnp.int32), axis=-1)
    tiles = (order % nkt).astype(jnp.int32).reshape(-1)
    counts = active.sum(axis=-1).astype(jnp.int32).reshape(-1)

    o3 = pl.pallas_call(
        functools.partial(_nsa_attn_kernel, qt, kt, nsel, nkt, nq),
        out_shape=jax.ShapeDtypeStruct((b, s, qd), BF16),
        grid_spec=pltpu.PrefetchScalarGridSpec(
            num_scalar_prefetch=2,
            grid=(b, g_, nq),
            in_specs=[q_spec, seq_kv(s), seq_kv(s), seq_kv(s), seq_kv(s), sel_spec, gate_spec, o_spec],
            out_specs=o_spec,
        ),
        compiler_params=_params("parallel", "parallel", "arbitrary"),
        name="nsa_attention",
    )(tiles, counts, q5, ks4, vs4, kw4, vw4, sel, gates, oc)
    return _outproj(x.reshape(t, d), o3.reshape(t, qd), gate, w_out, s).reshape(b, s, d)


def kernel(x, c, ada_w, ada_b, norm_mix_w, norm_ffn_w, ffn_w_gate, ffn_w_up, ffn_w_down, nsa_w_in, nsa_cmp_pos, nsa_cmp_w1, nsa_cmp_w2, nsa_qk_norm, nsa_w_out, rwkv_mu, rwkv_w_rkv, rwkv_w0, rwkv_wd_a, rwkv_wd_b, rwkv_a0, rwkv_wa_a, rwkv_wa_b, rwkv_wg_a, rwkv_wg_b, rwkv_k_k, rwkv_k_a, rwkv_r_k, rwkv_ln_w, rwkv_ln_b, rwkv_w_out, pool_w, pool_b, pool_scale):
    b, s, d = x.shape
    depth = ada_w.shape[0]
    mod = _ada_mod(c, ada_w, ada_b)
    for i in range(depth):
        sh_m, sc_m, g_m, sh_f, sc_f, g_f = jnp.split(mod[i], N_MOD, axis=-1)
        kind, j = i % N_MIXERS, i // N_MIXERS
        if kind == 0:
            x = _nsa_layer(x, norm_mix_w[i], sh_m, sc_m, g_m, nsa_w_in[j], nsa_cmp_pos[j], nsa_cmp_w1[j], nsa_cmp_w2[j],
                           nsa_qk_norm[j], nsa_w_out[j])
        elif kind == 1:
            x = _rwkv_layer(x, norm_mix_w[i], sh_m, sc_m, g_m, rwkv_mu[j], rwkv_w_rkv[j], rwkv_w0[j], rwkv_wd_a[j],
                            rwkv_wd_b[j], rwkv_a0[j], rwkv_wa_a[j], rwkv_wa_b[j], rwkv_wg_a[j], rwkv_wg_b[j],
                            rwkv_k_k[j], rwkv_k_a[j], rwkv_r_k[j], rwkv_ln_w[j], rwkv_ln_b[j], rwkv_w_out[j])
        elif kind == 2:
            x = _pool_layer(x, norm_mix_w[i], sh_m, sc_m, g_m, pool_w[j], pool_b[j], pool_scale[j])
        x = _ffn(x.reshape(b * s, d), norm_ffn_w[i], sh_f, sc_f, g_f, ffn_w_gate[i], ffn_w_up[i], ffn_w_down[i], s).reshape(b, s, d)
    return x
```

```python
import functools
import math

import jax
import jax.numpy as jnp
from jax import lax
from jax.experimental import pallas as pl
from jax.experimental.pallas import tpu as pltpu

F32 = jnp.float32
BF16 = jnp.bfloat16

V7X_LANES = 128
V7X_VMEM_LIMIT_BYTES = 56 * 1024 * 1024

RMS_EPS = 1e-6
N_MOD = 6
N_MIXERS = 3

NSA_HEAD_DIM = 64
NSA_KV_GROUPS = 4
NSA_HEADS_PER_GROUP = 4
NSA_HEADS = NSA_KV_GROUPS * NSA_HEADS_PER_GROUP
CMP_LEN = 32
CMP_STRIDE = 16
SEL_LEN = 64
SEL_TOPK = 16
WINDOW = 512
NSA_SCALE = NSA_HEAD_DIM ** -0.5
LOG2_E = math.log2(math.e)
SEL_BIG = 1e9
MASKED_SCORE = -1e30

RWKV_HEAD_DIM = 64
RWKV_GN_EPS = 64e-5
RWKV_CHUNK = 64

POOL_WINDOWS = (2, 4, 8, 16)
POOL_HALO = 16

ROW_TILE = 512


def _params(*sem):
    return pltpu.CompilerParams(dimension_semantics=sem, vmem_limit_bytes=V7X_VMEM_LIMIT_BYTES)


def _sigmoid(z):
    return 1.0 / (1.0 + jnp.exp(-z))


def _normmod(x, nw, shift, scale):
    ms = jnp.mean(x * x, axis=-1, keepdims=True)
    return (x * lax.rsqrt(ms + RMS_EPS) * nw) * (1.0 + scale) + shift


def _dot(a, b):
    return jnp.dot(a.astype(BF16), b.astype(BF16), preferred_element_type=F32)


def _dot_nt(a, b):
    return lax.dot_general(a.astype(BF16), b.astype(BF16), (((1,), (1,)), ((), ())), preferred_element_type=F32)


def _dot_tn(a, b):
    return lax.dot_general(a.astype(BF16), b.astype(BF16), (((0,), (0,)), ((), ())), preferred_element_type=F32)


def _split_bf16(x, parts):
    out = []
    for _ in range(parts):
        p = x.astype(BF16)
        out.append(p)
        x = x - p.astype(F32)
    return out


def _mod_kernel(c_ref, w_ref, b_ref, o_ref):
    c = c_ref[...]
    ca = c * _sigmoid(c)
    o_ref[0] = _dot(ca, w_ref[0]) + b_ref[0]


def _ada_mod(c, ada_w, ada_b):
    depth, d, n = ada_w.shape
    b = c.shape[0]
    rows = 8
    tn = 1536
    assert b <= rows and n % tn == 0
    c_pad = jnp.zeros((rows, d), F32).at[:b].set(c)
    out = pl.pallas_call(
        _mod_kernel,
        out_shape=jax.ShapeDtypeStruct((depth, rows, n), F32),
        grid=(depth, n // tn),
        in_specs=[
            pl.BlockSpec((rows, d), lambda i, j: (0, 0)),
            pl.BlockSpec((1, d, tn), lambda i, j: (i, 0, j)),
            pl.BlockSpec((1, 1, tn), lambda i, j: (i, 0, j)),
        ],
        out_specs=pl.BlockSpec((1, rows, tn), lambda i, j: (i, 0, j)),
        compiler_params=_params("parallel", "parallel"),
        name="ada_mod",
    )(c_pad, ada_w, ada_b.reshape(depth, 1, n))
    return out[:, :b]


def _ffn_kernel(nj, x_ref, nw_ref, sh_ref, sc_ref, g_ref, wg_ref, wu_ref, wd_ref, o_ref, h_scr, acc_scr):
    j = pl.program_id(1)

    @pl.when(j == 0)
    def _():
        h_scr[...] = _normmod(x_ref[...], nw_ref[...], sh_ref[0], sc_ref[0]).astype(BF16)
        acc_scr[...] = jnp.zeros_like(acc_scr)

    h = h_scr[...]
    g = jnp.dot(h, wg_ref[...], preferred_element_type=F32)
    u = jnp.dot(h, wu_ref[...], preferred_element_type=F32)
    a = g * _sigmoid(g) * u
    acc_scr[...] += jnp.dot(a.astype(BF16), wd_ref[...], preferred_element_type=F32)

    @pl.when(j == nj - 1)
    def _():
        o_ref[...] = x_ref[...] + g_ref[0] * acc_scr[...]


def _ffn(x2, nw, shift, scale, gate, wg, wu, wd, seq):
    t, d = x2.shape
    f = wg.shape[1]
    tm = ROW_TILE
    fc = f // 2
    assert t % tm == 0 and seq % tm == 0 and fc % V7X_LANES == 0
    tpb = seq // tm
    nj = f // fc
    mod_spec = pl.BlockSpec((1, 1, d), lambda i, j: (i // tpb, 0, 0))
    return pl.pallas_call(
        functools.partial(_ffn_kernel, nj),
        out_shape=jax.ShapeDtypeStruct((t, d), F32),
        grid=(t // tm, nj),
        in_specs=[
            pl.BlockSpec((tm, d), lambda i, j: (i, 0)),
            pl.BlockSpec((1, d), lambda i, j: (0, 0)),
            mod_spec, mod_spec, mod_spec,
            pl.BlockSpec((d, fc), lambda i, j: (0, j)),
            pl.BlockSpec((d, fc), lambda i, j: (0, j)),
            pl.BlockSpec((fc, d), lambda i, j: (j, 0)),
        ],
        out_specs=pl.BlockSpec((tm, d), lambda i, j: (i, 0)),
        scratch_shapes=[pltpu.VMEM((tm, d), BF16), pltpu.VMEM((tm, d), F32)],
        compiler_params=_params("parallel", "arbitrary"),
        name="ffn",
    )(x2, nw.reshape(1, d), shift[:, None], scale[:, None], gate[:, None],
      wg.astype(BF16), wu.astype(BF16), wd.astype(BF16))


def _outproj_kernel(x_ref, z_ref, g_ref, w_ref, o_ref):
    o_ref[...] = x_ref[...] + g_ref[0] * _dot(z_ref[...], w_ref[...])


def _outproj(x2, z2, gate, w, seq):
    t, d = x2.shape
    k = z2.shape[1]
    tm = ROW_TILE
    tpb = seq // tm
    return pl.pallas_call(
        _outproj_kernel,
        out_shape=jax.ShapeDtypeStruct((t, d), F32),
        grid=(t // tm,),
        in_specs=[
            pl.BlockSpec((tm, d), lambda i: (i, 0)),
            pl.BlockSpec((tm, k), lambda i: (i, 0)),
            pl.BlockSpec((1, 1, d), lambda i: (i // tpb, 0, 0)),
            pl.BlockSpec((k, d), lambda i: (0, 0)),
        ],
        out_specs=pl.BlockSpec((tm, d), lambda i: (i, 0)),
        compiler_params=_params("parallel"),
        name="outproj",
    )(x2, z2, gate[:, None], w.astype(BF16))


def _pool_kernel(tm, gd, x_ref, nw_ref, sh_ref, sc_ref, g_ref, pw_ref, pb_ref, ps_ref, o_ref, ext_scr):
    s = pl.program_id(1)
    x = x_ref[0]
    h = _normmod(x, nw_ref[...], sh_ref[0], sc_ref[0])

    @pl.when(s == 0)
    def _():
        ext_scr[0:POOL_HALO, :] = jnp.zeros((POOL_HALO, x.shape[1]), F32)

    ext_scr[POOL_HALO:POOL_HALO + tm, :] = h
    row = s * tm + lax.broadcasted_iota(jnp.int32, (tm, 1), 0)
    ys = []
    for gi, win in enumerate(POOL_WINDOWS):
        lanes = slice(gi * gd, (gi + 1) * gd)
        hg = h[:, lanes]
        acc = hg
        for k in range(1, win):
            acc = acc + ext_scr[POOL_HALO - k:POOL_HALO - k + tm, lanes]
        cnt = jnp.minimum(row + 1, win).astype(F32)
        ys.append(_dot(acc / cnt - hg, pw_ref[gi]))
    y = (jnp.concatenate(ys, axis=-1) + pb_ref[...]) * ps_ref[...]
    o_ref[0] = x + g_ref[0] * y
    ext_scr[0:POOL_HALO, :] = ext_scr[tm:tm + POOL_HALO, :]


def _pool_layer(x, nw, shift, scale, gate, pw, pb, ps):
    b, s, d = x.shape
    tm = ROW_TILE
    gd = d // len(POOL_WINDOWS)
    assert s % tm == 0 and max(POOL_WINDOWS) <= POOL_HALO
    mod_spec = pl.BlockSpec((1, 1, d), lambda i, j: (i, 0, 0))
    vec_spec = pl.BlockSpec((1, d), lambda i, j: (0, 0))
    return pl.pallas_call(
        functools.partial(_pool_kernel, tm, gd),
        out_shape=jax.ShapeDtypeStruct((b, s, d), F32),
        grid=(b, s // tm),
        in_specs=[
            pl.BlockSpec((1, tm, d), lambda i, j: (i, j, 0)),
            vec_spec, mod_spec, mod_spec, mod_spec,
            pl.BlockSpec(pw.shape, lambda i, j: (0, 0, 0)),
            vec_spec, vec_spec,
        ],
        out_specs=pl.BlockSpec((1, tm, d), lambda i, j: (i, j, 0)),
        scratch_shapes=[pltpu.VMEM((tm + POOL_HALO, d), F32)],
        compiler_params=_params("arbitrary", "arbitrary"),
        name="pool_mixer",
    )(x, nw.reshape(1, d), shift[:, None], scale[:, None], gate[:, None],
      pw.astype(BF16), pb.reshape(1, d), ps.reshape(1, d))


def _rwkv_proj_kernel(tm, x_ref, nw_ref, sh_ref, sc_ref, mu_ref, wr_ref, wk_ref, wv_ref,
                      w0_ref, wda_ref, wdb_ref, a0_ref, waa_ref, wab_ref, wga_ref, wgb_ref,
                      r_ref, k_ref, v_ref, lw_ref, a_ref, g_ref, ext_scr):
    s = pl.program_id(1)
    h = _normmod(x_ref[0], nw_ref[...], sh_ref[0], sc_ref[0])

    @pl.when(s == 0)
    def _():
        ext_scr[0:8, :] = jnp.zeros((8, h.shape[1]), F32)

    ext_scr[8:8 + tm, :] = h
    xx = ext_scr[7:7 + tm, :] - h

    def mix(i):
        return h + xx * mu_ref[i:i + 1, :]

    r_ref[0] = _dot(mix(0), wr_ref[...])
    k_ref[0] = _dot(mix(2), wk_ref[...])
    v_ref[0] = _dot(mix(3), wv_ref[...])
    dw = w0_ref[...] + _dot(jnp.tanh(_dot(mix(1), wda_ref[...])), wdb_ref[...])
    softplus_neg = jnp.maximum(-dw, 0.0) + jnp.log(1.0 + jnp.exp(-jnp.abs(dw)))
    lw_ref[0] = -jnp.exp(-softplus_neg - 0.5)
    a_ref[0] = _sigmoid(a0_ref[...] + _dot(_dot(mix(4), waa_ref[...]), wab_ref[...]))
    g_ref[0] = _dot(_sigmoid(_dot(mix(5), wga_ref[...])), wgb_ref[...])
    ext_scr[0:8, :] = ext_scr[tm:tm + 8, :]


def _rwkv_scan_kernel(ct, r_ref, k_ref, v_ref, lw_ref, a_ref, g_ref, kk_ref, ka_ref, rk_ref, lnw_ref, lnb_ref,
                      z_ref, s_scr):
    c, n = RWKV_CHUNK, RWKV_HEAD_DIM
    w = 2 * n
    shift = int(math.log2(n))

    @pl.when(pl.program_id(2) == 0)
    def _():
        s_scr[...] = jnp.zeros_like(s_scr)

    r, k, v, lw, a = r_ref[0], k_ref[0], v_ref[0], lw_ref[0], a_ref[0]
    head0 = lax.broadcasted_iota(jnp.int32, (1, w), 1) < n
    ri = lax.broadcasted_iota(jnp.int32, (w, w), 0)
    ci = lax.broadcasted_iota(jnp.int32, (w, w), 1)
    same_head = jnp.where((ri >> shift) == (ci >> shift), 1.0, 0.0).astype(BF16)
    strict, incl, eye = ri > ci, ri >= ci, ri == ci
    rt = lax.broadcasted_iota(jnp.int32, (ct, ct), 0)
    cc = lax.broadcasted_iota(jnp.int32, (ct, ct), 1)
    cshift = int(math.log2(c))
    chunk_tri = jnp.where(((rt >> cshift) == (cc >> cshift)) & (rt >= cc), 1.0, 0.0).astype(BF16)

    def head_sum(xf):
        return sum(jnp.dot(p, same_head, preferred_element_type=F32) for p in _split_bf16(xf, 2))

    kk = k * kk_ref[...]
    kkn = kk * lax.rsqrt(jnp.maximum(head_sum(kk * kk), 1e-24))
    k2 = k * (1.0 + (a - 1.0) * ka_ref[...])
    beta = kkn * a
    cum = sum(jnp.dot(chunk_tri, p, preferred_element_type=F32) for p in _split_bf16(lw, 3))
    at_all = -kkn * jnp.exp(cum - lw)
    rt_all = r * jnp.exp(cum)
    einv = jnp.exp(-cum)
    kb_all = beta * einv
    kq_all = k2 * einv

    nb = ct // c

    def stack(xf):
        x3 = xf.reshape(nb, c, w)
        return jnp.concatenate([jnp.where(head0, x3, 0.0), jnp.where(head0, 0.0, x3)], axis=1)

    def bmm(x, y):
        return jnp.einsum('bij,bjk->bik', x.astype(BF16), y.astype(BF16), preferred_element_type=F32)

    def bmm_nt(x, y):
        return jnp.einsum('bik,bjk->bij', x.astype(BF16), y.astype(BF16), preferred_element_type=F32)

    at, rt_, kb, kq, vs = stack(at_all), stack(rt_all), stack(kb_all), stack(kq_all), stack(v)
    pc = jnp.exp(cum.reshape(nb, c, w)[:, c - 1:c, :])
    kbe, kqe = kb * pc, kq * pc
    tt = bmm_nt(jnp.concatenate([at, rt_], axis=1), jnp.concatenate([kb, kq], axis=1))
    a_m = jnp.where(strict, tt[:, :w, :w], 0.0)
    b_m = jnp.where(strict, tt[:, :w, w:], 0.0)
    ar_m = jnp.where(incl, tt[:, w:, :w], 0.0)
    br_m = jnp.where(incl, tt[:, w:, w:], 0.0)
    rsum = jnp.where(eye, 1.0, 0.0) + a_m
    pw = bmm(a_m, a_m)
    for _ in range(cshift - 2):
        both = bmm(pw, jnp.concatenate([rsum, pw], axis=2))
        rsum = rsum + both[:, :, :w]
        pw = both[:, :, w:]
    tinv = rsum + bmm(pw, rsum)
    w12 = bmm(tinv, jnp.concatenate([at, bmm(b_m, vs)], axis=2))
    xmat = jnp.concatenate([w12, jnp.concatenate([jnp.zeros_like(vs), vs], axis=2)], axis=1)
    kbe_t = jnp.stack([kbe[i].T for i in range(nb)])
    kqe_t = jnp.stack([kqe[i].T for i in range(nb)])
    lhs = jnp.concatenate([jnp.concatenate([ar_m, br_m], axis=2), jnp.concatenate([kbe_t, kqe_t], axis=2)], axis=1)
    res = bmm(lhs, xmat)
    g_m = rt_ + res[:, :w, :w]
    y0 = res[:, :w, w:]
    mt = jnp.where(eye, jnp.broadcast_to(pc, (nb, w, w)), 0.0) + res[:, w:, :w]
    nt = res[:, w:, w:]

    state_t = s_scr[...]
    ys = []
    for idx in range(nb):
        step = _dot(jnp.concatenate([g_m[idx], mt[idx]], axis=0), state_t)
        yst = step[:w] + y0[idx]
        state_t = step[w:] + nt[idx]
        ys.append(yst[:c] + yst[c:])
    s_scr[...] = state_t
    y = jnp.concatenate(ys, axis=0)

    mean = head_sum(y) * (1.0 / n)
    dlt = y - mean
    var = head_sum(dlt * dlt) * (1.0 / n)
    yn = dlt * lax.rsqrt(var + RWKV_GN_EPS) * lnw_ref[...] + lnb_ref[...]
    bonus = head_sum(r * k2 * rk_ref[...]) * v
    z_ref[0] = ((yn + bonus) * g_ref[0]).astype(z_ref.dtype)


def _rwkv_layer(x, nw, shift, scale, gate, mu, w_rkv, w0, wd_a, wd_b, a0, wa_a, wa_b, wg_a, wg_b,
                k_k, k_a, r_k, ln_w, ln_b, w_out):
    b, s, d = x.shape
    tm = ROW_TILE
    assert s % tm == 0
    mod_spec = pl.BlockSpec((1, 1, d), lambda i, j: (i, 0, 0))
    vec_spec = pl.BlockSpec((1, d), lambda i, j: (0, 0))
    tok_spec = pl.BlockSpec((1, tm, d), lambda i, j: (i, j, 0))

    def full(arr):
        return pl.BlockSpec(arr.shape, lambda i, j: (0,) * arr.ndim)

    weights = [w_rkv[0].astype(BF16), w_rkv[1].astype(BF16), w_rkv[2].astype(BF16),
               w0.reshape(1, d), wd_a.astype(BF16), wd_b.astype(BF16),
               a0.reshape(1, d), wa_a.astype(BF16), wa_b.astype(BF16),
               wg_a.astype(BF16), wg_b.astype(BF16)]
    tok_shape = jax.ShapeDtypeStruct((b, s, d), F32)
    r, k, v, lw, a, g = pl.pallas_call(
        functools.partial(_rwkv_proj_kernel, tm),
        out_shape=[tok_shape] * 6,
        grid=(b, s // tm),
        in_specs=[tok_spec, vec_spec, mod_spec, mod_spec, full(mu)] + [full(wt) for wt in weights],
        out_specs=[tok_spec] * 6,
        scratch_shapes=[pltpu.VMEM((tm + 8, d), F32)],
        compiler_params=_params("arbitrary", "arbitrary"),
        name="rwkv_proj",
    )(x, nw.reshape(1, d), shift[:, None], scale[:, None], mu, *weights)

    ct = 512
    pair = 2 * RWKV_HEAD_DIM
    assert s % ct == 0 and d % pair == 0 and pair == V7X_LANES
    seq_spec = pl.BlockSpec((1, ct, pair), lambda i, p, j: (i, j, p))
    par_spec = pl.BlockSpec((1, pair), lambda i, p, j: (0, p))
    z = pl.pallas_call(
        functools.partial(_rwkv_scan_kernel, ct),
        out_shape=jax.ShapeDtypeStruct((b, s, d), BF16),
        grid=(b, d // pair, s // ct),
        in_specs=[seq_spec] * 6 + [par_spec] * 5,
        out_specs=seq_spec,
        scratch_shapes=[pltpu.VMEM((pair, pair), F32)],
        compiler_params=_params("parallel", "parallel", "arbitrary"),
        name="rwkv_scan",
    )(r, k, v, lw, a, g, k_k.reshape(1, d), k_a.reshape(1, d), r_k.reshape(1, d), ln_w.reshape(1, d), ln_b.reshape(1, d))
    return _outproj(x.reshape(b * s, d), z.reshape(b * s, d), gate, w_out, s).reshape(b, s, d)


NSA_Q_TILE = 128
NSA_ATTN_Q_TILE = 256
NSA_KV_TILE = 512
NSA_PROJ_PAD = 128
NSA_QK_WIDTH = 128


def _head_rms(xf, p_ref, pt_ref, wvec):
    sums = sum(jnp.dot(part, p_ref[...], preferred_element_type=F32) for part in _split_bf16(xf * xf, 2))
    inv = lax.rsqrt(sums * (1.0 / NSA_HEAD_DIM) + RMS_EPS)
    inv_full = sum(jnp.dot(part, pt_ref[...], preferred_element_type=F32) for part in _split_bf16(inv, 3))
    return xf * inv_full * wvec


def _pos_lanes(pos):
    lane = lax.broadcasted_iota(jnp.int32, (1, NSA_QK_WIDTH), 1)
    hi = ((pos >> 6) << 6).astype(F32)
    lo = (pos & 63).astype(F32)
    d = NSA_HEAD_DIM
    return jnp.where((lane >= d) & (lane < d + 3), hi, jnp.where((lane >= d + 3) & (lane < d + 6), lo, 0.0))


def _nsa_proj_kernel(tm, qd, kd, x_ref, nw_ref, sh_ref, sc_ref, w_ref, pq_ref, pqt_ref, pk_ref, pkt_ref,
                     nq_ref, nks_ref, nkw_ref, qf_ref,
                     q_ref, ks_ref, vs_ref, kw_ref, vw_ref, kc_ref, vc_ref, gt_ref):
    hg, dk = NSA_HEADS_PER_GROUP, NSA_HEAD_DIM
    h = _normmod(x_ref[0], nw_ref[...], sh_ref[0], sc_ref[0])
    proj = _dot(h, w_ref[...])
    low = lax.broadcasted_iota(jnp.int32, (1, NSA_QK_WIDTH), 1) < dk
    key_lanes = _pos_lanes(pl.program_id(1) * tm + lax.broadcasted_iota(jnp.int32, (tm, 1), 0))

    def heads(x, nheads):
        for pair in range(nheads // 2):
            xp = x[:, pair * 2 * dk:(pair + 1) * 2 * dk]
            yield 2 * pair, xp
            yield 2 * pair + 1, pltpu.roll(xp, dk, 1)

    one_lane = lax.broadcasted_iota(jnp.int32, (1, NSA_QK_WIDTH), 1) == dk
    qn = _head_rms(proj[:, :qd], pq_ref, pqt_ref, nq_ref[...]) * (NSA_SCALE * LOG2_E)
    for hd, xs in heads(qn, NSA_HEADS):
        q_ref[0, hd // hg, hd % hg] = jnp.where(low, xs, qf_ref[hd:hd + 1, :]).astype(BF16)
    o = qd
    kc_ref[0] = proj[:, o:o + kd]
    vc_ref[0] = proj[:, o + kd:o + 2 * kd]
    for g, xs in heads(_head_rms(proj[:, o + 2 * kd:o + 3 * kd], pk_ref, pkt_ref, nks_ref[...]), NSA_KV_GROUPS):
        ks_ref[0, g] = jnp.where(low, xs, key_lanes).astype(BF16)
    for g, xs in heads(proj[:, o + 3 * kd:o + 4 * kd], NSA_KV_GROUPS):
        vs_ref[0, g] = jnp.where(one_lane, 1.0, xs).astype(BF16)
    for g, xs in heads(_head_rms(proj[:, o + 4 * kd:o + 5 * kd], pk_ref, pkt_ref, nkw_ref[...]), NSA_KV_GROUPS):
        kw_ref[0, g] = jnp.where(low, xs, key_lanes).astype(BF16)
    for g, xs in heads(proj[:, o + 5 * kd:o + 6 * kd], NSA_KV_GROUPS):
        vw_ref[0, g] = jnp.where(one_lane, 1.0, xs).astype(BF16)
    o += 6 * kd
    for g in range(NSA_KV_GROUPS):
        gt_ref[0, g] = _sigmoid(proj[:, o + g * NSA_PROJ_PAD:o + (g + 1) * NSA_PROJ_PAD])


def _gelu_tanh(x):
    return 0.5 * x * (1.0 + jnp.tanh(math.sqrt(2.0 / math.pi) * (x + 0.044715 * (x * x * x))))


def _nsa_compress_kernel(half, kc_ref, vc_ref, pos_ref, w1_ref, w2_ref, nk_ref, kco_ref, vco_ref):
    nsub = kc_ref.shape[2]

    def comp(x, i):
        ya = _dot(x + pos_ref[2 * i:2 * i + 1, :], w1_ref[i, 0:half, :])
        yb = _dot(x + pos_ref[2 * i + 1:2 * i + 2, :], w1_ref[i, half:2 * half, :])
        hid = ya + pltpu.roll(yb, nsub - 1, 0)
        return _dot(_gelu_tanh(hid), w2_ref[i])

    kcm = comp(kc_ref[0, 0], 0)
    ms = jnp.sum(kcm * kcm, axis=-1, keepdims=True) * (1.0 / NSA_HEAD_DIM)
    block_end = lax.broadcasted_iota(jnp.int32, (nsub, 1), 0) * CMP_STRIDE + CMP_LEN - 1
    kco_ref[0, 0] = (kcm * lax.rsqrt(ms + RMS_EPS) * nk_ref[...] + _pos_lanes(block_end)).astype(BF16)
    vco_ref[0, 0] = comp(vc_ref[0, 0], 1).astype(BF16)


def _pack_heads(per_head):
    low = lax.broadcasted_iota(jnp.int32, (1, NSA_QK_WIDTH), 1) < NSA_HEAD_DIM
    pairs = [jnp.where(low, per_head[i], pltpu.roll(per_head[i + 1], NSA_HEAD_DIM, 1))
             for i in range(0, len(per_head), 2)]
    return jnp.concatenate(pairs, axis=-1)


def _nsa_cmp_kernel(qt, nc, nsel, q_ref, kc_ref, vc_ref, g_ref, oc_ref, sel_ref):
    hg = NSA_HEADS_PER_GROUP
    q0 = pl.program_id(2) * qt
    q = q_ref[0, 0].reshape(hg * qt, NSA_QK_WIDTH)
    s3 = _dot_nt(q, kc_ref[0, 0]).reshape(hg, qt, nc)
    t = q0 + lax.broadcasted_iota(jnp.int32, (qt, nc), 0)
    n = lax.broadcasted_iota(jnp.int32, (qt, nc), 1)
    mask = (t >= n * CMP_STRIDE + CMP_LEN - 1)[None]
    sm = jnp.where(mask, s3, MASKED_SCORE)
    m = jnp.max(sm, axis=-1, keepdims=True)
    e = jnp.where(mask, jnp.exp2(sm - m), 0.0)
    p = e / jnp.maximum(jnp.sum(e, axis=-1, keepdims=True), 1e-30)
    oc = _dot(p.reshape(hg * qt, nc), vc_ref[0, 0])
    g = g_ref[0, 0]
    oc_ref[0] = _pack_heads([oc[hd * qt:(hd + 1) * qt] * g[:, hd:hd + 1] for hd in range(hg)])

    psum = p[0] + p[1] + p[2] + p[3]
    jo = lax.broadcasted_iota(jnp.int32, (nsel, nc), 0)
    no = lax.broadcasted_iota(jnp.int32, (nsel, nc), 1)
    ratio = SEL_LEN // CMP_STRIDE
    first = jo * ratio - (CMP_LEN // CMP_STRIDE - 1)
    overlap_t = jnp.where((no >= first) & (no < (jo + 1) * ratio) & (no < nc - 1), 1.0, 0.0).astype(BF16)
    imp_t = sum(lax.dot_general(overlap_t, part, (((1,), (1,)), ((), ())), preferred_element_type=F32)
                for part in _split_bf16(psum, 3))

    j = lax.broadcasted_iota(jnp.int32, (nsel, qt), 0)
    tt = q0 + lax.broadcasted_iota(jnp.int32, (nsel, qt), 1)
    valid = j * SEL_LEN <= tt
    qb = tt >> int(math.log2(SEL_LEN))
    forced = (j == 0) | (j == qb) | (j == qb - 1)
    score = jnp.where(valid, jnp.where(forced, SEL_BIG, imp_t), -SEL_BIG)
    jf = j.astype(F32)
    sel = jnp.zeros((nsel, qt), F32)
    for _ in range(min(SEL_TOPK, nsel)):
        mx = jnp.max(score, axis=0, keepdims=True)
        jmin = jnp.min(jnp.where(score == mx, jf, float(nsel)), axis=0, keepdims=True)
        pick = jf == jmin
        sel = jnp.where(pick, 1.0, sel)
        score = jnp.where(pick, -3e38, score)
    sel = jnp.where(valid, sel, 0.0)
    sel_ref[0, 0] = sel.T.astype(BF16)


def _nsa_attn_kernel(qt, kt, nsel, nkt, nq, tiles_ref, cnt_ref,
                     q_ref, ks_ref, vs_ref, kw_ref, vw_ref, sel_ref, g_ref, oc_ref, o_ref):
    hg, dk = NSA_HEADS_PER_GROUP, NSA_HEAD_DIM
    rows = hg * qt
    step = (pl.program_id(0) * NSA_KV_GROUPS + pl.program_id(1)) * nq + pl.program_id(2)
    q0 = pl.program_id(2) * qt
    q = q_ref[0, 0].reshape(rows, NSA_QK_WIDTH)
    sel = sel_ref[0, 0]
    lsel = int(math.log2(SEL_LEN))

    t_k = q0 + lax.broadcasted_iota(jnp.int32, (qt, kt), 0)
    c_k = lax.broadcasted_iota(jnp.int32, (qt, kt), 1)
    blk_of_col = lax.broadcasted_iota(jnp.int32, (nsel, kt), 0) - (lax.broadcasted_iota(jnp.int32, (nsel, kt), 1) >> lsel)

    def sel_step(i, carry, diagonal):
        m, acc = carry
        k0 = pl.multiple_of(tiles_ref[step * nkt + i] * kt, kt)
        s3 = _dot_nt(q, ks_ref[0, 0, pl.ds(k0, kt), :]).reshape(hg, qt, kt)
        expand = jnp.where(blk_of_col == (k0 >> lsel), 1.0, 0.0).astype(BF16)
        mask = jnp.dot(sel, expand, preferred_element_type=F32) > 0.5
        if diagonal:
            mask = mask & (k0 + c_k <= t_k)
        sm = jnp.where(mask[None], s3, MASKED_SCORE).reshape(rows, kt)
        m_new = jnp.maximum(m, jnp.max(sm, axis=-1, keepdims=True))
        acc = jnp.exp2(m - m_new) * acc + _dot(jnp.exp2(sm - m_new), vs_ref[0, 0, pl.ds(k0, kt), :])
        return m_new, acc

    init = (jnp.full((rows, 1), MASKED_SCORE, F32), jnp.zeros((rows, NSA_QK_WIDTH), F32))
    last = cnt_ref[step] - 1
    carry = lax.fori_loop(0, last, functools.partial(sel_step, diagonal=False), init)
    _, acc_s = sel_step(last, carry, diagonal=True)
    o_s = acc_s / jnp.maximum(acc_s[:, dk:dk + 1], 1e-30)

    wk = WINDOW + qt
    ws = pl.multiple_of(jnp.maximum(q0 - WINDOW, 0), qt)
    s3 = _dot_nt(q, kw_ref[0, 0, pl.ds(ws, wk), :]).reshape(hg, qt, wk)
    dist = (q0 + lax.broadcasted_iota(jnp.int32, (qt, wk), 0)) - (ws + lax.broadcasted_iota(jnp.int32, (qt, wk), 1))
    mask = ((dist >= 0) & (dist < WINDOW))[None]
    sm = jnp.where(mask, s3, MASKED_SCORE).reshape(rows, wk)
    e = jnp.exp2(sm - jnp.max(sm, axis=-1, keepdims=True))
    acc_w = _dot(e, vw_ref[0, 0, pl.ds(ws, wk), :])
    o_w = acc_w / jnp.maximum(acc_w[:, dk:dk + 1], 1e-30)

    g = g_ref[0, 0]
    per_head = [g[:, hg + hd:hg + hd + 1] * o_s[hd * qt:(hd + 1) * qt]
                + g[:, 2 * hg + hd:2 * hg + hd + 1] * o_w[hd * qt:(hd + 1) * qt] for hd in range(hg)]
    o_ref[0] = (oc_ref[0] + _pack_heads(per_head)).astype(o_ref.dtype)


def _nsa_layer(x, nw, shift, scale, gate, w_in, cmp_pos, cmp_w1, cmp_w2, qk_norm, w_out):
    b, s, d = x.shape
    t = b * s
    g_, hg, dk = NSA_KV_GROUPS, NSA_HEADS_PER_GROUP, NSA_HEAD_DIM
    qd, kd = NSA_HEADS * dk, g_ * dk
    tm = ROW_TILE
    wl = NSA_QK_WIDTH
    n_main = qd + 6 * kd
    assert w_in.shape[1] == n_main + 3 * NSA_HEADS and s % tm == 0
    gate_cols = w_in[:, n_main:].reshape(d, 3, g_, hg).transpose(0, 2, 1, 3).reshape(d, g_, 3 * hg)
    gate_cols = jnp.pad(gate_cols, ((0, 0), (0, 0), (0, NSA_PROJ_PAD - 3 * hg))).reshape(d, g_ * NSA_PROJ_PAD)
    w_pad = jnp.concatenate([w_in[:, :n_main], gate_cols], axis=1).astype(BF16)

    def head_onehot(width):
        lane_head = jnp.arange(width)[:, None] // dk
        return (lane_head == jnp.arange(V7X_LANES)[None, :]).astype(BF16)

    pq, pk = head_onehot(qd), head_onehot(kd)

    def tile_w(wv, reps):
        return jnp.tile(wv, reps).reshape(1, reps * dk)

    slopes = jnp.exp2(-8.0 * (jnp.arange(NSA_HEADS, dtype=F32) + 1) / NSA_HEADS)
    slope_parts = jnp.stack([p.astype(F32) for p in _split_bf16(slopes * LOG2_E, 3)], axis=-1)
    q_lanes = jnp.zeros((NSA_HEADS, wl), F32).at[:, dk:dk + 6].set(jnp.concatenate([slope_parts, slope_parts], axis=-1))

    consts = [pq, pq.T, pk, pk.T, tile_w(qk_norm[0], NSA_HEADS), tile_w(qk_norm[2], g_), tile_w(qk_norm[3], g_), q_lanes]
    full2 = lambda arr: pl.BlockSpec(arr.shape, lambda i, j: (0, 0))
    mod_spec = pl.BlockSpec((1, 1, d), lambda i, j: (i, 0, 0))
    grp = lambda dt: jax.ShapeDtypeStruct((b, g_, s, wl), dt)
    grp_spec = pl.BlockSpec((1, g_, tm, wl), lambda i, j: (i, 0, j, 0))
    tokf = jax.ShapeDtypeStruct((b, s, kd), F32)
    tokf_spec = pl.BlockSpec((1, tm, kd), lambda i, j: (i, j, 0))
    q5, ks4, vs4, kw4, vw4, kc, vc, gates = pl.pallas_call(
        functools.partial(_nsa_proj_kernel, tm, qd, kd),
        out_shape=[jax.ShapeDtypeStruct((b, g_, hg, s, wl), BF16), grp(BF16), grp(BF16), grp(BF16), grp(BF16),
                   tokf, tokf, grp(F32)],
        grid=(b, s // tm),
        in_specs=[pl.BlockSpec((1, tm, d), lambda i, j: (i, j, 0)), pl.BlockSpec((1, d), lambda i, j: (0, 0)),
                  mod_spec, mod_spec, full2(w_pad)] + [full2(cst) for cst in consts],
        out_specs=[pl.BlockSpec((1, g_, hg, tm, wl), lambda i, j: (i, 0, 0, j, 0)),
                   grp_spec, grp_spec, grp_spec, grp_spec, tokf_spec, tokf_spec, grp_spec],
        compiler_params=_params("parallel", "parallel"),
        name="nsa_proj",
    )(x, nw.reshape(1, d), shift[:, None], scale[:, None], w_pad, *consts)

    nsub = s // CMP_STRIDE
    half = CMP_STRIDE * dk
    sub_rows = lambda arr: arr.reshape(b, s, g_, dk).transpose(0, 2, 1, 3).reshape(b, g_, nsub, half)
    kc3, vc3 = sub_rows(kc), sub_rows(vc)

    pos = jnp.stack([cmp_pos[0, :CMP_STRIDE].reshape(half), cmp_pos[0, CMP_STRIDE:].reshape(half),
                     cmp_pos[1, :CMP_STRIDE].reshape(half), cmp_pos[1, CMP_STRIDE:].reshape(half)])
    w1b = cmp_w1.astype(BF16)
    w2b = jnp.pad(cmp_w2, ((0, 0), (0, 0), (0, wl - dk))).astype(BF16)
    nk_lanes = jnp.pad(qk_norm[1], (0, wl - dk)).reshape(1, wl)
    cmp_in = pl.BlockSpec((1, 1, nsub, half), lambda i, j: (i, j, 0, 0))
    cmp_out = pl.BlockSpec((1, 1, nsub, wl), lambda i, j: (i, j, 0, 0))
    kcc, vcc = pl.pallas_call(
        functools.partial(_nsa_compress_kernel, half),
        out_shape=[jax.ShapeDtypeStruct((b, g_, nsub, wl), BF16)] * 2,
        grid=(b, g_),
        in_specs=[cmp_in, cmp_in, pl.BlockSpec(pos.shape, lambda i, j: (0, 0)),
                  pl.BlockSpec(w1b.shape, lambda i, j: (0, 0, 0)), pl.BlockSpec(w2b.shape, lambda i, j: (0, 0, 0)),
                  pl.BlockSpec((1, wl), lambda i, j: (0, 0))],
        out_specs=[cmp_out, cmp_out],
        compiler_params=_params("parallel", "parallel"),
        name="nsa_compress",
    )(kc3, vc3, pos, w1b, w2b, nk_lanes)

    nsel = s // SEL_LEN
    kt = min(NSA_KV_TILE, s)
    nkt = s // kt
    seq_kv = lambda n: pl.BlockSpec((1, 1, n, wl), lambda i, j, k, *_: (i, j, 0, 0))

    def tile_specs(qt):
        assert s % qt == 0 and s >= WINDOW + qt and kt % qt == 0
        return (pl.BlockSpec((1, 1, hg, qt, wl), lambda i, j, k, *_: (i, j, 0, k, 0)),
                pl.BlockSpec((1, qt, hg * dk), lambda i, j, k, *_: (i, k, j)),
                pl.BlockSpec((1, 1, qt, wl), lambda i, j, k, *_: (i, j, k, 0)),
                pl.BlockSpec((1, 1, qt, nsel), lambda i, j, k, *_: (i, j, k, 0)))

    qt = NSA_Q_TILE
    q_spec, o_spec, gate_spec, sel_spec = tile_specs(qt)
    oc, sel = pl.pallas_call(
        functools.partial(_nsa_cmp_kernel, qt, nsub, nsel),
        out_shape=[jax.ShapeDtypeStruct((b, s, qd), F32), jax.ShapeDtypeStruct((b, g_, s, nsel), BF16)],
        grid=(b, g_, s // qt),
        in_specs=[q_spec, seq_kv(nsub), seq_kv(nsub), gate_spec],
        out_specs=[o_spec, sel_spec],
        compiler_params=_params("parallel", "parallel", "parallel"),
        name="nsa_cmp_select",
    )(q5, kcc, vcc, gates)

    qt = NSA_ATTN_Q_TILE
    nq = s // qt
    q_spec, o_spec, gate_spec, sel_spec = tile_specs(qt)
    active = sel.reshape(b, g_, nq, qt, nkt, kt // SEL_LEN).max(axis=(3, 5)) > 0
    order = jnp.sort(jnp.where(active, 0, nkt) + jnp.arange(nkt, dtype=jnp.int32), axis=-1)
    tiles = (order % nkt).astype(jnp.int32).reshape(-1)
    counts = active.sum(axis=-1).astype(jnp.int32).reshape(-1)

    o3 = pl.pallas_call(
        functools.partial(_nsa_attn_kernel, qt, kt, nsel, nkt, nq),
        out_shape=jax.ShapeDtypeStruct((b, s, qd), BF16),
        grid_spec=pltpu.PrefetchScalarGridSpec(
            num_scalar_prefetch=2,
            grid=(b, g_, nq),
            in_specs=[q_spec, seq_kv(s), seq_kv(s), seq_kv(s), seq_kv(s), sel_spec, gate_spec, o_spec],
            out_specs=o_spec,
        ),
        compiler_params=_params("parallel", "parallel", "arbitrary"),
        name="nsa_attention",
    )(tiles, counts, q5, ks4, vs4, kw4, vw4, sel, gates, oc)
    return _outproj(x.reshape(t, d), o3.reshape(t, qd), gate, w_out, s).reshape(b, s, d)


def kernel(x, c, ada_w, ada_b, norm_mix_w, norm_ffn_w, ffn_w_gate, ffn_w_up, ffn_w_down, nsa_w_in, nsa_cmp_pos, nsa_cmp_w1, nsa_cmp_w2, nsa_qk_norm, nsa_w_out, rwkv_mu, rwkv_w_rkv, rwkv_w0, rwkv_wd_a, rwkv_wd_b, rwkv_a0, rwkv_wa_a, rwkv_wa_b, rwkv_wg_a, rwkv_wg_b, rwkv_k_k, rwkv_k_a, rwkv_r_k, rwkv_ln_w, rwkv_ln_b, rwkv_w_out, pool_w, pool_b, pool_scale):
    b, s, d = x.shape
    depth = ada_w.shape[0]
    mod = _ada_mod(c, ada_w, ada_b)
    for i in range(depth):
        sh_m, sc_m, g_m, sh_f, sc_f, g_f = jnp.split(mod[i], N_MOD, axis=-1)
        kind, j = i % N_MIXERS, i // N_MIXERS
        if kind == 0:
            x = _nsa_layer(x, norm_mix_w[i], sh_m, sc_m, g_m, nsa_w_in[j], nsa_cmp_pos[j], nsa_cmp_w1[j], nsa_cmp_w2[j],
                           nsa_qk_norm[j], nsa_w_out[j])
        elif kind == 1:
            x = _rwkv_layer(x, norm_mix_w[i], sh_m, sc_m, g_m, rwkv_mu[j], rwkv_w_rkv[j], rwkv_w0[j], rwkv_wd_a[j],
                            rwkv_wd_b[j], rwkv_a0[j], rwkv_wa_a[j], rwkv_wa_b[j], rwkv_wg_a[j], rwkv_wg_b[j],
                            rwkv_k_k[j], rwkv_k_a[j], rwkv_r_k[j], rwkv_ln_w[j], rwkv_ln_b[j], rwkv_w_out[j])
        elif kind == 2:
            x = _pool_layer(x, norm_mix_w[i], sh_m, sc_m, g_m, pool_w[j], pool_b[j], pool_scale[j])
        x = _ffn(x.reshape(b * s, d), norm_ffn_w[i], sh_f, sc_f, g_f, ffn_w_gate[i], ffn_w_up[i], ffn_w_down[i], s).reshape(b, s, d)
    return x
```

```python
import functools
import math

import jax
import jax.numpy as jnp
from jax import lax
from jax.experimental import pallas as pl
from jax.experimental.pallas import tpu as pltpu

F32 = jnp.float32
BF16 = jnp.bfloat16

V7X_LANES = 128
V7X_VMEM_LIMIT_BYTES = 56 * 1024 * 1024

RMS_EPS = 1e-6
N_MOD = 6
N_MIXERS = 3

NSA_HEAD_DIM = 64
NSA_KV_GROUPS = 4
NSA_HEADS_PER_GROUP = 4
NSA_HEADS = NSA_KV_GROUPS * NSA_HEADS_PER_GROUP
CMP_LEN = 32
CMP_STRIDE = 16
SEL_LEN = 64
SEL_TOPK = 16
WINDOW = 512
NSA_SCALE = NSA_HEAD_DIM ** -0.5
LOG2_E = math.log2(math.e)
SEL_BIG = 1e9
MASKED_SCORE = -1e30

RWKV_HEAD_DIM = 64
RWKV_GN_EPS = 64e-5
RWKV_CHUNK = 64

POOL_WINDOWS = (2, 4, 8, 16)
POOL_HALO = 16

ROW_TILE = 512


def _params(*sem):
    return pltpu.CompilerParams(dimension_semantics=sem, vmem_limit_bytes=V7X_VMEM_LIMIT_BYTES)


def _sigmoid(z):
    return 1.0 / (1.0 + jnp.exp(-z))


def _normmod(x, nw, shift, scale):
    ms = jnp.mean(x * x, axis=-1, keepdims=True)
    return (x * lax.rsqrt(ms + RMS_EPS) * nw) * (1.0 + scale) + shift


def _dot(a, b):
    return jnp.dot(a.astype(BF16), b.astype(BF16), preferred_element_type=F32)


def _dot_nt(a, b, out_dtype=F32):
    out = lax.dot_general(a.astype(BF16), b.astype(BF16), (((1,), (1,)), ((), ())), preferred_element_type=F32)
    return out.astype(out_dtype)


def _dot_tn(a, b):
    return lax.dot_general(a.astype(BF16), b.astype(BF16), (((0,), (0,)), ((), ())), preferred_element_type=F32)


def _split_bf16(x, parts):
    out = []
    for _ in range(parts):
        p = x.astype(BF16)
        out.append(p)
        x = x - p.astype(F32)
    return out


def _mod_kernel(c_ref, w_ref, b_ref, o_ref):
    c = c_ref[...]
    ca = c * _sigmoid(c)
    o_ref[0] = _dot(ca, w_ref[0]) + b_ref[0]


def _ada_mod(c, ada_w, ada_b):
    depth, d, n = ada_w.shape
    b = c.shape[0]
    rows = 8
    tn = 1536
    assert b <= rows and n % tn == 0
    c_pad = jnp.zeros((rows, d), F32).at[:b].set(c)
    out = pl.pallas_call(
        _mod_kernel,
        out_shape=jax.ShapeDtypeStruct((depth, rows, n), F32),
        grid=(depth, n // tn),
        in_specs=[
            pl.BlockSpec((rows, d), lambda i, j: (0, 0)),
            pl.BlockSpec((1, d, tn), lambda i, j: (i, 0, j)),
            pl.BlockSpec((1, 1, tn), lambda i, j: (i, 0, j)),
        ],
        out_specs=pl.BlockSpec((1, rows, tn), lambda i, j: (i, 0, j)),
        compiler_params=_params("parallel", "parallel"),
        name="ada_mod",
    )(c_pad, ada_w, ada_b.reshape(depth, 1, n))
    return out[:, :b]


def _ffn_kernel(nj, x_ref, nw_ref, sh_ref, sc_ref, g_ref, wg_ref, wu_ref, wd_ref, o_ref, h_scr, acc_scr):
    j = pl.program_id(1)

    @pl.when(j == 0)
    def _():
        h_scr[...] = _normmod(x_ref[...], nw_ref[...], sh_ref[0], sc_ref[0]).astype(BF16)
        acc_scr[...] = jnp.zeros_like(acc_scr)

    h = h_scr[...]
    g = jnp.dot(h, wg_ref[...], preferred_element_type=F32)
    u = jnp.dot(h, wu_ref[...], preferred_element_type=F32)
    a = g * _sigmoid(g) * u
    acc_scr[...] += jnp.dot(a.astype(BF16), wd_ref[...], preferred_element_type=F32)

    @pl.when(j == nj - 1)
    def _():
        o_ref[...] = x_ref[...] + g_ref[0] * acc_scr[...]


def _ffn(x2, nw, shift, scale, gate, wg, wu, wd, seq):
    t, d = x2.shape
    f = wg.shape[1]
    tm = ROW_TILE
    fc = f // 2
    assert t % tm == 0 and seq % tm == 0 and fc % V7X_LANES == 0
    tpb = seq // tm
    nj = f // fc
    mod_spec = pl.BlockSpec((1, 1, d), lambda i, j: (i // tpb, 0, 0))
    return pl.pallas_call(
        functools.partial(_ffn_kernel, nj),
        out_shape=jax.ShapeDtypeStruct((t, d), F32),
        grid=(t // tm, nj),
        in_specs=[
            pl.BlockSpec((tm, d), lambda i, j: (i, 0)),
            pl.BlockSpec((1, d), lambda i, j: (0, 0)),
            mod_spec, mod_spec, mod_spec,
            pl.BlockSpec((d, fc), lambda i, j: (0, j)),
            pl.BlockSpec((d, fc), lambda i, j: (0, j)),
            pl.BlockSpec((fc, d), lambda i, j: (j, 0)),
        ],
        out_specs=pl.BlockSpec((tm, d), lambda i, j: (i, 0)),
        scratch_shapes=[pltpu.VMEM((tm, d), BF16), pltpu.VMEM((tm, d), F32)],
        compiler_params=_params("parallel", "arbitrary"),
        name="ffn",
    )(x2, nw.reshape(1, d), shift[:, None], scale[:, None], gate[:, None],
      wg.astype(BF16), wu.astype(BF16), wd.astype(BF16))


def _outproj_kernel(x_ref, z_ref, g_ref, w_ref, o_ref):
    o_ref[...] = x_ref[...] + g_ref[0] * _dot(z_ref[...], w_ref[...])


def _outproj(x2, z2, gate, w, seq):
    t, d = x2.shape
    k = z2.shape[1]
    tm = ROW_TILE
    tpb = seq // tm
    return pl.pallas_call(
        _outproj_kernel,
        out_shape=jax.ShapeDtypeStruct((t, d), F32),
        grid=(t // tm,),
        in_specs=[
            pl.BlockSpec((tm, d), lambda i: (i, 0)),
            pl.BlockSpec((tm, k), lambda i: (i, 0)),
            pl.BlockSpec((1, 1, d), lambda i: (i // tpb, 0, 0)),
            pl.BlockSpec((k, d), lambda i: (0, 0)),
        ],
        out_specs=pl.BlockSpec((tm, d), lambda i: (i, 0)),
        compiler_params=_params("parallel"),
        name="outproj",
    )(x2, z2, gate[:, None], w.astype(BF16))


def _pool_kernel(tm, gd, x_ref, nw_ref, sh_ref, sc_ref, g_ref, pw_ref, pb_ref, ps_ref, o_ref, ext_scr):
    s = pl.program_id(1)
    x = x_ref[0]
    h = _normmod(x, nw_ref[...], sh_ref[0], sc_ref[0])

    @pl.when(s == 0)
    def _():
        ext_scr[0:POOL_HALO, :] = jnp.zeros((POOL_HALO, x.shape[1]), F32)

    ext_scr[POOL_HALO:POOL_HALO + tm, :] = h
    row = s * tm + lax.broadcasted_iota(jnp.int32, (tm, 1), 0)
    ys = []
    for gi, win in enumerate(POOL_WINDOWS):
        lanes = slice(gi * gd, (gi + 1) * gd)
        hg = h[:, lanes]
        acc = hg
        for k in range(1, win):
            acc = acc + ext_scr[POOL_HALO - k:POOL_HALO - k + tm, lanes]
        cnt = jnp.minimum(row + 1, win).astype(F32)
        ys.append(_dot(acc / cnt - hg, pw_ref[gi]))
    y = (jnp.concatenate(ys, axis=-1) + pb_ref[...]) * ps_ref[...]
    o_ref[0] = x + g_ref[0] * y
    ext_scr[0:POOL_HALO, :] = ext_scr[tm:tm + POOL_HALO, :]


def _pool_layer(x, nw, shift, scale, gate, pw, pb, ps):
    b, s, d = x.shape
    tm = ROW_TILE
    gd = d // len(POOL_WINDOWS)
    assert s % tm == 0 and max(POOL_WINDOWS) <= POOL_HALO
    mod_spec = pl.BlockSpec((1, 1, d), lambda i, j: (i, 0, 0))
    vec_spec = pl.BlockSpec((1, d), lambda i, j: (0, 0))
    return pl.pallas_call(
        functools.partial(_pool_kernel, tm, gd),
        out_shape=jax.ShapeDtypeStruct((b, s, d), F32),
        grid=(b, s // tm),
        in_specs=[
            pl.BlockSpec((1, tm, d), lambda i, j: (i, j, 0)),
            vec_spec, mod_spec, mod_spec, mod_spec,
            pl.BlockSpec(pw.shape, lambda i, j: (0, 0, 0)),
            vec_spec, vec_spec,
        ],
        out_specs=pl.BlockSpec((1, tm, d), lambda i, j: (i, j, 0)),
        scratch_shapes=[pltpu.VMEM((tm + POOL_HALO, d), F32)],
        compiler_params=_params("arbitrary", "arbitrary"),
        name="pool_mixer",
    )(x, nw.reshape(1, d), shift[:, None], scale[:, None], gate[:, None],
      pw.astype(BF16), pb.reshape(1, d), ps.reshape(1, d))


def _rwkv_proj_kernel(tm, x_ref, nw_ref, sh_ref, sc_ref, mu_ref, wr_ref, wk_ref, wv_ref,
                      w0_ref, wda_ref, wdb_ref, a0_ref, waa_ref, wab_ref, wga_ref, wgb_ref,
                      r_ref, k_ref, v_ref, lw_ref, a_ref, g_ref, ext_scr):
    s = pl.program_id(1)
    h = _normmod(x_ref[0], nw_ref[...], sh_ref[0], sc_ref[0])

    @pl.when(s == 0)
    def _():
        ext_scr[0:8, :] = jnp.zeros((8, h.shape[1]), F32)

    ext_scr[8:8 + tm, :] = h
    xx = ext_scr[7:7 + tm, :] - h

    def mix(i):
        return h + xx * mu_ref[i:i + 1, :]

    r_ref[0] = _dot(mix(0), wr_ref[...])
    k_ref[0] = _dot(mix(2), wk_ref[...])
    v_ref[0] = _dot(mix(3), wv_ref[...])
    dw = w0_ref[...] + _dot(jnp.tanh(_dot(mix(1), wda_ref[...])), wdb_ref[...])
    softplus_neg = jnp.maximum(-dw, 0.0) + jnp.log(1.0 + jnp.exp(-jnp.abs(dw)))
    lw_ref[0] = -jnp.exp(-softplus_neg - 0.5)
    a_ref[0] = _sigmoid(a0_ref[...] + _dot(_dot(mix(4), waa_ref[...]), wab_ref[...]))
    g_ref[0] = _dot(_sigmoid(_dot(mix(5), wga_ref[...])), wgb_ref[...])
    ext_scr[0:8, :] = ext_scr[tm:tm + 8, :]


def _rwkv_scan_kernel(ct, r_ref, k_ref, v_ref, lw_ref, a_ref, g_ref, kk_ref, ka_ref, rk_ref, lnw_ref, lnb_ref,
                      z_ref, s_scr):
    c, n = RWKV_CHUNK, RWKV_HEAD_DIM
    w = 2 * n
    shift = int(math.log2(n))

    @pl.when(pl.program_id(2) == 0)
    def _():
        s_scr[...] = jnp.zeros_like(s_scr)

    r, k, v, lw, a = r_ref[0], k_ref[0], v_ref[0], lw_ref[0], a_ref[0]
    head0 = lax.broadcasted_iota(jnp.int32, (1, w), 1) < n
    ri = lax.broadcasted_iota(jnp.int32, (w, w), 0)
    ci = lax.broadcasted_iota(jnp.int32, (w, w), 1)
    same_head = jnp.where((ri >> shift) == (ci >> shift), 1.0, 0.0).astype(BF16)
    strict, incl, eye = ri > ci, ri >= ci, ri == ci
    rt = lax.broadcasted_iota(jnp.int32, (ct, ct), 0)
    cc = lax.broadcasted_iota(jnp.int32, (ct, ct), 1)
    cshift = int(math.log2(c))
    chunk_tri = jnp.where(((rt >> cshift) == (cc >> cshift)) & (rt >= cc), 1.0, 0.0).astype(BF16)

    def head_sum(xf):
        return sum(jnp.dot(p, same_head, preferred_element_type=F32) for p in _split_bf16(xf, 2))

    kk = k * kk_ref[...]
    kkn = kk * lax.rsqrt(jnp.maximum(head_sum(kk * kk), 1e-24))
    k2 = k * (1.0 + (a - 1.0) * ka_ref[...])
    beta = kkn * a
    cum = sum(jnp.dot(chunk_tri, p, preferred_element_type=F32) for p in _split_bf16(lw, 3))
    at_all = -kkn * jnp.exp(cum - lw)
    rt_all = r * jnp.exp(cum)
    einv = jnp.exp(-cum)
    kb_all = beta * einv
    kq_all = k2 * einv

    nb = ct // c

    def stack(xf):
        x3 = xf.reshape(nb, c, w)
        return jnp.concatenate([jnp.where(head0, x3, 0.0), jnp.where(head0, 0.0, x3)], axis=1)

    def bmm(x, y):
        return jnp.einsum('bij,bjk->bik', x.astype(BF16), y.astype(BF16), preferred_element_type=F32)

    def bmm_nt(x, y):
        return jnp.einsum('bik,bjk->bij', x.astype(BF16), y.astype(BF16), preferred_element_type=F32)

    at, rt_, kb, kq, vs = stack(at_all), stack(rt_all), stack(kb_all), stack(kq_all), stack(v)
    pc = jnp.exp(cum.reshape(nb, c, w)[:, c - 1:c, :])
    kbe, kqe = kb * pc, kq * pc
    tt = bmm_nt(jnp.concatenate([at, rt_], axis=1), jnp.concatenate([kb, kq], axis=1))
    a_m = jnp.where(strict, tt[:, :w, :w], 0.0)
    b_m = jnp.where(strict, tt[:, :w, w:], 0.0)
    ar_m = jnp.where(incl, tt[:, w:, :w], 0.0)
    br_m = jnp.where(incl, tt[:, w:, w:], 0.0)
    rsum = jnp.where(eye, 1.0, 0.0) + a_m
    pw = bmm(a_m, a_m)
    for _ in range(cshift - 2):
        both = bmm(pw, jnp.concatenate([rsum, pw], axis=2))
        rsum = rsum + both[:, :, :w]
        pw = both[:, :, w:]
    tinv = rsum + bmm(pw, rsum)
    w12 = bmm(tinv, jnp.concatenate([at, bmm(b_m, vs)], axis=2))
    xmat = jnp.concatenate([w12, jnp.concatenate([jnp.zeros_like(vs), vs], axis=2)], axis=1)
    kbe_t = jnp.stack([kbe[i].T for i in range(nb)])
    kqe_t = jnp.stack([kqe[i].T for i in range(nb)])
    lhs = jnp.concatenate([jnp.concatenate([ar_m, br_m], axis=2), jnp.concatenate([kbe_t, kqe_t], axis=2)], axis=1)
    res = bmm(lhs, xmat)
    g_m = rt_ + res[:, :w, :w]
    y0 = res[:, :w, w:]
    mt = jnp.where(eye, jnp.broadcast_to(pc, (nb, w, w)), 0.0) + res[:, w:, :w]
    nt = res[:, w:, w:]

    state_t = s_scr[...]
    ys = []
    for idx in range(nb):
        step = _dot(jnp.concatenate([g_m[idx], mt[idx]], axis=0), state_t)
        yst = step[:w] + y0[idx]
        state_t = step[w:] + nt[idx]
        ys.append(yst[:c] + yst[c:])
    s_scr[...] = state_t
    y = jnp.concatenate(ys, axis=0)

    mean = head_sum(y) * (1.0 / n)
    dlt = y - mean
    var = head_sum(dlt * dlt) * (1.0 / n)
    yn = dlt * lax.rsqrt(var + RWKV_GN_EPS) * lnw_ref[...] + lnb_ref[...]
    bonus = head_sum(r * k2 * rk_ref[...]) * v
    z_ref[0] = ((yn + bonus) * g_ref[0]).astype(z_ref.dtype)


def _rwkv_layer(x, nw, shift, scale, gate, mu, w_rkv, w0, wd_a, wd_b, a0, wa_a, wa_b, wg_a, wg_b,
                k_k, k_a, r_k, ln_w, ln_b, w_out):
    b, s, d = x.shape
    tm = ROW_TILE
    assert s % tm == 0
    mod_spec = pl.BlockSpec((1, 1, d), lambda i, j: (i, 0, 0))
    vec_spec = pl.BlockSpec((1, d), lambda i, j: (0, 0))
    tok_spec = pl.BlockSpec((1, tm, d), lambda i, j: (i, j, 0))

    def full(arr):
        return pl.BlockSpec(arr.shape, lambda i, j: (0,) * arr.ndim)

    weights = [w_rkv[0].astype(BF16), w_rkv[1].astype(BF16), w_rkv[2].astype(BF16),
               w0.reshape(1, d), wd_a.astype(BF16), wd_b.astype(BF16),
               a0.reshape(1, d), wa_a.astype(BF16), wa_b.astype(BF16),
               wg_a.astype(BF16), wg_b.astype(BF16)]
    tok_shape = jax.ShapeDtypeStruct((b, s, d), F32)
    r, k, v, lw, a, g = pl.pallas_call(
        functools.partial(_rwkv_proj_kernel, tm),
        out_shape=[tok_shape] * 6,
        grid=(b, s // tm),
        in_specs=[tok_spec, vec_spec, mod_spec, mod_spec, full(mu)] + [full(wt) for wt in weights],
        out_specs=[tok_spec] * 6,
        scratch_shapes=[pltpu.VMEM((tm + 8, d), F32)],
        compiler_params=_params("arbitrary", "arbitrary"),
        name="rwkv_proj",
    )(x, nw.reshape(1, d), shift[:, None], scale[:, None], mu, *weights)

    ct = 512
    pair = 2 * RWKV_HEAD_DIM
    assert s % ct == 0 and d % pair == 0 and pair == V7X_LANES
    seq_spec = pl.BlockSpec((1, ct, pair), lambda i, p, j: (i, j, p))
    par_spec = pl.BlockSpec((1, pair), lambda i, p, j: (0, p))
    z = pl.pallas_call(
        functools.partial(_rwkv_scan_kernel, ct),
        out_shape=jax.ShapeDtypeStruct((b, s, d), BF16),
        grid=(b, d // pair, s // ct),
        in_specs=[seq_spec] * 6 + [par_spec] * 5,
        out_specs=seq_spec,
        scratch_shapes=[pltpu.VMEM((pair, pair), F32)],
        compiler_params=_params("parallel", "parallel", "arbitrary"),
        name="rwkv_scan",
    )(r, k, v, lw, a, g, k_k.reshape(1, d), k_a.reshape(1, d), r_k.reshape(1, d), ln_w.reshape(1, d), ln_b.reshape(1, d))
    return _outproj(x.reshape(b * s, d), z.reshape(b * s, d), gate, w_out, s).reshape(b, s, d)


NSA_Q_TILE = 128
NSA_ATTN_Q_TILE = 256
NSA_KV_TILE = 512
NSA_PROJ_PAD = 128
NSA_QK_WIDTH = 128


def _head_rms(xf, p_ref, pt_ref, wvec):
    sums = sum(jnp.dot(part, p_ref[...], preferred_element_type=F32) for part in _split_bf16(xf * xf, 2))
    inv = lax.rsqrt(sums * (1.0 / NSA_HEAD_DIM) + RMS_EPS)
    inv_full = sum(jnp.dot(part, pt_ref[...], preferred_element_type=F32) for part in _split_bf16(inv, 3))
    return xf * inv_full * wvec


def _pos_lanes(pos):
    lane = lax.broadcasted_iota(jnp.int32, (1, NSA_QK_WIDTH), 1)
    hi = ((pos >> 6) << 6).astype(F32)
    lo = (pos & 63).astype(F32)
    d = NSA_HEAD_DIM
    ones = jnp.where((lane >= d + 6) & (lane < d + 9), 1.0, 0.0)
    return jnp.where((lane >= d) & (lane < d + 3), hi, jnp.where((lane >= d + 3) & (lane < d + 6), lo, ones))


def _nsa_proj_kernel(tm, qd, kd, x_ref, nw_ref, sh_ref, sc_ref, w_ref, pq_ref, pqt_ref, pk_ref, pkt_ref,
                     nq_ref, nks_ref, nkw_ref, qf_ref,
                     q_ref, ks_ref, vs_ref, kw_ref, vw_ref, kc_ref, vc_ref, gt_ref):
    hg, dk = NSA_HEADS_PER_GROUP, NSA_HEAD_DIM
    h = _normmod(x_ref[0], nw_ref[...], sh_ref[0], sc_ref[0])
    proj = _dot(h, w_ref[...])
    low = lax.broadcasted_iota(jnp.int32, (1, NSA_QK_WIDTH), 1) < dk
    key_lanes = _pos_lanes(pl.program_id(1) * tm + lax.broadcasted_iota(jnp.int32, (tm, 1), 0))

    def heads(x, nheads):
        for pair in range(nheads // 2):
            xp = x[:, pair * 2 * dk:(pair + 1) * 2 * dk]
            yield 2 * pair, xp
            yield 2 * pair + 1, pltpu.roll(xp, dk, 1)

    one_lane = lax.broadcasted_iota(jnp.int32, (1, NSA_QK_WIDTH), 1) == dk
    qn = _head_rms(proj[:, :qd], pq_ref, pqt_ref, nq_ref[...]) * (NSA_SCALE * LOG2_E)
    for hd, xs in heads(qn, NSA_HEADS):
        q_ref[0, hd // hg, hd % hg] = jnp.where(low, xs, qf_ref[hd:hd + 1, :]).astype(BF16)
    o = qd
    kc_ref[0] = proj[:, o:o + kd]
    vc_ref[0] = proj[:, o + kd:o + 2 * kd]
    for g, xs in heads(_head_rms(proj[:, o + 2 * kd:o + 3 * kd], pk_ref, pkt_ref, nks_ref[...]), NSA_KV_GROUPS):
        ks_ref[0, g] = jnp.where(low, xs, key_lanes).astype(BF16)
    for g, xs in heads(proj[:, o + 3 * kd:o + 4 * kd], NSA_KV_GROUPS):
        vs_ref[0, g] = jnp.where(one_lane, 1.0, xs).astype(BF16)
    for g, xs in heads(_head_rms(proj[:, o + 4 * kd:o + 5 * kd], pk_ref, pkt_ref, nkw_ref[...]), NSA_KV_GROUPS):
        kw_ref[0, g] = jnp.where(low, xs, key_lanes).astype(BF16)
    for g, xs in heads(proj[:, o + 5 * kd:o + 6 * kd], NSA_KV_GROUPS):
        vw_ref[0, g] = jnp.where(one_lane, 1.0, xs).astype(BF16)
    o += 6 * kd
    for g in range(NSA_KV_GROUPS):
        gt_ref[0, g] = _sigmoid(proj[:, o + g * NSA_PROJ_PAD:o + (g + 1) * NSA_PROJ_PAD])


def _gelu_tanh(x):
    return 0.5 * x * (1.0 + jnp.tanh(math.sqrt(2.0 / math.pi) * (x + 0.044715 * (x * x * x))))


def _nsa_compress_kernel(half, kc_ref, vc_ref, pos_ref, w1_ref, w2_ref, nk_ref, kco_ref, vco_ref):
    nsub = kc_ref.shape[2]

    def comp(x, i):
        ya = _dot(x + pos_ref[2 * i:2 * i + 1, :], w1_ref[i, 0:half, :])
        yb = _dot(x + pos_ref[2 * i + 1:2 * i + 2, :], w1_ref[i, half:2 * half, :])
        hid = ya + pltpu.roll(yb, nsub - 1, 0)
        return _dot(_gelu_tanh(hid), w2_ref[i])

    kcm = comp(kc_ref[0, 0], 0)
    ms = jnp.sum(kcm * kcm, axis=-1, keepdims=True) * (1.0 / NSA_HEAD_DIM)
    block_end = lax.broadcasted_iota(jnp.int32, (nsub, 1), 0) * CMP_STRIDE + CMP_LEN - 1
    kco_ref[0, 0] = (kcm * lax.rsqrt(ms + RMS_EPS) * nk_ref[...] + _pos_lanes(block_end)).astype(BF16)
    vco_ref[0, 0] = comp(vc_ref[0, 0], 1).astype(BF16)


def _pack_heads(per_head):
    low = lax.broadcasted_iota(jnp.int32, (1, NSA_QK_WIDTH), 1) < NSA_HEAD_DIM
    pairs = [jnp.where(low, per_head[i], pltpu.roll(per_head[i + 1], NSA_HEAD_DIM, 1))
             for i in range(0, len(per_head), 2)]
    return jnp.concatenate(pairs, axis=-1)


def _nsa_cmp_kernel(qt, nc, nsel, q_ref, kc_ref, vc_ref, g_ref, oc_ref, sel_ref):
    hg = NSA_HEADS_PER_GROUP
    q0 = pl.program_id(2) * qt
    q = q_ref[0, 0].reshape(hg * qt, NSA_QK_WIDTH)
    s3 = _dot_nt(q, kc_ref[0, 0]).reshape(hg, qt, nc)
    t = q0 + lax.broadcasted_iota(jnp.int32, (qt, nc), 0)
    n = lax.broadcasted_iota(jnp.int32, (qt, nc), 1)
    mask = (t >= n * CMP_STRIDE + CMP_LEN - 1)[None]
    sm = jnp.where(mask, s3, MASKED_SCORE)
    m = jnp.max(sm, axis=-1, keepdims=True)
    e = jnp.where(mask, jnp.exp2(sm - m), 0.0)
    p = e / jnp.maximum(jnp.sum(e, axis=-1, keepdims=True), 1e-30)
    oc = _dot(p.reshape(hg * qt, nc), vc_ref[0, 0])
    g = g_ref[0, 0]
    oc_ref[0] = _pack_heads([oc[hd * qt:(hd + 1) * qt] * g[:, hd:hd + 1] for hd in range(hg)])

    psum = p[0] + p[1] + p[2] + p[3]
    jo = lax.broadcasted_iota(jnp.int32, (nsel, nc), 0)
    no = lax.broadcasted_iota(jnp.int32, (nsel, nc), 1)
    ratio = SEL_LEN // CMP_STRIDE
    first = jo * ratio - (CMP_LEN // CMP_STRIDE - 1)
    overlap_t = jnp.where((no >= first) & (no < (jo + 1) * ratio) & (no < nc - 1), 1.0, 0.0).astype(BF16)
    imp_t = sum(lax.dot_general(overlap_t, part, (((1,), (1,)), ((), ())), preferred_element_type=F32)
                for part in _split_bf16(psum, 3))

    j = lax.broadcasted_iota(jnp.int32, (nsel, qt), 0)
    tt = q0 + lax.broadcasted_iota(jnp.int32, (nsel, qt), 1)
    valid = j * SEL_LEN <= tt
    qb = tt >> int(math.log2(SEL_LEN))
    forced = (j == 0) | (j == qb) | (j == qb - 1)
    score = jnp.where(valid, jnp.where(forced, SEL_BIG, imp_t), -SEL_BIG)
    jf = j.astype(F32)
    sel = jnp.zeros((nsel, qt), F32)
    for _ in range(min(SEL_TOPK, nsel)):
        mx = jnp.max(score, axis=0, keepdims=True)
        jmin = jnp.min(jnp.where(score == mx, jf, float(nsel)), axis=0, keepdims=True)
        pick = jf == jmin
        sel = jnp.where(pick, 1.0, sel)
        score = jnp.where(pick, -3e38, score)
    sel = jnp.where(valid, sel, 0.0)
    sel_ref[0, 0] = sel.T.astype(BF16)


def _nsa_attn_kernel(qt, kt, nsel, nkt, nq, tiles_ref, cnt_ref,
                     q_ref, ks_ref, vs_ref, kw_ref, vw_ref, sel_ref, g_ref, sl_ref, oc_ref, o_ref):
    hg, dk = NSA_HEADS_PER_GROUP, NSA_HEAD_DIM
    rows = hg * qt
    step = (pl.program_id(0) * NSA_KV_GROUPS + pl.program_id(1)) * nq + pl.program_id(2)
    q0 = pl.program_id(2) * qt
    lane = lax.broadcasted_iota(jnp.int32, (1, NSA_QK_WIDTH), 1)
    t_row = (q0 + lax.broadcasted_iota(jnp.int32, (qt, 1), 0)).astype(F32)
    q_heads = []
    for hd in range(hg):
        qh = q_ref[0, 0, hd]
        for i, part in enumerate(_split_bf16(-sl_ref[0, hd:hd + 1, :] * t_row, 3)):
            qh = jnp.where(lane == dk + 6 + i, part, qh)
        q_heads.append(qh)
    q = jnp.concatenate(q_heads, axis=0)
    sel = sel_ref[0, 0]
    lsel = int(math.log2(SEL_LEN))

    t_k = q0 + lax.broadcasted_iota(jnp.int32, (qt, kt), 0)
    c_k = lax.broadcasted_iota(jnp.int32, (qt, kt), 1)
    blk_of_col = lax.broadcasted_iota(jnp.int32, (nsel, kt), 0) - (lax.broadcasted_iota(jnp.int32, (nsel, kt), 1) >> lsel)

    def sel_step(i, carry, diagonal):
        m, acc = carry
        k0 = pl.multiple_of(tiles_ref[step * nkt + i] * kt, kt)
        s3 = _dot_nt(q, ks_ref[0, 0, pl.ds(k0, kt), :], BF16).reshape(hg, qt, kt)
        expand = jnp.where(blk_of_col == (k0 >> lsel), 1.0, 0.0).astype(BF16)
        mask = jnp.dot(sel, expand, preferred_element_type=F32) > 0.5
        if diagonal:
            mask = mask & (k0 + c_k <= t_k)
        sm = jnp.where(mask[None], s3, MASKED_SCORE).reshape(rows, kt)
        m_new = jnp.maximum(m, jnp.max(sm, axis=-1, keepdims=True))
        alpha = jnp.exp2((m - m_new).astype(F32))
        acc = alpha * acc + _dot(jnp.exp2(sm - m_new), vs_ref[0, 0, pl.ds(k0, kt), :])
        return m_new, acc

    init = (jnp.full((rows, 1), MASKED_SCORE, BF16), jnp.zeros((rows, NSA_QK_WIDTH), F32))
    last = cnt_ref[step] - 1
    carry = lax.fori_loop(0, last, functools.partial(sel_step, diagonal=False), init)
    _, acc_s = sel_step(last, carry, diagonal=True)
    o_s = acc_s / jnp.maximum(acc_s[:, dk:dk + 1], 1e-30)

    wk = WINDOW + qt
    ws = pl.multiple_of(jnp.maximum(q0 - WINDOW, 0), qt)
    s3 = _dot_nt(q, kw_ref[0, 0, pl.ds(ws, wk), :], BF16).reshape(hg, qt, wk)
    dist = (q0 + lax.broadcasted_iota(jnp.int32, (qt, wk), 0)) - (ws + lax.broadcasted_iota(jnp.int32, (qt, wk), 1))
    mask = ((dist >= 0) & (dist < WINDOW))[None]
    sm = jnp.where(mask, s3, MASKED_SCORE).reshape(rows, wk)
    e = jnp.exp2(sm - jnp.max(sm, axis=-1, keepdims=True))
    acc_w = _dot(e, vw_ref[0, 0, pl.ds(ws, wk), :])
    o_w = acc_w / jnp.maximum(acc_w[:, dk:dk + 1], 1e-30)

    g = g_ref[0, 0]
    per_head = [g[:, hg + hd:hg + hd + 1] * o_s[hd * qt:(hd + 1) * qt]
                + g[:, 2 * hg + hd:2 * hg + hd + 1] * o_w[hd * qt:(hd + 1) * qt] for hd in range(hg)]
    o_ref[0] = (oc_ref[0] + _pack_heads(per_head)).astype(o_ref.dtype)


def _nsa_layer(x, nw, shift, scale, gate, w_in, cmp_pos, cmp_w1, cmp_w2, qk_norm, w_out):
    b, s, d = x.shape
    t = b * s
    g_, hg, dk = NSA_KV_GROUPS, NSA_HEADS_PER_GROUP, NSA_HEAD_DIM
    qd, kd = NSA_HEADS * dk, g_ * dk
    tm = ROW_TILE
    wl = NSA_QK_WIDTH
    n_main = qd + 6 * kd
    assert w_in.shape[1] == n_main + 3 * NSA_HEADS and s % tm == 0
    gate_cols = w_in[:, n_main:].reshape(d, 3, g_, hg).transpose(0, 2, 1, 3).reshape(d, g_, 3 * hg)
    gate_cols = jnp.pad(gate_cols, ((0, 0), (0, 0), (0, NSA_PROJ_PAD - 3 * hg))).reshape(d, g_ * NSA_PROJ_PAD)
    w_pad = jnp.concatenate([w_in[:, :n_main], gate_cols], axis=1).astype(BF16)

    def head_onehot(width):
        lane_head = jnp.arange(width)[:, None] // dk
        return (lane_head == jnp.arange(V7X_LANES)[None, :]).astype(BF16)

    pq, pk = head_onehot(qd), head_onehot(kd)

    def tile_w(wv, reps):
        return jnp.tile(wv, reps).reshape(1, reps * dk)

    slopes = jnp.exp2(-8.0 * (jnp.arange(NSA_HEADS, dtype=F32) + 1) / NSA_HEADS)
    slope_parts = jnp.stack([p.astype(F32) for p in _split_bf16(slopes * LOG2_E, 3)], axis=-1)
    slope_rows = jnp.broadcast_to((slopes * LOG2_E).reshape(g_, hg, 1), (g_, hg, wl))
    q_lanes = jnp.zeros((NSA_HEADS, wl), F32).at[:, dk:dk + 6].set(jnp.concatenate([slope_parts, slope_parts], axis=-1))

    consts = [pq, pq.T, pk, pk.T, tile_w(qk_norm[0], NSA_HEADS), tile_w(qk_norm[2], g_), tile_w(qk_norm[3], g_), q_lanes]
    full2 = lambda arr: pl.BlockSpec(arr.shape, lambda i, j: (0, 0))
    mod_spec = pl.BlockSpec((1, 1, d), lambda i, j: (i, 0, 0))
    grp = lambda dt: jax.ShapeDtypeStruct((b, g_, s, wl), dt)
    grp_spec = pl.BlockSpec((1, g_, tm, wl), lambda i, j: (i, 0, j, 0))
    tokf = jax.ShapeDtypeStruct((b, s, kd), F32)
    tokf_spec = pl.BlockSpec((1, tm, kd), lambda i, j: (i, j, 0))
    q5, ks4, vs4, kw4, vw4, kc, vc, gates = pl.pallas_call(
        functools.partial(_nsa_proj_kernel, tm, qd, kd),
        out_shape=[jax.ShapeDtypeStruct((b, g_, hg, s, wl), BF16), grp(BF16), grp(BF16), grp(BF16), grp(BF16),
                   tokf, tokf, grp(F32)],
        grid=(b, s // tm),
        in_specs=[pl.BlockSpec((1, tm, d), lambda i, j: (i, j, 0)), pl.BlockSpec((1, d), lambda i, j: (0, 0)),
                  mod_spec, mod_spec, full2(w_pad)] + [full2(cst) for cst in consts],
        out_specs=[pl.BlockSpec((1, g_, hg, tm, wl), lambda i, j: (i, 0, 0, j, 0)),
                   grp_spec, grp_spec, grp_spec, grp_spec, tokf_spec, tokf_spec, grp_spec],
        compiler_params=_params("parallel", "parallel"),
        name="nsa_proj",
    )(x, nw.reshape(1, d), shift[:, None], scale[:, None], w_pad, *consts)

    nsub = s // CMP_STRIDE
    half = CMP_STRIDE * dk
    sub_rows = lambda arr: arr.reshape(b, s, g_, dk).transpose(0, 2, 1, 3).reshape(b, g_, nsub, half)
    kc3, vc3 = sub_rows(kc), sub_rows(vc)

    pos = jnp.stack([cmp_pos[0, :CMP_STRIDE].reshape(half), cmp_pos[0, CMP_STRIDE:].reshape(half),
                     cmp_pos[1, :CMP_STRIDE].reshape(half), cmp_pos[1, CMP_STRIDE:].reshape(half)])
    w1b = cmp_w1.astype(BF16)
    w2b = jnp.pad(cmp_w2, ((0, 0), (0, 0), (0, wl - dk))).astype(BF16)
    nk_lanes = jnp.pad(qk_norm[1], (0, wl - dk)).reshape(1, wl)
    cmp_in = pl.BlockSpec((1, 1, nsub, half), lambda i, j: (i, j, 0, 0))
    cmp_out = pl.BlockSpec((1, 1, nsub, wl), lambda i, j: (i, j, 0, 0))
    kcc, vcc = pl.pallas_call(
        functools.partial(_nsa_compress_kernel, half),
        out_shape=[jax.ShapeDtypeStruct((b, g_, nsub, wl), BF16)] * 2,
        grid=(b, g_),
        in_specs=[cmp_in, cmp_in, pl.BlockSpec(pos.shape, lambda i, j: (0, 0)),
                  pl.BlockSpec(w1b.shape, lambda i, j: (0, 0, 0)), pl.BlockSpec(w2b.shape, lambda i, j: (0, 0, 0)),
                  pl.BlockSpec((1, wl), lambda i, j: (0, 0))],
        out_specs=[cmp_out, cmp_out],
        compiler_params=_params("parallel", "parallel"),
        name="nsa_compress",
    )(kc3, vc3, pos, w1b, w2b, nk_lanes)

    nsel = s // SEL_LEN
    kt = min(NSA_KV_TILE, s)
    nkt = s // kt
    seq_kv = lambda n: pl.BlockSpec((1, 1, n, wl), lambda i, j, k, *_: (i, j, 0, 0))

    def tile_specs(qt):
        assert s % qt == 0 and s >= WINDOW + qt and kt % qt == 0
        return (pl.BlockSpec((1, 1, hg, qt, wl), lambda i, j, k, *_: (i, j, 0, k, 0)),
                pl.BlockSpec((1, qt, hg * dk), lambda i, j, k, *_: (i, k, j)),
                pl.BlockSpec((1, 1, qt, wl), lambda i, j, k, *_: (i, j, k, 0)),
                pl.BlockSpec((1, 1, qt, nsel), lambda i, j, k, *_: (i, j, k, 0)))

    qt = NSA_Q_TILE
    q_spec, o_spec, gate_spec, sel_spec = tile_specs(qt)
    oc, sel = pl.pallas_call(
        functools.partial(_nsa_cmp_kernel, qt, nsub, nsel),
        out_shape=[jax.ShapeDtypeStruct((b, s, qd), F32), jax.ShapeDtypeStruct((b, g_, s, nsel), BF16)],
        grid=(b, g_, s // qt),
        in_specs=[q_spec, seq_kv(nsub), seq_kv(nsub), gate_spec],
        out_specs=[o_spec, sel_spec],
        compiler_params=_params("parallel", "parallel", "parallel"),
        name="nsa_cmp_select",
    )(q5, kcc, vcc, gates)

    qt = NSA_ATTN_Q_TILE
    nq = s // qt
    q_spec, o_spec, gate_spec, sel_spec = tile_specs(qt)
    active = sel.reshape(b, g_, nq, qt, nkt, kt // SEL_LEN).max(axis=(3, 5)) > 0
    order = jnp.sort(jnp.where(active, 0, nkt) + jnp.arange(nkt, dtype=jnp.int32), axis=-1)
    tiles = (order % nkt).astype(jnp.int32).reshape(-1)
    counts = active.sum(axis=-1).astype(jnp.int32).reshape(-1)

    o3 = pl.pallas_call(
        functools.partial(_nsa_attn_kernel, qt, kt, nsel, nkt, nq),
        out_shape=jax.ShapeDtypeStruct((b, s, qd), BF16),
        grid_spec=pltpu.PrefetchScalarGridSpec(
            num_scalar_prefetch=2,
            grid=(b, g_, nq),
            in_specs=[q_spec, seq_kv(s), seq_kv(s), seq_kv(s), seq_kv(s), sel_spec, gate_spec,
                      pl.BlockSpec((1, hg, wl), lambda i, j, k, *_: (j, 0, 0)), o_spec],
            out_specs=o_spec,
        ),
        compiler_params=_params("parallel", "parallel", "arbitrary"),
        name="nsa_attention",
    )(tiles, counts, q5, ks4, vs4, kw4, vw4, sel, gates, slope_rows, oc)
    return _outproj(x.reshape(t, d), o3.reshape(t, qd), gate, w_out, s).reshape(b, s, d)


def kernel(x, c, ada_w, ada_b, norm_mix_w, norm_ffn_w, ffn_w_gate, ffn_w_up, ffn_w_down, nsa_w_in, nsa_cmp_pos, nsa_cmp_w1, nsa_cmp_w2, nsa_qk_norm, nsa_w_out, rwkv_mu, rwkv_w_rkv, rwkv_w0, rwkv_wd_a, rwkv_wd_b, rwkv_a0, rwkv_wa_a, rwkv_wa_b, rwkv_wg_a, rwkv_wg_b, rwkv_k_k, rwkv_k_a, rwkv_r_k, rwkv_ln_w, rwkv_ln_b, rwkv_w_out, pool_w, pool_b, pool_scale):
    b, s, d = x.shape
    depth = ada_w.shape[0]
    mod = _ada_mod(c, ada_w, ada_b)
    for i in range(depth):
        sh_m, sc_m, g_m, sh_f, sc_f, g_f = jnp.split(mod[i], N_MOD, axis=-1)
        kind, j = i % N_MIXERS, i // N_MIXERS
        if kind == 0:
            x = _nsa_layer(x, norm_mix_w[i], sh_m, sc_m, g_m, nsa_w_in[j], nsa_cmp_pos[j], nsa_cmp_w1[j], nsa_cmp_w2[j],
                           nsa_qk_norm[j], nsa_w_out[j])
        elif kind == 1:
            x = _rwkv_layer(x, norm_mix_w[i], sh_m, sc_m, g_m, rwkv_mu[j], rwkv_w_rkv[j], rwkv_w0[j], rwkv_wd_a[j],
                            rwkv_wd_b[j], rwkv_a0[j], rwkv_wa_a[j], rwkv_wa_b[j], rwkv_wg_a[j], rwkv_wg_b[j],
                            rwkv_k_k[j], rwkv_k_a[j], rwkv_r_k[j], rwkv_ln_w[j], rwkv_ln_b[j], rwkv_w_out[j])
        elif kind == 2:
            x = _pool_layer(x, norm_mix_w[i], sh_m, sc_m, g_m, pool_w[j], pool_b[j], pool_scale[j])
        x = _ffn(x.reshape(b * s, d), norm_ffn_w[i], sh_f, sc_f, g_f, ffn_w_gate[i], ffn_w_up[i], ffn_w_down[i], s).reshape(b, s, d)
    return x
```

```python
import functools
import math

import jax
import jax.numpy as jnp
from jax import lax
from jax.experimental import pallas as pl
from jax.experimental.pallas import tpu as pltpu

F32 = jnp.float32
BF16 = jnp.bfloat16

V7X_LANES = 128
V7X_VMEM_LIMIT_BYTES = 56 * 1024 * 1024

RMS_EPS = 1e-6
N_MOD = 6
N_MIXERS = 3

NSA_HEAD_DIM = 64
NSA_KV_GROUPS = 4
NSA_HEADS_PER_GROUP = 4
NSA_HEADS = NSA_KV_GROUPS * NSA_HEADS_PER_GROUP
CMP_LEN = 32
CMP_STRIDE = 16
SEL_LEN = 64
SEL_TOPK = 16
WINDOW = 512
NSA_SCALE = NSA_HEAD_DIM ** -0.5
LOG2_E = math.log2(math.e)
SEL_BIG = 1e9
MASKED_SCORE = -1e30
NSA_UNSELECTED = -(2.0 ** 100)

RWKV_HEAD_DIM = 64
RWKV_GN_EPS = 64e-5
RWKV_CHUNK = 64

POOL_WINDOWS = (2, 4, 8, 16)
POOL_HALO = 16

ROW_TILE = 512


def _params(*sem):
    return pltpu.CompilerParams(dimension_semantics=sem, vmem_limit_bytes=V7X_VMEM_LIMIT_BYTES)


def _sigmoid(z):
    return 1.0 / (1.0 + jnp.exp(-z))


def _normmod(x, nw, shift, scale):
    ms = jnp.mean(x * x, axis=-1, keepdims=True)
    return (x * lax.rsqrt(ms + RMS_EPS) * nw) * (1.0 + scale) + shift


def _dot(a, b):
    return jnp.dot(a.astype(BF16), b.astype(BF16), preferred_element_type=F32)


def _dot_nt(a, b, out_dtype=F32):
    out = lax.dot_general(a.astype(BF16), b.astype(BF16), (((1,), (1,)), ((), ())), preferred_element_type=F32)
    return out.astype(out_dtype)


def _dot_tn(a, b):
    return lax.dot_general(a.astype(BF16), b.astype(BF16), (((0,), (0,)), ((), ())), preferred_element_type=F32)


def _split_bf16(x, parts):
    out = []
    for _ in range(parts):
        p = x.astype(BF16)
        out.append(p)
        x = x - p.astype(F32)
    return out


def _mod_kernel(c_ref, w_ref, b_ref, o_ref):
    c = c_ref[...]
    ca = c * _sigmoid(c)
    o_ref[0] = _dot(ca, w_ref[0]) + b_ref[0]


def _ada_mod(c, ada_w, ada_b):
    depth, d, n = ada_w.shape
    b = c.shape[0]
    rows = 8
    tn = 1536
    assert b <= rows and n % tn == 0
    c_pad = jnp.zeros((rows, d), F32).at[:b].set(c)
    out = pl.pallas_call(
        _mod_kernel,
        out_shape=jax.ShapeDtypeStruct((depth, rows, n), F32),
        grid=(depth, n // tn),
        in_specs=[
            pl.BlockSpec((rows, d), lambda i, j: (0, 0)),
            pl.BlockSpec((1, d, tn), lambda i, j: (i, 0, j)),
            pl.BlockSpec((1, 1, tn), lambda i, j: (i, 0, j)),
        ],
        out_specs=pl.BlockSpec((1, rows, tn), lambda i, j: (i, 0, j)),
        compiler_params=_params("parallel", "parallel"),
        name="ada_mod",
    )(c_pad, ada_w, ada_b.reshape(depth, 1, n))
    return out[:, :b]


def _ffn_kernel(nj, x_ref, nw_ref, sh_ref, sc_ref, g_ref, wg_ref, wu_ref, wd_ref, o_ref, h_scr, acc_scr):
    j = pl.program_id(1)

    @pl.when(j == 0)
    def _():
        h_scr[...] = _normmod(x_ref[...], nw_ref[...], sh_ref[0], sc_ref[0]).astype(BF16)
        acc_scr[...] = jnp.zeros_like(acc_scr)

    h = h_scr[...]
    g = jnp.dot(h, wg_ref[...], preferred_element_type=F32)
    u = jnp.dot(h, wu_ref[...], preferred_element_type=F32)
    a = g * _sigmoid(g) * u
    acc_scr[...] += jnp.dot(a.astype(BF16), wd_ref[...], preferred_element_type=F32)

    @pl.when(j == nj - 1)
    def _():
        o_ref[...] = x_ref[...] + g_ref[0] * acc_scr[...]


def _ffn(x2, nw, shift, scale, gate, wg, wu, wd, seq):
    t, d = x2.shape
    f = wg.shape[1]
    tm = ROW_TILE
    fc = f // 2
    assert t % tm == 0 and seq % tm == 0 and fc % V7X_LANES == 0
    tpb = seq // tm
    nj = f // fc
    mod_spec = pl.BlockSpec((1, 1, d), lambda i, j: (i // tpb, 0, 0))
    return pl.pallas_call(
        functools.partial(_ffn_kernel, nj),
        out_shape=jax.ShapeDtypeStruct((t, d), F32),
        grid=(t // tm, nj),
        in_specs=[
            pl.BlockSpec((tm, d), lambda i, j: (i, 0)),
            pl.BlockSpec((1, d), lambda i, j: (0, 0)),
            mod_spec, mod_spec, mod_spec,
            pl.BlockSpec((d, fc), lambda i, j: (0, j)),
            pl.BlockSpec((d, fc), lambda i, j: (0, j)),
            pl.BlockSpec((fc, d), lambda i, j: (j, 0)),
        ],
        out_specs=pl.BlockSpec((tm, d), lambda i, j: (i, 0)),
        scratch_shapes=[pltpu.VMEM((tm, d), BF16), pltpu.VMEM((tm, d), F32)],
        compiler_params=_params("parallel", "arbitrary"),
        name="ffn",
    )(x2, nw.reshape(1, d), shift[:, None], scale[:, None], gate[:, None],
      wg.astype(BF16), wu.astype(BF16), wd.astype(BF16))


def _outproj_kernel(x_ref, z_ref, g_ref, w_ref, o_ref):
    o_ref[...] = x_ref[...] + g_ref[0] * _dot(z_ref[...], w_ref[...])


def _outproj(x2, z2, gate, w, seq):
    t, d = x2.shape
    k = z2.shape[1]
    tm = ROW_TILE
    tpb = seq // tm
    return pl.pallas_call(
        _outproj_kernel,
        out_shape=jax.ShapeDtypeStruct((t, d), F32),
        grid=(t // tm,),
        in_specs=[
            pl.BlockSpec((tm, d), lambda i: (i, 0)),
            pl.BlockSpec((tm, k), lambda i: (i, 0)),
            pl.BlockSpec((1, 1, d), lambda i: (i // tpb, 0, 0)),
            pl.BlockSpec((k, d), lambda i: (0, 0)),
        ],
        out_specs=pl.BlockSpec((tm, d), lambda i: (i, 0)),
        compiler_params=_params("parallel"),
        name="outproj",
    )(x2, z2, gate[:, None], w.astype(BF16))


def _pool_kernel(tm, gd, x_ref, nw_ref, sh_ref, sc_ref, g_ref, pw_ref, pb_ref, ps_ref, o_ref, ext_scr):
    s = pl.program_id(1)
    x = x_ref[0]
    h = _normmod(x, nw_ref[...], sh_ref[0], sc_ref[0])

    @pl.when(s == 0)
    def _():
        ext_scr[0:POOL_HALO, :] = jnp.zeros((POOL_HALO, x.shape[1]), F32)

    ext_scr[POOL_HALO:POOL_HALO + tm, :] = h
    row = s * tm + lax.broadcasted_iota(jnp.int32, (tm, 1), 0)
    ys = []
    for gi, win in enumerate(POOL_WINDOWS):
        lanes = slice(gi * gd, (gi + 1) * gd)
        hg = h[:, lanes]
        acc = hg
        for k in range(1, win):
            acc = acc + ext_scr[POOL_HALO - k:POOL_HALO - k + tm, lanes]
        cnt = jnp.minimum(row + 1, win).astype(F32)
        ys.append(_dot(acc / cnt - hg, pw_ref[gi]))
    y = (jnp.concatenate(ys, axis=-1) + pb_ref[...]) * ps_ref[...]
    o_ref[0] = x + g_ref[0] * y
    ext_scr[0:POOL_HALO, :] = ext_scr[tm:tm + POOL_HALO, :]


def _pool_layer(x, nw, shift, scale, gate, pw, pb, ps):
    b, s, d = x.shape
    tm = ROW_TILE
    gd = d // len(POOL_WINDOWS)
    assert s % tm == 0 and max(POOL_WINDOWS) <= POOL_HALO
    mod_spec = pl.BlockSpec((1, 1, d), lambda i, j: (i, 0, 0))
    vec_spec = pl.BlockSpec((1, d), lambda i, j: (0, 0))
    return pl.pallas_call(
        functools.partial(_pool_kernel, tm, gd),
        out_shape=jax.ShapeDtypeStruct((b, s, d), F32),
        grid=(b, s // tm),
        in_specs=[
            pl.BlockSpec((1, tm, d), lambda i, j: (i, j, 0)),
            vec_spec, mod_spec, mod_spec, mod_spec,
            pl.BlockSpec(pw.shape, lambda i, j: (0, 0, 0)),
            vec_spec, vec_spec,
        ],
        out_specs=pl.BlockSpec((1, tm, d), lambda i, j: (i, j, 0)),
        scratch_shapes=[pltpu.VMEM((tm + POOL_HALO, d), F32)],
        compiler_params=_params("arbitrary", "arbitrary"),
        name="pool_mixer",
    )(x, nw.reshape(1, d), shift[:, None], scale[:, None], gate[:, None],
      pw.astype(BF16), pb.reshape(1, d), ps.reshape(1, d))


def _rwkv_proj_kernel(tm, x_ref, nw_ref, sh_ref, sc_ref, mu_ref, wr_ref, wk_ref, wv_ref,
                      w0_ref, wda_ref, wdb_ref, a0_ref, waa_ref, wab_ref, wga_ref, wgb_ref,
                      r_ref, k_ref, v_ref, lw_ref, a_ref, g_ref, ext_scr):
    s = pl.program_id(1)
    h = _normmod(x_ref[0], nw_ref[...], sh_ref[0], sc_ref[0])

    @pl.when(s == 0)
    def _():
        ext_scr[0:8, :] = jnp.zeros((8, h.shape[1]), F32)

    ext_scr[8:8 + tm, :] = h
    xx = ext_scr[7:7 + tm, :] - h

    def mix(i):
        return h + xx * mu_ref[i:i + 1, :]

    r_ref[0] = _dot(mix(0), wr_ref[...])
    k_ref[0] = _dot(mix(2), wk_ref[...])
    v_ref[0] = _dot(mix(3), wv_ref[...])
    dw = w0_ref[...] + _dot(jnp.tanh(_dot(mix(1), wda_ref[...])), wdb_ref[...])
    softplus_neg = jnp.maximum(-dw, 0.0) + jnp.log(1.0 + jnp.exp(-jnp.abs(dw)))
    lw_ref[0] = -jnp.exp(-softplus_neg - 0.5)
    a_ref[0] = _sigmoid(a0_ref[...] + _dot(_dot(mix(4), waa_ref[...]), wab_ref[...]))
    g_ref[0] = _dot(_sigmoid(_dot(mix(5), wga_ref[...])), wgb_ref[...])
    ext_scr[0:8, :] = ext_scr[tm:tm + 8, :]


def _rwkv_scan_kernel(ct, r_ref, k_ref, v_ref, lw_ref, a_ref, g_ref, kk_ref, ka_ref, rk_ref, lnw_ref, lnb_ref,
                      z_ref, s_scr):
    c, n = RWKV_CHUNK, RWKV_HEAD_DIM
    w = 2 * n
    shift = int(math.log2(n))

    @pl.when(pl.program_id(2) == 0)
    def _():
        s_scr[...] = jnp.zeros_like(s_scr)

    r, k, v, lw, a = r_ref[0], k_ref[0], v_ref[0], lw_ref[0], a_ref[0]
    head0 = lax.broadcasted_iota(jnp.int32, (1, w), 1) < n
    ri = lax.broadcasted_iota(jnp.int32, (w, w), 0)
    ci = lax.broadcasted_iota(jnp.int32, (w, w), 1)
    same_head = jnp.where((ri >> shift) == (ci >> shift), 1.0, 0.0).astype(BF16)
    strict, incl, eye = ri > ci, ri >= ci, ri == ci
    rt = lax.broadcasted_iota(jnp.int32, (ct, ct), 0)
    cc = lax.broadcasted_iota(jnp.int32, (ct, ct), 1)
    cshift = int(math.log2(c))
    chunk_tri = jnp.where(((rt >> cshift) == (cc >> cshift)) & (rt >= cc), 1.0, 0.0).astype(BF16)

    def head_sum(xf):
        return sum(jnp.dot(p, same_head, preferred_element_type=F32) for p in _split_bf16(xf, 2))

    kk = k * kk_ref[...]
    kkn = kk * lax.rsqrt(jnp.maximum(head_sum(kk * kk), 1e-24))
    k2 = k * (1.0 + (a - 1.0) * ka_ref[...])
    beta = kkn * a
    cum = sum(jnp.dot(chunk_tri, p, preferred_element_type=F32) for p in _split_bf16(lw, 3))
    at_all = -kkn * jnp.exp(cum - lw)
    rt_all = r * jnp.exp(cum)
    einv = jnp.exp(-cum)
    kb_all = beta * einv
    kq_all = k2 * einv

    nb = ct // c

    def stack(xf):
        x3 = xf.reshape(nb, c, w)
        return jnp.concatenate([jnp.where(head0, x3, 0.0), jnp.where(head0, 0.0, x3)], axis=1)

    def bmm(x, y):
        return jnp.einsum('bij,bjk->bik', x.astype(BF16), y.astype(BF16), preferred_element_type=F32)

    def bmm_nt(x, y):
        return jnp.einsum('bik,bjk->bij', x.astype(BF16), y.astype(BF16), preferred_element_type=F32)

    at, rt_, kb, kq, vs = stack(at_all), stack(rt_all), stack(kb_all), stack(kq_all), stack(v)
    pc = jnp.exp(cum.reshape(nb, c, w)[:, c - 1:c, :])
    kbe, kqe = kb * pc, kq * pc
    tt = bmm_nt(jnp.concatenate([at, rt_], axis=1), jnp.concatenate([kb, kq], axis=1))
    a_m = jnp.where(strict, tt[:, :w, :w], 0.0)
    b_m = jnp.where(strict, tt[:, :w, w:], 0.0)
    ar_m = jnp.where(incl, tt[:, w:, :w], 0.0)
    br_m = jnp.where(incl, tt[:, w:, w:], 0.0)
    rsum = jnp.where(eye, 1.0, 0.0) + a_m
    pw = bmm(a_m, a_m)
    for _ in range(cshift - 2):
        both = bmm(pw, jnp.concatenate([rsum, pw], axis=2))
        rsum = rsum + both[:, :, :w]
        pw = both[:, :, w:]
    tinv = rsum + bmm(pw, rsum)
    w12 = bmm(tinv, jnp.concatenate([at, bmm(b_m, vs)], axis=2))
    xmat = jnp.concatenate([w12, jnp.concatenate([jnp.zeros_like(vs), vs], axis=2)], axis=1)
    kbe_t = jnp.stack([kbe[i].T for i in range(nb)])
    kqe_t = jnp.stack([kqe[i].T for i in range(nb)])
    lhs = jnp.concatenate([jnp.concatenate([ar_m, br_m], axis=2), jnp.concatenate([kbe_t, kqe_t], axis=2)], axis=1)
    res = bmm(lhs, xmat)
    g_m = rt_ + res[:, :w, :w]
    y0 = res[:, :w, w:]
    mt = jnp.where(eye, jnp.broadcast_to(pc, (nb, w, w)), 0.0) + res[:, w:, :w]
    nt = res[:, w:, w:]

    state_t = s_scr[...]
    ys = []
    for idx in range(nb):
        step = _dot(jnp.concatenate([g_m[idx], mt[idx]], axis=0), state_t)
        yst = step[:w] + y0[idx]
        state_t = step[w:] + nt[idx]
        ys.append(yst[:c] + yst[c:])
    s_scr[...] = state_t
    y = jnp.concatenate(ys, axis=0)

    mean = head_sum(y) * (1.0 / n)
    dlt = y - mean
    var = head_sum(dlt * dlt) * (1.0 / n)
    yn = dlt * lax.rsqrt(var + RWKV_GN_EPS) * lnw_ref[...] + lnb_ref[...]
    bonus = head_sum(r * k2 * rk_ref[...]) * v
    z_ref[0] = ((yn + bonus) * g_ref[0]).astype(z_ref.dtype)


def _rwkv_layer(x, nw, shift, scale, gate, mu, w_rkv, w0, wd_a, wd_b, a0, wa_a, wa_b, wg_a, wg_b,
                k_k, k_a, r_k, ln_w, ln_b, w_out):
    b, s, d = x.shape
    tm = ROW_TILE
    assert s % tm == 0
    mod_spec = pl.BlockSpec((1, 1, d), lambda i, j: (i, 0, 0))
    vec_spec = pl.BlockSpec((1, d), lambda i, j: (0, 0))
    tok_spec = pl.BlockSpec((1, tm, d), lambda i, j: (i, j, 0))

    def full(arr):
        return pl.BlockSpec(arr.shape, lambda i, j: (0,) * arr.ndim)

    weights = [w_rkv[0].astype(BF16), w_rkv[1].astype(BF16), w_rkv[2].astype(BF16),
               w0.reshape(1, d), wd_a.astype(BF16), wd_b.astype(BF16),
               a0.reshape(1, d), wa_a.astype(BF16), wa_b.astype(BF16),
               wg_a.astype(BF16), wg_b.astype(BF16)]
    tok_shape = jax.ShapeDtypeStruct((b, s, d), F32)
    r, k, v, lw, a, g = pl.pallas_call(
        functools.partial(_rwkv_proj_kernel, tm),
        out_shape=[tok_shape] * 6,
        grid=(b, s // tm),
        in_specs=[tok_spec, vec_spec, mod_spec, mod_spec, full(mu)] + [full(wt) for wt in weights],
        out_specs=[tok_spec] * 6,
        scratch_shapes=[pltpu.VMEM((tm + 8, d), F32)],
        compiler_params=_params("arbitrary", "arbitrary"),
        name="rwkv_proj",
    )(x, nw.reshape(1, d), shift[:, None], scale[:, None], mu, *weights)

    ct = 512
    pair = 2 * RWKV_HEAD_DIM
    assert s % ct == 0 and d % pair == 0 and pair == V7X_LANES
    seq_spec = pl.BlockSpec((1, ct, pair), lambda i, p, j: (i, j, p))
    par_spec = pl.BlockSpec((1, pair), lambda i, p, j: (0, p))
    z = pl.pallas_call(
        functools.partial(_rwkv_scan_kernel, ct),
        out_shape=jax.ShapeDtypeStruct((b, s, d), BF16),
        grid=(b, d // pair, s // ct),
        in_specs=[seq_spec] * 6 + [par_spec] * 5,
        out_specs=seq_spec,
        scratch_shapes=[pltpu.VMEM((pair, pair), F32)],
        compiler_params=_params("parallel", "parallel", "arbitrary"),
        name="rwkv_scan",
    )(r, k, v, lw, a, g, k_k.reshape(1, d), k_a.reshape(1, d), r_k.reshape(1, d), ln_w.reshape(1, d), ln_b.reshape(1, d))
    return _outproj(x.reshape(b * s, d), z.reshape(b * s, d), gate, w_out, s).reshape(b, s, d)


NSA_Q_TILE = 128
NSA_ATTN_Q_TILE = 256
NSA_KV_TILE = 512
NSA_PROJ_PAD = 128
NSA_QK_WIDTH = 128


def _head_rms(xf, p_ref, pt_ref, wvec):
    sums = sum(jnp.dot(part, p_ref[...], preferred_element_type=F32) for part in _split_bf16(xf * xf, 2))
    inv = lax.rsqrt(sums * (1.0 / NSA_HEAD_DIM) + RMS_EPS)
    inv_full = sum(jnp.dot(part, pt_ref[...], preferred_element_type=F32) for part in _split_bf16(inv, 3))
    return xf * inv_full * wvec


def _pos_lanes(pos):
    lane = lax.broadcasted_iota(jnp.int32, (1, NSA_QK_WIDTH), 1)
    hi = ((pos >> 6) << 6).astype(F32)
    lo = (pos & 63).astype(F32)
    d = NSA_HEAD_DIM
    ones = jnp.where((lane >= d + 6) & (lane < d + 9), 1.0, 0.0)
    return jnp.where((lane >= d) & (lane < d + 3), hi, jnp.where((lane >= d + 3) & (lane < d + 6), lo, ones))


def _nsa_proj_kernel(tm, qd, kd, x_ref, nw_ref, sh_ref, sc_ref, w_ref, pq_ref, pqt_ref, pk_ref, pkt_ref,
                     nq_ref, nks_ref, nkw_ref, qf_ref,
                     q_ref, ks_ref, vs_ref, kw_ref, vw_ref, kc_ref, vc_ref, gt_ref):
    hg, dk = NSA_HEADS_PER_GROUP, NSA_HEAD_DIM
    h = _normmod(x_ref[0], nw_ref[...], sh_ref[0], sc_ref[0])
    proj = _dot(h, w_ref[...])
    low = lax.broadcasted_iota(jnp.int32, (1, NSA_QK_WIDTH), 1) < dk
    key_lanes = _pos_lanes(pl.program_id(1) * tm + lax.broadcasted_iota(jnp.int32, (tm, 1), 0))

    def heads(x, nheads):
        for pair in range(nheads // 2):
            xp = x[:, pair * 2 * dk:(pair + 1) * 2 * dk]
            yield 2 * pair, xp
            yield 2 * pair + 1, pltpu.roll(xp, dk, 1)

    one_lane = lax.broadcasted_iota(jnp.int32, (1, NSA_QK_WIDTH), 1) == dk
    qn = _head_rms(proj[:, :qd], pq_ref, pqt_ref, nq_ref[...]) * (NSA_SCALE * LOG2_E)
    for hd, xs in heads(qn, NSA_HEADS):
        q_ref[0, hd // hg, hd % hg] = jnp.where(low, xs, qf_ref[hd:hd + 1, :]).astype(BF16)
    o = qd
    kc_ref[0] = proj[:, o:o + kd]
    vc_ref[0] = proj[:, o + kd:o + 2 * kd]
    for g, xs in heads(_head_rms(proj[:, o + 2 * kd:o + 3 * kd], pk_ref, pkt_ref, nks_ref[...]), NSA_KV_GROUPS):
        ks_ref[0, g] = jnp.where(low, xs, key_lanes).astype(BF16)
    for g, xs in heads(proj[:, o + 3 * kd:o + 4 * kd], NSA_KV_GROUPS):
        vs_ref[0, g] = jnp.where(one_lane, 1.0, xs).astype(BF16)
    for g, xs in heads(_head_rms(proj[:, o + 4 * kd:o + 5 * kd], pk_ref, pkt_ref, nkw_ref[...]), NSA_KV_GROUPS):
        kw_ref[0, g] = jnp.where(low, xs, key_lanes).astype(BF16)
    for g, xs in heads(proj[:, o + 5 * kd:o + 6 * kd], NSA_KV_GROUPS):
        vw_ref[0, g] = jnp.where(one_lane, 1.0, xs).astype(BF16)
    o += 6 * kd
    for g in range(NSA_KV_GROUPS):
        gt_ref[0, g] = _sigmoid(proj[:, o + g * NSA_PROJ_PAD:o + (g + 1) * NSA_PROJ_PAD])


def _gelu_tanh(x):
    return 0.5 * x * (1.0 + jnp.tanh(math.sqrt(2.0 / math.pi) * (x + 0.044715 * (x * x * x))))


def _nsa_compress_kernel(half, kc_ref, vc_ref, pos_ref, w1_ref, w2_ref, nk_ref, kco_ref, vco_ref):
    nsub = kc_ref.shape[2]

    def comp(x, i):
        ya = _dot(x + pos_ref[2 * i:2 * i + 1, :], w1_ref[i, 0:half, :])
        yb = _dot(x + pos_ref[2 * i + 1:2 * i + 2, :], w1_ref[i, half:2 * half, :])
        hid = ya + pltpu.roll(yb, nsub - 1, 0)
        return _dot(_gelu_tanh(hid), w2_ref[i])

    kcm = comp(kc_ref[0, 0], 0)
    ms = jnp.sum(kcm * kcm, axis=-1, keepdims=True) * (1.0 / NSA_HEAD_DIM)
    block_end = lax.broadcasted_iota(jnp.int32, (nsub, 1), 0) * CMP_STRIDE + CMP_LEN - 1
    kco_ref[0, 0] = (kcm * lax.rsqrt(ms + RMS_EPS) * nk_ref[...] + _pos_lanes(block_end)).astype(BF16)
    vco_ref[0, 0] = comp(vc_ref[0, 0], 1).astype(BF16)


def _pack_heads(per_head):
    low = lax.broadcasted_iota(jnp.int32, (1, NSA_QK_WIDTH), 1) < NSA_HEAD_DIM
    pairs = [jnp.where(low, per_head[i], pltpu.roll(per_head[i + 1], NSA_HEAD_DIM, 1))
             for i in range(0, len(per_head), 2)]
    return jnp.concatenate(pairs, axis=-1)


def _nsa_cmp_kernel(qt, nc, nsel, q_ref, kc_ref, vc_ref, g_ref, oc_ref, sel_ref):
    hg = NSA_HEADS_PER_GROUP
    q0 = pl.program_id(2) * qt
    q = q_ref[0, 0].reshape(hg * qt, NSA_QK_WIDTH)
    s3 = _dot_nt(q, kc_ref[0, 0]).reshape(hg, qt, nc)
    t = q0 + lax.broadcasted_iota(jnp.int32, (qt, nc), 0)
    n = lax.broadcasted_iota(jnp.int32, (qt, nc), 1)
    mask = (t >= n * CMP_STRIDE + CMP_LEN - 1)[None]
    sm = jnp.where(mask, s3, MASKED_SCORE)
    m = jnp.max(sm, axis=-1, keepdims=True)
    e = jnp.where(mask, jnp.exp2(sm - m), 0.0)
    p = e / jnp.maximum(jnp.sum(e, axis=-1, keepdims=True), 1e-30)
    oc = _dot(p.reshape(hg * qt, nc), vc_ref[0, 0])
    g = g_ref[0, 0]
    oc_ref[0] = _pack_heads([oc[hd * qt:(hd + 1) * qt] * g[:, hd:hd + 1] for hd in range(hg)])

    psum = p[0] + p[1] + p[2] + p[3]
    jo = lax.broadcasted_iota(jnp.int32, (nsel, nc), 0)
    no = lax.broadcasted_iota(jnp.int32, (nsel, nc), 1)
    ratio = SEL_LEN // CMP_STRIDE
    first = jo * ratio - (CMP_LEN // CMP_STRIDE - 1)
    overlap_t = jnp.where((no >= first) & (no < (jo + 1) * ratio) & (no < nc - 1), 1.0, 0.0).astype(BF16)
    imp_t = sum(lax.dot_general(overlap_t, part, (((1,), (1,)), ((), ())), preferred_element_type=F32)
                for part in _split_bf16(psum, 3))

    j = lax.broadcasted_iota(jnp.int32, (nsel, qt), 0)
    tt = q0 + lax.broadcasted_iota(jnp.int32, (nsel, qt), 1)
    valid = j * SEL_LEN <= tt
    qb = tt >> int(math.log2(SEL_LEN))
    forced = (j == 0) | (j == qb) | (j == qb - 1)
    score = jnp.where(valid, jnp.where(forced, SEL_BIG, imp_t), -SEL_BIG)
    jf = j.astype(F32)
    sel = jnp.zeros((nsel, qt), F32)
    for _ in range(min(SEL_TOPK, nsel)):
        mx = jnp.max(score, axis=0, keepdims=True)
        jmin = jnp.min(jnp.where(score == mx, jf, float(nsel)), axis=0, keepdims=True)
        pick = jf == jmin
        sel = jnp.where(pick, 1.0, sel)
        score = jnp.where(pick, -3e38, score)
    sel = jnp.where(valid, sel, 0.0)
    sel_ref[0, 0] = sel.T.astype(BF16)


def _nsa_attn_kernel(qt, kt, nsel, nkt, nq, tiles_ref, cnt_ref,
                     q_ref, ks_ref, vs_ref, kw_ref, vw_ref, oh_ref, sel_ref, g_ref, sl_ref, oc_ref, o_ref):
    hg, dk = NSA_HEADS_PER_GROUP, NSA_HEAD_DIM
    rows = hg * qt
    step = (pl.program_id(0) * NSA_KV_GROUPS + pl.program_id(1)) * nq + pl.program_id(2)
    q0 = pl.program_id(2) * qt
    lane = lax.broadcasted_iota(jnp.int32, (1, NSA_QK_WIDTH), 1)
    t_row = (q0 + lax.broadcasted_iota(jnp.int32, (qt, 1), 0)).astype(F32)
    q_heads = []
    for hd in range(hg):
        qh = q_ref[0, 0, hd]
        for i, part in enumerate(_split_bf16(-sl_ref[0, hd:hd + 1, :] * t_row, 3)):
            qh = jnp.where(lane == dk + 6 + i, part, qh)
        q_heads.append(qh)
    q = jnp.concatenate(q_heads, axis=0)
    unsel = (1.0 - sel_ref[0, 0]) * NSA_UNSELECTED
    if nsel < NSA_QK_WIDTH:
        unsel = jnp.concatenate([unsel, jnp.zeros((qt, NSA_QK_WIDTH - nsel), BF16)], axis=1)
    q_sel = jnp.concatenate([q, jnp.concatenate([unsel] * hg, axis=0)], axis=1)

    t_k = q0 + lax.broadcasted_iota(jnp.int32, (qt, kt), 0)
    c_k = lax.broadcasted_iota(jnp.int32, (qt, kt), 1)

    def sel_step(i, carry, diagonal):
        m, acc = carry
        k0 = pl.multiple_of(tiles_ref[step * nkt + i] * kt, kt)
        keys = jnp.concatenate([ks_ref[0, 0, pl.ds(k0, kt), :], oh_ref[pl.ds(k0, kt), :]], axis=1)
        sm = _dot_nt(q_sel, keys, BF16)
        if diagonal:
            causal = (k0 + c_k <= t_k)[None]
            sm = jnp.where(causal, sm.reshape(hg, qt, kt), MASKED_SCORE).reshape(rows, kt)
        m_new = jnp.maximum(m, jnp.max(sm, axis=-1, keepdims=True))
        alpha = jnp.exp2((m - m_new).astype(F32))
        acc = alpha * acc + _dot(jnp.exp2(sm - m_new), vs_ref[0, 0, pl.ds(k0, kt), :])
        return m_new, acc

    init = (jnp.full((rows, 1), MASKED_SCORE, BF16), jnp.zeros((rows, NSA_QK_WIDTH), F32))
    last = cnt_ref[step] - 1
    carry = lax.fori_loop(0, last, functools.partial(sel_step, diagonal=False), init)
    _, acc_s = sel_step(last, carry, diagonal=True)
    o_s = acc_s / jnp.maximum(acc_s[:, dk:dk + 1], 1e-30)

    wk = WINDOW + qt
    ws = pl.multiple_of(jnp.maximum(q0 - WINDOW, 0), qt)
    s3 = _dot_nt(q, kw_ref[0, 0, pl.ds(ws, wk), :], BF16).reshape(hg, qt, wk)
    dist = (q0 + lax.broadcasted_iota(jnp.int32, (qt, wk), 0)) - (ws + lax.broadcasted_iota(jnp.int32, (qt, wk), 1))
    mask = ((dist >= 0) & (dist < WINDOW))[None]
    sm = jnp.where(mask, s3, MASKED_SCORE).reshape(rows, wk)
    e = jnp.exp2(sm - jnp.max(sm, axis=-1, keepdims=True))
    acc_w = _dot(e, vw_ref[0, 0, pl.ds(ws, wk), :])
    o_w = acc_w / jnp.maximum(acc_w[:, dk:dk + 1], 1e-30)

    g = g_ref[0, 0]
    per_head = [g[:, hg + hd:hg + hd + 1] * o_s[hd * qt:(hd + 1) * qt]
                + g[:, 2 * hg + hd:2 * hg + hd + 1] * o_w[hd * qt:(hd + 1) * qt] for hd in range(hg)]
    o_ref[0] = (oc_ref[0] + _pack_heads(per_head)).astype(o_ref.dtype)


def _nsa_layer(x, nw, shift, scale, gate, w_in, cmp_pos, cmp_w1, cmp_w2, qk_norm, w_out):
    b, s, d = x.shape
    t = b * s
    g_, hg, dk = NSA_KV_GROUPS, NSA_HEADS_PER_GROUP, NSA_HEAD_DIM
    qd, kd = NSA_HEADS * dk, g_ * dk
    tm = ROW_TILE
    wl = NSA_QK_WIDTH
    n_main = qd + 6 * kd
    assert w_in.shape[1] == n_main + 3 * NSA_HEADS and s % tm == 0
    gate_cols = w_in[:, n_main:].reshape(d, 3, g_, hg).transpose(0, 2, 1, 3).reshape(d, g_, 3 * hg)
    gate_cols = jnp.pad(gate_cols, ((0, 0), (0, 0), (0, NSA_PROJ_PAD - 3 * hg))).reshape(d, g_ * NSA_PROJ_PAD)
    w_pad = jnp.concatenate([w_in[:, :n_main], gate_cols], axis=1).astype(BF16)

    def head_onehot(width):
        lane_head = jnp.arange(width)[:, None] // dk
        return (lane_head == jnp.arange(V7X_LANES)[None, :]).astype(BF16)

    pq, pk = head_onehot(qd), head_onehot(kd)

    def tile_w(wv, reps):
        return jnp.tile(wv, reps).reshape(1, reps * dk)

    slopes = jnp.exp2(-8.0 * (jnp.arange(NSA_HEADS, dtype=F32) + 1) / NSA_HEADS)
    slope_parts = jnp.stack([p.astype(F32) for p in _split_bf16(slopes * LOG2_E, 3)], axis=-1)
    slope_rows = jnp.broadcast_to((slopes * LOG2_E).reshape(g_, hg, 1), (g_, hg, wl))
    q_lanes = jnp.zeros((NSA_HEADS, wl), F32).at[:, dk:dk + 6].set(jnp.concatenate([slope_parts, slope_parts], axis=-1))

    consts = [pq, pq.T, pk, pk.T, tile_w(qk_norm[0], NSA_HEADS), tile_w(qk_norm[2], g_), tile_w(qk_norm[3], g_), q_lanes]
    full2 = lambda arr: pl.BlockSpec(arr.shape, lambda i, j: (0, 0))
    mod_spec = pl.BlockSpec((1, 1, d), lambda i, j: (i, 0, 0))
    grp = lambda dt: jax.ShapeDtypeStruct((b, g_, s, wl), dt)
    grp_spec = pl.BlockSpec((1, g_, tm, wl), lambda i, j: (i, 0, j, 0))
    tokf = jax.ShapeDtypeStruct((b, s, kd), F32)
    tokf_spec = pl.BlockSpec((1, tm, kd), lambda i, j: (i, j, 0))
    q5, ks4, vs4, kw4, vw4, kc, vc, gates = pl.pallas_call(
        functools.partial(_nsa_proj_kernel, tm, qd, kd),
        out_shape=[jax.ShapeDtypeStruct((b, g_, hg, s, wl), BF16), grp(BF16), grp(BF16), grp(BF16), grp(BF16),
                   tokf, tokf, grp(F32)],
        grid=(b, s // tm),
        in_specs=[pl.BlockSpec((1, tm, d), lambda i, j: (i, j, 0)), pl.BlockSpec((1, d), lambda i, j: (0, 0)),
                  mod_spec, mod_spec, full2(w_pad)] + [full2(cst) for cst in consts],
        out_specs=[pl.BlockSpec((1, g_, hg, tm, wl), lambda i, j: (i, 0, 0, j, 0)),
                   grp_spec, grp_spec, grp_spec, grp_spec, tokf_spec, tokf_spec, grp_spec],
        compiler_params=_params("parallel", "parallel"),
        name="nsa_proj",
    )(x, nw.reshape(1, d), shift[:, None], scale[:, None], w_pad, *consts)

    nsub = s // CMP_STRIDE
    half = CMP_STRIDE * dk
    sub_rows = lambda arr: arr.reshape(b, s, g_, dk).transpose(0, 2, 1, 3).reshape(b, g_, nsub, half)
    kc3, vc3 = sub_rows(kc), sub_rows(vc)

    pos = jnp.stack([cmp_pos[0, :CMP_STRIDE].reshape(half), cmp_pos[0, CMP_STRIDE:].reshape(half),
                     cmp_pos[1, :CMP_STRIDE].reshape(half), cmp_pos[1, CMP_STRIDE:].reshape(half)])
    w1b = cmp_w1.astype(BF16)
    w2b = jnp.pad(cmp_w2, ((0, 0), (0, 0), (0, wl - dk))).astype(BF16)
    nk_lanes = jnp.pad(qk_norm[1], (0, wl - dk)).reshape(1, wl)
    cmp_in = pl.BlockSpec((1, 1, nsub, half), lambda i, j: (i, j, 0, 0))
    cmp_out = pl.BlockSpec((1, 1, nsub, wl), lambda i, j: (i, j, 0, 0))
    kcc, vcc = pl.pallas_call(
        functools.partial(_nsa_compress_kernel, half),
        out_shape=[jax.ShapeDtypeStruct((b, g_, nsub, wl), BF16)] * 2,
        grid=(b, g_),
        in_specs=[cmp_in, cmp_in, pl.BlockSpec(pos.shape, lambda i, j: (0, 0)),
                  pl.BlockSpec(w1b.shape, lambda i, j: (0, 0, 0)), pl.BlockSpec(w2b.shape, lambda i, j: (0, 0, 0)),
                  pl.BlockSpec((1, wl), lambda i, j: (0, 0))],
        out_specs=[cmp_out, cmp_out],
        compiler_params=_params("parallel", "parallel"),
        name="nsa_compress",
    )(kc3, vc3, pos, w1b, w2b, nk_lanes)

    nsel = s // SEL_LEN
    kt = min(NSA_KV_TILE, s)
    nkt = s // kt
    seq_kv = lambda n: pl.BlockSpec((1, 1, n, wl), lambda i, j, k, *_: (i, j, 0, 0))

    def tile_specs(qt):
        assert s % qt == 0 and s >= WINDOW + qt and kt % qt == 0
        return (pl.BlockSpec((1, 1, hg, qt, wl), lambda i, j, k, *_: (i, j, 0, k, 0)),
                pl.BlockSpec((1, qt, hg * dk), lambda i, j, k, *_: (i, k, j)),
                pl.BlockSpec((1, 1, qt, wl), lambda i, j, k, *_: (i, j, k, 0)),
                pl.BlockSpec((1, 1, qt, nsel), lambda i, j, k, *_: (i, j, k, 0)))

    qt = NSA_Q_TILE
    q_spec, o_spec, gate_spec, sel_spec = tile_specs(qt)
    oc, sel = pl.pallas_call(
        functools.partial(_nsa_cmp_kernel, qt, nsub, nsel),
        out_shape=[jax.ShapeDtypeStruct((b, s, qd), F32), jax.ShapeDtypeStruct((b, g_, s, nsel), BF16)],
        grid=(b, g_, s // qt),
        in_specs=[q_spec, seq_kv(nsub), seq_kv(nsub), gate_spec],
        out_specs=[o_spec, sel_spec],
        compiler_params=_params("parallel", "parallel", "parallel"),
        name="nsa_cmp_select",
    )(q5, kcc, vcc, gates)

    assert nsel <= wl
    block_onehot = (jnp.arange(s)[:, None] // SEL_LEN == jnp.arange(wl)[None, :]).astype(BF16)

    qt = NSA_ATTN_Q_TILE
    nq = s // qt
    q_spec, o_spec, gate_spec, sel_spec = tile_specs(qt)
    active = sel.reshape(b, g_, nq, qt, nkt, kt // SEL_LEN).max(axis=(3, 5)) > 0
    order = jnp.sort(jnp.where(active, 0, nkt) + jnp.arange(nkt, dtype=jnp.int32), axis=-1)
    tiles = (order % nkt).astype(jnp.int32).reshape(-1)
    counts = active.sum(axis=-1).astype(jnp.int32).reshape(-1)

    o3 = pl.pallas_call(
        functools.partial(_nsa_attn_kernel, qt, kt, nsel, nkt, nq),
        out_shape=jax.ShapeDtypeStruct((b, s, qd), BF16),
        grid_spec=pltpu.PrefetchScalarGridSpec(
            num_scalar_prefetch=2,
            grid=(b, g_, nq),
            in_specs=[q_spec, seq_kv(s), seq_kv(s), seq_kv(s), seq_kv(s),
                      pl.BlockSpec((s, wl), lambda i, j, k, *_: (0, 0)), sel_spec, gate_spec,
                      pl.BlockSpec((1, hg, wl), lambda i, j, k, *_: (j, 0, 0)), o_spec],
            out_specs=o_spec,
        ),
        compiler_params=_params("parallel", "parallel", "arbitrary"),
        name="nsa_attention",
    )(tiles, counts, q5, ks4, vs4, kw4, vw4, block_onehot, sel, gates, slope_rows, oc)
    return _outproj(x.reshape(t, d), o3.reshape(t, qd), gate, w_out, s).reshape(b, s, d)


def kernel(x, c, ada_w, ada_b, norm_mix_w, norm_ffn_w, ffn_w_gate, ffn_w_up, ffn_w_down, nsa_w_in, nsa_cmp_pos, nsa_cmp_w1, nsa_cmp_w2, nsa_qk_norm, nsa_w_out, rwkv_mu, rwkv_w_rkv, rwkv_w0, rwkv_wd_a, rwkv_wd_b, rwkv_a0, rwkv_wa_a, rwkv_wa_b, rwkv_wg_a, rwkv_wg_b, rwkv_k_k, rwkv_k_a, rwkv_r_k, rwkv_ln_w, rwkv_ln_b, rwkv_w_out, pool_w, pool_b, pool_scale):
    b, s, d = x.shape
    depth = ada_w.shape[0]
    mod = _ada_mod(c, ada_w, ada_b)
    for i in range(depth):
        sh_m, sc_m, g_m, sh_f, sc_f, g_f = jnp.split(mod[i], N_MOD, axis=-1)
        kind, j = i % N_MIXERS, i // N_MIXERS
        if kind == 0:
            x = _nsa_layer(x, norm_mix_w[i], sh_m, sc_m, g_m, nsa_w_in[j], nsa_cmp_pos[j], nsa_cmp_w1[j], nsa_cmp_w2[j],
                           nsa_qk_norm[j], nsa_w_out[j])
        elif kind == 1:
            x = _rwkv_layer(x, norm_mix_w[i], sh_m, sc_m, g_m, rwkv_mu[j], rwkv_w_rkv[j], rwkv_w0[j], rwkv_wd_a[j],
                            rwkv_wd_b[j], rwkv_a0[j], rwkv_wa_a[j], rwkv_wa_b[j], rwkv_wg_a[j], rwkv_wg_b[j],
                            rwkv_k_k[j], rwkv_k_a[j], rwkv_r_k[j], rwkv_ln_w[j], rwkv_ln_b[j], rwkv_w_out[j])
        elif kind == 2:
            x = _pool_layer(x, norm_mix_w[i], sh_m, sc_m, g_m, pool_w[j], pool_b[j], pool_scale[j])
        x = _ffn(x.reshape(b * s, d), norm_ffn_w[i], sh_f, sc_f, g_f, ffn_w_gate[i], ffn_w_up[i], ffn_w_down[i], s).reshape(b, s, d)
    return x
```

```python
import functools
import math

import jax
import jax.numpy as jnp
from jax import lax
from jax.experimental import pallas as pl
from jax.experimental.pallas import tpu as pltpu

F32 = jnp.float32
BF16 = jnp.bfloat16

V7X_LANES = 128
V7X_VMEM_LIMIT_BYTES = 56 * 1024 * 1024

RMS_EPS = 1e-6
N_MOD = 6
N_MIXERS = 3

NSA_HEAD_DIM = 64
NSA_KV_GROUPS = 4
NSA_HEADS_PER_GROUP = 4
NSA_HEADS = NSA_KV_GROUPS * NSA_HEADS_PER_GROUP
CMP_LEN = 32
CMP_STRIDE = 16
SEL_LEN = 64
SEL_TOPK = 16
WINDOW = 512
NSA_SCALE = NSA_HEAD_DIM ** -0.5
LOG2_E = math.log2(math.e)
SEL_BIG = 1e9
MASKED_SCORE = -1e30
NSA_UNSELECTED = -(2.0 ** 100)

RWKV_HEAD_DIM = 64
RWKV_GN_EPS = 64e-5
RWKV_CHUNK = 64

POOL_WINDOWS = (2, 4, 8, 16)
POOL_HALO = 16

ROW_TILE = 512


def _params(*sem):
    return pltpu.CompilerParams(dimension_semantics=sem, vmem_limit_bytes=V7X_VMEM_LIMIT_BYTES)


def _sigmoid(z):
    return 1.0 / (1.0 + jnp.exp(-z))


def _normmod(x, nw, shift, scale):
    ms = jnp.mean(x * x, axis=-1, keepdims=True)
    return (x * lax.rsqrt(ms + RMS_EPS) * nw) * (1.0 + scale) + shift


def _dot(a, b):
    return jnp.dot(a.astype(BF16), b.astype(BF16), preferred_element_type=F32)


def _dot_nt(a, b, out_dtype=F32):
    out = lax.dot_general(a.astype(BF16), b.astype(BF16), (((1,), (1,)), ((), ())), preferred_element_type=F32)
    return out.astype(out_dtype)


def _dot_tn(a, b):
    return lax.dot_general(a.astype(BF16), b.astype(BF16), (((0,), (0,)), ((), ())), preferred_element_type=F32)


def _split_bf16(x, parts):
    out = []
    for _ in range(parts):
        p = x.astype(BF16)
        out.append(p)
        x = x - p.astype(F32)
    return out


def _mod_kernel(c_ref, w_ref, b_ref, o_ref):
    c = c_ref[...]
    ca = c * _sigmoid(c)
    o_ref[0] = _dot(ca, w_ref[0]) + b_ref[0]


def _ada_mod(c, ada_w, ada_b):
    depth, d, n = ada_w.shape
    b = c.shape[0]
    rows = 8
    tn = 1536
    assert b <= rows and n % tn == 0
    c_pad = jnp.zeros((rows, d), F32).at[:b].set(c)
    out = pl.pallas_call(
        _mod_kernel,
        out_shape=jax.ShapeDtypeStruct((depth, rows, n), F32),
        grid=(depth, n // tn),
        in_specs=[
            pl.BlockSpec((rows, d), lambda i, j: (0, 0)),
            pl.BlockSpec((1, d, tn), lambda i, j: (i, 0, j)),
            pl.BlockSpec((1, 1, tn), lambda i, j: (i, 0, j)),
        ],
        out_specs=pl.BlockSpec((1, rows, tn), lambda i, j: (i, 0, j)),
        compiler_params=_params("parallel", "parallel"),
        name="ada_mod",
    )(c_pad, ada_w, ada_b.reshape(depth, 1, n))
    return out[:, :b]


def _ffn_kernel(nj, x_ref, nw_ref, sh_ref, sc_ref, g_ref, wg_ref, wu_ref, wd_ref, o_ref, h_scr, acc_scr):
    j = pl.program_id(1)

    @pl.when(j == 0)
    def _():
        h_scr[...] = _normmod(x_ref[...], nw_ref[...], sh_ref[0], sc_ref[0]).astype(BF16)
        acc_scr[...] = jnp.zeros_like(acc_scr)

    h = h_scr[...]
    g = jnp.dot(h, wg_ref[...], preferred_element_type=F32)
    u = jnp.dot(h, wu_ref[...], preferred_element_type=F32)
    a = g * _sigmoid(g) * u
    acc_scr[...] += jnp.dot(a.astype(BF16), wd_ref[...], preferred_element_type=F32)

    @pl.when(j == nj - 1)
    def _():
        o_ref[...] = x_ref[...] + g_ref[0] * acc_scr[...]


def _ffn(x2, nw, shift, scale, gate, wg, wu, wd, seq):
    t, d = x2.shape
    f = wg.shape[1]
    tm = ROW_TILE
    fc = f // 2
    assert t % tm == 0 and seq % tm == 0 and fc % V7X_LANES == 0
    tpb = seq // tm
    nj = f // fc
    mod_spec = pl.BlockSpec((1, 1, d), lambda i, j: (i // tpb, 0, 0))
    return pl.pallas_call(
        functools.partial(_ffn_kernel, nj),
        out_shape=jax.ShapeDtypeStruct((t, d), F32),
        grid=(t // tm, nj),
        in_specs=[
            pl.BlockSpec((tm, d), lambda i, j: (i, 0)),
            pl.BlockSpec((1, d), lambda i, j: (0, 0)),
            mod_spec, mod_spec, mod_spec,
            pl.BlockSpec((d, fc), lambda i, j: (0, j)),
            pl.BlockSpec((d, fc), lambda i, j: (0, j)),
            pl.BlockSpec((fc, d), lambda i, j: (j, 0)),
        ],
        out_specs=pl.BlockSpec((tm, d), lambda i, j: (i, 0)),
        scratch_shapes=[pltpu.VMEM((tm, d), BF16), pltpu.VMEM((tm, d), F32)],
        compiler_params=_params("parallel", "arbitrary"),
        name="ffn",
    )(x2, nw.reshape(1, d), shift[:, None], scale[:, None], gate[:, None],
      wg.astype(BF16), wu.astype(BF16), wd.astype(BF16))


def _outproj_kernel(x_ref, z_ref, g_ref, w_ref, o_ref):
    o_ref[...] = x_ref[...] + g_ref[0] * _dot(z_ref[...], w_ref[...])


def _outproj(x2, z2, gate, w, seq):
    t, d = x2.shape
    k = z2.shape[1]
    tm = ROW_TILE
    tpb = seq // tm
    return pl.pallas_call(
        _outproj_kernel,
        out_shape=jax.ShapeDtypeStruct((t, d), F32),
        grid=(t // tm,),
        in_specs=[
            pl.BlockSpec((tm, d), lambda i: (i, 0)),
            pl.BlockSpec((tm, k), lambda i: (i, 0)),
            pl.BlockSpec((1, 1, d), lambda i: (i // tpb, 0, 0)),
            pl.BlockSpec((k, d), lambda i: (0, 0)),
        ],
        out_specs=pl.BlockSpec((tm, d), lambda i: (i, 0)),
        compiler_params=_params("parallel"),
        name="outproj",
    )(x2, z2, gate[:, None], w.astype(BF16))


def _pool_kernel(tm, gd, x_ref, nw_ref, sh_ref, sc_ref, g_ref, pw_ref, pb_ref, ps_ref, o_ref, ext_scr):
    s = pl.program_id(1)
    x = x_ref[0]
    h = _normmod(x, nw_ref[...], sh_ref[0], sc_ref[0])

    @pl.when(s == 0)
    def _():
        ext_scr[0:POOL_HALO, :] = jnp.zeros((POOL_HALO, x.shape[1]), F32)

    ext_scr[POOL_HALO:POOL_HALO + tm, :] = h
    row = s * tm + lax.broadcasted_iota(jnp.int32, (tm, 1), 0)
    ys = []
    for gi, win in enumerate(POOL_WINDOWS):
        lanes = slice(gi * gd, (gi + 1) * gd)
        hg = h[:, lanes]
        acc = hg
        for k in range(1, win):
            acc = acc + ext_scr[POOL_HALO - k:POOL_HALO - k + tm, lanes]
        cnt = jnp.minimum(row + 1, win).astype(F32)
        ys.append(_dot(acc / cnt - hg, pw_ref[gi]))
    y = (jnp.concatenate(ys, axis=-1) + pb_ref[...]) * ps_ref[...]
    o_ref[0] = x + g_ref[0] * y
    ext_scr[0:POOL_HALO, :] = ext_scr[tm:tm + POOL_HALO, :]


def _pool_layer(x, nw, shift, scale, gate, pw, pb, ps):
    b, s, d = x.shape
    tm = ROW_TILE
    gd = d // len(POOL_WINDOWS)
    assert s % tm == 0 and max(POOL_WINDOWS) <= POOL_HALO
    mod_spec = pl.BlockSpec((1, 1, d), lambda i, j: (i, 0, 0))
    vec_spec = pl.BlockSpec((1, d), lambda i, j: (0, 0))
    return pl.pallas_call(
        functools.partial(_pool_kernel, tm, gd),
        out_shape=jax.ShapeDtypeStruct((b, s, d), F32),
        grid=(b, s // tm),
        in_specs=[
            pl.BlockSpec((1, tm, d), lambda i, j: (i, j, 0)),
            vec_spec, mod_spec, mod_spec, mod_spec,
            pl.BlockSpec(pw.shape, lambda i, j: (0, 0, 0)),
            vec_spec, vec_spec,
        ],
        out_specs=pl.BlockSpec((1, tm, d), lambda i, j: (i, j, 0)),
        scratch_shapes=[pltpu.VMEM((tm + POOL_HALO, d), F32)],
        compiler_params=_params("arbitrary", "arbitrary"),
        name="pool_mixer",
    )(x, nw.reshape(1, d), shift[:, None], scale[:, None], gate[:, None],
      pw.astype(BF16), pb.reshape(1, d), ps.reshape(1, d))


def _rwkv_proj_kernel(tm, x_ref, nw_ref, sh_ref, sc_ref, mu_ref, wr_ref, wk_ref, wv_ref,
                      w0_ref, wda_ref, wdb_ref, a0_ref, waa_ref, wab_ref, wga_ref, wgb_ref,
                      r_ref, k_ref, v_ref, lw_ref, a_ref, g_ref, ext_scr):
    s = pl.program_id(1)
    h = _normmod(x_ref[0], nw_ref[...], sh_ref[0], sc_ref[0])

    @pl.when(s == 0)
    def _():
        ext_scr[0:8, :] = jnp.zeros((8, h.shape[1]), F32)

    ext_scr[8:8 + tm, :] = h
    xx = ext_scr[7:7 + tm, :] - h

    def mix(i):
        return h + xx * mu_ref[i:i + 1, :]

    r_ref[0] = _dot(mix(0), wr_ref[...])
    k_ref[0] = _dot(mix(2), wk_ref[...])
    v_ref[0] = _dot(mix(3), wv_ref[...])
    dw = w0_ref[...] + _dot(jnp.tanh(_dot(mix(1), wda_ref[...])), wdb_ref[...])
    softplus_neg = jnp.maximum(-dw, 0.0) + jnp.log(1.0 + jnp.exp(-jnp.abs(dw)))
    lw_ref[0] = -jnp.exp(-softplus_neg - 0.5)
    a_ref[0] = _sigmoid(a0_ref[...] + _dot(_dot(mix(4), waa_ref[...]), wab_ref[...]))
    g_ref[0] = _dot(_sigmoid(_dot(mix(5), wga_ref[...])), wgb_ref[...])
    ext_scr[0:8, :] = ext_scr[tm:tm + 8, :]


def _rwkv_scan_kernel(ct, npair, r_ref, k_ref, v_ref, lw_ref, a_ref, g_ref, kk_ref, ka_ref, rk_ref, lnw_ref, lnb_ref,
                      z_ref, s_scr):
    c, n = RWKV_CHUNK, RWKV_HEAD_DIM
    w = 2 * n
    shift = int(math.log2(n))

    @pl.when(pl.program_id(2) == 0)
    def _():
        s_scr[...] = jnp.zeros_like(s_scr)

    head0 = lax.broadcasted_iota(jnp.int32, (1, w), 1) < n
    ri = lax.broadcasted_iota(jnp.int32, (w, w), 0)
    ci = lax.broadcasted_iota(jnp.int32, (w, w), 1)
    same_head = jnp.where((ri >> shift) == (ci >> shift), 1.0, 0.0).astype(BF16)
    strict, incl, eye = ri > ci, ri >= ci, ri == ci
    rt = lax.broadcasted_iota(jnp.int32, (ct, ct), 0)
    cc = lax.broadcasted_iota(jnp.int32, (ct, ct), 1)
    cshift = int(math.log2(c))
    chunk_tri = jnp.where(((rt >> cshift) == (cc >> cshift)) & (rt >= cc), 1.0, 0.0).astype(BF16)
    nb = ct // c

    def head_sum(xf):
        return sum(jnp.dot(p, same_head, preferred_element_type=F32) for p in _split_bf16(xf, 2))

    def stack(xf):
        x3 = xf.reshape(nb, c, w)
        return jnp.concatenate([jnp.where(head0, x3, 0.0), jnp.where(head0, 0.0, x3)], axis=1)

    def bmm(x, y):
        return jnp.einsum('bij,bjk->bik', x.astype(BF16), y.astype(BF16), preferred_element_type=F32)

    def bmm_nt(x, y):
        return jnp.einsum('bik,bjk->bij', x.astype(BF16), y.astype(BF16), preferred_element_type=F32)

    pairs = []
    stacked = []
    for hp in range(npair):
        ln = slice(hp * w, (hp + 1) * w)
        r, k, v, lw, a = r_ref[0, :, ln], k_ref[0, :, ln], v_ref[0, :, ln], lw_ref[0, :, ln], a_ref[0, :, ln]
        kk = k * kk_ref[:, ln]
        kkn = kk * lax.rsqrt(jnp.maximum(head_sum(kk * kk), 1e-24))
        k2 = k * (1.0 + (a - 1.0) * ka_ref[:, ln])
        cum = sum(jnp.dot(chunk_tri, p, preferred_element_type=F32) for p in _split_bf16(lw, 3))
        einv = jnp.exp(-cum)
        pairs.append((r, k2, v))
        stacked.append((stack(-kkn * jnp.exp(cum - lw)), stack(r * jnp.exp(cum)), stack(kkn * a * einv),
                        stack(k2 * einv), stack(v), jnp.exp(cum.reshape(nb, c, w)[:, c - 1:c, :])))
    at, rt_, kb, kq, vs, pc = (jnp.concatenate([st[i] for st in stacked], axis=0) for i in range(6))
    nbt = npair * nb
    kbe, kqe = kb * pc, kq * pc
    tt = bmm_nt(jnp.concatenate([at, rt_], axis=1), jnp.concatenate([kb, kq], axis=1))
    a_m = jnp.where(strict, tt[:, :w, :w], 0.0)
    b_m = jnp.where(strict, tt[:, :w, w:], 0.0)
    ar_m = jnp.where(incl, tt[:, w:, :w], 0.0)
    br_m = jnp.where(incl, tt[:, w:, w:], 0.0)
    rsum = jnp.where(eye, 1.0, 0.0) + a_m
    pw = bmm(a_m, a_m)
    for _ in range(cshift - 2):
        both = bmm(pw, jnp.concatenate([rsum, pw], axis=2))
        rsum = rsum + both[:, :, :w]
        pw = both[:, :, w:]
    tinv = rsum + bmm(pw, rsum)
    w12 = bmm(tinv, jnp.concatenate([at, bmm(b_m, vs)], axis=2))
    xmat = jnp.concatenate([w12, jnp.concatenate([jnp.zeros_like(vs), vs], axis=2)], axis=1)
    kbe_t = jnp.stack([kbe[i].T for i in range(nbt)])
    kqe_t = jnp.stack([kqe[i].T for i in range(nbt)])
    lhs = jnp.concatenate([jnp.concatenate([ar_m, br_m], axis=2), jnp.concatenate([kbe_t, kqe_t], axis=2)], axis=1)
    res = bmm(lhs, xmat)
    g_m = rt_ + res[:, :w, :w]
    y0 = res[:, :w, w:]
    mt = jnp.where(eye, jnp.broadcast_to(pc, (nbt, w, w)), 0.0) + res[:, w:, :w]
    nt = res[:, w:, w:]

    states = [s_scr[hp] for hp in range(npair)]
    ys = [[] for _ in range(npair)]
    for idx in range(nb):
        for hp in range(npair):
            bi = hp * nb + idx
            step = _dot(jnp.concatenate([g_m[bi], mt[bi]], axis=0), states[hp])
            yst = step[:w] + y0[bi]
            states[hp] = step[w:] + nt[bi]
            ys[hp].append(yst[:c] + yst[c:])
    for hp in range(npair):
        ln = slice(hp * w, (hp + 1) * w)
        s_scr[hp] = states[hp]
        y = jnp.concatenate(ys[hp], axis=0)
        r, k2, v = pairs[hp]
        mean = head_sum(y) * (1.0 / n)
        dlt = y - mean
        var = head_sum(dlt * dlt) * (1.0 / n)
        yn = dlt * lax.rsqrt(var + RWKV_GN_EPS) * lnw_ref[:, ln] + lnb_ref[:, ln]
        bonus = head_sum(r * k2 * rk_ref[:, ln]) * v
        z_ref[0, :, ln] = ((yn + bonus) * g_ref[0, :, ln]).astype(z_ref.dtype)


def _rwkv_layer(x, nw, shift, scale, gate, mu, w_rkv, w0, wd_a, wd_b, a0, wa_a, wa_b, wg_a, wg_b,
                k_k, k_a, r_k, ln_w, ln_b, w_out):
    b, s, d = x.shape
    tm = ROW_TILE
    assert s % tm == 0
    mod_spec = pl.BlockSpec((1, 1, d), lambda i, j: (i, 0, 0))
    vec_spec = pl.BlockSpec((1, d), lambda i, j: (0, 0))
    tok_spec = pl.BlockSpec((1, tm, d), lambda i, j: (i, j, 0))

    def full(arr):
        return pl.BlockSpec(arr.shape, lambda i, j: (0,) * arr.ndim)

    weights = [w_rkv[0].astype(BF16), w_rkv[1].astype(BF16), w_rkv[2].astype(BF16),
               w0.reshape(1, d), wd_a.astype(BF16), wd_b.astype(BF16),
               a0.reshape(1, d), wa_a.astype(BF16), wa_b.astype(BF16),
               wg_a.astype(BF16), wg_b.astype(BF16)]
    tok_shape = jax.ShapeDtypeStruct((b, s, d), F32)
    r, k, v, lw, a, g = pl.pallas_call(
        functools.partial(_rwkv_proj_kernel, tm),
        out_shape=[tok_shape] * 6,
        grid=(b, s // tm),
        in_specs=[tok_spec, vec_spec, mod_spec, mod_spec, full(mu)] + [full(wt) for wt in weights],
        out_specs=[tok_spec] * 6,
        scratch_shapes=[pltpu.VMEM((tm + 8, d), F32)],
        compiler_params=_params("arbitrary", "arbitrary"),
        name="rwkv_proj",
    )(x, nw.reshape(1, d), shift[:, None], scale[:, None], mu, *weights)

    ct = 512
    npair = 2
    pair = 2 * RWKV_HEAD_DIM
    lanes = npair * pair
    assert s % ct == 0 and d % lanes == 0 and pair == V7X_LANES
    seq_spec = pl.BlockSpec((1, ct, lanes), lambda i, p, j: (i, j, p))
    par_spec = pl.BlockSpec((1, lanes), lambda i, p, j: (0, p))
    z = pl.pallas_call(
        functools.partial(_rwkv_scan_kernel, ct, npair),
        out_shape=jax.ShapeDtypeStruct((b, s, d), BF16),
        grid=(b, d // lanes, s // ct),
        in_specs=[seq_spec] * 6 + [par_spec] * 5,
        out_specs=seq_spec,
        scratch_shapes=[pltpu.VMEM((npair, pair, pair), F32)],
        compiler_params=_params("parallel", "parallel", "arbitrary"),
        name="rwkv_scan",
    )(r, k, v, lw, a, g, k_k.reshape(1, d), k_a.reshape(1, d), r_k.reshape(1, d), ln_w.reshape(1, d), ln_b.reshape(1, d))
    return _outproj(x.reshape(b * s, d), z.reshape(b * s, d), gate, w_out, s).reshape(b, s, d)


NSA_Q_TILE = 128
NSA_ATTN_Q_TILE = 256
NSA_KV_TILE = 512
NSA_PROJ_PAD = 128
NSA_QK_WIDTH = 128


def _head_rms(xf, p_ref, pt_ref, wvec):
    sums = sum(jnp.dot(part, p_ref[...], preferred_element_type=F32) for part in _split_bf16(xf * xf, 2))
    inv = lax.rsqrt(sums * (1.0 / NSA_HEAD_DIM) + RMS_EPS)
    inv_full = sum(jnp.dot(part, pt_ref[...], preferred_element_type=F32) for part in _split_bf16(inv, 3))
    return xf * inv_full * wvec


def _pos_lanes(pos):
    lane = lax.broadcasted_iota(jnp.int32, (1, NSA_QK_WIDTH), 1)
    hi = ((pos >> 6) << 6).astype(F32)
    lo = (pos & 63).astype(F32)
    d = NSA_HEAD_DIM
    ones = jnp.where((lane >= d + 6) & (lane < d + 9), 1.0, 0.0)
    return jnp.where((lane >= d) & (lane < d + 3), hi, jnp.where((lane >= d + 3) & (lane < d + 6), lo, ones))


def _nsa_proj_kernel(tm, qd, kd, x_ref, nw_ref, sh_ref, sc_ref, w_ref, pq_ref, pqt_ref, pk_ref, pkt_ref,
                     nq_ref, nks_ref, nkw_ref, qf_ref,
                     q_ref, ks_ref, vs_ref, kw_ref, vw_ref, kc_ref, vc_ref, gt_ref):
    hg, dk = NSA_HEADS_PER_GROUP, NSA_HEAD_DIM
    h = _normmod(x_ref[0], nw_ref[...], sh_ref[0], sc_ref[0])
    proj = _dot(h, w_ref[...])
    low = lax.broadcasted_iota(jnp.int32, (1, NSA_QK_WIDTH), 1) < dk
    key_lanes = _pos_lanes(pl.program_id(1) * tm + lax.broadcasted_iota(jnp.int32, (tm, 1), 0))

    def heads(x, nheads):
        for pair in range(nheads // 2):
            xp = x[:, pair * 2 * dk:(pair + 1) * 2 * dk]
            yield 2 * pair, xp
            yield 2 * pair + 1, pltpu.roll(xp, dk, 1)

    one_lane = lax.broadcasted_iota(jnp.int32, (1, NSA_QK_WIDTH), 1) == dk
    qn = _head_rms(proj[:, :qd], pq_ref, pqt_ref, nq_ref[...]) * (NSA_SCALE * LOG2_E)
    for hd, xs in heads(qn, NSA_HEADS):
        q_ref[0, hd // hg, hd % hg] = jnp.where(low, xs, qf_ref[hd:hd + 1, :]).astype(BF16)
    o = qd
    kc_ref[0] = proj[:, o:o + kd]
    vc_ref[0] = proj[:, o + kd:o + 2 * kd]
    for g, xs in heads(_head_rms(proj[:, o + 2 * kd:o + 3 * kd], pk_ref, pkt_ref, nks_ref[...]), NSA_KV_GROUPS):
        ks_ref[0, g] = jnp.where(low, xs, key_lanes).astype(BF16)
    for g, xs in heads(proj[:, o + 3 * kd:o + 4 * kd], NSA_KV_GROUPS):
        vs_ref[0, g] = jnp.where(one_lane, 1.0, xs).astype(BF16)
    for g, xs in heads(_head_rms(proj[:, o + 4 * kd:o + 5 * kd], pk_ref, pkt_ref, nkw_ref[...]), NSA_KV_GROUPS):
        kw_ref[0, g] = jnp.where(low, xs, key_lanes).astype(BF16)
    for g, xs in heads(proj[:, o + 5 * kd:o + 6 * kd], NSA_KV_GROUPS):
        vw_ref[0, g] = jnp.where(one_lane, 1.0, xs).astype(BF16)
    o += 6 * kd
    for g in range(NSA_KV_GROUPS):
        gt_ref[0, g] = _sigmoid(proj[:, o + g * NSA_PROJ_PAD:o + (g + 1) * NSA_PROJ_PAD])


def _gelu_tanh(x):
    return 0.5 * x * (1.0 + jnp.tanh(math.sqrt(2.0 / math.pi) * (x + 0.044715 * (x * x * x))))


def _nsa_compress_kernel(half, kc_ref, vc_ref, pos_ref, w1_ref, w2_ref, nk_ref, kco_ref, vco_ref):
    nsub = kc_ref.shape[2]

    def comp(x, i):
        ya = _dot(x + pos_ref[2 * i:2 * i + 1, :], w1_ref[i, 0:half, :])
        yb = _dot(x + pos_ref[2 * i + 1:2 * i + 2, :], w1_ref[i, half:2 * half, :])
        hid = ya + pltpu.roll(yb, nsub - 1, 0)
        return _dot(_gelu_tanh(hid), w2_ref[i])

    kcm = comp(kc_ref[0, 0], 0)
    ms = jnp.sum(kcm * kcm, axis=-1, keepdims=True) * (1.0 / NSA_HEAD_DIM)
    block_end = lax.broadcasted_iota(jnp.int32, (nsub, 1), 0) * CMP_STRIDE + CMP_LEN - 1
    kco_ref[0, 0] = (kcm * lax.rsqrt(ms + RMS_EPS) * nk_ref[...] + _pos_lanes(block_end)).astype(BF16)
    vco_ref[0, 0] = comp(vc_ref[0, 0], 1).astype(BF16)


def _pack_heads(per_head):
    low = lax.broadcasted_iota(jnp.int32, (1, NSA_QK_WIDTH), 1) < NSA_HEAD_DIM
    pairs = [jnp.where(low, per_head[i], pltpu.roll(per_head[i + 1], NSA_HEAD_DIM, 1))
             for i in range(0, len(per_head), 2)]
    return jnp.concatenate(pairs, axis=-1)


def _nsa_cmp_kernel(qt, nc, nsel, q_ref, kc_ref, vc_ref, g_ref, oc_ref, sel_ref):
    hg = NSA_HEADS_PER_GROUP
    q0 = pl.program_id(2) * qt
    q = q_ref[0, 0].reshape(hg * qt, NSA_QK_WIDTH)
    s3 = _dot_nt(q, kc_ref[0, 0]).reshape(hg, qt, nc)
    t = q0 + lax.broadcasted_iota(jnp.int32, (qt, nc), 0)
    n = lax.broadcasted_iota(jnp.int32, (qt, nc), 1)
    mask = (t >= n * CMP_STRIDE + CMP_LEN - 1)[None]
    sm = jnp.where(mask, s3, MASKED_SCORE)
    e = jnp.exp2(sm - jnp.max(sm, axis=-1, keepdims=True))
    has_key = (q0 + lax.broadcasted_iota(jnp.int32, (1, qt, 1), 1)) >= CMP_LEN - 1
    p = e * jnp.where(has_key, 1.0 / jnp.maximum(jnp.sum(e, axis=-1, keepdims=True), 1e-30), 0.0)
    oc = _dot(p.reshape(hg * qt, nc), vc_ref[0, 0])
    g = g_ref[0, 0]
    oc_ref[0] = _pack_heads([oc[hd * qt:(hd + 1) * qt] * g[:, hd:hd + 1] for hd in range(hg)])

    psum = p[0] + p[1] + p[2] + p[3]
    jo = lax.broadcasted_iota(jnp.int32, (nsel, nc), 0)
    no = lax.broadcasted_iota(jnp.int32, (nsel, nc), 1)
    ratio = SEL_LEN // CMP_STRIDE
    first = jo * ratio - (CMP_LEN // CMP_STRIDE - 1)
    overlap_t = jnp.where((no >= first) & (no < (jo + 1) * ratio) & (no < nc - 1), 1.0, 0.0).astype(BF16)
    imp_t = sum(lax.dot_general(overlap_t, part, (((1,), (1,)), ((), ())), preferred_element_type=F32)
                for part in _split_bf16(psum, 3))

    j = lax.broadcasted_iota(jnp.int32, (nsel, qt), 0)
    tt = q0 + lax.broadcasted_iota(jnp.int32, (nsel, qt), 1)
    valid = j * SEL_LEN <= tt
    qb = tt >> int(math.log2(SEL_LEN))
    forced = (j == 0) | (j == qb) | (j == qb - 1)
    score = jnp.where(valid, jnp.where(forced, SEL_BIG, imp_t), -SEL_BIG)
    jf = j.astype(F32)
    picked = -3e38
    for _ in range(min(SEL_TOPK, nsel)):
        mx = jnp.max(score, axis=0, keepdims=True)
        jmin = jnp.min(jnp.where(score == mx, jf, float(nsel)), axis=0, keepdims=True)
        score = jnp.where(jf == jmin, picked, score)
    sel = jnp.where((score == picked) & valid, 1.0, 0.0)
    sel_ref[0, 0] = sel.T.astype(BF16)


def _nsa_attn_kernel(qt, kt, nsel, nkt, nq, tiles_ref, cnt_ref,
                     q_ref, ks_ref, vs_ref, kw_ref, vw_ref, oh_ref, sel_ref, g_ref, sl_ref, oc_ref, o_ref):
    hg, dk = NSA_HEADS_PER_GROUP, NSA_HEAD_DIM
    rows = hg * qt
    step = (pl.program_id(0) * NSA_KV_GROUPS + pl.program_id(1)) * nq + pl.program_id(2)
    q0 = pl.program_id(2) * qt
    lane = lax.broadcasted_iota(jnp.int32, (1, NSA_QK_WIDTH), 1)
    t_row = (q0 + lax.broadcasted_iota(jnp.int32, (qt, 1), 0)).astype(F32)
    q_heads = []
    for hd in range(hg):
        qh = q_ref[0, 0, hd]
        for i, part in enumerate(_split_bf16(-sl_ref[0, hd:hd + 1, :] * t_row, 3)):
            qh = jnp.where(lane == dk + 6 + i, part, qh)
        q_heads.append(qh)
    q = jnp.concatenate(q_heads, axis=0)
    unsel = (1.0 - sel_ref[0, 0]) * NSA_UNSELECTED
    if nsel < NSA_QK_WIDTH:
        unsel = jnp.concatenate([unsel, jnp.zeros((qt, NSA_QK_WIDTH - nsel), BF16)], axis=1)
    q_sel = jnp.concatenate([q, jnp.concatenate([unsel] * hg, axis=0)], axis=1)

    t_k = q0 + lax.broadcasted_iota(jnp.int32, (qt, kt), 0)
    c_k = lax.broadcasted_iota(jnp.int32, (qt, kt), 1)

    def sel_step(i, carry, diagonal):
        m, acc = carry
        k0 = pl.multiple_of(tiles_ref[step * nkt + i] * kt, kt)
        keys = jnp.concatenate([ks_ref[0, 0, pl.ds(k0, kt), :], oh_ref[pl.ds(k0, kt), :]], axis=1)
        sm = _dot_nt(q_sel, keys, BF16)
        if diagonal:
            causal = (k0 + c_k <= t_k)[None]
            sm = jnp.where(causal, sm.reshape(hg, qt, kt), MASKED_SCORE).reshape(rows, kt)
        m_new = jnp.maximum(m, jnp.max(sm, axis=-1, keepdims=True))
        alpha = jnp.exp2((m - m_new).astype(F32))
        acc = alpha * acc + _dot(jnp.exp2(sm - m_new), vs_ref[0, 0, pl.ds(k0, kt), :])
        return m_new, acc

    init = (jnp.full((rows, 1), MASKED_SCORE, BF16), jnp.zeros((rows, NSA_QK_WIDTH), F32))
    last = cnt_ref[step] - 1
    carry = lax.fori_loop(0, last, functools.partial(sel_step, diagonal=False), init)
    _, acc_s = sel_step(last, carry, diagonal=True)
    o_s = acc_s / jnp.maximum(acc_s[:, dk:dk + 1], 1e-30)

    wk = WINDOW + qt
    ws = pl.multiple_of(jnp.maximum(q0 - WINDOW, 0), qt)
    s3 = _dot_nt(q, kw_ref[0, 0, pl.ds(ws, wk), :], BF16).reshape(hg, qt, wk)
    dist = (q0 + lax.broadcasted_iota(jnp.int32, (qt, wk), 0)) - (ws + lax.broadcasted_iota(jnp.int32, (qt, wk), 1))
    mask = ((dist >= 0) & (dist < WINDOW))[None]
    sm = jnp.where(mask, s3, MASKED_SCORE).reshape(rows, wk)
    e = jnp.exp2(sm - jnp.max(sm, axis=-1, keepdims=True))
    acc_w = _dot(e, vw_ref[0, 0, pl.ds(ws, wk), :])
    o_w = acc_w / jnp.maximum(acc_w[:, dk:dk + 1], 1e-30)

    g = g_ref[0, 0]
    per_head = [g[:, hg + hd:hg + hd + 1] * o_s[hd * qt:(hd + 1) * qt]
                + g[:, 2 * hg + hd:2 * hg + hd + 1] * o_w[hd * qt:(hd + 1) * qt] for hd in range(hg)]
    o_ref[0] = (oc_ref[0] + _pack_heads(per_head)).astype(o_ref.dtype)


def _nsa_layer(x, nw, shift, scale, gate, w_in, cmp_pos, cmp_w1, cmp_w2, qk_norm, w_out):
    b, s, d = x.shape
    t = b * s
    g_, hg, dk = NSA_KV_GROUPS, NSA_HEADS_PER_GROUP, NSA_HEAD_DIM
    qd, kd = NSA_HEADS * dk, g_ * dk
    tm = ROW_TILE
    wl = NSA_QK_WIDTH
    n_main = qd + 6 * kd
    assert w_in.shape[1] == n_main + 3 * NSA_HEADS and s % tm == 0
    gate_cols = w_in[:, n_main:].reshape(d, 3, g_, hg).transpose(0, 2, 1, 3).reshape(d, g_, 3 * hg)
    gate_cols = jnp.pad(gate_cols, ((0, 0), (0, 0), (0, NSA_PROJ_PAD - 3 * hg))).reshape(d, g_ * NSA_PROJ_PAD)
    w_pad = jnp.concatenate([w_in[:, :n_main], gate_cols], axis=1).astype(BF16)

    def head_onehot(width):
        lane_head = jnp.arange(width)[:, None] // dk
        return (lane_head == jnp.arange(V7X_LANES)[None, :]).astype(BF16)

    pq, pk = head_onehot(qd), head_onehot(kd)

    def tile_w(wv, reps):
        return jnp.tile(wv, reps).reshape(1, reps * dk)

    slopes = jnp.exp2(-8.0 * (jnp.arange(NSA_HEADS, dtype=F32) + 1) / NSA_HEADS)
    slope_parts = jnp.stack([p.astype(F32) for p in _split_bf16(slopes * LOG2_E, 3)], axis=-1)
    slope_rows = jnp.broadcast_to((slopes * LOG2_E).reshape(g_, hg, 1), (g_, hg, wl))
    q_lanes = jnp.zeros((NSA_HEADS, wl), F32).at[:, dk:dk + 6].set(jnp.concatenate([slope_parts, slope_parts], axis=-1))

    consts = [pq, pq.T, pk, pk.T, tile_w(qk_norm[0], NSA_HEADS), tile_w(qk_norm[2], g_), tile_w(qk_norm[3], g_), q_lanes]
    full2 = lambda arr: pl.BlockSpec(arr.shape, lambda i, j: (0, 0))
    mod_spec = pl.BlockSpec((1, 1, d), lambda i, j: (i, 0, 0))
    grp = lambda dt: jax.ShapeDtypeStruct((b, g_, s, wl), dt)
    grp_spec = pl.BlockSpec((1, g_, tm, wl), lambda i, j: (i, 0, j, 0))
    tokf = jax.ShapeDtypeStruct((b, s, kd), F32)
    tokf_spec = pl.BlockSpec((1, tm, kd), lambda i, j: (i, j, 0))
    q5, ks4, vs4, kw4, vw4, kc, vc, gates = pl.pallas_call(
        functools.partial(_nsa_proj_kernel, tm, qd, kd),
        out_shape=[jax.ShapeDtypeStruct((b, g_, hg, s, wl), BF16), grp(BF16), grp(BF16), grp(BF16), grp(BF16),
                   tokf, tokf, grp(F32)],
        grid=(b, s // tm),
        in_specs=[pl.BlockSpec((1, tm, d), lambda i, j: (i, j, 0)), pl.BlockSpec((1, d), lambda i, j: (0, 0)),
                  mod_spec, mod_spec, full2(w_pad)] + [full2(cst) for cst in consts],
        out_specs=[pl.BlockSpec((1, g_, hg, tm, wl), lambda i, j: (i, 0, 0, j, 0)),
                   grp_spec, grp_spec, grp_spec, grp_spec, tokf_spec, tokf_spec, grp_spec],
        compiler_params=_params("parallel", "parallel"),
        name="nsa_proj",
    )(x, nw.reshape(1, d), shift[:, None], scale[:, None], w_pad, *consts)

    nsub = s // CMP_STRIDE
    half = CMP_STRIDE * dk
    sub_rows = lambda arr: arr.reshape(b, s, g_, dk).transpose(0, 2, 1, 3).reshape(b, g_, nsub, half)
    kc3, vc3 = sub_rows(kc), sub_rows(vc)

    pos = jnp.stack([cmp_pos[0, :CMP_STRIDE].reshape(half), cmp_pos[0, CMP_STRIDE:].reshape(half),
                     cmp_pos[1, :CMP_STRIDE].reshape(half), cmp_pos[1, CMP_STRIDE:].reshape(half)])
    w1b = cmp_w1.astype(BF16)
    w2b = jnp.pad(cmp_w2, ((0, 0), (0, 0), (0, wl - dk))).astype(BF16)
    nk_lanes = jnp.pad(qk_norm[1], (0, wl - dk)).reshape(1, wl)
    cmp_in = pl.BlockSpec((1, 1, nsub, half), lambda i, j: (i, j, 0, 0))
    cmp_out = pl.BlockSpec((1, 1, nsub, wl), lambda i, j: (i, j, 0, 0))
    kcc, vcc = pl.pallas_call(
        functools.partial(_nsa_compress_kernel, half),
        out_shape=[jax.ShapeDtypeStruct((b, g_, nsub, wl), BF16)] * 2,
        grid=(b, g_),
        in_specs=[cmp_in, cmp_in, pl.BlockSpec(pos.shape, lambda i, j: (0, 0)),
                  pl.BlockSpec(w1b.shape, lambda i, j: (0, 0, 0)), pl.BlockSpec(w2b.shape, lambda i, j: (0, 0, 0)),
                  pl.BlockSpec((1, wl), lambda i, j: (0, 0))],
        out_specs=[cmp_out, cmp_out],
        compiler_params=_params("parallel", "parallel"),
        name="nsa_compress",
    )(kc3, vc3, pos, w1b, w2b, nk_lanes)

    nsel = s // SEL_LEN
    kt = min(NSA_KV_TILE, s)
    nkt = s // kt
    seq_kv = lambda n: pl.BlockSpec((1, 1, n, wl), lambda i, j, k, *_: (i, j, 0, 0))

    def tile_specs(qt):
        assert s % qt == 0 and s >= WINDOW + qt and kt % qt == 0
        return (pl.BlockSpec((1, 1, hg, qt, wl), lambda i, j, k, *_: (i, j, 0, k, 0)),
                pl.BlockSpec((1, qt, hg * dk), lambda i, j, k, *_: (i, k, j)),
                pl.BlockSpec((1, 1, qt, wl), lambda i, j, k, *_: (i, j, k, 0)),
                pl.BlockSpec((1, 1, qt, nsel), lambda i, j, k, *_: (i, j, k, 0)))

    qt = NSA_Q_TILE
    q_spec, o_spec, gate_spec, sel_spec = tile_specs(qt)
    oc, sel = pl.pallas_call(
        functools.partial(_nsa_cmp_kernel, qt, nsub, nsel),
        out_shape=[jax.ShapeDtypeStruct((b, s, qd), F32), jax.ShapeDtypeStruct((b, g_, s, nsel), BF16)],
        grid=(b, g_, s // qt),
        in_specs=[q_spec, seq_kv(nsub), seq_kv(nsub), gate_spec],
        out_specs=[o_spec, sel_spec],
        compiler_params=_params("parallel", "parallel", "parallel"),
        name="nsa_cmp_select",
    )(q5, kcc, vcc, gates)

    assert nsel <= wl
    block_onehot = (jnp.arange(s)[:, None] // SEL_LEN == jnp.arange(wl)[None, :]).astype(BF16)

    qt = NSA_ATTN_Q_TILE
    nq = s // qt
    q_spec, o_spec, gate_spec, sel_spec = tile_specs(qt)
    active = sel.reshape(b, g_, nq, qt, nkt, kt // SEL_LEN).max(axis=(3, 5)) > 0
    order = jnp.sort(jnp.where(active, 0, nkt) + jnp.arange(nkt, dtype=jnp.int32), axis=-1)
    tiles = (order % nkt).astype(jnp.int32).reshape(-1)
    counts = active.sum(axis=-1).astype(jnp.int32).reshape(-1)

    o3 = pl.pallas_call(
        functools.partial(_nsa_attn_kernel, qt, kt, nsel, nkt, nq),
        out_shape=jax.ShapeDtypeStruct((b, s, qd), BF16),
        grid_spec=pltpu.PrefetchScalarGridSpec(
            num_scalar_prefetch=2,
            grid=(b, g_, nq),
            in_specs=[q_spec, seq_kv(s), seq_kv(s), seq_kv(s), seq_kv(s),
                      pl.BlockSpec((s, wl), lambda i, j, k, *_: (0, 0)), sel_spec, gate_spec,
                      pl.BlockSpec((1, hg, wl), lambda i, j, k, *_: (j, 0, 0)), o_spec],
            out_specs=o_spec,
        ),
        compiler_params=_params("parallel", "parallel", "arbitrary"),
        name="nsa_attention",
    )(tiles, counts, q5, ks4, vs4, kw4, vw4, block_onehot, sel, gates, slope_rows, oc)
    return _outproj(x.reshape(t, d), o3.reshape(t, qd), gate, w_out, s).reshape(b, s, d)


def kernel(x, c, ada_w, ada_b, norm_mix_w, norm_ffn_w, ffn_w_gate, ffn_w_up, ffn_w_down, nsa_w_in, nsa_cmp_pos, nsa_cmp_w1, nsa_cmp_w2, nsa_qk_norm, nsa_w_out, rwkv_mu, rwkv_w_rkv, rwkv_w0, rwkv_wd_a, rwkv_wd_b, rwkv_a0, rwkv_wa_a, rwkv_wa_b, rwkv_wg_a, rwkv_wg_b, rwkv_k_k, rwkv_k_a, rwkv_r_k, rwkv_ln_w, rwkv_ln_b, rwkv_w_out, pool_w, pool_b, pool_scale):
    b, s, d = x.shape
    depth = ada_w.shape[0]
    mod = _ada_mod(c, ada_w, ada_b)
    for i in range(depth):
        sh_m, sc_m, g_m, sh_f, sc_f, g_f = jnp.split(mod[i], N_MOD, axis=-1)
        kind, j = i % N_MIXERS, i // N_MIXERS
        if kind == 0:
            x = _nsa_layer(x, norm_mix_w[i], sh_m, sc_m, g_m, nsa_w_in[j], nsa_cmp_pos[j], nsa_cmp_w1[j], nsa_cmp_w2[j],
                           nsa_qk_norm[j], nsa_w_out[j])
        elif kind == 1:
            x = _rwkv_layer(x, norm_mix_w[i], sh_m, sc_m, g_m, rwkv_mu[j], rwkv_w_rkv[j], rwkv_w0[j], rwkv_wd_a[j],
                            rwkv_wd_b[j], rwkv_a0[j], rwkv_wa_a[j], rwkv_wa_b[j], rwkv_wg_a[j], rwkv_wg_b[j],
                            rwkv_k_k[j], rwkv_k_a[j], rwkv_r_k[j], rwkv_ln_w[j], rwkv_ln_b[j], rwkv_w_out[j])
        elif kind == 2:
            x = _pool_layer(x, norm_mix_w[i], sh_m, sc_m, g_m, pool_w[j], pool_b[j], pool_scale[j])
        x = _ffn(x.reshape(b * s, d), norm_ffn_w[i], sh_f, sc_f, g_f, ffn_w_gate[i], ffn_w_up[i], ffn_w_down[i], s).reshape(b, s, d)
    return x
```

```python
import functools
import math

import jax
import jax.numpy as jnp
from jax import lax
from jax.experimental import pallas as pl
from jax.experimental.pallas import tpu as pltpu

F32 = jnp.float32
BF16 = jnp.bfloat16

V7X_LANES = 128
V7X_VMEM_LIMIT_BYTES = 56 * 1024 * 1024

RMS_EPS = 1e-6
N_MOD = 6
N_MIXERS = 3

NSA_HEAD_DIM = 64
NSA_KV_GROUPS = 4
NSA_HEADS_PER_GROUP = 4
NSA_HEADS = NSA_KV_GROUPS * NSA_HEADS_PER_GROUP
CMP_LEN = 32
CMP_STRIDE = 16
SEL_LEN = 64
SEL_TOPK = 16
WINDOW = 512
NSA_SCALE = NSA_HEAD_DIM ** -0.5
LOG2_E = math.log2(math.e)
SEL_BIG = 1e9
MASKED_SCORE = -1e30
NSA_UNSELECTED = -(2.0 ** 100)

RWKV_HEAD_DIM = 64
RWKV_GN_EPS = 64e-5
RWKV_CHUNK = 64

POOL_WINDOWS = (2, 4, 8, 16)
POOL_HALO = 16

ROW_TILE = 512


def _params(*sem):
    return pltpu.CompilerParams(dimension_semantics=sem, vmem_limit_bytes=V7X_VMEM_LIMIT_BYTES)


def _sigmoid(z):
    return 1.0 / (1.0 + jnp.exp(-z))


def _normmod(x, nw, shift, scale):
    ms = jnp.mean(x * x, axis=-1, keepdims=True)
    return (x * lax.rsqrt(ms + RMS_EPS) * nw) * (1.0 + scale) + shift


def _dot(a, b):
    return jnp.dot(a.astype(BF16), b.astype(BF16), preferred_element_type=F32)


def _dot_nt(a, b, out_dtype=F32):
    out = lax.dot_general(a.astype(BF16), b.astype(BF16), (((1,), (1,)), ((), ())), preferred_element_type=F32)
    return out.astype(out_dtype)


def _dot_tn(a, b):
    return lax.dot_general(a.astype(BF16), b.astype(BF16), (((0,), (0,)), ((), ())), preferred_element_type=F32)


def _split_bf16(x, parts):
    out = []
    for _ in range(parts):
        p = x.astype(BF16)
        out.append(p)
        x = x - p.astype(F32)
    return out


def _mod_kernel(c_ref, w_ref, b_ref, o_ref):
    c = c_ref[...]
    ca = c * _sigmoid(c)
    o_ref[0] = _dot(ca, w_ref[0]) + b_ref[0]


def _ada_mod(c, ada_w, ada_b):
    depth, d, n = ada_w.shape
    b = c.shape[0]
    rows = 8
    tn = 1536
    assert b <= rows and n % tn == 0
    c_pad = jnp.zeros((rows, d), F32).at[:b].set(c)
    out = pl.pallas_call(
        _mod_kernel,
        out_shape=jax.ShapeDtypeStruct((depth, rows, n), F32),
        grid=(depth, n // tn),
        in_specs=[
            pl.BlockSpec((rows, d), lambda i, j: (0, 0)),
            pl.BlockSpec((1, d, tn), lambda i, j: (i, 0, j)),
            pl.BlockSpec((1, 1, tn), lambda i, j: (i, 0, j)),
        ],
        out_specs=pl.BlockSpec((1, rows, tn), lambda i, j: (i, 0, j)),
        compiler_params=_params("parallel", "parallel"),
        name="ada_mod",
    )(c_pad, ada_w, ada_b.reshape(depth, 1, n))
    return out[:, :b]


def _ffn_kernel(nj, x_ref, nw_ref, sh_ref, sc_ref, g_ref, wg_ref, wu_ref, wd_ref, o_ref, h_scr, acc_scr):
    j = pl.program_id(1)

    @pl.when(j == 0)
    def _():
        h_scr[...] = _normmod(x_ref[...], nw_ref[...], sh_ref[0], sc_ref[0]).astype(BF16)
        acc_scr[...] = jnp.zeros_like(acc_scr)

    h = h_scr[...]
    g = jnp.dot(h, wg_ref[...], preferred_element_type=F32)
    u = jnp.dot(h, wu_ref[...], preferred_element_type=F32)
    a = g * _sigmoid(g) * u
    acc_scr[...] += jnp.dot(a.astype(BF16), wd_ref[...], preferred_element_type=F32)

    @pl.when(j == nj - 1)
    def _():
        o_ref[...] = x_ref[...] + g_ref[0] * acc_scr[...]


def _ffn(x2, nw, shift, scale, gate, wg, wu, wd, seq):
    t, d = x2.shape
    f = wg.shape[1]
    tm = ROW_TILE
    fc = f // 2
    assert t % tm == 0 and seq % tm == 0 and fc % V7X_LANES == 0
    tpb = seq // tm
    nj = f // fc
    mod_spec = pl.BlockSpec((1, 1, d), lambda i, j: (i // tpb, 0, 0))
    return pl.pallas_call(
        functools.partial(_ffn_kernel, nj),
        out_shape=jax.ShapeDtypeStruct((t, d), F32),
        grid=(t // tm, nj),
        in_specs=[
            pl.BlockSpec((tm, d), lambda i, j: (i, 0)),
            pl.BlockSpec((1, d), lambda i, j: (0, 0)),
            mod_spec, mod_spec, mod_spec,
            pl.BlockSpec((d, fc), lambda i, j: (0, j)),
            pl.BlockSpec((d, fc), lambda i, j: (0, j)),
            pl.BlockSpec((fc, d), lambda i, j: (j, 0)),
        ],
        out_specs=pl.BlockSpec((tm, d), lambda i, j: (i, 0)),
        scratch_shapes=[pltpu.VMEM((tm, d), BF16), pltpu.VMEM((tm, d), F32)],
        compiler_params=_params("parallel", "arbitrary"),
        name="ffn",
    )(x2, nw.reshape(1, d), shift[:, None], scale[:, None], gate[:, None],
      wg.astype(BF16), wu.astype(BF16), wd.astype(BF16))


def _outproj_kernel(x_ref, z_ref, g_ref, w_ref, o_ref):
    o_ref[...] = x_ref[...] + g_ref[0] * _dot(z_ref[...], w_ref[...])


def _outproj(x2, z2, gate, w, seq):
    t, d = x2.shape
    k = z2.shape[1]
    tm = ROW_TILE
    tpb = seq // tm
    return pl.pallas_call(
        _outproj_kernel,
        out_shape=jax.ShapeDtypeStruct((t, d), F32),
        grid=(t // tm,),
        in_specs=[
            pl.BlockSpec((tm, d), lambda i: (i, 0)),
            pl.BlockSpec((tm, k), lambda i: (i, 0)),
            pl.BlockSpec((1, 1, d), lambda i: (i // tpb, 0, 0)),
            pl.BlockSpec((k, d), lambda i: (0, 0)),
        ],
        out_specs=pl.BlockSpec((tm, d), lambda i: (i, 0)),
        compiler_params=_params("parallel"),
        name="outproj",
    )(x2, z2, gate[:, None], w.astype(BF16))


def _pool_kernel(tm, gd, x_ref, nw_ref, sh_ref, sc_ref, g_ref, pw_ref, pb_ref, ps_ref, o_ref, ext_scr):
    s = pl.program_id(1)
    x = x_ref[0]
    h = _normmod(x, nw_ref[...], sh_ref[0], sc_ref[0])

    @pl.when(s == 0)
    def _():
        ext_scr[0:POOL_HALO, :] = jnp.zeros((POOL_HALO, x.shape[1]), F32)

    ext_scr[POOL_HALO:POOL_HALO + tm, :] = h
    row = s * tm + lax.broadcasted_iota(jnp.int32, (tm, 1), 0)
    ys = []
    for gi, win in enumerate(POOL_WINDOWS):
        lanes = slice(gi * gd, (gi + 1) * gd)
        hg = h[:, lanes]
        acc = hg
        for k in range(1, win):
            acc = acc + ext_scr[POOL_HALO - k:POOL_HALO - k + tm, lanes]
        cnt = jnp.minimum(row + 1, win).astype(F32)
        ys.append(_dot(acc / cnt - hg, pw_ref[gi]))
    y = (jnp.concatenate(ys, axis=-1) + pb_ref[...]) * ps_ref[...]
    o_ref[0] = x + g_ref[0] * y
    ext_scr[0:POOL_HALO, :] = ext_scr[tm:tm + POOL_HALO, :]


def _pool_layer(x, nw, shift, scale, gate, pw, pb, ps):
    b, s, d = x.shape
    tm = ROW_TILE
    gd = d // len(POOL_WINDOWS)
    assert s % tm == 0 and max(POOL_WINDOWS) <= POOL_HALO
    mod_spec = pl.BlockSpec((1, 1, d), lambda i, j: (i, 0, 0))
    vec_spec = pl.BlockSpec((1, d), lambda i, j: (0, 0))
    return pl.pallas_call(
        functools.partial(_pool_kernel, tm, gd),
        out_shape=jax.ShapeDtypeStruct((b, s, d), F32),
        grid=(b, s // tm),
        in_specs=[
            pl.BlockSpec((1, tm, d), lambda i, j: (i, j, 0)),
            vec_spec, mod_spec, mod_spec, mod_spec,
            pl.BlockSpec(pw.shape, lambda i, j: (0, 0, 0)),
            vec_spec, vec_spec,
        ],
        out_specs=pl.BlockSpec((1, tm, d), lambda i, j: (i, j, 0)),
        scratch_shapes=[pltpu.VMEM((tm + POOL_HALO, d), F32)],
        compiler_params=_params("arbitrary", "arbitrary"),
        name="pool_mixer",
    )(x, nw.reshape(1, d), shift[:, None], scale[:, None], gate[:, None],
      pw.astype(BF16), pb.reshape(1, d), ps.reshape(1, d))


def _rwkv_proj_kernel(tm, x_ref, nw_ref, sh_ref, sc_ref, mu_ref, wr_ref, wk_ref, wv_ref,
                      w0_ref, wda_ref, wdb_ref, a0_ref, waa_ref, wab_ref, wga_ref, wgb_ref,
                      r_ref, k_ref, v_ref, lw_ref, a_ref, g_ref, ext_scr):
    s = pl.program_id(1)
    h = _normmod(x_ref[0], nw_ref[...], sh_ref[0], sc_ref[0])

    @pl.when(s == 0)
    def _():
        ext_scr[0:8, :] = jnp.zeros((8, h.shape[1]), F32)

    ext_scr[8:8 + tm, :] = h
    xx = ext_scr[7:7 + tm, :] - h

    def mix(i):
        return h + xx * mu_ref[i:i + 1, :]

    r_ref[0] = _dot(mix(0), wr_ref[...])
    k_ref[0] = _dot(mix(2), wk_ref[...])
    v_ref[0] = _dot(mix(3), wv_ref[...])
    dw = w0_ref[...] + _dot(jnp.tanh(_dot(mix(1), wda_ref[...])), wdb_ref[...])
    softplus_neg = jnp.maximum(-dw, 0.0) + jnp.log(1.0 + jnp.exp(-jnp.abs(dw)))
    lw_ref[0] = -jnp.exp(-softplus_neg - 0.5)
    a_ref[0] = _sigmoid(a0_ref[...] + _dot(_dot(mix(4), waa_ref[...]), wab_ref[...]))
    g_ref[0] = _dot(_sigmoid(_dot(mix(5), wga_ref[...])), wgb_ref[...])
    ext_scr[0:8, :] = ext_scr[tm:tm + 8, :]


def _rwkv_scan_kernel(ct, npair, r_ref, k_ref, v_ref, lw_ref, a_ref, g_ref, kk_ref, ka_ref, rk_ref, lnw_ref, lnb_ref,
                      z_ref, s_scr):
    c, n = RWKV_CHUNK, RWKV_HEAD_DIM
    w = 2 * n
    shift = int(math.log2(n))

    @pl.when(pl.program_id(2) == 0)
    def _():
        s_scr[...] = jnp.zeros_like(s_scr)

    head0 = lax.broadcasted_iota(jnp.int32, (1, w), 1) < n
    ri = lax.broadcasted_iota(jnp.int32, (w, w), 0)
    ci = lax.broadcasted_iota(jnp.int32, (w, w), 1)
    same_head = jnp.where((ri >> shift) == (ci >> shift), 1.0, 0.0).astype(BF16)
    strict, incl, eye = ri > ci, ri >= ci, ri == ci
    rt = lax.broadcasted_iota(jnp.int32, (ct, ct), 0)
    cc = lax.broadcasted_iota(jnp.int32, (ct, ct), 1)
    cshift = int(math.log2(c))
    chunk_tri = jnp.where(((rt >> cshift) == (cc >> cshift)) & (rt >= cc), 1.0, 0.0).astype(BF16)
    nb = ct // c

    def head_sum(xf):
        return sum(jnp.dot(p, same_head, preferred_element_type=F32) for p in _split_bf16(xf, 2))

    def stack(xf):
        x3 = xf.reshape(nb, c, w)
        return jnp.concatenate([jnp.where(head0, x3, 0.0), jnp.where(head0, 0.0, x3)], axis=1)

    def bmm(x, y):
        return jnp.einsum('bij,bjk->bik', x.astype(BF16), y.astype(BF16), preferred_element_type=F32)

    def bmm_nt(x, y):
        return jnp.einsum('bik,bjk->bij', x.astype(BF16), y.astype(BF16), preferred_element_type=F32)

    pairs = []
    stacked = []
    for hp in range(npair):
        ln = slice(hp * w, (hp + 1) * w)
        r, k, v, lw, a = r_ref[0, :, ln], k_ref[0, :, ln], v_ref[0, :, ln], lw_ref[0, :, ln], a_ref[0, :, ln]
        kk = k * kk_ref[:, ln]
        kkn = kk * lax.rsqrt(jnp.maximum(head_sum(kk * kk), 1e-24))
        k2 = k * (1.0 + (a - 1.0) * ka_ref[:, ln])
        cum = sum(jnp.dot(chunk_tri, p, preferred_element_type=F32) for p in _split_bf16(lw, 3))
        einv = jnp.exp(-cum)
        pairs.append((r, k2, v))
        stacked.append((stack(-kkn * jnp.exp(cum - lw)), stack(r * jnp.exp(cum)), stack(kkn * a * einv),
                        stack(k2 * einv), stack(v), jnp.exp(cum.reshape(nb, c, w)[:, c - 1:c, :])))
    at, rt_, kb, kq, vs, pc = (jnp.concatenate([st[i] for st in stacked], axis=0) for i in range(6))
    nbt = npair * nb
    kbe, kqe = kb * pc, kq * pc
    tt = bmm_nt(jnp.concatenate([at, rt_], axis=1), jnp.concatenate([kb, kq], axis=1))
    a_m = jnp.where(strict, tt[:, :w, :w], 0.0)
    b_m = jnp.where(strict, tt[:, :w, w:], 0.0)
    ar_m = jnp.where(incl, tt[:, w:, :w], 0.0)
    br_m = jnp.where(incl, tt[:, w:, w:], 0.0)
    rsum = jnp.where(eye, 1.0, 0.0) + a_m
    pw = bmm(a_m, a_m)
    for _ in range(cshift - 2):
        both = bmm(pw, jnp.concatenate([rsum, pw], axis=2))
        rsum = rsum + both[:, :, :w]
        pw = both[:, :, w:]
    tinv = rsum + bmm(pw, rsum)
    w12 = bmm(tinv, jnp.concatenate([at, bmm(b_m, vs)], axis=2))
    xmat = jnp.concatenate([w12, jnp.concatenate([jnp.zeros_like(vs), vs], axis=2)], axis=1)
    kbe_t = jnp.stack([kbe[i].T for i in range(nbt)])
    kqe_t = jnp.stack([kqe[i].T for i in range(nbt)])
    lhs = jnp.concatenate([jnp.concatenate([ar_m, br_m], axis=2), jnp.concatenate([kbe_t, kqe_t], axis=2)], axis=1)
    res = bmm(lhs, xmat)
    g_m = rt_ + res[:, :w, :w]
    y0 = res[:, :w, w:]
    mt = jnp.where(eye, jnp.broadcast_to(pc, (nbt, w, w)), 0.0) + res[:, w:, :w]
    nt = res[:, w:, w:]

    states = [s_scr[hp] for hp in range(npair)]
    ys = [[] for _ in range(npair)]
    for idx in range(nb):
        for hp in range(npair):
            bi = hp * nb + idx
            step = _dot(jnp.concatenate([g_m[bi], mt[bi]], axis=0), states[hp])
            yst = step[:w] + y0[bi]
            states[hp] = step[w:] + nt[bi]
            ys[hp].append(yst[:c] + yst[c:])
    for hp in range(npair):
        ln = slice(hp * w, (hp + 1) * w)
        s_scr[hp] = states[hp]
        y = jnp.concatenate(ys[hp], axis=0)
        r, k2, v = pairs[hp]
        mean = head_sum(y) * (1.0 / n)
        dlt = y - mean
        var = head_sum(dlt * dlt) * (1.0 / n)
        yn = dlt * lax.rsqrt(var + RWKV_GN_EPS) * lnw_ref[:, ln] + lnb_ref[:, ln]
        bonus = head_sum(r * k2 * rk_ref[:, ln]) * v
        z_ref[0, :, ln] = ((yn + bonus) * g_ref[0, :, ln]).astype(z_ref.dtype)


def _rwkv_layer(x, nw, shift, scale, gate, mu, w_rkv, w0, wd_a, wd_b, a0, wa_a, wa_b, wg_a, wg_b,
                k_k, k_a, r_k, ln_w, ln_b, w_out):
    b, s, d = x.shape
    tm = ROW_TILE
    assert s % tm == 0
    mod_spec = pl.BlockSpec((1, 1, d), lambda i, j: (i, 0, 0))
    vec_spec = pl.BlockSpec((1, d), lambda i, j: (0, 0))
    tok_spec = pl.BlockSpec((1, tm, d), lambda i, j: (i, j, 0))

    def full(arr):
        return pl.BlockSpec(arr.shape, lambda i, j: (0,) * arr.ndim)

    weights = [w_rkv[0].astype(BF16), w_rkv[1].astype(BF16), w_rkv[2].astype(BF16),
               w0.reshape(1, d), wd_a.astype(BF16), wd_b.astype(BF16),
               a0.reshape(1, d), wa_a.astype(BF16), wa_b.astype(BF16),
               wg_a.astype(BF16), wg_b.astype(BF16)]
    tok_shape = jax.ShapeDtypeStruct((b, s, d), F32)
    r, k, v, lw, a, g = pl.pallas_call(
        functools.partial(_rwkv_proj_kernel, tm),
        out_shape=[tok_shape] * 6,
        grid=(b, s // tm),
        in_specs=[tok_spec, vec_spec, mod_spec, mod_spec, full(mu)] + [full(wt) for wt in weights],
        out_specs=[tok_spec] * 6,
        scratch_shapes=[pltpu.VMEM((tm + 8, d), F32)],
        compiler_params=_params("arbitrary", "arbitrary"),
        name="rwkv_proj",
    )(x, nw.reshape(1, d), shift[:, None], scale[:, None], mu, *weights)

    ct = 512
    npair = 2
    pair = 2 * RWKV_HEAD_DIM
    lanes = npair * pair
    assert s % ct == 0 and d % lanes == 0 and pair == V7X_LANES
    seq_spec = pl.BlockSpec((1, ct, lanes), lambda i, p, j: (i, j, p))
    par_spec = pl.BlockSpec((1, lanes), lambda i, p, j: (0, p))
    z = pl.pallas_call(
        functools.partial(_rwkv_scan_kernel, ct, npair),
        out_shape=jax.ShapeDtypeStruct((b, s, d), BF16),
        grid=(b, d // lanes, s // ct),
        in_specs=[seq_spec] * 6 + [par_spec] * 5,
        out_specs=seq_spec,
        scratch_shapes=[pltpu.VMEM((npair, pair, pair), F32)],
        compiler_params=_params("parallel", "parallel", "arbitrary"),
        name="rwkv_scan",
    )(r, k, v, lw, a, g, k_k.reshape(1, d), k_a.reshape(1, d), r_k.reshape(1, d), ln_w.reshape(1, d), ln_b.reshape(1, d))
    return _outproj(x.reshape(b * s, d), z.reshape(b * s, d), gate, w_out, s).reshape(b, s, d)


NSA_Q_TILE = 128
NSA_ATTN_Q_TILE = 256
NSA_KV_TILE = 512
NSA_PROJ_PAD = 128
NSA_QK_WIDTH = 128


def _head_rms(xf, p_ref, pt_ref, wvec):
    sums = sum(jnp.dot(part, p_ref[...], preferred_element_type=F32) for part in _split_bf16(xf * xf, 2))
    inv = lax.rsqrt(sums * (1.0 / NSA_HEAD_DIM) + RMS_EPS)
    inv_full = sum(jnp.dot(part, pt_ref[...], preferred_element_type=F32) for part in _split_bf16(inv, 3))
    return xf * inv_full * wvec


def _pos_lanes(pos):
    lane = lax.broadcasted_iota(jnp.int32, (1, NSA_QK_WIDTH), 1)
    hi = ((pos >> 6) << 6).astype(F32)
    lo = (pos & 63).astype(F32)
    d = NSA_HEAD_DIM
    ones = jnp.where((lane >= d + 6) & (lane < d + 9), 1.0, 0.0)
    return jnp.where((lane >= d) & (lane < d + 3), hi, jnp.where((lane >= d + 3) & (lane < d + 6), lo, ones))


def _nsa_proj_kernel(tm, qd, kd, x_ref, nw_ref, sh_ref, sc_ref, w_ref, pq_ref, pqt_ref, pk_ref, pkt_ref,
                     nq_ref, nks_ref, nkw_ref, qf_ref,
                     q_ref, ks_ref, vs_ref, kw_ref, vw_ref, kc_ref, vc_ref, gt_ref):
    hg, dk = NSA_HEADS_PER_GROUP, NSA_HEAD_DIM
    h = _normmod(x_ref[0], nw_ref[...], sh_ref[0], sc_ref[0])
    proj = _dot(h, w_ref[...])
    low = lax.broadcasted_iota(jnp.int32, (1, NSA_QK_WIDTH), 1) < dk
    key_lanes = _pos_lanes(pl.program_id(1) * tm + lax.broadcasted_iota(jnp.int32, (tm, 1), 0))

    def heads(x, nheads):
        for pair in range(nheads // 2):
            xp = x[:, pair * 2 * dk:(pair + 1) * 2 * dk]
            yield 2 * pair, xp
            yield 2 * pair + 1, pltpu.roll(xp, dk, 1)

    one_lane = lax.broadcasted_iota(jnp.int32, (1, NSA_QK_WIDTH), 1) == dk
    qn = _head_rms(proj[:, :qd], pq_ref, pqt_ref, nq_ref[...]) * (NSA_SCALE * LOG2_E)
    for hd, xs in heads(qn, NSA_HEADS):
        q_ref[0, hd // hg, hd % hg] = jnp.where(low, xs, qf_ref[hd:hd + 1, :]).astype(BF16)
    o = qd
    kc_ref[0] = proj[:, o:o + kd]
    vc_ref[0] = proj[:, o + kd:o + 2 * kd]
    for g, xs in heads(_head_rms(proj[:, o + 2 * kd:o + 3 * kd], pk_ref, pkt_ref, nks_ref[...]), NSA_KV_GROUPS):
        ks_ref[0, g] = jnp.where(low, xs, key_lanes).astype(BF16)
    for g, xs in heads(proj[:, o + 3 * kd:o + 4 * kd], NSA_KV_GROUPS):
        vs_ref[0, g] = jnp.where(one_lane, 1.0, xs).astype(BF16)
    for g, xs in heads(_head_rms(proj[:, o + 4 * kd:o + 5 * kd], pk_ref, pkt_ref, nkw_ref[...]), NSA_KV_GROUPS):
        kw_ref[0, g] = jnp.where(low, xs, key_lanes).astype(BF16)
    for g, xs in heads(proj[:, o + 5 * kd:o + 6 * kd], NSA_KV_GROUPS):
        vw_ref[0, g] = jnp.where(one_lane, 1.0, xs).astype(BF16)
    o += 6 * kd
    for g in range(NSA_KV_GROUPS):
        gt_ref[0, g] = _sigmoid(proj[:, o + g * NSA_PROJ_PAD:o + (g + 1) * NSA_PROJ_PAD])


def _gelu_tanh(x):
    return 0.5 * x * (1.0 + jnp.tanh(math.sqrt(2.0 / math.pi) * (x + 0.044715 * (x * x * x))))


def _nsa_compress_kernel(half, kc_ref, vc_ref, pos_ref, w1_ref, w2_ref, nk_ref, kco_ref, vco_ref):
    nsub = kc_ref.shape[2]

    def comp(x, i):
        ya = _dot(x + pos_ref[2 * i:2 * i + 1, :], w1_ref[i, 0:half, :])
        yb = _dot(x + pos_ref[2 * i + 1:2 * i + 2, :], w1_ref[i, half:2 * half, :])
        hid = ya + pltpu.roll(yb, nsub - 1, 0)
        return _dot(_gelu_tanh(hid), w2_ref[i])

    kcm = comp(kc_ref[0, 0], 0)
    ms = jnp.sum(kcm * kcm, axis=-1, keepdims=True) * (1.0 / NSA_HEAD_DIM)
    block_end = lax.broadcasted_iota(jnp.int32, (nsub, 1), 0) * CMP_STRIDE + CMP_LEN - 1
    kco_ref[0, 0] = (kcm * lax.rsqrt(ms + RMS_EPS) * nk_ref[...] + _pos_lanes(block_end)).astype(BF16)
    vco_ref[0, 0] = comp(vc_ref[0, 0], 1).astype(BF16)


def _pack_heads(per_head):
    low = lax.broadcasted_iota(jnp.int32, (1, NSA_QK_WIDTH), 1) < NSA_HEAD_DIM
    pairs = [jnp.where(low, per_head[i], pltpu.roll(per_head[i + 1], NSA_HEAD_DIM, 1))
             for i in range(0, len(per_head), 2)]
    return jnp.concatenate(pairs, axis=-1)


def _nsa_cmp_kernel(qt, nc, nsel, q_ref, kc_ref, vc_ref, g_ref, oc_ref, sel_ref):
    hg = NSA_HEADS_PER_GROUP
    q0 = pl.program_id(2) * qt
    q = q_ref[0, 0].reshape(hg * qt, NSA_QK_WIDTH)
    s3 = _dot_nt(q, kc_ref[0, 0]).reshape(hg, qt, nc)
    t = q0 + lax.broadcasted_iota(jnp.int32, (qt, nc), 0)
    n = lax.broadcasted_iota(jnp.int32, (qt, nc), 1)
    mask = (t >= n * CMP_STRIDE + CMP_LEN - 1)[None]
    sm = jnp.where(mask, s3, MASKED_SCORE)
    e = jnp.exp2(sm - jnp.max(sm, axis=-1, keepdims=True))
    has_key = (q0 + lax.broadcasted_iota(jnp.int32, (1, qt, 1), 1)) >= CMP_LEN - 1
    p = e * jnp.where(has_key, 1.0 / jnp.maximum(jnp.sum(e, axis=-1, keepdims=True), 1e-30), 0.0)
    oc = _dot(p.reshape(hg * qt, nc), vc_ref[0, 0])
    g = g_ref[0, 0]
    oc_ref[0] = _pack_heads([oc[hd * qt:(hd + 1) * qt] * g[:, hd:hd + 1] for hd in range(hg)])

    psum = p[0] + p[1] + p[2] + p[3]
    jo = lax.broadcasted_iota(jnp.int32, (nsel, nc), 0)
    no = lax.broadcasted_iota(jnp.int32, (nsel, nc), 1)
    ratio = SEL_LEN // CMP_STRIDE
    first = jo * ratio - (CMP_LEN // CMP_STRIDE - 1)
    overlap_t = jnp.where((no >= first) & (no < (jo + 1) * ratio) & (no < nc - 1), 1.0, 0.0).astype(BF16)
    imp_t = sum(lax.dot_general(overlap_t, part, (((1,), (1,)), ((), ())), preferred_element_type=F32)
                for part in _split_bf16(psum, 3))

    j = lax.broadcasted_iota(jnp.int32, (nsel, qt), 0)
    tt = q0 + lax.broadcasted_iota(jnp.int32, (nsel, qt), 1)
    valid = j * SEL_LEN <= tt
    qb = tt >> int(math.log2(SEL_LEN))
    forced = (j == 0) | (j == qb) | (j == qb - 1)
    score = jnp.where(valid, jnp.where(forced, SEL_BIG, imp_t), -SEL_BIG)
    jf = j.astype(F32)
    picked = -(2.0 ** 120)
    for _ in range(min(SEL_TOPK, nsel)):
        mx = jnp.max(score, axis=0, keepdims=True)
        jmin = jnp.min(jnp.where(score == mx, jf, float(nsel)), axis=0, keepdims=True)
        score = jnp.where(jf == jmin, picked, score)
    sel = jnp.where((score == picked) & valid, 1.0, 0.0)
    sel_ref[0, 0] = sel.T.astype(BF16)


def _nsa_attn_kernel(qt, kt, nsel, nkt, nq, tiles_ref, cnt_ref,
                     q_ref, ks_ref, vs_ref, kw_ref, vw_ref, oh_ref, sel_ref, g_ref, sl_ref, oc_ref, o_ref):
    hg, dk = NSA_HEADS_PER_GROUP, NSA_HEAD_DIM
    rows = hg * qt
    step = (pl.program_id(0) * NSA_KV_GROUPS + pl.program_id(1)) * nq + pl.program_id(2)
    q0 = pl.program_id(2) * qt
    lane = lax.broadcasted_iota(jnp.int32, (1, NSA_QK_WIDTH), 1)
    t_row = (q0 + lax.broadcasted_iota(jnp.int32, (qt, 1), 0)).astype(F32)
    q_heads = []
    for hd in range(hg):
        qh = q_ref[0, 0, hd]
        for i, part in enumerate(_split_bf16(-sl_ref[0, hd:hd + 1, :] * t_row, 3)):
            qh = jnp.where(lane == dk + 6 + i, part, qh)
        q_heads.append(qh)
    q = jnp.concatenate(q_heads, axis=0)
    unsel = (1.0 - sel_ref[0, 0]) * NSA_UNSELECTED
    if nsel < NSA_QK_WIDTH:
        unsel = jnp.concatenate([unsel, jnp.zeros((qt, NSA_QK_WIDTH - nsel), BF16)], axis=1)
    q_sel = jnp.concatenate([q, jnp.concatenate([unsel] * hg, axis=0)], axis=1)

    t_k = q0 + lax.broadcasted_iota(jnp.int32, (qt, kt), 0)
    c_k = lax.broadcasted_iota(jnp.int32, (qt, kt), 1)

    def sel_step(i, carry, diagonal):
        m, acc = carry
        k0 = pl.multiple_of(tiles_ref[step * nkt + i] * kt, kt)
        keys = jnp.concatenate([ks_ref[0, 0, pl.ds(k0, kt), :], oh_ref[pl.ds(k0, kt), :]], axis=1)
        sm = _dot_nt(q_sel, keys, BF16)
        if diagonal:
            causal = (k0 + c_k <= t_k)[None]
            sm = jnp.where(causal, sm.reshape(hg, qt, kt), MASKED_SCORE).reshape(rows, kt)
        m_new = jnp.maximum(m, jnp.max(sm, axis=-1, keepdims=True))
        alpha = jnp.exp2((m - m_new).astype(F32))
        acc = alpha * acc + _dot(jnp.exp2(sm - m_new), vs_ref[0, 0, pl.ds(k0, kt), :])
        return m_new, acc

    init = (jnp.full((rows, 1), MASKED_SCORE, BF16), jnp.zeros((rows, NSA_QK_WIDTH), F32))
    last = cnt_ref[step] - 1
    carry = lax.fori_loop(0, last, functools.partial(sel_step, diagonal=False), init)
    _, acc_s = sel_step(last, carry, diagonal=True)
    o_s = acc_s / jnp.maximum(acc_s[:, dk:dk + 1], 1e-30)

    wk = WINDOW + qt
    ws = pl.multiple_of(jnp.maximum(q0 - WINDOW, 0), qt)
    s3 = _dot_nt(q, kw_ref[0, 0, pl.ds(ws, wk), :], BF16).reshape(hg, qt, wk)
    dist = (q0 + lax.broadcasted_iota(jnp.int32, (qt, wk), 0)) - (ws + lax.broadcasted_iota(jnp.int32, (qt, wk), 1))
    mask = ((dist >= 0) & (dist < WINDOW))[None]
    sm = jnp.where(mask, s3, MASKED_SCORE).reshape(rows, wk)
    e = jnp.exp2(sm - jnp.max(sm, axis=-1, keepdims=True))
    acc_w = _dot(e, vw_ref[0, 0, pl.ds(ws, wk), :])
    o_w = acc_w / jnp.maximum(acc_w[:, dk:dk + 1], 1e-30)

    g = g_ref[0, 0]
    per_head = [g[:, hg + hd:hg + hd + 1] * o_s[hd * qt:(hd + 1) * qt]
                + g[:, 2 * hg + hd:2 * hg + hd + 1] * o_w[hd * qt:(hd + 1) * qt] for hd in range(hg)]
    o_ref[0] = (oc_ref[0] + _pack_heads(per_head)).astype(o_ref.dtype)


def _nsa_layer(x, nw, shift, scale, gate, w_in, cmp_pos, cmp_w1, cmp_w2, qk_norm, w_out):
    b, s, d = x.shape
    t = b * s
    g_, hg, dk = NSA_KV_GROUPS, NSA_HEADS_PER_GROUP, NSA_HEAD_DIM
    qd, kd = NSA_HEADS * dk, g_ * dk
    tm = ROW_TILE
    wl = NSA_QK_WIDTH
    n_main = qd + 6 * kd
    assert w_in.shape[1] == n_main + 3 * NSA_HEADS and s % tm == 0
    gate_cols = w_in[:, n_main:].reshape(d, 3, g_, hg).transpose(0, 2, 1, 3).reshape(d, g_, 3 * hg)
    gate_cols = jnp.pad(gate_cols, ((0, 0), (0, 0), (0, NSA_PROJ_PAD - 3 * hg))).reshape(d, g_ * NSA_PROJ_PAD)
    w_pad = jnp.concatenate([w_in[:, :n_main], gate_cols], axis=1).astype(BF16)

    def head_onehot(width):
        lane_head = jnp.arange(width)[:, None] // dk
        return (lane_head == jnp.arange(V7X_LANES)[None, :]).astype(BF16)

    pq, pk = head_onehot(qd), head_onehot(kd)

    def tile_w(wv, reps):
        return jnp.tile(wv, reps).reshape(1, reps * dk)

    slopes = jnp.exp2(-8.0 * (jnp.arange(NSA_HEADS, dtype=F32) + 1) / NSA_HEADS)
    slope_parts = jnp.stack([p.astype(F32) for p in _split_bf16(slopes * LOG2_E, 3)], axis=-1)
    slope_rows = jnp.broadcast_to((slopes * LOG2_E).reshape(g_, hg, 1), (g_, hg, wl))
    q_lanes = jnp.zeros((NSA_HEADS, wl), F32).at[:, dk:dk + 6].set(jnp.concatenate([slope_parts, slope_parts], axis=-1))

    consts = [pq, pq.T, pk, pk.T, tile_w(qk_norm[0], NSA_HEADS), tile_w(qk_norm[2], g_), tile_w(qk_norm[3], g_), q_lanes]
    full2 = lambda arr: pl.BlockSpec(arr.shape, lambda i, j: (0, 0))
    mod_spec = pl.BlockSpec((1, 1, d), lambda i, j: (i, 0, 0))
    grp = lambda dt: jax.ShapeDtypeStruct((b, g_, s, wl), dt)
    grp_spec = pl.BlockSpec((1, g_, tm, wl), lambda i, j: (i, 0, j, 0))
    tokf = jax.ShapeDtypeStruct((b, s, kd), F32)
    tokf_spec = pl.BlockSpec((1, tm, kd), lambda i, j: (i, j, 0))
    q5, ks4, vs4, kw4, vw4, kc, vc, gates = pl.pallas_call(
        functools.partial(_nsa_proj_kernel, tm, qd, kd),
        out_shape=[jax.ShapeDtypeStruct((b, g_, hg, s, wl), BF16), grp(BF16), grp(BF16), grp(BF16), grp(BF16),
                   tokf, tokf, grp(F32)],
        grid=(b, s // tm),
        in_specs=[pl.BlockSpec((1, tm, d), lambda i, j: (i, j, 0)), pl.BlockSpec((1, d), lambda i, j: (0, 0)),
                  mod_spec, mod_spec, full2(w_pad)] + [full2(cst) for cst in consts],
        out_specs=[pl.BlockSpec((1, g_, hg, tm, wl), lambda i, j: (i, 0, 0, j, 0)),
                   grp_spec, grp_spec, grp_spec, grp_spec, tokf_spec, tokf_spec, grp_spec],
        compiler_params=_params("parallel", "parallel"),
        name="nsa_proj",
    )(x, nw.reshape(1, d), shift[:, None], scale[:, None], w_pad, *consts)

    nsub = s // CMP_STRIDE
    half = CMP_STRIDE * dk
    sub_rows = lambda arr: arr.reshape(b, s, g_, dk).transpose(0, 2, 1, 3).reshape(b, g_, nsub, half)
    kc3, vc3 = sub_rows(kc), sub_rows(vc)

    pos = jnp.stack([cmp_pos[0, :CMP_STRIDE].reshape(half), cmp_pos[0, CMP_STRIDE:].reshape(half),
                     cmp_pos[1, :CMP_STRIDE].reshape(half), cmp_pos[1, CMP_STRIDE:].reshape(half)])
    w1b = cmp_w1.astype(BF16)
    w2b = jnp.pad(cmp_w2, ((0, 0), (0, 0), (0, wl - dk))).astype(BF16)
    nk_lanes = jnp.pad(qk_norm[1], (0, wl - dk)).reshape(1, wl)
    cmp_in = pl.BlockSpec((1, 1, nsub, half), lambda i, j: (i, j, 0, 0))
    cmp_out = pl.BlockSpec((1, 1, nsub, wl), lambda i, j: (i, j, 0, 0))
    kcc, vcc = pl.pallas_call(
        functools.partial(_nsa_compress_kernel, half),
        out_shape=[jax.ShapeDtypeStruct((b, g_, nsub, wl), BF16)] * 2,
        grid=(b, g_),
        in_specs=[cmp_in, cmp_in, pl.BlockSpec(pos.shape, lambda i, j: (0, 0)),
                  pl.BlockSpec(w1b.shape, lambda i, j: (0, 0, 0)), pl.BlockSpec(w2b.shape, lambda i, j: (0, 0, 0)),
                  pl.BlockSpec((1, wl), lambda i, j: (0, 0))],
        out_specs=[cmp_out, cmp_out],
        compiler_params=_params("parallel", "parallel"),
        name="nsa_compress",
    )(kc3, vc3, pos, w1b, w2b, nk_lanes)

    nsel = s // SEL_LEN
    kt = min(NSA_KV_TILE, s)
    nkt = s // kt
    seq_kv = lambda n: pl.BlockSpec((1, 1, n, wl), lambda i, j, k, *_: (i, j, 0, 0))

    def tile_specs(qt):
        assert s % qt == 0 and s >= WINDOW + qt and kt % qt == 0
        return (pl.BlockSpec((1, 1, hg, qt, wl), lambda i, j, k, *_: (i, j, 0, k, 0)),
                pl.BlockSpec((1, qt, hg * dk), lambda i, j, k, *_: (i, k, j)),
                pl.BlockSpec((1, 1, qt, wl), lambda i, j, k, *_: (i, j, k, 0)),
                pl.BlockSpec((1, 1, qt, nsel), lambda i, j, k, *_: (i, j, k, 0)))

    qt = NSA_Q_TILE
    q_spec, o_spec, gate_spec, sel_spec = tile_specs(qt)
    oc, sel = pl.pallas_call(
        functools.partial(_nsa_cmp_kernel, qt, nsub, nsel),
        out_shape=[jax.ShapeDtypeStruct((b, s, qd), F32), jax.ShapeDtypeStruct((b, g_, s, nsel), BF16)],
        grid=(b, g_, s // qt),
        in_specs=[q_spec, seq_kv(nsub), seq_kv(nsub), gate_spec],
        out_specs=[o_spec, sel_spec],
        compiler_params=_params("parallel", "parallel", "parallel"),
        name="nsa_cmp_select",
    )(q5, kcc, vcc, gates)

    assert nsel <= wl
    block_onehot = (jnp.arange(s)[:, None] // SEL_LEN == jnp.arange(wl)[None, :]).astype(BF16)

    qt = NSA_ATTN_Q_TILE
    nq = s // qt
    q_spec, o_spec, gate_spec, sel_spec = tile_specs(qt)
    active = sel.reshape(b, g_, nq, qt, nkt, kt // SEL_LEN).max(axis=(3, 5)) > 0
    order = jnp.sort(jnp.where(active, 0, nkt) + jnp.arange(nkt, dtype=jnp.int32), axis=-1)
    tiles = (order % nkt).astype(jnp.int32).reshape(-1)
    counts = active.sum(axis=-1).astype(jnp.int32).reshape(-1)

    o3 = pl.pallas_call(
        functools.partial(_nsa_attn_kernel, qt, kt, nsel, nkt, nq),
        out_shape=jax.ShapeDtypeStruct((b, s, qd), BF16),
        grid_spec=pltpu.PrefetchScalarGridSpec(
            num_scalar_prefetch=2,
            grid=(b, g_, nq),
            in_specs=[q_spec, seq_kv(s), seq_kv(s), seq_kv(s), seq_kv(s),
                      pl.BlockSpec((s, wl), lambda i, j, k, *_: (0, 0)), sel_spec, gate_spec,
                      pl.BlockSpec((1, hg, wl), lambda i, j, k, *_: (j, 0, 0)), o_spec],
            out_specs=o_spec,
        ),
        compiler_params=_params("parallel", "parallel", "arbitrary"),
        name="nsa_attention",
    )(tiles, counts, q5, ks4, vs4, kw4, vw4, block_onehot, sel, gates, slope_rows, oc)
    return _outproj(x.reshape(t, d), o3.reshape(t, qd), gate, w_out, s).reshape(b, s, d)


def kernel(x, c, ada_w, ada_b, norm_mix_w, norm_ffn_w, ffn_w_gate, ffn_w_up, ffn_w_down, nsa_w_in, nsa_cmp_pos, nsa_cmp_w1, nsa_cmp_w2, nsa_qk_norm, nsa_w_out, rwkv_mu, rwkv_w_rkv, rwkv_w0, rwkv_wd_a, rwkv_wd_b, rwkv_a0, rwkv_wa_a, rwkv_wa_b, rwkv_wg_a, rwkv_wg_b, rwkv_k_k, rwkv_k_a, rwkv_r_k, rwkv_ln_w, rwkv_ln_b, rwkv_w_out, pool_w, pool_b, pool_scale):
    b, s, d = x.shape
    depth = ada_w.shape[0]
    mod = _ada_mod(c, ada_w, ada_b)
    for i in range(depth):
        sh_m, sc_m, g_m, sh_f, sc_f, g_f = jnp.split(mod[i], N_MOD, axis=-1)
        kind, j = i % N_MIXERS, i // N_MIXERS
        if kind == 0:
            x = _nsa_layer(x, norm_mix_w[i], sh_m, sc_m, g_m, nsa_w_in[j], nsa_cmp_pos[j], nsa_cmp_w1[j], nsa_cmp_w2[j],
                           nsa_qk_norm[j], nsa_w_out[j])
        elif kind == 1:
            x = _rwkv_layer(x, norm_mix_w[i], sh_m, sc_m, g_m, rwkv_mu[j], rwkv_w_rkv[j], rwkv_w0[j], rwkv_wd_a[j],
                            rwkv_wd_b[j], rwkv_a0[j], rwkv_wa_a[j], rwkv_wa_b[j], rwkv_wg_a[j], rwkv_wg_b[j],
                            rwkv_k_k[j], rwkv_k_a[j], rwkv_r_k[j], rwkv_ln_w[j], rwkv_ln_b[j], rwkv_w_out[j])
        elif kind == 2:
            x = _pool_layer(x, norm_mix_w[i], sh_m, sc_m, g_m, pool_w[j], pool_b[j], pool_scale[j])
        x = _ffn(x.reshape(b * s, d), norm_ffn_w[i], sh_f, sc_f, g_f, ffn_w_gate[i], ffn_w_up[i], ffn_w_down[i], s).reshape(b, s, d)
    return x
```

```python
import functools
import math

import jax
import jax.numpy as jnp
from jax import lax
from jax.experimental import pallas as pl
from jax.experimental.pallas import tpu as pltpu

F32 = jnp.float32
BF16 = jnp.bfloat16

V7X_LANES = 128
V7X_VMEM_LIMIT_BYTES = 56 * 1024 * 1024

RMS_EPS = 1e-6
N_MOD = 6
N_MIXERS = 3

NSA_HEAD_DIM = 64
NSA_KV_GROUPS = 4
NSA_HEADS_PER_GROUP = 4
NSA_HEADS = NSA_KV_GROUPS * NSA_HEADS_PER_GROUP
CMP_LEN = 32
CMP_STRIDE = 16
SEL_LEN = 64
SEL_TOPK = 16
WINDOW = 512
NSA_SCALE = NSA_HEAD_DIM ** -0.5
LOG2_E = math.log2(math.e)
SEL_BIG = 1e9
MASKED_SCORE = -1e30
NSA_UNSELECTED = -(2.0 ** 100)

RWKV_HEAD_DIM = 64
RWKV_GN_EPS = 64e-5
RWKV_CHUNK = 64

POOL_WINDOWS = (2, 4, 8, 16)
POOL_HALO = 16

ROW_TILE = 512


def _params(*sem):
    return pltpu.CompilerParams(dimension_semantics=sem, vmem_limit_bytes=V7X_VMEM_LIMIT_BYTES)


def _sigmoid(z):
    return 1.0 / (1.0 + jnp.exp(-z))


def _normmod(x, nw, shift, scale):
    ms = jnp.mean(x * x, axis=-1, keepdims=True)
    return (x * lax.rsqrt(ms + RMS_EPS) * nw) * (1.0 + scale) + shift


def _dot(a, b):
    return jnp.dot(a.astype(BF16), b.astype(BF16), preferred_element_type=F32)


def _dot_nt(a, b, out_dtype=F32):
    out = lax.dot_general(a.astype(BF16), b.astype(BF16), (((1,), (1,)), ((), ())), preferred_element_type=F32)
    return out.astype(out_dtype)


def _dot_tn(a, b):
    return lax.dot_general(a.astype(BF16), b.astype(BF16), (((0,), (0,)), ((), ())), preferred_element_type=F32)


def _split_bf16(x, parts):
    out = []
    for _ in range(parts):
        p = x.astype(BF16)
        out.append(p)
        x = x - p.astype(F32)
    return out


def _mod_kernel(c_ref, w_ref, b_ref, o_ref):
    c = c_ref[...]
    ca = c * _sigmoid(c)
    o_ref[0] = _dot(ca, w_ref[0]) + b_ref[0]


def _ada_mod(c, ada_w, ada_b):
    depth, d, n = ada_w.shape
    b = c.shape[0]
    rows = 8
    tn = 1536
    assert b <= rows and n % tn == 0
    c_pad = jnp.zeros((rows, d), F32).at[:b].set(c)
    out = pl.pallas_call(
        _mod_kernel,
        out_shape=jax.ShapeDtypeStruct((depth, rows, n), F32),
        grid=(depth, n // tn),
        in_specs=[
            pl.BlockSpec((rows, d), lambda i, j: (0, 0)),
            pl.BlockSpec((1, d, tn), lambda i, j: (i, 0, j)),
            pl.BlockSpec((1, 1, tn), lambda i, j: (i, 0, j)),
        ],
        out_specs=pl.BlockSpec((1, rows, tn), lambda i, j: (i, 0, j)),
        compiler_params=_params("parallel", "parallel"),
        name="ada_mod",
    )(c_pad, ada_w, ada_b.reshape(depth, 1, n))
    return out[:, :b]


def _ffn_kernel(nj, x_ref, nw_ref, sh_ref, sc_ref, g_ref, wg_ref, wu_ref, wd_ref, o_ref, h_scr, acc_scr):
    j = pl.program_id(1)

    @pl.when(j == 0)
    def _():
        h_scr[...] = _normmod(x_ref[...], nw_ref[...], sh_ref[0], sc_ref[0]).astype(BF16)
        acc_scr[...] = jnp.zeros_like(acc_scr)

    h = h_scr[...]
    g = jnp.dot(h, wg_ref[...], preferred_element_type=F32)
    u = jnp.dot(h, wu_ref[...], preferred_element_type=F32)
    a = g * _sigmoid(g) * u
    acc_scr[...] += jnp.dot(a.astype(BF16), wd_ref[...], preferred_element_type=F32)

    @pl.when(j == nj - 1)
    def _():
        o_ref[...] = x_ref[...] + g_ref[0] * acc_scr[...]


def _ffn(x2, nw, shift, scale, gate, wg, wu, wd, seq):
    t, d = x2.shape
    f = wg.shape[1]
    tm = ROW_TILE
    fc = f // 2
    assert t % tm == 0 and seq % tm == 0 and fc % V7X_LANES == 0
    tpb = seq // tm
    nj = f // fc
    mod_spec = pl.BlockSpec((1, 1, d), lambda i, j: (i // tpb, 0, 0))
    return pl.pallas_call(
        functools.partial(_ffn_kernel, nj),
        out_shape=jax.ShapeDtypeStruct((t, d), F32),
        grid=(t // tm, nj),
        in_specs=[
            pl.BlockSpec((tm, d), lambda i, j: (i, 0)),
            pl.BlockSpec((1, d), lambda i, j: (0, 0)),
            mod_spec, mod_spec, mod_spec,
            pl.BlockSpec((d, fc), lambda i, j: (0, j)),
            pl.BlockSpec((d, fc), lambda i, j: (0, j)),
            pl.BlockSpec((fc, d), lambda i, j: (j, 0)),
        ],
        out_specs=pl.BlockSpec((tm, d), lambda i, j: (i, 0)),
        scratch_shapes=[pltpu.VMEM((tm, d), BF16), pltpu.VMEM((tm, d), F32)],
        compiler_params=_params("parallel", "arbitrary"),
        name="ffn",
    )(x2, nw.reshape(1, d), shift[:, None], scale[:, None], gate[:, None],
      wg.astype(BF16), wu.astype(BF16), wd.astype(BF16))


def _outproj_kernel(x_ref, z_ref, g_ref, w_ref, o_ref):
    o_ref[...] = x_ref[...] + g_ref[0] * _dot(z_ref[...], w_ref[...])


def _outproj(x2, z2, gate, w, seq):
    t, d = x2.shape
    k = z2.shape[1]
    tm = ROW_TILE
    tpb = seq // tm
    return pl.pallas_call(
        _outproj_kernel,
        out_shape=jax.ShapeDtypeStruct((t, d), F32),
        grid=(t // tm,),
        in_specs=[
            pl.BlockSpec((tm, d), lambda i: (i, 0)),
            pl.BlockSpec((tm, k), lambda i: (i, 0)),
            pl.BlockSpec((1, 1, d), lambda i: (i // tpb, 0, 0)),
            pl.BlockSpec((k, d), lambda i: (0, 0)),
        ],
        out_specs=pl.BlockSpec((tm, d), lambda i: (i, 0)),
        compiler_params=_params("parallel"),
        name="outproj",
    )(x2, z2, gate[:, None], w.astype(BF16))


def _pool_kernel(tm, gd, x_ref, nw_ref, sh_ref, sc_ref, g_ref, pw_ref, pb_ref, ps_ref, o_ref, ext_scr):
    s = pl.program_id(1)
    x = x_ref[0]
    h = _normmod(x, nw_ref[...], sh_ref[0], sc_ref[0])

    @pl.when(s == 0)
    def _():
        ext_scr[0:POOL_HALO, :] = jnp.zeros((POOL_HALO, x.shape[1]), F32)

    ext_scr[POOL_HALO:POOL_HALO + tm, :] = h
    row = s * tm + lax.broadcasted_iota(jnp.int32, (tm, 1), 0)
    ys = []
    for gi, win in enumerate(POOL_WINDOWS):
        lanes = slice(gi * gd, (gi + 1) * gd)
        hg = h[:, lanes]
        acc = hg
        for k in range(1, win):
            acc = acc + ext_scr[POOL_HALO - k:POOL_HALO - k + tm, lanes]
        cnt = jnp.minimum(row + 1, win).astype(F32)
        ys.append(_dot(acc / cnt - hg, pw_ref[gi]))
    y = (jnp.concatenate(ys, axis=-1) + pb_ref[...]) * ps_ref[...]
    o_ref[0] = x + g_ref[0] * y
    ext_scr[0:POOL_HALO, :] = ext_scr[tm:tm + POOL_HALO, :]


def _pool_layer(x, nw, shift, scale, gate, pw, pb, ps):
    b, s, d = x.shape
    tm = ROW_TILE
    gd = d // len(POOL_WINDOWS)
    assert s % tm == 0 and max(POOL_WINDOWS) <= POOL_HALO
    mod_spec = pl.BlockSpec((1, 1, d), lambda i, j: (i, 0, 0))
    vec_spec = pl.BlockSpec((1, d), lambda i, j: (0, 0))
    return pl.pallas_call(
        functools.partial(_pool_kernel, tm, gd),
        out_shape=jax.ShapeDtypeStruct((b, s, d), F32),
        grid=(b, s // tm),
        in_specs=[
            pl.BlockSpec((1, tm, d), lambda i, j: (i, j, 0)),
            vec_spec, mod_spec, mod_spec, mod_spec,
            pl.BlockSpec(pw.shape, lambda i, j: (0, 0, 0)),
            vec_spec, vec_spec,
        ],
        out_specs=pl.BlockSpec((1, tm, d), lambda i, j: (i, j, 0)),
        scratch_shapes=[pltpu.VMEM((tm + POOL_HALO, d), F32)],
        compiler_params=_params("arbitrary", "arbitrary"),
        name="pool_mixer",
    )(x, nw.reshape(1, d), shift[:, None], scale[:, None], gate[:, None],
      pw.astype(BF16), pb.reshape(1, d), ps.reshape(1, d))


def _rwkv_proj_kernel(tm, x_ref, nw_ref, sh_ref, sc_ref, mu_ref, wr_ref, wk_ref, wv_ref,
                      w0_ref, wda_ref, wdb_ref, a0_ref, waa_ref, wab_ref, wga_ref, wgb_ref,
                      r_ref, k_ref, v_ref, lw_ref, a_ref, g_ref, ext_scr):
    s = pl.program_id(1)
    h = _normmod(x_ref[0], nw_ref[...], sh_ref[0], sc_ref[0])

    @pl.when(s == 0)
    def _():
        ext_scr[0:8, :] = jnp.zeros((8, h.shape[1]), F32)

    ext_scr[8:8 + tm, :] = h
    xx = ext_scr[7:7 + tm, :] - h

    def mix(i):
        return h + xx * mu_ref[i:i + 1, :]

    r_ref[0] = _dot(mix(0), wr_ref[...])
    k_ref[0] = _dot(mix(2), wk_ref[...])
    v_ref[0] = _dot(mix(3), wv_ref[...])
    dw = w0_ref[...] + _dot(jnp.tanh(_dot(mix(1), wda_ref[...])), wdb_ref[...])
    softplus_neg = jnp.maximum(-dw, 0.0) + jnp.log(1.0 + jnp.exp(-jnp.abs(dw)))
    lw_ref[0] = -jnp.exp(-softplus_neg - 0.5)
    a_ref[0] = _sigmoid(a0_ref[...] + _dot(_dot(mix(4), waa_ref[...]), wab_ref[...]))
    g_ref[0] = _dot(_sigmoid(_dot(mix(5), wga_ref[...])), wgb_ref[...])
    ext_scr[0:8, :] = ext_scr[tm:tm + 8, :]


def _rwkv_scan_kernel(ct, npair, r_ref, k_ref, v_ref, lw_ref, a_ref, g_ref, kk_ref, ka_ref, rk_ref, lnw_ref, lnb_ref,
                      z_ref, s_scr):
    c, n = RWKV_CHUNK, RWKV_HEAD_DIM
    w = 2 * n
    shift = int(math.log2(n))

    @pl.when(pl.program_id(2) == 0)
    def _():
        s_scr[...] = jnp.zeros_like(s_scr)

    head0 = lax.broadcasted_iota(jnp.int32, (1, w), 1) < n
    ri = lax.broadcasted_iota(jnp.int32, (w, w), 0)
    ci = lax.broadcasted_iota(jnp.int32, (w, w), 1)
    same_head = jnp.where((ri >> shift) == (ci >> shift), 1.0, 0.0).astype(BF16)
    strict, incl, eye = ri > ci, ri >= ci, ri == ci
    rt = lax.broadcasted_iota(jnp.int32, (ct, ct), 0)
    cc = lax.broadcasted_iota(jnp.int32, (ct, ct), 1)
    cshift = int(math.log2(c))
    chunk_tri = jnp.where(((rt >> cshift) == (cc >> cshift)) & (rt >= cc), 1.0, 0.0).astype(BF16)
    nb = ct // c

    def head_sum(xf):
        return sum(jnp.dot(p, same_head, preferred_element_type=F32) for p in _split_bf16(xf, 2))

    def stack(xf):
        x3 = xf.reshape(nb, c, w)
        return jnp.concatenate([jnp.where(head0, x3, 0.0), jnp.where(head0, 0.0, x3)], axis=1)

    def bmm(x, y):
        return jnp.einsum('bij,bjk->bik', x.astype(BF16), y.astype(BF16), preferred_element_type=F32)

    def bmm_nt(x, y):
        return jnp.einsum('bik,bjk->bij', x.astype(BF16), y.astype(BF16), preferred_element_type=F32)

    pairs = []
    stacked = []
    for hp in range(npair):
        ln = slice(hp * w, (hp + 1) * w)
        r, k, v, lw, a = r_ref[0, :, ln], k_ref[0, :, ln], v_ref[0, :, ln], lw_ref[0, :, ln], a_ref[0, :, ln]
        kk = k * kk_ref[:, ln]
        kkn = kk * lax.rsqrt(jnp.maximum(head_sum(kk * kk), 1e-24))
        k2 = k * (1.0 + (a - 1.0) * ka_ref[:, ln])
        cum = sum(jnp.dot(chunk_tri, p, preferred_element_type=F32) for p in _split_bf16(lw, 3))
        einv = jnp.exp(-cum)
        pairs.append((r, k2, v))
        stacked.append((stack(-kkn * jnp.exp(cum - lw)), stack(r * jnp.exp(cum)), stack(kkn * a * einv),
                        stack(k2 * einv), stack(v), jnp.exp(cum.reshape(nb, c, w)[:, c - 1:c, :])))
    at, rt_, kb, kq, vs, pc = (jnp.concatenate([st[i] for st in stacked], axis=0) for i in range(6))
    nbt = npair * nb
    kbe, kqe = kb * pc, kq * pc
    tt = bmm_nt(jnp.concatenate([at, rt_], axis=1), jnp.concatenate([kb, kq], axis=1))
    a_m = jnp.where(strict, tt[:, :w, :w], 0.0)
    b_m = jnp.where(strict, tt[:, :w, w:], 0.0)
    ar_m = jnp.where(incl, tt[:, w:, :w], 0.0)
    br_m = jnp.where(incl, tt[:, w:, w:], 0.0)
    rsum = jnp.where(eye, 1.0, 0.0) + a_m
    pw = bmm(a_m, a_m)
    for _ in range(cshift - 2):
        both = bmm(pw, jnp.concatenate([rsum, pw], axis=2))
        rsum = rsum + both[:, :, :w]
        pw = both[:, :, w:]
    tinv = rsum + bmm(pw, rsum)
    w12 = bmm(tinv, jnp.concatenate([at, bmm(b_m, vs)], axis=2))
    xmat = jnp.concatenate([w12, jnp.concatenate([jnp.zeros_like(vs), vs], axis=2)], axis=1)
    kbe_t = jnp.stack([kbe[i].T for i in range(nbt)])
    kqe_t = jnp.stack([kqe[i].T for i in range(nbt)])
    lhs = jnp.concatenate([jnp.concatenate([ar_m, br_m], axis=2), jnp.concatenate([kbe_t, kqe_t], axis=2)], axis=1)
    res = bmm(lhs, xmat)
    g_m = rt_ + res[:, :w, :w]
    y0 = res[:, :w, w:]
    mt = jnp.where(eye, jnp.broadcast_to(pc, (nbt, w, w)), 0.0) + res[:, w:, :w]
    nt = res[:, w:, w:]

    states = [s_scr[hp] for hp in range(npair)]
    ys = [[] for _ in range(npair)]
    for idx in range(nb):
        for hp in range(npair):
            bi = hp * nb + idx
            step = _dot(jnp.concatenate([g_m[bi], mt[bi]], axis=0), states[hp])
            yst = step[:w] + y0[bi]
            states[hp] = step[w:] + nt[bi]
            ys[hp].append(yst[:c] + yst[c:])
    for hp in range(npair):
        ln = slice(hp * w, (hp + 1) * w)
        s_scr[hp] = states[hp]
        y = jnp.concatenate(ys[hp], axis=0)
        r, k2, v = pairs[hp]
        mean = head_sum(y) * (1.0 / n)
        dlt = y - mean
        var = head_sum(dlt * dlt) * (1.0 / n)
        yn = dlt * lax.rsqrt(var + RWKV_GN_EPS) * lnw_ref[:, ln] + lnb_ref[:, ln]
        bonus = head_sum(r * k2 * rk_ref[:, ln]) * v
        z_ref[0, :, ln] = ((yn + bonus) * g_ref[0, :, ln]).astype(z_ref.dtype)


def _rwkv_layer(x, nw, shift, scale, gate, mu, w_rkv, w0, wd_a, wd_b, a0, wa_a, wa_b, wg_a, wg_b,
                k_k, k_a, r_k, ln_w, ln_b, w_out):
    b, s, d = x.shape
    tm = ROW_TILE
    assert s % tm == 0
    mod_spec = pl.BlockSpec((1, 1, d), lambda i, j: (i, 0, 0))
    vec_spec = pl.BlockSpec((1, d), lambda i, j: (0, 0))
    tok_spec = pl.BlockSpec((1, tm, d), lambda i, j: (i, j, 0))

    def full(arr):
        return pl.BlockSpec(arr.shape, lambda i, j: (0,) * arr.ndim)

    weights = [w_rkv[0].astype(BF16), w_rkv[1].astype(BF16), w_rkv[2].astype(BF16),
               w0.reshape(1, d), wd_a.astype(BF16), wd_b.astype(BF16),
               a0.reshape(1, d), wa_a.astype(BF16), wa_b.astype(BF16),
               wg_a.astype(BF16), wg_b.astype(BF16)]
    tok_shape = jax.ShapeDtypeStruct((b, s, d), F32)
    r, k, v, lw, a, g = pl.pallas_call(
        functools.partial(_rwkv_proj_kernel, tm),
        out_shape=[tok_shape] * 6,
        grid=(b, s // tm),
        in_specs=[tok_spec, vec_spec, mod_spec, mod_spec, full(mu)] + [full(wt) for wt in weights],
        out_specs=[tok_spec] * 6,
        scratch_shapes=[pltpu.VMEM((tm + 8, d), F32)],
        compiler_params=_params("arbitrary", "arbitrary"),
        name="rwkv_proj",
    )(x, nw.reshape(1, d), shift[:, None], scale[:, None], mu, *weights)

    ct = 512
    npair = 2
    pair = 2 * RWKV_HEAD_DIM
    lanes = npair * pair
    assert s % ct == 0 and d % lanes == 0 and pair == V7X_LANES
    seq_spec = pl.BlockSpec((1, ct, lanes), lambda i, p, j: (i, j, p))
    par_spec = pl.BlockSpec((1, lanes), lambda i, p, j: (0, p))
    z = pl.pallas_call(
        functools.partial(_rwkv_scan_kernel, ct, npair),
        out_shape=jax.ShapeDtypeStruct((b, s, d), BF16),
        grid=(b, d // lanes, s // ct),
        in_specs=[seq_spec] * 6 + [par_spec] * 5,
        out_specs=seq_spec,
        scratch_shapes=[pltpu.VMEM((npair, pair, pair), F32)],
        compiler_params=_params("parallel", "parallel", "arbitrary"),
        name="rwkv_scan",
    )(r, k, v, lw, a, g, k_k.reshape(1, d), k_a.reshape(1, d), r_k.reshape(1, d), ln_w.reshape(1, d), ln_b.reshape(1, d))
    return _outproj(x.reshape(b * s, d), z.reshape(b * s, d), gate, w_out, s).reshape(b, s, d)


NSA_Q_TILE = 512
NSA_CMP_SUB_TILE = 128
NSA_ATTN_Q_TILE = 256
NSA_KV_TILE = 512
NSA_PROJ_PAD = 128
NSA_QK_WIDTH = 128


def _head_rms(xf, p_ref, pt_ref, wvec):
    sums = sum(jnp.dot(part, p_ref[...], preferred_element_type=F32) for part in _split_bf16(xf * xf, 2))
    inv = lax.rsqrt(sums * (1.0 / NSA_HEAD_DIM) + RMS_EPS)
    inv_full = sum(jnp.dot(part, pt_ref[...], preferred_element_type=F32) for part in _split_bf16(inv, 3))
    return xf * inv_full * wvec


def _pos_lanes(pos):
    lane = lax.broadcasted_iota(jnp.int32, (1, NSA_QK_WIDTH), 1)
    hi = ((pos >> 6) << 6).astype(F32)
    lo = (pos & 63).astype(F32)
    d = NSA_HEAD_DIM
    ones = jnp.where((lane >= d + 6) & (lane < d + 9), 1.0, 0.0)
    return jnp.where((lane >= d) & (lane < d + 3), hi, jnp.where((lane >= d + 3) & (lane < d + 6), lo, ones))


def _nsa_proj_kernel(tm, qd, kd, x_ref, nw_ref, sh_ref, sc_ref, w_ref, pq_ref, pqt_ref, pk_ref, pkt_ref,
                     nq_ref, nks_ref, nkw_ref, qf_ref,
                     q_ref, ks_ref, vs_ref, kw_ref, vw_ref, kc_ref, vc_ref, gt_ref):
    hg, dk = NSA_HEADS_PER_GROUP, NSA_HEAD_DIM
    h = _normmod(x_ref[0], nw_ref[...], sh_ref[0], sc_ref[0])
    proj = _dot(h, w_ref[...])
    low = lax.broadcasted_iota(jnp.int32, (1, NSA_QK_WIDTH), 1) < dk
    key_lanes = _pos_lanes(pl.program_id(1) * tm + lax.broadcasted_iota(jnp.int32, (tm, 1), 0))

    def heads(x, nheads):
        for pair in range(nheads // 2):
            xp = x[:, pair * 2 * dk:(pair + 1) * 2 * dk]
            yield 2 * pair, xp
            yield 2 * pair + 1, pltpu.roll(xp, dk, 1)

    one_lane = lax.broadcasted_iota(jnp.int32, (1, NSA_QK_WIDTH), 1) == dk
    qn = _head_rms(proj[:, :qd], pq_ref, pqt_ref, nq_ref[...]) * (NSA_SCALE * LOG2_E)
    for hd, xs in heads(qn, NSA_HEADS):
        q_ref[0, hd // hg, hd % hg] = jnp.where(low, xs, qf_ref[hd:hd + 1, :]).astype(BF16)
    o = qd
    kc_ref[0] = proj[:, o:o + kd]
    vc_ref[0] = proj[:, o + kd:o + 2 * kd]
    for g, xs in heads(_head_rms(proj[:, o + 2 * kd:o + 3 * kd], pk_ref, pkt_ref, nks_ref[...]), NSA_KV_GROUPS):
        ks_ref[0, g] = jnp.where(low, xs, key_lanes).astype(BF16)
    for g, xs in heads(proj[:, o + 3 * kd:o + 4 * kd], NSA_KV_GROUPS):
        vs_ref[0, g] = jnp.where(one_lane, 1.0, xs).astype(BF16)
    for g, xs in heads(_head_rms(proj[:, o + 4 * kd:o + 5 * kd], pk_ref, pkt_ref, nkw_ref[...]), NSA_KV_GROUPS):
        kw_ref[0, g] = jnp.where(low, xs, key_lanes).astype(BF16)
    for g, xs in heads(proj[:, o + 5 * kd:o + 6 * kd], NSA_KV_GROUPS):
        vw_ref[0, g] = jnp.where(one_lane, 1.0, xs).astype(BF16)
    o += 6 * kd
    for g in range(NSA_KV_GROUPS):
        gt_ref[0, g] = _sigmoid(proj[:, o + g * NSA_PROJ_PAD:o + (g + 1) * NSA_PROJ_PAD])


def _gelu_tanh(x):
    return 0.5 * x * (1.0 + jnp.tanh(math.sqrt(2.0 / math.pi) * (x + 0.044715 * (x * x * x))))


def _nsa_compress_kernel(half, kc_ref, vc_ref, pos_ref, w1_ref, w2_ref, nk_ref, kco_ref, vco_ref):
    nsub = kc_ref.shape[2]

    def comp(x, i):
        ya = _dot(x + pos_ref[2 * i:2 * i + 1, :], w1_ref[i, 0:half, :])
        yb = _dot(x + pos_ref[2 * i + 1:2 * i + 2, :], w1_ref[i, half:2 * half, :])
        hid = ya + pltpu.roll(yb, nsub - 1, 0)
        return _dot(_gelu_tanh(hid), w2_ref[i])

    kcm = comp(kc_ref[0, 0], 0)
    ms = jnp.sum(kcm * kcm, axis=-1, keepdims=True) * (1.0 / NSA_HEAD_DIM)
    block_end = lax.broadcasted_iota(jnp.int32, (nsub, 1), 0) * CMP_STRIDE + CMP_LEN - 1
    kco_ref[0, 0] = (kcm * lax.rsqrt(ms + RMS_EPS) * nk_ref[...] + _pos_lanes(block_end)).astype(BF16)
    vco_ref[0, 0] = comp(vc_ref[0, 0], 1).astype(BF16)


def _pack_heads(per_head):
    low = lax.broadcasted_iota(jnp.int32, (1, NSA_QK_WIDTH), 1) < NSA_HEAD_DIM
    pairs = [jnp.where(low, per_head[i], pltpu.roll(per_head[i + 1], NSA_HEAD_DIM, 1))
             for i in range(0, len(per_head), 2)]
    return jnp.concatenate(pairs, axis=-1)


def _nsa_cmp_kernel(qt, nc, nsel, q_ref, kc_ref, vc_ref, g_ref, oc_ref, sel_ref):
    hg = NSA_HEADS_PER_GROUP
    q0 = pl.program_id(2) * qt
    qs = NSA_CMP_SUB_TILE
    jo = lax.broadcasted_iota(jnp.int32, (nsel, nc), 0)
    no = lax.broadcasted_iota(jnp.int32, (nsel, nc), 1)
    ratio = SEL_LEN // CMP_STRIDE
    first = jo * ratio - (CMP_LEN // CMP_STRIDE - 1)
    overlap_t = jnp.where((no >= first) & (no < (jo + 1) * ratio) & (no < nc - 1), 1.0, 0.0).astype(BF16)
    n = lax.broadcasted_iota(jnp.int32, (qs, nc), 1)
    imp_parts = []
    for sub in range(qt // qs):
        rows = slice(sub * qs, (sub + 1) * qs)
        q = q_ref[0, 0, :, rows, :].reshape(hg * qs, NSA_QK_WIDTH)
        s3 = _dot_nt(q, kc_ref[0, 0]).reshape(hg, qs, nc)
        t = q0 + sub * qs + lax.broadcasted_iota(jnp.int32, (qs, nc), 0)
        mask = (t >= n * CMP_STRIDE + CMP_LEN - 1)[None]
        sm = jnp.where(mask, s3, MASKED_SCORE)
        e = jnp.exp2(sm - jnp.max(sm, axis=-1, keepdims=True))
        has_key = (q0 + sub * qs + lax.broadcasted_iota(jnp.int32, (1, qs, 1), 1)) >= CMP_LEN - 1
        p = e * jnp.where(has_key, 1.0 / jnp.maximum(jnp.sum(e, axis=-1, keepdims=True), 1e-30), 0.0)
        oc = _dot(p.reshape(hg * qs, nc), vc_ref[0, 0])
        g = g_ref[0, 0, rows, :]
        oc_ref[0, rows, :] = _pack_heads([oc[hd * qs:(hd + 1) * qs] * g[:, hd:hd + 1] for hd in range(hg)])
        psum = p[0] + p[1] + p[2] + p[3]
        imp_parts.append(sum(lax.dot_general(overlap_t, part, (((1,), (1,)), ((), ())), preferred_element_type=F32)
                             for part in _split_bf16(psum, 3)))
    imp_t = jnp.concatenate(imp_parts, axis=1)

    j = lax.broadcasted_iota(jnp.int32, (nsel, qt), 0)
    tt = q0 + lax.broadcasted_iota(jnp.int32, (nsel, qt), 1)
    valid = j * SEL_LEN <= tt
    qb = tt >> int(math.log2(SEL_LEN))
    forced = (j == 0) | (j == qb) | (j == qb - 1)
    score = jnp.where(valid, jnp.where(forced, SEL_BIG, imp_t), -SEL_BIG)
    jf = j.astype(F32)
    picked = -(2.0 ** 120)
    for _ in range(min(SEL_TOPK, nsel)):
        mx = jnp.max(score, axis=0, keepdims=True)
        jmin = jnp.min(jnp.where(score == mx, jf, float(nsel)), axis=0, keepdims=True)
        score = jnp.where(jf == jmin, picked, score)
    sel = jnp.where((score == picked) & valid, 1.0, 0.0)
    sel_ref[0, 0] = sel.T.astype(BF16)


def _nsa_attn_kernel(qt, kt, nsel, nkt, nq, tiles_ref, cnt_ref,
                     q_ref, ks_ref, vs_ref, kw_ref, vw_ref, oh_ref, sel_ref, g_ref, sl_ref, oc_ref, o_ref):
    hg, dk = NSA_HEADS_PER_GROUP, NSA_HEAD_DIM
    rows = hg * qt
    step = (pl.program_id(0) * NSA_KV_GROUPS + pl.program_id(1)) * nq + pl.program_id(2)
    q0 = pl.program_id(2) * qt
    lane = lax.broadcasted_iota(jnp.int32, (1, NSA_QK_WIDTH), 1)
    t_row = (q0 + lax.broadcasted_iota(jnp.int32, (qt, 1), 0)).astype(F32)
    q_heads = []
    for hd in range(hg):
        qh = q_ref[0, 0, hd]
        for i, part in enumerate(_split_bf16(-sl_ref[0, hd:hd + 1, :] * t_row, 3)):
            qh = jnp.where(lane == dk + 6 + i, part, qh)
        q_heads.append(qh)
    q = jnp.concatenate(q_heads, axis=0)
    unsel = (1.0 - sel_ref[0, 0]) * NSA_UNSELECTED
    if nsel < NSA_QK_WIDTH:
        unsel = jnp.concatenate([unsel, jnp.zeros((qt, NSA_QK_WIDTH - nsel), BF16)], axis=1)
    q_sel = jnp.concatenate([q, jnp.concatenate([unsel] * hg, axis=0)], axis=1)

    t_k = q0 + lax.broadcasted_iota(jnp.int32, (qt, kt), 0)
    c_k = lax.broadcasted_iota(jnp.int32, (qt, kt), 1)

    def sel_step(i, carry, diagonal):
        m, acc = carry
        k0 = pl.multiple_of(tiles_ref[step * nkt + i] * kt, kt)
        keys = jnp.concatenate([ks_ref[0, 0, pl.ds(k0, kt), :], oh_ref[pl.ds(k0, kt), :]], axis=1)
        sm = _dot_nt(q_sel, keys, BF16)
        if diagonal:
            causal = (k0 + c_k <= t_k)[None]
            sm = jnp.where(causal, sm.reshape(hg, qt, kt), MASKED_SCORE).reshape(rows, kt)
        m_new = jnp.maximum(m, jnp.max(sm, axis=-1, keepdims=True))
        alpha = jnp.exp2((m - m_new).astype(F32))
        acc = alpha * acc + _dot(jnp.exp2(sm - m_new), vs_ref[0, 0, pl.ds(k0, kt), :])
        return m_new, acc

    init = (jnp.full((rows, 1), MASKED_SCORE, BF16), jnp.zeros((rows, NSA_QK_WIDTH), F32))
    last = cnt_ref[step] - 1
    carry = lax.fori_loop(0, last, functools.partial(sel_step, diagonal=False), init)
    _, acc_s = sel_step(last, carry, diagonal=True)
    o_s = acc_s / jnp.maximum(acc_s[:, dk:dk + 1], 1e-30)

    wk = WINDOW + qt
    ws = pl.multiple_of(jnp.maximum(q0 - WINDOW, 0), qt)
    s3 = _dot_nt(q, kw_ref[0, 0, pl.ds(ws, wk), :], BF16).reshape(hg, qt, wk)
    dist = (q0 + lax.broadcasted_iota(jnp.int32, (qt, wk), 0)) - (ws + lax.broadcasted_iota(jnp.int32, (qt, wk), 1))
    mask = ((dist >= 0) & (dist < WINDOW))[None]
    sm = jnp.where(mask, s3, MASKED_SCORE).reshape(rows, wk)
    e = jnp.exp2(sm - jnp.max(sm, axis=-1, keepdims=True))
    acc_w = _dot(e, vw_ref[0, 0, pl.ds(ws, wk), :])
    o_w = acc_w / jnp.maximum(acc_w[:, dk:dk + 1], 1e-30)

    g = g_ref[0, 0]
    per_head = [g[:, hg + hd:hg + hd + 1] * o_s[hd * qt:(hd + 1) * qt]
                + g[:, 2 * hg + hd:2 * hg + hd + 1] * o_w[hd * qt:(hd + 1) * qt] for hd in range(hg)]
    o_ref[0] = (oc_ref[0] + _pack_heads(per_head)).astype(o_ref.dtype)


def _nsa_layer(x, nw, shift, scale, gate, w_in, cmp_pos, cmp_w1, cmp_w2, qk_norm, w_out):
    b, s, d = x.shape
    t = b * s
    g_, hg, dk = NSA_KV_GROUPS, NSA_HEADS_PER_GROUP, NSA_HEAD_DIM
    qd, kd = NSA_HEADS * dk, g_ * dk
    tm = ROW_TILE
    wl = NSA_QK_WIDTH
    n_main = qd + 6 * kd
    assert w_in.shape[1] == n_main + 3 * NSA_HEADS and s % tm == 0
    gate_cols = w_in[:, n_main:].reshape(d, 3, g_, hg).transpose(0, 2, 1, 3).reshape(d, g_, 3 * hg)
    gate_cols = jnp.pad(gate_cols, ((0, 0), (0, 0), (0, NSA_PROJ_PAD - 3 * hg))).reshape(d, g_ * NSA_PROJ_PAD)
    w_pad = jnp.concatenate([w_in[:, :n_main], gate_cols], axis=1).astype(BF16)

    def head_onehot(width):
        lane_head = jnp.arange(width)[:, None] // dk
        return (lane_head == jnp.arange(V7X_LANES)[None, :]).astype(BF16)

    pq, pk = head_onehot(qd), head_onehot(kd)

    def tile_w(wv, reps):
        return jnp.tile(wv, reps).reshape(1, reps * dk)

    slopes = jnp.exp2(-8.0 * (jnp.arange(NSA_HEADS, dtype=F32) + 1) / NSA_HEADS)
    slope_parts = jnp.stack([p.astype(F32) for p in _split_bf16(slopes * LOG2_E, 3)], axis=-1)
    slope_rows = jnp.broadcast_to((slopes * LOG2_E).reshape(g_, hg, 1), (g_, hg, wl))
    q_lanes = jnp.zeros((NSA_HEADS, wl), F32).at[:, dk:dk + 6].set(jnp.concatenate([slope_parts, slope_parts], axis=-1))

    consts = [pq, pq.T, pk, pk.T, tile_w(qk_norm[0], NSA_HEADS), tile_w(qk_norm[2], g_), tile_w(qk_norm[3], g_), q_lanes]
    full2 = lambda arr: pl.BlockSpec(arr.shape, lambda i, j: (0, 0))
    mod_spec = pl.BlockSpec((1, 1, d), lambda i, j: (i, 0, 0))
    grp = lambda dt: jax.ShapeDtypeStruct((b, g_, s, wl), dt)
    grp_spec = pl.BlockSpec((1, g_, tm, wl), lambda i, j: (i, 0, j, 0))
    tokf = jax.ShapeDtypeStruct((b, s, kd), F32)
    tokf_spec = pl.BlockSpec((1, tm, kd), lambda i, j: (i, j, 0))
    q5, ks4, vs4, kw4, vw4, kc, vc, gates = pl.pallas_call(
        functools.partial(_nsa_proj_kernel, tm, qd, kd),
        out_shape=[jax.ShapeDtypeStruct((b, g_, hg, s, wl), BF16), grp(BF16), grp(BF16), grp(BF16), grp(BF16),
                   tokf, tokf, grp(F32)],
        grid=(b, s // tm),
        in_specs=[pl.BlockSpec((1, tm, d), lambda i, j: (i, j, 0)), pl.BlockSpec((1, d), lambda i, j: (0, 0)),
                  mod_spec, mod_spec, full2(w_pad)] + [full2(cst) for cst in consts],
        out_specs=[pl.BlockSpec((1, g_, hg, tm, wl), lambda i, j: (i, 0, 0, j, 0)),
                   grp_spec, grp_spec, grp_spec, grp_spec, tokf_spec, tokf_spec, grp_spec],
        compiler_params=_params("parallel", "parallel"),
        name="nsa_proj",
    )(x, nw.reshape(1, d), shift[:, None], scale[:, None], w_pad, *consts)

    nsub = s // CMP_STRIDE
    half = CMP_STRIDE * dk
    sub_rows = lambda arr: arr.reshape(b, s, g_, dk).transpose(0, 2, 1, 3).reshape(b, g_, nsub, half)
    kc3, vc3 = sub_rows(kc), sub_rows(vc)

    pos = jnp.stack([cmp_pos[0, :CMP_STRIDE].reshape(half), cmp_pos[0, CMP_STRIDE:].reshape(half),
                     cmp_pos[1, :CMP_STRIDE].reshape(half), cmp_pos[1, CMP_STRIDE:].reshape(half)])
    w1b = cmp_w1.astype(BF16)
    w2b = jnp.pad(cmp_w2, ((0, 0), (0, 0), (0, wl - dk))).astype(BF16)
    nk_lanes = jnp.pad(qk_norm[1], (0, wl - dk)).reshape(1, wl)
    cmp_in = pl.BlockSpec((1, 1, nsub, half), lambda i, j: (i, j, 0, 0))
    cmp_out = pl.BlockSpec((1, 1, nsub, wl), lambda i, j: (i, j, 0, 0))
    kcc, vcc = pl.pallas_call(
        functools.partial(_nsa_compress_kernel, half),
        out_shape=[jax.ShapeDtypeStruct((b, g_, nsub, wl), BF16)] * 2,
        grid=(b, g_),
        in_specs=[cmp_in, cmp_in, pl.BlockSpec(pos.shape, lambda i, j: (0, 0)),
                  pl.BlockSpec(w1b.shape, lambda i, j: (0, 0, 0)), pl.BlockSpec(w2b.shape, lambda i, j: (0, 0, 0)),
                  pl.BlockSpec((1, wl), lambda i, j: (0, 0))],
        out_specs=[cmp_out, cmp_out],
        compiler_params=_params("parallel", "parallel"),
        name="nsa_compress",
    )(kc3, vc3, pos, w1b, w2b, nk_lanes)

    nsel = s // SEL_LEN
    kt = min(NSA_KV_TILE, s)
    nkt = s // kt
    seq_kv = lambda n: pl.BlockSpec((1, 1, n, wl), lambda i, j, k, *_: (i, j, 0, 0))

    def tile_specs(qt):
        assert s % qt == 0 and s >= WINDOW + qt and kt % qt == 0
        return (pl.BlockSpec((1, 1, hg, qt, wl), lambda i, j, k, *_: (i, j, 0, k, 0)),
                pl.BlockSpec((1, qt, hg * dk), lambda i, j, k, *_: (i, k, j)),
                pl.BlockSpec((1, 1, qt, wl), lambda i, j, k, *_: (i, j, k, 0)),
                pl.BlockSpec((1, 1, qt, nsel), lambda i, j, k, *_: (i, j, k, 0)))

    qt = NSA_Q_TILE
    q_spec, o_spec, gate_spec, sel_spec = tile_specs(qt)
    oc, sel = pl.pallas_call(
        functools.partial(_nsa_cmp_kernel, qt, nsub, nsel),
        out_shape=[jax.ShapeDtypeStruct((b, s, qd), F32), jax.ShapeDtypeStruct((b, g_, s, nsel), BF16)],
        grid=(b, g_, s // qt),
        in_specs=[q_spec, seq_kv(nsub), seq_kv(nsub), gate_spec],
        out_specs=[o_spec, sel_spec],
        compiler_params=_params("parallel", "parallel", "parallel"),
        name="nsa_cmp_select",
    )(q5, kcc, vcc, gates)

    assert nsel <= wl
    block_onehot = (jnp.arange(s)[:, None] // SEL_LEN == jnp.arange(wl)[None, :]).astype(BF16)

    qt = NSA_ATTN_Q_TILE
    nq = s // qt
    q_spec, o_spec, gate_spec, sel_spec = tile_specs(qt)
    active = sel.reshape(b, g_, nq, qt, nkt, kt // SEL_LEN).max(axis=(3, 5)) > 0
    order = jnp.sort(jnp.where(active, 0, nkt) + jnp.arange(nkt, dtype=jnp.int32), axis=-1)
    tiles = (order % nkt).astype(jnp.int32).reshape(-1)
    counts = active.sum(axis=-1).astype(jnp.int32).reshape(-1)

    o3 = pl.pallas_call(
        functools.partial(_nsa_attn_kernel, qt, kt, nsel, nkt, nq),
        out_shape=jax.ShapeDtypeStruct((b, s, qd), BF16),
        grid_spec=pltpu.PrefetchScalarGridSpec(
            num_scalar_prefetch=2,
            grid=(b, g_, nq),
            in_specs=[q_spec, seq_kv(s), seq_kv(s), seq_kv(s), seq_kv(s),
                      pl.BlockSpec((s, wl), lambda i, j, k, *_: (0, 0)), sel_spec, gate_spec,
                      pl.BlockSpec((1, hg, wl), lambda i, j, k, *_: (j, 0, 0)), o_spec],
            out_specs=o_spec,
        ),
        compiler_params=_params("parallel", "parallel", "arbitrary"),
        name="nsa_attention",
    )(tiles, counts, q5, ks4, vs4, kw4, vw4, block_onehot, sel, gates, slope_rows, oc)
    return _outproj(x.reshape(t, d), o3.reshape(t, qd), gate, w_out, s).reshape(b, s, d)


def kernel(x, c, ada_w, ada_b, norm_mix_w, norm_ffn_w, ffn_w_gate, ffn_w_up, ffn_w_down, nsa_w_in, nsa_cmp_pos, nsa_cmp_w1, nsa_cmp_w2, nsa_qk_norm, nsa_w_out, rwkv_mu, rwkv_w_rkv, rwkv_w0, rwkv_wd_a, rwkv_wd_b, rwkv_a0, rwkv_wa_a, rwkv_wa_b, rwkv_wg_a, rwkv_wg_b, rwkv_k_k, rwkv_k_a, rwkv_r_k, rwkv_ln_w, rwkv_ln_b, rwkv_w_out, pool_w, pool_b, pool_scale):
    b, s, d = x.shape
    depth = ada_w.shape[0]
    mod = _ada_mod(c, ada_w, ada_b)
    for i in range(depth):
        sh_m, sc_m, g_m, sh_f, sc_f, g_f = jnp.split(mod[i], N_MOD, axis=-1)
        kind, j = i % N_MIXERS, i // N_MIXERS
        if kind == 0:
            x = _nsa_layer(x, norm_mix_w[i], sh_m, sc_m, g_m, nsa_w_in[j], nsa_cmp_pos[j], nsa_cmp_w1[j], nsa_cmp_w2[j],
                           nsa_qk_norm[j], nsa_w_out[j])
        elif kind == 1:
            x = _rwkv_layer(x, norm_mix_w[i], sh_m, sc_m, g_m, rwkv_mu[j], rwkv_w_rkv[j], rwkv_w0[j], rwkv_wd_a[j],
                            rwkv_wd_b[j], rwkv_a0[j], rwkv_wa_a[j], rwkv_wa_b[j], rwkv_wg_a[j], rwkv_wg_b[j],
                            rwkv_k_k[j], rwkv_k_a[j], rwkv_r_k[j], rwkv_ln_w[j], rwkv_ln_b[j], rwkv_w_out[j])
        elif kind == 2:
            x = _pool_layer(x, norm_mix_w[i], sh_m, sc_m, g_m, pool_w[j], pool_b[j], pool_scale[j])
        x = _ffn(x.reshape(b * s, d), norm_ffn_w[i], sh_f, sc_f, g_f, ffn_w_gate[i], ffn_w_up[i], ffn_w_down[i], s).reshape(b, s, d)
    return x
```

```python
import functools
import math

import jax
import jax.numpy as jnp
from jax import lax
from jax.experimental import pallas as pl
from jax.experimental.pallas import tpu as pltpu

F32 = jnp.float32
BF16 = jnp.bfloat16

V7X_LANES = 128
V7X_VMEM_LIMIT_BYTES = 56 * 1024 * 1024

RMS_EPS = 1e-6
N_MOD = 6
N_MIXERS = 3

NSA_HEAD_DIM = 64
NSA_KV_GROUPS = 4
NSA_HEADS_PER_GROUP = 4
NSA_HEADS = NSA_KV_GROUPS * NSA_HEADS_PER_GROUP
CMP_LEN = 32
CMP_STRIDE = 16
SEL_LEN = 64
SEL_TOPK = 16
WINDOW = 512
NSA_SCALE = NSA_HEAD_DIM ** -0.5
LOG2_E = math.log2(math.e)
SEL_BIG = 1e9
MASKED_SCORE = -1e30
NSA_UNSELECTED = -(2.0 ** 100)

RWKV_HEAD_DIM = 64
RWKV_GN_EPS = 64e-5
RWKV_CHUNK = 64

POOL_WINDOWS = (2, 4, 8, 16)
POOL_HALO = 16

ROW_TILE = 512
FFN_ROW_TILE = 1024
FFN_COL_CHUNK = 256


def _params(*sem):
    return pltpu.CompilerParams(dimension_semantics=sem, vmem_limit_bytes=V7X_VMEM_LIMIT_BYTES)


def _sigmoid(z):
    return 1.0 / (1.0 + jnp.exp(-z))


def _normmod(x, nw, shift, scale):
    ms = jnp.mean(x * x, axis=-1, keepdims=True)
    return (x * lax.rsqrt(ms + RMS_EPS) * nw) * (1.0 + scale) + shift


def _dot(a, b):
    return jnp.dot(a.astype(BF16), b.astype(BF16), preferred_element_type=F32)


def _dot_nt(a, b, out_dtype=F32):
    out = lax.dot_general(a.astype(BF16), b.astype(BF16), (((1,), (1,)), ((), ())), preferred_element_type=F32)
    return out.astype(out_dtype)


def _dot_tn(a, b):
    return lax.dot_general(a.astype(BF16), b.astype(BF16), (((0,), (0,)), ((), ())), preferred_element_type=F32)


def _split_bf16(x, parts):
    out = []
    for _ in range(parts):
        p = x.astype(BF16)
        out.append(p)
        x = x - p.astype(F32)
    return out


def _mod_kernel(c_ref, w_ref, b_ref, o_ref):
    c = c_ref[...]
    ca = c * _sigmoid(c)
    o_ref[0] = _dot(ca, w_ref[0]) + b_ref[0]


def _ada_mod(c, ada_w, ada_b):
    depth, d, n = ada_w.shape
    b = c.shape[0]
    rows = 8
    tn = 1536
    assert b <= rows and n % tn == 0
    c_pad = jnp.zeros((rows, d), F32).at[:b].set(c)
    out = pl.pallas_call(
        _mod_kernel,
        out_shape=jax.ShapeDtypeStruct((depth, rows, n), F32),
        grid=(depth, n // tn),
        in_specs=[
            pl.BlockSpec((rows, d), lambda i, j: (0, 0)),
            pl.BlockSpec((1, d, tn), lambda i, j: (i, 0, j)),
            pl.BlockSpec((1, 1, tn), lambda i, j: (i, 0, j)),
        ],
        out_specs=pl.BlockSpec((1, rows, tn), lambda i, j: (i, 0, j)),
        compiler_params=_params("parallel", "parallel"),
        name="ada_mod",
    )(c_pad, ada_w, ada_b.reshape(depth, 1, n))
    return out[:, :b]


def _ffn_kernel(x_ref, nw_ref, sh_ref, sc_ref, g_ref, wg_ref, wu_ref, wd_ref, o_ref):
    x = x_ref[...]
    h = _normmod(x, nw_ref[...], sh_ref[0], sc_ref[0]).astype(BF16)
    f = wg_ref.shape[1]
    acts = []
    for c0 in range(0, f, FFN_COL_CHUNK):
        cols = slice(c0, c0 + FFN_COL_CHUNK)
        g = jnp.dot(h, wg_ref[:, cols], preferred_element_type=F32)
        u = jnp.dot(h, wu_ref[:, cols], preferred_element_type=F32)
        acts.append((g * _sigmoid(g) * u).astype(BF16))
    o_ref[...] = x + g_ref[0] * jnp.dot(jnp.concatenate(acts, axis=1), wd_ref[...], preferred_element_type=F32)


def _ffn(x2, nw, shift, scale, gate, wg, wu, wd, seq):
    t, d = x2.shape
    f = wg.shape[1]
    tm = FFN_ROW_TILE
    assert t % tm == 0 and seq % tm == 0 and f % FFN_COL_CHUNK == 0
    tpb = seq // tm
    mod_spec = pl.BlockSpec((1, 1, d), lambda i: (i // tpb, 0, 0))
    resident = lambda shape: pl.BlockSpec(shape, lambda i: (0, 0), pipeline_mode=pl.Buffered(1))
    return pl.pallas_call(
        _ffn_kernel,
        out_shape=jax.ShapeDtypeStruct((t, d), F32),
        grid=(t // tm,),
        in_specs=[
            pl.BlockSpec((tm, d), lambda i: (i, 0)),
            pl.BlockSpec((1, d), lambda i: (0, 0)),
            mod_spec, mod_spec, mod_spec,
            resident((d, f)), resident((d, f)), resident((f, d)),
        ],
        out_specs=pl.BlockSpec((tm, d), lambda i: (i, 0)),
        compiler_params=_params("parallel"),
        name="ffn",
    )(x2, nw.reshape(1, d), shift[:, None], scale[:, None], gate[:, None],
      wg.astype(BF16), wu.astype(BF16), wd.astype(BF16))


def _outproj_kernel(x_ref, z_ref, g_ref, w_ref, o_ref):
    o_ref[...] = x_ref[...] + g_ref[0] * _dot(z_ref[...], w_ref[...])


def _outproj(x2, z2, gate, w, seq):
    t, d = x2.shape
    k = z2.shape[1]
    tm = ROW_TILE
    tpb = seq // tm
    return pl.pallas_call(
        _outproj_kernel,
        out_shape=jax.ShapeDtypeStruct((t, d), F32),
        grid=(t // tm,),
        in_specs=[
            pl.BlockSpec((tm, d), lambda i: (i, 0)),
            pl.BlockSpec((tm, k), lambda i: (i, 0)),
            pl.BlockSpec((1, 1, d), lambda i: (i // tpb, 0, 0)),
            pl.BlockSpec((k, d), lambda i: (0, 0)),
        ],
        out_specs=pl.BlockSpec((tm, d), lambda i: (i, 0)),
        compiler_params=_params("parallel"),
        name="outproj",
    )(x2, z2, gate[:, None], w.astype(BF16))


def _pool_kernel(tm, gd, x_ref, nw_ref, sh_ref, sc_ref, g_ref, pw_ref, pb_ref, ps_ref, o_ref, ext_scr):
    s = pl.program_id(1)
    x = x_ref[0]
    h = _normmod(x, nw_ref[...], sh_ref[0], sc_ref[0])

    @pl.when(s == 0)
    def _():
        ext_scr[0:POOL_HALO, :] = jnp.zeros((POOL_HALO, x.shape[1]), F32)

    ext_scr[POOL_HALO:POOL_HALO + tm, :] = h
    row = s * tm + lax.broadcasted_iota(jnp.int32, (tm, 1), 0)
    ys = []
    for gi, win in enumerate(POOL_WINDOWS):
        lanes = slice(gi * gd, (gi + 1) * gd)
        hg = h[:, lanes]
        acc = hg
        for k in range(1, win):
            acc = acc + ext_scr[POOL_HALO - k:POOL_HALO - k + tm, lanes]
        cnt = jnp.minimum(row + 1, win).astype(F32)
        ys.append(_dot(acc / cnt - hg, pw_ref[gi]))
    y = (jnp.concatenate(ys, axis=-1) + pb_ref[...]) * ps_ref[...]
    o_ref[0] = x + g_ref[0] * y
    ext_scr[0:POOL_HALO, :] = ext_scr[tm:tm + POOL_HALO, :]


def _pool_layer(x, nw, shift, scale, gate, pw, pb, ps):
    b, s, d = x.shape
    tm = ROW_TILE
    gd = d // len(POOL_WINDOWS)
    assert s % tm == 0 and max(POOL_WINDOWS) <= POOL_HALO
    mod_spec = pl.BlockSpec((1, 1, d), lambda i, j: (i, 0, 0))
    vec_spec = pl.BlockSpec((1, d), lambda i, j: (0, 0))
    return pl.pallas_call(
        functools.partial(_pool_kernel, tm, gd),
        out_shape=jax.ShapeDtypeStruct((b, s, d), F32),
        grid=(b, s // tm),
        in_specs=[
            pl.BlockSpec((1, tm, d), lambda i, j: (i, j, 0)),
            vec_spec, mod_spec, mod_spec, mod_spec,
            pl.BlockSpec(pw.shape, lambda i, j: (0, 0, 0)),
            vec_spec, vec_spec,
        ],
        out_specs=pl.BlockSpec((1, tm, d), lambda i, j: (i, j, 0)),
        scratch_shapes=[pltpu.VMEM((tm + POOL_HALO, d), F32)],
        compiler_params=_params("arbitrary", "arbitrary"),
        name="pool_mixer",
    )(x, nw.reshape(1, d), shift[:, None], scale[:, None], gate[:, None],
      pw.astype(BF16), pb.reshape(1, d), ps.reshape(1, d))


def _rwkv_proj_kernel(tm, x_ref, nw_ref, sh_ref, sc_ref, mu_ref, wr_ref, wk_ref, wv_ref,
                      w0_ref, wda_ref, wdb_ref, a0_ref, waa_ref, wab_ref, wga_ref, wgb_ref,
                      r_ref, k_ref, v_ref, lw_ref, a_ref, g_ref, ext_scr):
    s = pl.program_id(1)
    h = _normmod(x_ref[0], nw_ref[...], sh_ref[0], sc_ref[0])

    @pl.when(s == 0)
    def _():
        ext_scr[0:8, :] = jnp.zeros((8, h.shape[1]), F32)

    ext_scr[8:8 + tm, :] = h
    xx = ext_scr[7:7 + tm, :] - h

    def mix(i):
        return h + xx * mu_ref[i:i + 1, :]

    r_ref[0] = _dot(mix(0), wr_ref[...])
    k_ref[0] = _dot(mix(2), wk_ref[...])
    v_ref[0] = _dot(mix(3), wv_ref[...])
    dw = w0_ref[...] + _dot(jnp.tanh(_dot(mix(1), wda_ref[...])), wdb_ref[...])
    softplus_neg = jnp.maximum(-dw, 0.0) + jnp.log(1.0 + jnp.exp(-jnp.abs(dw)))
    lw_ref[0] = -jnp.exp(-softplus_neg - 0.5)
    a_ref[0] = _sigmoid(a0_ref[...] + _dot(_dot(mix(4), waa_ref[...]), wab_ref[...]))
    g_ref[0] = _dot(_sigmoid(_dot(mix(5), wga_ref[...])), wgb_ref[...])
    ext_scr[0:8, :] = ext_scr[tm:tm + 8, :]


def _rwkv_scan_kernel(ct, npair, r_ref, k_ref, v_ref, lw_ref, a_ref, g_ref, kk_ref, ka_ref, rk_ref, lnw_ref, lnb_ref,
                      z_ref, s_scr):
    c, n = RWKV_CHUNK, RWKV_HEAD_DIM
    w = 2 * n
    shift = int(math.log2(n))

    @pl.when(pl.program_id(2) == 0)
    def _():
        s_scr[...] = jnp.zeros_like(s_scr)

    head0 = lax.broadcasted_iota(jnp.int32, (1, w), 1) < n
    ri = lax.broadcasted_iota(jnp.int32, (w, w), 0)
    ci = lax.broadcasted_iota(jnp.int32, (w, w), 1)
    same_head = jnp.where((ri >> shift) == (ci >> shift), 1.0, 0.0).astype(BF16)
    strict, incl, eye = ri > ci, ri >= ci, ri == ci
    rt = lax.broadcasted_iota(jnp.int32, (ct, ct), 0)
    cc = lax.broadcasted_iota(jnp.int32, (ct, ct), 1)
    cshift = int(math.log2(c))
    chunk_tri = jnp.where(((rt >> cshift) == (cc >> cshift)) & (rt >= cc), 1.0, 0.0).astype(BF16)
    nb = ct // c

    def head_sum(xf):
        return sum(jnp.dot(p, same_head, preferred_element_type=F32) for p in _split_bf16(xf, 2))

    def stack(xf):
        x3 = xf.reshape(nb, c, w)
        return jnp.concatenate([jnp.where(head0, x3, 0.0), jnp.where(head0, 0.0, x3)], axis=1)

    def bmm(x, y):
        return jnp.einsum('bij,bjk->bik', x.astype(BF16), y.astype(BF16), preferred_element_type=F32)

    def bmm_nt(x, y):
        return jnp.einsum('bik,bjk->bij', x.astype(BF16), y.astype(BF16), preferred_element_type=F32)

    pairs = []
    stacked = []
    for hp in range(npair):
        ln = slice(hp * w, (hp + 1) * w)
        r, k, v, lw, a = r_ref[0, :, ln], k_ref[0, :, ln], v_ref[0, :, ln], lw_ref[0, :, ln], a_ref[0, :, ln]
        kk = k * kk_ref[:, ln]
        kkn = kk * lax.rsqrt(jnp.maximum(head_sum(kk * kk), 1e-24))
        k2 = k * (1.0 + (a - 1.0) * ka_ref[:, ln])
        cum = sum(jnp.dot(chunk_tri, p, preferred_element_type=F32) for p in _split_bf16(lw, 3))
        einv = jnp.exp(-cum)
        pairs.append((r, k2, v))
        stacked.append((stack(-kkn * jnp.exp(cum - lw)), stack(r * jnp.exp(cum)), stack(kkn * a * einv),
                        stack(k2 * einv), stack(v), jnp.exp(cum.reshape(nb, c, w)[:, c - 1:c, :])))
    at, rt_, kb, kq, vs, pc = (jnp.concatenate([st[i] for st in stacked], axis=0) for i in range(6))
    nbt = npair * nb
    kbe, kqe = kb * pc, kq * pc
    tt = bmm_nt(jnp.concatenate([at, rt_], axis=1), jnp.concatenate([kb, kq], axis=1))
    a_m = jnp.where(strict, tt[:, :w, :w], 0.0)
    b_m = jnp.where(strict, tt[:, :w, w:], 0.0)
    ar_m = jnp.where(incl, tt[:, w:, :w], 0.0)
    br_m = jnp.where(incl, tt[:, w:, w:], 0.0)
    rsum = jnp.where(eye, 1.0, 0.0) + a_m
    pw = bmm(a_m, a_m)
    for _ in range(cshift - 2):
        both = bmm(pw, jnp.concatenate([rsum, pw], axis=2))
        rsum = rsum + both[:, :, :w]
        pw = both[:, :, w:]
    tinv = rsum + bmm(pw, rsum)
    w12 = bmm(tinv, jnp.concatenate([at, bmm(b_m, vs)], axis=2))
    xmat = jnp.concatenate([w12, jnp.concatenate([jnp.zeros_like(vs), vs], axis=2)], axis=1)
    kbe_t = jnp.stack([kbe[i].T for i in range(nbt)])
    kqe_t = jnp.stack([kqe[i].T for i in range(nbt)])
    lhs = jnp.concatenate([jnp.concatenate([ar_m, br_m], axis=2), jnp.concatenate([kbe_t, kqe_t], axis=2)], axis=1)
    res = bmm(lhs, xmat)
    g_m = rt_ + res[:, :w, :w]
    y0 = res[:, :w, w:]
    mt = jnp.where(eye, jnp.broadcast_to(pc, (nbt, w, w)), 0.0) + res[:, w:, :w]
    nt = res[:, w:, w:]

    states = [s_scr[hp] for hp in range(npair)]
    ys = [[] for _ in range(npair)]
    for idx in range(nb):
        for hp in range(npair):
            bi = hp * nb + idx
            step = _dot(jnp.concatenate([g_m[bi], mt[bi]], axis=0), states[hp])
            yst = step[:w] + y0[bi]
            states[hp] = step[w:] + nt[bi]
            ys[hp].append(yst[:c] + yst[c:])
    for hp in range(npair):
        ln = slice(hp * w, (hp + 1) * w)
        s_scr[hp] = states[hp]
        y = jnp.concatenate(ys[hp], axis=0)
        r, k2, v = pairs[hp]
        mean = head_sum(y) * (1.0 / n)
        dlt = y - mean
        var = head_sum(dlt * dlt) * (1.0 / n)
        yn = dlt * lax.rsqrt(var + RWKV_GN_EPS) * lnw_ref[:, ln] + lnb_ref[:, ln]
        bonus = head_sum(r * k2 * rk_ref[:, ln]) * v
        z_ref[0, :, ln] = ((yn + bonus) * g_ref[0, :, ln]).astype(z_ref.dtype)


def _rwkv_layer(x, nw, shift, scale, gate, mu, w_rkv, w0, wd_a, wd_b, a0, wa_a, wa_b, wg_a, wg_b,
                k_k, k_a, r_k, ln_w, ln_b, w_out):
    b, s, d = x.shape
    tm = ROW_TILE
    assert s % tm == 0
    mod_spec = pl.BlockSpec((1, 1, d), lambda i, j: (i, 0, 0))
    vec_spec = pl.BlockSpec((1, d), lambda i, j: (0, 0))
    tok_spec = pl.BlockSpec((1, tm, d), lambda i, j: (i, j, 0))

    def full(arr):
        return pl.BlockSpec(arr.shape, lambda i, j: (0,) * arr.ndim)

    weights = [w_rkv[0].astype(BF16), w_rkv[1].astype(BF16), w_rkv[2].astype(BF16),
               w0.reshape(1, d), wd_a.astype(BF16), wd_b.astype(BF16),
               a0.reshape(1, d), wa_a.astype(BF16), wa_b.astype(BF16),
               wg_a.astype(BF16), wg_b.astype(BF16)]
    tok_shape = jax.ShapeDtypeStruct((b, s, d), F32)
    r, k, v, lw, a, g = pl.pallas_call(
        functools.partial(_rwkv_proj_kernel, tm),
        out_shape=[tok_shape] * 6,
        grid=(b, s // tm),
        in_specs=[tok_spec, vec_spec, mod_spec, mod_spec, full(mu)] + [full(wt) for wt in weights],
        out_specs=[tok_spec] * 6,
        scratch_shapes=[pltpu.VMEM((tm + 8, d), F32)],
        compiler_params=_params("arbitrary", "arbitrary"),
        name="rwkv_proj",
    )(x, nw.reshape(1, d), shift[:, None], scale[:, None], mu, *weights)

    ct = 512
    npair = 4
    pair = 2 * RWKV_HEAD_DIM
    lanes = npair * pair
    assert s % ct == 0 and d % lanes == 0 and pair == V7X_LANES
    seq_spec = pl.BlockSpec((1, ct, lanes), lambda i, p, j: (i, j, p))
    par_spec = pl.BlockSpec((1, lanes), lambda i, p, j: (0, p))
    z = pl.pallas_call(
        functools.partial(_rwkv_scan_kernel, ct, npair),
        out_shape=jax.ShapeDtypeStruct((b, s, d), BF16),
        grid=(b, d // lanes, s // ct),
        in_specs=[seq_spec] * 6 + [par_spec] * 5,
        out_specs=seq_spec,
        scratch_shapes=[pltpu.VMEM((npair, pair, pair), F32)],
        compiler_params=_params("parallel", "parallel", "arbitrary"),
        name="rwkv_scan",
    )(r, k, v, lw, a, g, k_k.reshape(1, d), k_a.reshape(1, d), r_k.reshape(1, d), ln_w.reshape(1, d), ln_b.reshape(1, d))
    return _outproj(x.reshape(b * s, d), z.reshape(b * s, d), gate, w_out, s).reshape(b, s, d)


NSA_Q_TILE = 512
NSA_CMP_SUB_TILE = 128
NSA_ATTN_Q_TILE = 512
NSA_KV_TILE = 512
NSA_PROJ_PAD = 128
NSA_QK_WIDTH = 128


def _head_rms(xf, p_ref, pt_ref, wvec):
    sums = sum(jnp.dot(part, p_ref[...], preferred_element_type=F32) for part in _split_bf16(xf * xf, 2))
    inv = lax.rsqrt(sums * (1.0 / NSA_HEAD_DIM) + RMS_EPS)
    inv_full = sum(jnp.dot(part, pt_ref[...], preferred_element_type=F32) for part in _split_bf16(inv, 3))
    return xf * inv_full * wvec


def _pos_lanes(pos):
    lane = lax.broadcasted_iota(jnp.int32, (1, NSA_QK_WIDTH), 1)
    hi = ((pos >> 6) << 6).astype(F32)
    lo = (pos & 63).astype(F32)
    d = NSA_HEAD_DIM
    ones = jnp.where((lane >= d + 6) & (lane < d + 9), 1.0, 0.0)
    return jnp.where((lane >= d) & (lane < d + 3), hi, jnp.where((lane >= d + 3) & (lane < d + 6), lo, ones))


def _nsa_proj_kernel(tm, qd, kd, x_ref, nw_ref, sh_ref, sc_ref, w_ref, pq_ref, pqt_ref, pk_ref, pkt_ref,
                     nq_ref, nks_ref, nkw_ref, qf_ref,
                     q_ref, ks_ref, vs_ref, kw_ref, vw_ref, kc_ref, vc_ref, gt_ref):
    hg, dk = NSA_HEADS_PER_GROUP, NSA_HEAD_DIM
    h = _normmod(x_ref[0], nw_ref[...], sh_ref[0], sc_ref[0])
    proj = _dot(h, w_ref[...])
    low = lax.broadcasted_iota(jnp.int32, (1, NSA_QK_WIDTH), 1) < dk
    key_lanes = _pos_lanes(pl.program_id(1) * tm + lax.broadcasted_iota(jnp.int32, (tm, 1), 0))

    def heads(x, nheads):
        for pair in range(nheads // 2):
            xp = x[:, pair * 2 * dk:(pair + 1) * 2 * dk]
            yield 2 * pair, xp
            yield 2 * pair + 1, pltpu.roll(xp, dk, 1)

    one_lane = lax.broadcasted_iota(jnp.int32, (1, NSA_QK_WIDTH), 1) == dk
    qn = _head_rms(proj[:, :qd], pq_ref, pqt_ref, nq_ref[...]) * (NSA_SCALE * LOG2_E)
    for hd, xs in heads(qn, NSA_HEADS):
        q_ref[0, hd // hg, hd % hg] = jnp.where(low, xs, qf_ref[hd:hd + 1, :]).astype(BF16)
    o = qd
    kc_ref[0] = proj[:, o:o + kd]
    vc_ref[0] = proj[:, o + kd:o + 2 * kd]
    for g, xs in heads(_head_rms(proj[:, o + 2 * kd:o + 3 * kd], pk_ref, pkt_ref, nks_ref[...]), NSA_KV_GROUPS):
        ks_ref[0, g] = jnp.where(low, xs, key_lanes).astype(BF16)
    for g, xs in heads(proj[:, o + 3 * kd:o + 4 * kd], NSA_KV_GROUPS):
        vs_ref[0, g] = jnp.where(one_lane, 1.0, xs).astype(BF16)
    for g, xs in heads(_head_rms(proj[:, o + 4 * kd:o + 5 * kd], pk_ref, pkt_ref, nkw_ref[...]), NSA_KV_GROUPS):
        kw_ref[0, g] = jnp.where(low, xs, key_lanes).astype(BF16)
    for g, xs in heads(proj[:, o + 5 * kd:o + 6 * kd], NSA_KV_GROUPS):
        vw_ref[0, g] = jnp.where(one_lane, 1.0, xs).astype(BF16)
    o += 6 * kd
    for g in range(NSA_KV_GROUPS):
        gt_ref[0, g] = _sigmoid(proj[:, o + g * NSA_PROJ_PAD:o + (g + 1) * NSA_PROJ_PAD])


def _gelu_tanh(x):
    return 0.5 * x * (1.0 + jnp.tanh(math.sqrt(2.0 / math.pi) * (x + 0.044715 * (x * x * x))))


def _nsa_compress_kernel(half, kc_ref, vc_ref, pos_ref, w1_ref, w2_ref, nk_ref, kco_ref, vco_ref):
    nsub = kc_ref.shape[2]

    def comp(x, i):
        ya = _dot(x + pos_ref[2 * i:2 * i + 1, :], w1_ref[i, 0:half, :])
        yb = _dot(x + pos_ref[2 * i + 1:2 * i + 2, :], w1_ref[i, half:2 * half, :])
        hid = ya + pltpu.roll(yb, nsub - 1, 0)
        return _dot(_gelu_tanh(hid), w2_ref[i])

    kcm = comp(kc_ref[0, 0], 0)
    ms = jnp.sum(kcm * kcm, axis=-1, keepdims=True) * (1.0 / NSA_HEAD_DIM)
    block_end = lax.broadcasted_iota(jnp.int32, (nsub, 1), 0) * CMP_STRIDE + CMP_LEN - 1
    kco_ref[0, 0] = (kcm * lax.rsqrt(ms + RMS_EPS) * nk_ref[...] + _pos_lanes(block_end)).astype(BF16)
    vco_ref[0, 0] = comp(vc_ref[0, 0], 1).astype(BF16)


def _pack_heads(per_head):
    low = lax.broadcasted_iota(jnp.int32, (1, NSA_QK_WIDTH), 1) < NSA_HEAD_DIM
    pairs = [jnp.where(low, per_head[i], pltpu.roll(per_head[i + 1], NSA_HEAD_DIM, 1))
             for i in range(0, len(per_head), 2)]
    return jnp.concatenate(pairs, axis=-1)


def _nsa_cmp_kernel(qt, nc, nsel, q_ref, kc_ref, vc_ref, g_ref, oc_ref, sel_ref):
    hg = NSA_HEADS_PER_GROUP
    q0 = pl.program_id(2) * qt
    qs = NSA_CMP_SUB_TILE
    jo = lax.broadcasted_iota(jnp.int32, (nsel, nc), 0)
    no = lax.broadcasted_iota(jnp.int32, (nsel, nc), 1)
    ratio = SEL_LEN // CMP_STRIDE
    first = jo * ratio - (CMP_LEN // CMP_STRIDE - 1)
    overlap_t = jnp.where((no >= first) & (no < (jo + 1) * ratio) & (no < nc - 1), 1.0, 0.0).astype(BF16)
    n = lax.broadcasted_iota(jnp.int32, (qs, nc), 1)
    imp_parts = []
    for sub in range(qt // qs):
        rows = slice(sub * qs, (sub + 1) * qs)
        q = q_ref[0, 0, :, rows, :].reshape(hg * qs, NSA_QK_WIDTH)
        s3 = _dot_nt(q, kc_ref[0, 0]).reshape(hg, qs, nc)
        t = q0 + sub * qs + lax.broadcasted_iota(jnp.int32, (qs, nc), 0)
        mask = (t >= n * CMP_STRIDE + CMP_LEN - 1)[None]
        sm = jnp.where(mask, s3, MASKED_SCORE)
        e = jnp.exp2(sm - jnp.max(sm, axis=-1, keepdims=True))
        has_key = (q0 + sub * qs + lax.broadcasted_iota(jnp.int32, (1, qs, 1), 1)) >= CMP_LEN - 1
        p = e * jnp.where(has_key, 1.0 / jnp.maximum(jnp.sum(e, axis=-1, keepdims=True), 1e-30), 0.0)
        oc = _dot(p.reshape(hg * qs, nc), vc_ref[0, 0])
        g = g_ref[0, 0, rows, :]
        oc_ref[0, rows, :] = _pack_heads([oc[hd * qs:(hd + 1) * qs] * g[:, hd:hd + 1] for hd in range(hg)])
        psum = p[0] + p[1] + p[2] + p[3]
        imp_parts.append(sum(lax.dot_general(overlap_t, part, (((1,), (1,)), ((), ())), preferred_element_type=F32)
                             for part in _split_bf16(psum, 3)))
    imp_t = jnp.concatenate(imp_parts, axis=1)

    j = lax.broadcasted_iota(jnp.int32, (nsel, qt), 0)
    tt = q0 + lax.broadcasted_iota(jnp.int32, (nsel, qt), 1)
    valid = j * SEL_LEN <= tt
    qb = tt >> int(math.log2(SEL_LEN))
    forced = (j == 0) | (j == qb) | (j == qb - 1)
    score = jnp.where(valid, jnp.where(forced, SEL_BIG, imp_t), -SEL_BIG)
    jf = j.astype(F32)
    picked = -(2.0 ** 120)
    for _ in range(min(SEL_TOPK, nsel)):
        mx = jnp.max(score, axis=0, keepdims=True)
        jmin = jnp.min(jnp.where(score == mx, jf, float(nsel)), axis=0, keepdims=True)
        score = jnp.where(jf == jmin, picked, score)
    sel = jnp.where((score == picked) & valid, 1.0, 0.0)
    sel_ref[0, 0] = sel.T.astype(BF16)


def _nsa_attn_kernel(qt, kt, nsel, nkt, nq, tiles_ref, cnt_ref,
                     q_ref, ks_ref, vs_ref, kw_ref, vw_ref, oh_ref, sel_ref, g_ref, sl_ref, oc_ref, o_ref):
    hg, dk = NSA_HEADS_PER_GROUP, NSA_HEAD_DIM
    rows = hg * qt
    step = (pl.program_id(0) * NSA_KV_GROUPS + pl.program_id(1)) * nq + pl.program_id(2)
    q0 = pl.program_id(2) * qt
    lane = lax.broadcasted_iota(jnp.int32, (1, NSA_QK_WIDTH), 1)
    t_row = (q0 + lax.broadcasted_iota(jnp.int32, (qt, 1), 0)).astype(F32)
    q_heads = []
    for hd in range(hg):
        qh = q_ref[0, 0, hd]
        for i, part in enumerate(_split_bf16(-sl_ref[0, hd:hd + 1, :] * t_row, 3)):
            qh = jnp.where(lane == dk + 6 + i, part, qh)
        q_heads.append(qh)
    q = jnp.concatenate(q_heads, axis=0)
    unsel = (1.0 - sel_ref[0, 0]) * NSA_UNSELECTED
    if nsel < NSA_QK_WIDTH:
        unsel = jnp.concatenate([unsel, jnp.zeros((qt, NSA_QK_WIDTH - nsel), BF16)], axis=1)
    q_sel = jnp.concatenate([q, jnp.concatenate([unsel] * hg, axis=0)], axis=1)

    t_k = q0 + lax.broadcasted_iota(jnp.int32, (qt, kt), 0)
    c_k = lax.broadcasted_iota(jnp.int32, (qt, kt), 1)

    def sel_step(i, carry, diagonal):
        m, acc = carry
        k0 = pl.multiple_of(tiles_ref[step * nkt + i] * kt, kt)
        keys = jnp.concatenate([ks_ref[0, 0, pl.ds(k0, kt), :], oh_ref[pl.ds(k0, kt), :]], axis=1)
        sm = _dot_nt(q_sel, keys, BF16)
        if diagonal:
            causal = (k0 + c_k <= t_k)[None]
            sm = jnp.where(causal, sm.reshape(hg, qt, kt), MASKED_SCORE).reshape(rows, kt)
        m_new = jnp.maximum(m, jnp.max(sm, axis=-1, keepdims=True))
        alpha = jnp.exp2((m - m_new).astype(F32))
        acc = alpha * acc + _dot(jnp.exp2(sm - m_new), vs_ref[0, 0, pl.ds(k0, kt), :])
        return m_new, acc

    init = (jnp.full((rows, 1), MASKED_SCORE, BF16), jnp.zeros((rows, NSA_QK_WIDTH), F32))
    last = cnt_ref[step] - 1
    carry = lax.fori_loop(0, last, functools.partial(sel_step, diagonal=False), init)
    _, acc_s = sel_step(last, carry, diagonal=True)
    o_s = acc_s / jnp.maximum(acc_s[:, dk:dk + 1], 1e-30)

    wk = WINDOW + qt
    ws = pl.multiple_of(jnp.maximum(q0 - WINDOW, 0), qt)
    s3 = _dot_nt(q, kw_ref[0, 0, pl.ds(ws, wk), :], BF16).reshape(hg, qt, wk)
    dist = (q0 + lax.broadcasted_iota(jnp.int32, (qt, wk), 0)) - (ws + lax.broadcasted_iota(jnp.int32, (qt, wk), 1))
    mask = ((dist >= 0) & (dist < WINDOW))[None]
    sm = jnp.where(mask, s3, MASKED_SCORE).reshape(rows, wk)
    e = jnp.exp2(sm - jnp.max(sm, axis=-1, keepdims=True))
    acc_w = _dot(e, vw_ref[0, 0, pl.ds(ws, wk), :])
    o_w = acc_w / jnp.maximum(acc_w[:, dk:dk + 1], 1e-30)

    g = g_ref[0, 0]
    per_head = [g[:, hg + hd:hg + hd + 1] * o_s[hd * qt:(hd + 1) * qt]
                + g[:, 2 * hg + hd:2 * hg + hd + 1] * o_w[hd * qt:(hd + 1) * qt] for hd in range(hg)]
    o_ref[0] = (oc_ref[0] + _pack_heads(per_head)).astype(o_ref.dtype)


def _nsa_layer(x, nw, shift, scale, gate, w_in, cmp_pos, cmp_w1, cmp_w2, qk_norm, w_out):
    b, s, d = x.shape
    t = b * s
    g_, hg, dk = NSA_KV_GROUPS, NSA_HEADS_PER_GROUP, NSA_HEAD_DIM
    qd, kd = NSA_HEADS * dk, g_ * dk
    tm = ROW_TILE
    wl = NSA_QK_WIDTH
    n_main = qd + 6 * kd
    assert w_in.shape[1] == n_main + 3 * NSA_HEADS and s % tm == 0
    gate_cols = w_in[:, n_main:].reshape(d, 3, g_, hg).transpose(0, 2, 1, 3).reshape(d, g_, 3 * hg)
    gate_cols = jnp.pad(gate_cols, ((0, 0), (0, 0), (0, NSA_PROJ_PAD - 3 * hg))).reshape(d, g_ * NSA_PROJ_PAD)
    w_pad = jnp.concatenate([w_in[:, :n_main], gate_cols], axis=1).astype(BF16)

    def head_onehot(width):
        lane_head = jnp.arange(width)[:, None] // dk
        return (lane_head == jnp.arange(V7X_LANES)[None, :]).astype(BF16)

    pq, pk = head_onehot(qd), head_onehot(kd)

    def tile_w(wv, reps):
        return jnp.tile(wv, reps).reshape(1, reps * dk)

    slopes = jnp.exp2(-8.0 * (jnp.arange(NSA_HEADS, dtype=F32) + 1) / NSA_HEADS)
    slope_parts = jnp.stack([p.astype(F32) for p in _split_bf16(slopes * LOG2_E, 3)], axis=-1)
    slope_rows = jnp.broadcast_to((slopes * LOG2_E).reshape(g_, hg, 1), (g_, hg, wl))
    q_lanes = jnp.zeros((NSA_HEADS, wl), F32).at[:, dk:dk + 6].set(jnp.concatenate([slope_parts, slope_parts], axis=-1))

    consts = [pq, pq.T, pk, pk.T, tile_w(qk_norm[0], NSA_HEADS), tile_w(qk_norm[2], g_), tile_w(qk_norm[3], g_), q_lanes]
    full2 = lambda arr: pl.BlockSpec(arr.shape, lambda i, j: (0, 0))
    mod_spec = pl.BlockSpec((1, 1, d), lambda i, j: (i, 0, 0))
    grp = lambda dt: jax.ShapeDtypeStruct((b, g_, s, wl), dt)
    grp_spec = pl.BlockSpec((1, g_, tm, wl), lambda i, j: (i, 0, j, 0))
    tokf = jax.ShapeDtypeStruct((b, s, kd), F32)
    tokf_spec = pl.BlockSpec((1, tm, kd), lambda i, j: (i, j, 0))
    q5, ks4, vs4, kw4, vw4, kc, vc, gates = pl.pallas_call(
        functools.partial(_nsa_proj_kernel, tm, qd, kd),
        out_shape=[jax.ShapeDtypeStruct((b, g_, hg, s, wl), BF16), grp(BF16), grp(BF16), grp(BF16), grp(BF16),
                   tokf, tokf, grp(F32)],
        grid=(b, s // tm),
        in_specs=[pl.BlockSpec((1, tm, d), lambda i, j: (i, j, 0)), pl.BlockSpec((1, d), lambda i, j: (0, 0)),
                  mod_spec, mod_spec, full2(w_pad)] + [full2(cst) for cst in consts],
        out_specs=[pl.BlockSpec((1, g_, hg, tm, wl), lambda i, j: (i, 0, 0, j, 0)),
                   grp_spec, grp_spec, grp_spec, grp_spec, tokf_spec, tokf_spec, grp_spec],
        compiler_params=_params("parallel", "parallel"),
        name="nsa_proj",
    )(x, nw.reshape(1, d), shift[:, None], scale[:, None], w_pad, *consts)

    nsub = s // CMP_STRIDE
    half = CMP_STRIDE * dk
    sub_rows = lambda arr: arr.reshape(b, s, g_, dk).transpose(0, 2, 1, 3).reshape(b, g_, nsub, half)
    kc3, vc3 = sub_rows(kc), sub_rows(vc)

    pos = jnp.stack([cmp_pos[0, :CMP_STRIDE].reshape(half), cmp_pos[0, CMP_STRIDE:].reshape(half),
                     cmp_pos[1, :CMP_STRIDE].reshape(half), cmp_pos[1, CMP_STRIDE:].reshape(half)])
    w1b = cmp_w1.astype(BF16)
    w2b = jnp.pad(cmp_w2, ((0, 0), (0, 0), (0, wl - dk))).astype(BF16)
    nk_lanes = jnp.pad(qk_norm[1], (0, wl - dk)).reshape(1, wl)
    cmp_in = pl.BlockSpec((1, 1, nsub, half), lambda i, j: (i, j, 0, 0))
    cmp_out = pl.BlockSpec((1, 1, nsub, wl), lambda i, j: (i, j, 0, 0))
    kcc, vcc = pl.pallas_call(
        functools.partial(_nsa_compress_kernel, half),
        out_shape=[jax.ShapeDtypeStruct((b, g_, nsub, wl), BF16)] * 2,
        grid=(b, g_),
        in_specs=[cmp_in, cmp_in, pl.BlockSpec(pos.shape, lambda i, j: (0, 0)),
                  pl.BlockSpec(w1b.shape, lambda i, j: (0, 0, 0)), pl.BlockSpec(w2b.shape, lambda i, j: (0, 0, 0)),
                  pl.BlockSpec((1, wl), lambda i, j: (0, 0))],
        out_specs=[cmp_out, cmp_out],
        compiler_params=_params("parallel", "parallel"),
        name="nsa_compress",
    )(kc3, vc3, pos, w1b, w2b, nk_lanes)

    nsel = s // SEL_LEN
    kt = min(NSA_KV_TILE, s)
    nkt = s // kt
    seq_kv = lambda n: pl.BlockSpec((1, 1, n, wl), lambda i, j, k, *_: (i, j, 0, 0))

    def tile_specs(qt):
        assert s % qt == 0 and s >= WINDOW + qt and kt % qt == 0
        return (pl.BlockSpec((1, 1, hg, qt, wl), lambda i, j, k, *_: (i, j, 0, k, 0)),
                pl.BlockSpec((1, qt, hg * dk), lambda i, j, k, *_: (i, k, j)),
                pl.BlockSpec((1, 1, qt, wl), lambda i, j, k, *_: (i, j, k, 0)),
                pl.BlockSpec((1, 1, qt, nsel), lambda i, j, k, *_: (i, j, k, 0)))

    qt = NSA_Q_TILE
    q_spec, o_spec, gate_spec, sel_spec = tile_specs(qt)
    oc, sel = pl.pallas_call(
        functools.partial(_nsa_cmp_kernel, qt, nsub, nsel),
        out_shape=[jax.ShapeDtypeStruct((b, s, qd), F32), jax.ShapeDtypeStruct((b, g_, s, nsel), BF16)],
        grid=(b, g_, s // qt),
        in_specs=[q_spec, seq_kv(nsub), seq_kv(nsub), gate_spec],
        out_specs=[o_spec, sel_spec],
        compiler_params=_params("parallel", "parallel", "parallel"),
        name="nsa_cmp_select",
    )(q5, kcc, vcc, gates)

    assert nsel <= wl
    block_onehot = (jnp.arange(s)[:, None] // SEL_LEN == jnp.arange(wl)[None, :]).astype(BF16)

    qt = NSA_ATTN_Q_TILE
    nq = s // qt
    q_spec, o_spec, gate_spec, sel_spec = tile_specs(qt)
    active = sel.reshape(b, g_, nq, qt, nkt, kt // SEL_LEN).max(axis=(3, 5)) > 0
    order = jnp.sort(jnp.where(active, 0, nkt) + jnp.arange(nkt, dtype=jnp.int32), axis=-1)
    tiles = (order % nkt).astype(jnp.int32).reshape(-1)
    counts = active.sum(axis=-1).astype(jnp.int32).reshape(-1)

    o3 = pl.pallas_call(
        functools.partial(_nsa_attn_kernel, qt, kt, nsel, nkt, nq),
        out_shape=jax.ShapeDtypeStruct((b, s, qd), BF16),
        grid_spec=pltpu.PrefetchScalarGridSpec(
            num_scalar_prefetch=2,
            grid=(b, g_, nq),
            in_specs=[q_spec, seq_kv(s), seq_kv(s), seq_kv(s), seq_kv(s),
                      pl.BlockSpec((s, wl), lambda i, j, k, *_: (0, 0)), sel_spec, gate_spec,
                      pl.BlockSpec((1, hg, wl), lambda i, j, k, *_: (j, 0, 0)), o_spec],
            out_specs=o_spec,
        ),
        compiler_params=_params("parallel", "parallel", "arbitrary"),
        name="nsa_attention",
    )(tiles, counts, q5, ks4, vs4, kw4, vw4, block_onehot, sel, gates, slope_rows, oc)
    return _outproj(x.reshape(t, d), o3.reshape(t, qd), gate, w_out, s).reshape(b, s, d)


def kernel(x, c, ada_w, ada_b, norm_mix_w, norm_ffn_w, ffn_w_gate, ffn_w_up, ffn_w_down, nsa_w_in, nsa_cmp_pos, nsa_cmp_w1, nsa_cmp_w2, nsa_qk_norm, nsa_w_out, rwkv_mu, rwkv_w_rkv, rwkv_w0, rwkv_wd_a, rwkv_wd_b, rwkv_a0, rwkv_wa_a, rwkv_wa_b, rwkv_wg_a, rwkv_wg_b, rwkv_k_k, rwkv_k_a, rwkv_r_k, rwkv_ln_w, rwkv_ln_b, rwkv_w_out, pool_w, pool_b, pool_scale):
    b, s, d = x.shape
    depth = ada_w.shape[0]
    mod = _ada_mod(c, ada_w, ada_b)
    for i in range(depth):
        sh_m, sc_m, g_m, sh_f, sc_f, g_f = jnp.split(mod[i], N_MOD, axis=-1)
        kind, j = i % N_MIXERS, i // N_MIXERS
        if kind == 0:
            x = _nsa_layer(x, norm_mix_w[i], sh_m, sc_m, g_m, nsa_w_in[j], nsa_cmp_pos[j], nsa_cmp_w1[j], nsa_cmp_w2[j],
                           nsa_qk_norm[j], nsa_w_out[j])
        elif kind == 1:
            x = _rwkv_layer(x, norm_mix_w[i], sh_m, sc_m, g_m, rwkv_mu[j], rwkv_w_rkv[j], rwkv_w0[j], rwkv_wd_a[j],
                            rwkv_wd_b[j], rwkv_a0[j], rwkv_wa_a[j], rwkv_wa_b[j], rwkv_wg_a[j], rwkv_wg_b[j],
                            rwkv_k_k[j], rwkv_k_a[j], rwkv_r_k[j], rwkv_ln_w[j], rwkv_ln_b[j], rwkv_w_out[j])
        elif kind == 2:
            x = _pool_layer(x, norm_mix_w[i], sh_m, sc_m, g_m, pool_w[j], pool_b[j], pool_scale[j])
        x = _ffn(x.reshape(b * s, d), norm_ffn_w[i], sh_f, sc_f, g_f, ffn_w_gate[i], ffn_w_up[i], ffn_w_down[i], s).reshape(b, s, d)
    return x
```

```python
import functools
import math

import jax
import jax.numpy as jnp
from jax import lax
from jax.experimental import pallas as pl
from jax.experimental.pallas import tpu as pltpu

F32 = jnp.float32
BF16 = jnp.bfloat16

V7X_LANES = 128
V7X_VMEM_LIMIT_BYTES = 56 * 1024 * 1024

RMS_EPS = 1e-6
N_MOD = 6
N_MIXERS = 3

NSA_HEAD_DIM = 64
NSA_KV_GROUPS = 4
NSA_HEADS_PER_GROUP = 4
NSA_HEADS = NSA_KV_GROUPS * NSA_HEADS_PER_GROUP
CMP_LEN = 32
CMP_STRIDE = 16
SEL_LEN = 64
SEL_TOPK = 16
WINDOW = 512
NSA_SCALE = NSA_HEAD_DIM ** -0.5
LOG2_E = math.log2(math.e)
SEL_BIG = 1e9
MASKED_SCORE = -1e30
NSA_UNSELECTED = -(2.0 ** 100)

RWKV_HEAD_DIM = 64
RWKV_GN_EPS = 64e-5
RWKV_CHUNK = 64

POOL_WINDOWS = (2, 4, 8, 16)
POOL_HALO = 16

ROW_TILE = 512
FFN_ROW_TILE = 1024
FFN_COL_CHUNK = 256


def _params(*sem):
    return pltpu.CompilerParams(dimension_semantics=sem, vmem_limit_bytes=V7X_VMEM_LIMIT_BYTES)


def _sigmoid(z):
    return 1.0 / (1.0 + jnp.exp(-z))


def _normmod(x, nw, shift, scale):
    ms = jnp.mean(x * x, axis=-1, keepdims=True)
    return (x * lax.rsqrt(ms + RMS_EPS) * nw) * (1.0 + scale) + shift


def _dot(a, b):
    return jnp.dot(a.astype(BF16), b.astype(BF16), preferred_element_type=F32)


def _dot_nt(a, b, out_dtype=F32):
    out = lax.dot_general(a.astype(BF16), b.astype(BF16), (((1,), (1,)), ((), ())), preferred_element_type=F32)
    return out.astype(out_dtype)


def _dot_tn(a, b):
    return lax.dot_general(a.astype(BF16), b.astype(BF16), (((0,), (0,)), ((), ())), preferred_element_type=F32)


def _split_bf16(x, parts):
    out = []
    for _ in range(parts):
        p = x.astype(BF16)
        out.append(p)
        x = x - p.astype(F32)
    return out


def _mod_kernel(c_ref, w_ref, b_ref, o_ref):
    c = c_ref[...]
    ca = c * _sigmoid(c)
    o_ref[0] = _dot(ca, w_ref[0]) + b_ref[0]


def _ada_mod(c, ada_w, ada_b):
    depth, d, n = ada_w.shape
    b = c.shape[0]
    rows = 8
    tn = 1536
    assert b <= rows and n % tn == 0
    c_pad = jnp.zeros((rows, d), F32).at[:b].set(c)
    out = pl.pallas_call(
        _mod_kernel,
        out_shape=jax.ShapeDtypeStruct((depth, rows, n), F32),
        grid=(depth, n // tn),
        in_specs=[
            pl.BlockSpec((rows, d), lambda i, j: (0, 0)),
            pl.BlockSpec((1, d, tn), lambda i, j: (i, 0, j)),
            pl.BlockSpec((1, 1, tn), lambda i, j: (i, 0, j)),
        ],
        out_specs=pl.BlockSpec((1, rows, tn), lambda i, j: (i, 0, j)),
        compiler_params=_params("parallel", "parallel"),
        name="ada_mod",
    )(c_pad, ada_w, ada_b.reshape(depth, 1, n))
    return out[:, :b]


def _ffn_kernel(x_ref, nw_ref, sh_ref, sc_ref, g_ref, wg_ref, wu_ref, wd_ref, o_ref):
    x = x_ref[...]
    h = _normmod(x, nw_ref[...], sh_ref[0], sc_ref[0]).astype(BF16)
    f = wg_ref.shape[1]
    acts = []
    for c0 in range(0, f, FFN_COL_CHUNK):
        cols = slice(c0, c0 + FFN_COL_CHUNK)
        g = jnp.dot(h, wg_ref[:, cols], preferred_element_type=F32)
        u = jnp.dot(h, wu_ref[:, cols], preferred_element_type=F32)
        acts.append((g * _sigmoid(g) * u).astype(BF16))
    o_ref[...] = x + g_ref[0] * jnp.dot(jnp.concatenate(acts, axis=1), wd_ref[...], preferred_element_type=F32)


def _ffn(x2, nw, shift, scale, gate, wg, wu, wd, seq):
    t, d = x2.shape
    f = wg.shape[1]
    tm = FFN_ROW_TILE
    assert t % tm == 0 and seq % tm == 0 and f % FFN_COL_CHUNK == 0
    tpb = seq // tm
    mod_spec = pl.BlockSpec((1, 1, d), lambda i: (i // tpb, 0, 0))
    resident = lambda shape: pl.BlockSpec(shape, lambda i: (0, 0), pipeline_mode=pl.Buffered(1))
    return pl.pallas_call(
        _ffn_kernel,
        out_shape=jax.ShapeDtypeStruct((t, d), F32),
        grid=(t // tm,),
        in_specs=[
            pl.BlockSpec((tm, d), lambda i: (i, 0)),
            pl.BlockSpec((1, d), lambda i: (0, 0)),
            mod_spec, mod_spec, mod_spec,
            resident((d, f)), resident((d, f)), resident((f, d)),
        ],
        out_specs=pl.BlockSpec((tm, d), lambda i: (i, 0)),
        compiler_params=_params("parallel"),
        name="ffn",
    )(x2, nw.reshape(1, d), shift[:, None], scale[:, None], gate[:, None],
      wg.astype(BF16), wu.astype(BF16), wd.astype(BF16))


def _outproj_kernel(x_ref, z_ref, g_ref, w_ref, o_ref):
    o_ref[...] = x_ref[...] + g_ref[0] * _dot(z_ref[...], w_ref[...])


def _outproj(x2, z2, gate, w, seq):
    t, d = x2.shape
    k = z2.shape[1]
    tm = ROW_TILE
    tpb = seq // tm
    return pl.pallas_call(
        _outproj_kernel,
        out_shape=jax.ShapeDtypeStruct((t, d), F32),
        grid=(t // tm,),
        in_specs=[
            pl.BlockSpec((tm, d), lambda i: (i, 0)),
            pl.BlockSpec((tm, k), lambda i: (i, 0)),
            pl.BlockSpec((1, 1, d), lambda i: (i // tpb, 0, 0)),
            pl.BlockSpec((k, d), lambda i: (0, 0)),
        ],
        out_specs=pl.BlockSpec((tm, d), lambda i: (i, 0)),
        compiler_params=_params("parallel"),
        name="outproj",
    )(x2, z2, gate[:, None], w.astype(BF16))


def _pool_kernel(tm, gd, x_ref, nw_ref, sh_ref, sc_ref, g_ref, pw_ref, pb_ref, ps_ref, o_ref, ext_scr):
    s = pl.program_id(1)
    x = x_ref[0]
    h = _normmod(x, nw_ref[...], sh_ref[0], sc_ref[0])

    @pl.when(s == 0)
    def _():
        ext_scr[0:POOL_HALO, :] = jnp.zeros((POOL_HALO, x.shape[1]), F32)

    ext_scr[POOL_HALO:POOL_HALO + tm, :] = h
    row = s * tm + lax.broadcasted_iota(jnp.int32, (tm, 1), 0)
    ys = []
    for gi, win in enumerate(POOL_WINDOWS):
        lanes = slice(gi * gd, (gi + 1) * gd)
        hg = h[:, lanes]
        acc = hg
        for k in range(1, win):
            acc = acc + ext_scr[POOL_HALO - k:POOL_HALO - k + tm, lanes]
        cnt = jnp.minimum(row + 1, win).astype(F32)
        ys.append(_dot(acc / cnt - hg, pw_ref[gi]))
    y = (jnp.concatenate(ys, axis=-1) + pb_ref[...]) * ps_ref[...]
    o_ref[0] = x + g_ref[0] * y
    ext_scr[0:POOL_HALO, :] = ext_scr[tm:tm + POOL_HALO, :]


def _pool_layer(x, nw, shift, scale, gate, pw, pb, ps):
    b, s, d = x.shape
    tm = ROW_TILE
    gd = d // len(POOL_WINDOWS)
    assert s % tm == 0 and max(POOL_WINDOWS) <= POOL_HALO
    mod_spec = pl.BlockSpec((1, 1, d), lambda i, j: (i, 0, 0))
    vec_spec = pl.BlockSpec((1, d), lambda i, j: (0, 0))
    return pl.pallas_call(
        functools.partial(_pool_kernel, tm, gd),
        out_shape=jax.ShapeDtypeStruct((b, s, d), F32),
        grid=(b, s // tm),
        in_specs=[
            pl.BlockSpec((1, tm, d), lambda i, j: (i, j, 0)),
            vec_spec, mod_spec, mod_spec, mod_spec,
            pl.BlockSpec(pw.shape, lambda i, j: (0, 0, 0)),
            vec_spec, vec_spec,
        ],
        out_specs=pl.BlockSpec((1, tm, d), lambda i, j: (i, j, 0)),
        scratch_shapes=[pltpu.VMEM((tm + POOL_HALO, d), F32)],
        compiler_params=_params("arbitrary", "arbitrary"),
        name="pool_mixer",
    )(x, nw.reshape(1, d), shift[:, None], scale[:, None], gate[:, None],
      pw.astype(BF16), pb.reshape(1, d), ps.reshape(1, d))


def _rwkv_proj_kernel(tm, x_ref, nw_ref, sh_ref, sc_ref, mu_ref, wr_ref, wk_ref, wv_ref,
                      w0_ref, wda_ref, wdb_ref, a0_ref, waa_ref, wab_ref, wga_ref, wgb_ref,
                      r_ref, k_ref, v_ref, lw_ref, a_ref, g_ref, ext_scr):
    s = pl.program_id(1)
    h = _normmod(x_ref[0], nw_ref[...], sh_ref[0], sc_ref[0])

    @pl.when(s == 0)
    def _():
        ext_scr[0:8, :] = jnp.zeros((8, h.shape[1]), F32)

    ext_scr[8:8 + tm, :] = h
    xx = ext_scr[7:7 + tm, :] - h

    def mix(i):
        return h + xx * mu_ref[i:i + 1, :]

    r_ref[0] = _dot(mix(0), wr_ref[...])
    k_ref[0] = _dot(mix(2), wk_ref[...])
    v_ref[0] = _dot(mix(3), wv_ref[...])
    dw = w0_ref[...] + _dot(jnp.tanh(_dot(mix(1), wda_ref[...])), wdb_ref[...])
    softplus_neg = jnp.maximum(-dw, 0.0) + jnp.log(1.0 + jnp.exp(-jnp.abs(dw)))
    lw_ref[0] = -jnp.exp(-softplus_neg - 0.5)
    a_ref[0] = _sigmoid(a0_ref[...] + _dot(_dot(mix(4), waa_ref[...]), wab_ref[...]))
    g_ref[0] = _dot(_sigmoid(_dot(mix(5), wga_ref[...])), wgb_ref[...])
    ext_scr[0:8, :] = ext_scr[tm:tm + 8, :]


def _rwkv_scan_kernel(ct, npair, r_ref, k_ref, v_ref, lw_ref, a_ref, g_ref, kk_ref, ka_ref, rk_ref, lnw_ref, lnb_ref,
                      z_ref, s_scr):
    c, n = RWKV_CHUNK, RWKV_HEAD_DIM
    w = 2 * n
    shift = int(math.log2(n))

    @pl.when(pl.program_id(2) == 0)
    def _():
        s_scr[...] = jnp.zeros_like(s_scr)

    head0 = lax.broadcasted_iota(jnp.int32, (1, w), 1) < n
    ri = lax.broadcasted_iota(jnp.int32, (w, w), 0)
    ci = lax.broadcasted_iota(jnp.int32, (w, w), 1)
    same_head = jnp.where((ri >> shift) == (ci >> shift), 1.0, 0.0).astype(BF16)
    strict, incl, eye = ri > ci, ri >= ci, ri == ci
    rt = lax.broadcasted_iota(jnp.int32, (ct, ct), 0)
    cc = lax.broadcasted_iota(jnp.int32, (ct, ct), 1)
    cshift = int(math.log2(c))
    chunk_tri = jnp.where(((rt >> cshift) == (cc >> cshift)) & (rt >= cc), 1.0, 0.0).astype(BF16)
    nb = ct // c

    def head_sum(xf):
        return sum(jnp.dot(p, same_head, preferred_element_type=F32) for p in _split_bf16(xf, 2))

    def stack(xf):
        x3 = xf.reshape(nb, c, w)
        return jnp.concatenate([jnp.where(head0, x3, 0.0), jnp.where(head0, 0.0, x3)], axis=1)

    def bmm(x, y):
        return jnp.einsum('bij,bjk->bik', x.astype(BF16), y.astype(BF16), preferred_element_type=F32)

    def bmm_nt(x, y):
        return jnp.einsum('bik,bjk->bij', x.astype(BF16), y.astype(BF16), preferred_element_type=F32)

    pairs = []
    stacked = []
    for hp in range(npair):
        ln = slice(hp * w, (hp + 1) * w)
        r, k, v, lw, a = r_ref[0, :, ln], k_ref[0, :, ln], v_ref[0, :, ln], lw_ref[0, :, ln], a_ref[0, :, ln]
        kk = k * kk_ref[:, ln]
        kkn = kk * lax.rsqrt(jnp.maximum(head_sum(kk * kk), 1e-24))
        k2 = k * (1.0 + (a - 1.0) * ka_ref[:, ln])
        cum = sum(jnp.dot(chunk_tri, p, preferred_element_type=F32) for p in _split_bf16(lw, 3))
        einv = jnp.exp(-cum)
        pairs.append((r, k2, v))
        stacked.append((stack(-kkn * jnp.exp(cum - lw)), stack(r * jnp.exp(cum)), stack(kkn * a * einv),
                        stack(k2 * einv), stack(v), jnp.exp(cum.reshape(nb, c, w)[:, c - 1:c, :])))
    at, rt_, kb, kq, vs, pc = (jnp.concatenate([st[i] for st in stacked], axis=0) for i in range(6))
    nbt = npair * nb
    kbe, kqe = kb * pc, kq * pc
    tt = bmm_nt(jnp.concatenate([at, rt_], axis=1), jnp.concatenate([kb, kq], axis=1))
    a_m = jnp.where(strict, tt[:, :w, :w], 0.0)
    b_m = jnp.where(strict, tt[:, :w, w:], 0.0)
    ar_m = jnp.where(incl, tt[:, w:, :w], 0.0)
    br_m = jnp.where(incl, tt[:, w:, w:], 0.0)
    rsum = jnp.where(eye, 1.0, 0.0) + a_m
    pw = bmm(a_m, a_m)
    for _ in range(cshift - 2):
        both = bmm(pw, jnp.concatenate([rsum, pw], axis=2))
        rsum = rsum + both[:, :, :w]
        pw = both[:, :, w:]
    tinv = rsum + bmm(pw, rsum)
    w12 = bmm(tinv, jnp.concatenate([at, bmm(b_m, vs)], axis=2))
    xmat = jnp.concatenate([w12, jnp.concatenate([jnp.zeros_like(vs), vs], axis=2)], axis=1)
    kbe_t = jnp.stack([kbe[i].T for i in range(nbt)])
    kqe_t = jnp.stack([kqe[i].T for i in range(nbt)])
    lhs = jnp.concatenate([jnp.concatenate([ar_m, br_m], axis=2), jnp.concatenate([kbe_t, kqe_t], axis=2)], axis=1)
    res = bmm(lhs, xmat)
    g_m = rt_ + res[:, :w, :w]
    y0 = res[:, :w, w:]
    mt = jnp.where(eye, jnp.broadcast_to(pc, (nbt, w, w)), 0.0) + res[:, w:, :w]
    nt = res[:, w:, w:]

    states = [s_scr[hp] for hp in range(npair)]
    ys = [[] for _ in range(npair)]
    for idx in range(nb):
        for hp in range(npair):
            bi = hp * nb + idx
            step = _dot(jnp.concatenate([g_m[bi], mt[bi]], axis=0), states[hp])
            yst = step[:w] + y0[bi]
            states[hp] = step[w:] + nt[bi]
            ys[hp].append(yst[:c] + yst[c:])
    for hp in range(npair):
        ln = slice(hp * w, (hp + 1) * w)
        s_scr[hp] = states[hp]
        y = jnp.concatenate(ys[hp], axis=0)
        r, k2, v = pairs[hp]
        mean = head_sum(y) * (1.0 / n)
        dlt = y - mean
        var = head_sum(dlt * dlt) * (1.0 / n)
        yn = dlt * lax.rsqrt(var + RWKV_GN_EPS) * lnw_ref[:, ln] + lnb_ref[:, ln]
        bonus = head_sum(r * k2 * rk_ref[:, ln]) * v
        z_ref[0, :, ln] = ((yn + bonus) * g_ref[0, :, ln]).astype(z_ref.dtype)


def _rwkv_layer(x, nw, shift, scale, gate, mu, w_rkv, w0, wd_a, wd_b, a0, wa_a, wa_b, wg_a, wg_b,
                k_k, k_a, r_k, ln_w, ln_b, w_out):
    b, s, d = x.shape
    tm = ROW_TILE
    assert s % tm == 0
    mod_spec = pl.BlockSpec((1, 1, d), lambda i, j: (i, 0, 0))
    vec_spec = pl.BlockSpec((1, d), lambda i, j: (0, 0))
    tok_spec = pl.BlockSpec((1, tm, d), lambda i, j: (i, j, 0))

    def full(arr):
        return pl.BlockSpec(arr.shape, lambda i, j: (0,) * arr.ndim)

    weights = [w_rkv[0].astype(BF16), w_rkv[1].astype(BF16), w_rkv[2].astype(BF16),
               w0.reshape(1, d), wd_a.astype(BF16), wd_b.astype(BF16),
               a0.reshape(1, d), wa_a.astype(BF16), wa_b.astype(BF16),
               wg_a.astype(BF16), wg_b.astype(BF16)]
    tok_shape = jax.ShapeDtypeStruct((b, s, d), F32)
    r, k, v, lw, a, g = pl.pallas_call(
        functools.partial(_rwkv_proj_kernel, tm),
        out_shape=[tok_shape] * 6,
        grid=(b, s // tm),
        in_specs=[tok_spec, vec_spec, mod_spec, mod_spec, full(mu)] + [full(wt) for wt in weights],
        out_specs=[tok_spec] * 6,
        scratch_shapes=[pltpu.VMEM((tm + 8, d), F32)],
        compiler_params=_params("arbitrary", "arbitrary"),
        name="rwkv_proj",
    )(x, nw.reshape(1, d), shift[:, None], scale[:, None], mu, *weights)

    ct = 512
    npair = 4
    pair = 2 * RWKV_HEAD_DIM
    lanes = npair * pair
    assert s % ct == 0 and d % lanes == 0 and pair == V7X_LANES
    seq_spec = pl.BlockSpec((1, ct, lanes), lambda i, p, j: (i, j, p))
    par_spec = pl.BlockSpec((1, lanes), lambda i, p, j: (0, p))
    z = pl.pallas_call(
        functools.partial(_rwkv_scan_kernel, ct, npair),
        out_shape=jax.ShapeDtypeStruct((b, s, d), BF16),
        grid=(b, d // lanes, s // ct),
        in_specs=[seq_spec] * 6 + [par_spec] * 5,
        out_specs=seq_spec,
        scratch_shapes=[pltpu.VMEM((npair, pair, pair), F32)],
        compiler_params=_params("parallel", "parallel", "arbitrary"),
        name="rwkv_scan",
    )(r, k, v, lw, a, g, k_k.reshape(1, d), k_a.reshape(1, d), r_k.reshape(1, d), ln_w.reshape(1, d), ln_b.reshape(1, d))
    return _outproj(x.reshape(b * s, d), z.reshape(b * s, d), gate, w_out, s).reshape(b, s, d)


NSA_Q_TILE = 512
NSA_CMP_SUB_TILE = 128
NSA_ATTN_Q_TILE = 512
NSA_KV_TILE = 512
NSA_PROJ_PAD = 128
NSA_QK_WIDTH = 128


def _head_rms(xf, p_ref, pt_ref, wvec):
    sums = sum(jnp.dot(part, p_ref[...], preferred_element_type=F32) for part in _split_bf16(xf * xf, 2))
    inv = lax.rsqrt(sums * (1.0 / NSA_HEAD_DIM) + RMS_EPS)
    inv_full = sum(jnp.dot(part, pt_ref[...], preferred_element_type=F32) for part in _split_bf16(inv, 3))
    return xf * inv_full * wvec


def _pos_lanes(pos):
    lane = lax.broadcasted_iota(jnp.int32, (1, NSA_QK_WIDTH), 1)
    hi = ((pos >> 6) << 6).astype(F32)
    lo = (pos & 63).astype(F32)
    d = NSA_HEAD_DIM
    ones = jnp.where((lane >= d + 6) & (lane < d + 9), 1.0, 0.0)
    return jnp.where((lane >= d) & (lane < d + 3), hi, jnp.where((lane >= d + 3) & (lane < d + 6), lo, ones))


def _nsa_proj_kernel(tm, qd, kd, x_ref, nw_ref, sh_ref, sc_ref, w_ref, pq_ref, pqt_ref, pk_ref, pkt_ref,
                     nq_ref, nks_ref, nkw_ref, qf_ref,
                     q_ref, ks_ref, vs_ref, kw_ref, vw_ref, kc_ref, vc_ref, gt_ref):
    hg, dk = NSA_HEADS_PER_GROUP, NSA_HEAD_DIM
    h = _normmod(x_ref[0], nw_ref[...], sh_ref[0], sc_ref[0])
    proj = _dot(h, w_ref[...])
    low = lax.broadcasted_iota(jnp.int32, (1, NSA_QK_WIDTH), 1) < dk
    key_lanes = _pos_lanes(pl.program_id(1) * tm + lax.broadcasted_iota(jnp.int32, (tm, 1), 0))

    def heads(x, nheads):
        for pair in range(nheads // 2):
            xp = x[:, pair * 2 * dk:(pair + 1) * 2 * dk]
            yield 2 * pair, xp
            yield 2 * pair + 1, pltpu.roll(xp, dk, 1)

    qn = _head_rms(proj[:, :qd], pq_ref, pqt_ref, nq_ref[...]) * (NSA_SCALE * LOG2_E)
    for hd, xs in heads(qn, NSA_HEADS):
        q_ref[0, hd // hg, hd % hg] = jnp.where(low, xs, qf_ref[hd:hd + 1, :]).astype(BF16)
    o = qd
    for g, xs in heads(proj[:, o:o + kd], NSA_KV_GROUPS):
        kc_ref[0, g] = xs
    for g, xs in heads(proj[:, o + kd:o + 2 * kd], NSA_KV_GROUPS):
        vc_ref[0, g] = xs
    for g, xs in heads(_head_rms(proj[:, o + 2 * kd:o + 3 * kd], pk_ref, pkt_ref, nks_ref[...]), NSA_KV_GROUPS):
        ks_ref[0, g] = jnp.where(low, xs, key_lanes).astype(BF16)
    for g, xs in heads(proj[:, o + 3 * kd:o + 4 * kd], NSA_KV_GROUPS):
        vs_ref[0, g] = jnp.where(low, xs, 1.0).astype(BF16)
    for g, xs in heads(_head_rms(proj[:, o + 4 * kd:o + 5 * kd], pk_ref, pkt_ref, nkw_ref[...]), NSA_KV_GROUPS):
        kw_ref[0, g] = jnp.where(low, xs, key_lanes).astype(BF16)
    for g, xs in heads(proj[:, o + 5 * kd:o + 6 * kd], NSA_KV_GROUPS):
        vw_ref[0, g] = jnp.where(low, xs, 1.0).astype(BF16)
    o += 6 * kd
    for g in range(NSA_KV_GROUPS):
        gt_ref[0, g] = _sigmoid(proj[:, o + g * NSA_PROJ_PAD:o + (g + 1) * NSA_PROJ_PAD])


def _gelu_tanh(x):
    return 0.5 * x * (1.0 + jnp.tanh(math.sqrt(2.0 / math.pi) * (x + 0.044715 * (x * x * x))))


def _nsa_compress_kernel(half, kc_ref, vc_ref, pos_ref, w1_ref, w2_ref, nk_ref, kco_ref, vco_ref):
    nsub = kc_ref.shape[2]

    def comp(x, i):
        ya = _dot(x + pos_ref[2 * i:2 * i + 1, :], w1_ref[i, 0:half, :])
        yb = _dot(x + pos_ref[2 * i + 1:2 * i + 2, :], w1_ref[i, half:2 * half, :])
        hid = ya + pltpu.roll(yb, nsub - 1, 0)
        return _dot(_gelu_tanh(hid), w2_ref[i])

    kcm = comp(kc_ref[0, 0], 0)
    ms = jnp.sum(kcm * kcm, axis=-1, keepdims=True) * (1.0 / NSA_HEAD_DIM)
    block_end = lax.broadcasted_iota(jnp.int32, (nsub, 1), 0) * CMP_STRIDE + CMP_LEN - 1
    kco_ref[0, 0] = (kcm * lax.rsqrt(ms + RMS_EPS) * nk_ref[...] + _pos_lanes(block_end)).astype(BF16)
    vco_ref[0, 0] = comp(vc_ref[0, 0], 1).astype(BF16)


def _pack_heads(per_head):
    low = lax.broadcasted_iota(jnp.int32, (1, NSA_QK_WIDTH), 1) < NSA_HEAD_DIM
    pairs = [jnp.where(low, per_head[i], pltpu.roll(per_head[i + 1], NSA_HEAD_DIM, 1))
             for i in range(0, len(per_head), 2)]
    return jnp.concatenate(pairs, axis=-1)


def _nsa_cmp_kernel(qt, nc, nsel, q_ref, kc_ref, vc_ref, g_ref, oc_ref, sel_ref):
    hg = NSA_HEADS_PER_GROUP
    q0 = pl.program_id(2) * qt
    qs = NSA_CMP_SUB_TILE
    jo = lax.broadcasted_iota(jnp.int32, (nsel, nc), 0)
    no = lax.broadcasted_iota(jnp.int32, (nsel, nc), 1)
    ratio = SEL_LEN // CMP_STRIDE
    first = jo * ratio - (CMP_LEN // CMP_STRIDE - 1)
    overlap_t = jnp.where((no >= first) & (no < (jo + 1) * ratio) & (no < nc - 1), 1.0, 0.0).astype(BF16)
    n = lax.broadcasted_iota(jnp.int32, (qs, nc), 1)
    imp_parts = []
    for sub in range(qt // qs):
        rows = slice(sub * qs, (sub + 1) * qs)
        q = q_ref[0, 0, :, rows, :].reshape(hg * qs, NSA_QK_WIDTH)
        s3 = _dot_nt(q, kc_ref[0, 0]).reshape(hg, qs, nc)
        t = q0 + sub * qs + lax.broadcasted_iota(jnp.int32, (qs, nc), 0)
        mask = (t >= n * CMP_STRIDE + CMP_LEN - 1)[None]
        sm = jnp.where(mask, s3, MASKED_SCORE)
        e = jnp.exp2(sm - jnp.max(sm, axis=-1, keepdims=True))
        has_key = (q0 + sub * qs + lax.broadcasted_iota(jnp.int32, (1, qs, 1), 1)) >= CMP_LEN - 1
        p = e * jnp.where(has_key, 1.0 / jnp.maximum(jnp.sum(e, axis=-1, keepdims=True), 1e-30), 0.0)
        oc = _dot(p.reshape(hg * qs, nc), vc_ref[0, 0])
        g = g_ref[0, 0, rows, :]
        oc_ref[0, rows, :] = _pack_heads([oc[hd * qs:(hd + 1) * qs] * g[:, hd:hd + 1] for hd in range(hg)])
        psum = p[0] + p[1] + p[2] + p[3]
        imp_parts.append(sum(lax.dot_general(overlap_t, part, (((1,), (1,)), ((), ())), preferred_element_type=F32)
                             for part in _split_bf16(psum, 3)))
    imp_t = jnp.concatenate(imp_parts, axis=1)

    j = lax.broadcasted_iota(jnp.int32, (nsel, qt), 0)
    tt = q0 + lax.broadcasted_iota(jnp.int32, (nsel, qt), 1)
    valid = j * SEL_LEN <= tt
    qb = tt >> int(math.log2(SEL_LEN))
    forced = (j == 0) | (j == qb) | (j == qb - 1)
    score = jnp.where(valid, jnp.where(forced, SEL_BIG, imp_t), -SEL_BIG)
    jf = j.astype(F32)
    picked = -(2.0 ** 120)
    for _ in range(min(SEL_TOPK, nsel)):
        mx = jnp.max(score, axis=0, keepdims=True)
        jmin = jnp.min(jnp.where(score == mx, jf, float(nsel)), axis=0, keepdims=True)
        score = jnp.where(jf == jmin, picked, score)
    sel = jnp.where((score == picked) & valid, 1.0, 0.0)
    sel_ref[0, 0] = sel.T.astype(BF16)


def _nsa_attn_kernel(qt, kt, nsel, nkt, nq, tiles_ref, cnt_ref,
                     q_ref, ks_ref, vs_ref, kw_ref, vw_ref, oh_ref, sel_ref, g_ref, sl_ref, oc_ref, o_ref):
    hg, dk = NSA_HEADS_PER_GROUP, NSA_HEAD_DIM
    rows = hg * qt
    step = (pl.program_id(0) * NSA_KV_GROUPS + pl.program_id(1)) * nq + pl.program_id(2)
    q0 = pl.program_id(2) * qt
    lane = lax.broadcasted_iota(jnp.int32, (1, NSA_QK_WIDTH), 1)
    t_row = (q0 + lax.broadcasted_iota(jnp.int32, (qt, 1), 0)).astype(F32)
    q_heads = []
    for hd in range(hg):
        qh = q_ref[0, 0, hd]
        for i, part in enumerate(_split_bf16(-sl_ref[0, hd:hd + 1, :] * t_row, 3)):
            qh = jnp.where(lane == dk + 6 + i, part, qh)
        q_heads.append(qh)
    q = jnp.concatenate(q_heads, axis=0)
    unsel = (1.0 - sel_ref[0, 0]) * NSA_UNSELECTED
    if nsel < NSA_QK_WIDTH:
        unsel = jnp.concatenate([unsel, jnp.zeros((qt, NSA_QK_WIDTH - nsel), BF16)], axis=1)
    q_sel = jnp.concatenate([q, jnp.concatenate([unsel] * hg, axis=0)], axis=1)

    t_k = q0 + lax.broadcasted_iota(jnp.int32, (qt, kt), 0)
    c_k = lax.broadcasted_iota(jnp.int32, (qt, kt), 1)

    def sel_step(i, carry, diagonal):
        m, acc = carry
        k0 = pl.multiple_of(tiles_ref[step * nkt + i] * kt, kt)
        keys = jnp.concatenate([ks_ref[0, 0, pl.ds(k0, kt), :], oh_ref[pl.ds(k0, kt), :]], axis=1)
        sm = _dot_nt(q_sel, keys, BF16)
        if diagonal:
            causal = (k0 + c_k <= t_k)[None]
            sm = jnp.where(causal, sm.reshape(hg, qt, kt), MASKED_SCORE).reshape(rows, kt)
        m_new = jnp.maximum(m, jnp.max(sm, axis=-1, keepdims=True))
        alpha = jnp.exp2((m - m_new).astype(F32))
        acc = alpha * acc + _dot(jnp.exp2(sm - m_new), vs_ref[0, 0, pl.ds(k0, kt), :])
        return m_new, acc

    init = (jnp.full((rows, 1), MASKED_SCORE, BF16), jnp.zeros((rows, NSA_QK_WIDTH), F32))
    last = cnt_ref[step] - 1
    carry = lax.fori_loop(0, last, functools.partial(sel_step, diagonal=False), init)
    _, acc_s = sel_step(last, carry, diagonal=True)
    o_s = acc_s / jnp.maximum(acc_s[:, dk:dk + 1], 1e-30)

    wk = WINDOW + qt
    ws = pl.multiple_of(jnp.maximum(q0 - WINDOW, 0), qt)
    s3 = _dot_nt(q, kw_ref[0, 0, pl.ds(ws, wk), :], BF16).reshape(hg, qt, wk)
    dist = (q0 + lax.broadcasted_iota(jnp.int32, (qt, wk), 0)) - (ws + lax.broadcasted_iota(jnp.int32, (qt, wk), 1))
    mask = ((dist >= 0) & (dist < WINDOW))[None]
    sm = jnp.where(mask, s3, MASKED_SCORE).reshape(rows, wk)
    e = jnp.exp2(sm - jnp.max(sm, axis=-1, keepdims=True))
    acc_w = _dot(e, vw_ref[0, 0, pl.ds(ws, wk), :])
    o_w = acc_w / jnp.maximum(acc_w[:, dk:dk + 1], 1e-30)

    g = g_ref[0, 0]
    per_head = [g[:, hg + hd:hg + hd + 1] * o_s[hd * qt:(hd + 1) * qt]
                + g[:, 2 * hg + hd:2 * hg + hd + 1] * o_w[hd * qt:(hd + 1) * qt] for hd in range(hg)]
    o_ref[0] = (oc_ref[0] + _pack_heads(per_head)).astype(o_ref.dtype)


def _nsa_layer(x, nw, shift, scale, gate, w_in, cmp_pos, cmp_w1, cmp_w2, qk_norm, w_out):
    b, s, d = x.shape
    t = b * s
    g_, hg, dk = NSA_KV_GROUPS, NSA_HEADS_PER_GROUP, NSA_HEAD_DIM
    qd, kd = NSA_HEADS * dk, g_ * dk
    tm = ROW_TILE
    wl = NSA_QK_WIDTH
    n_main = qd + 6 * kd
    assert w_in.shape[1] == n_main + 3 * NSA_HEADS and s % tm == 0
    gate_cols = w_in[:, n_main:].reshape(d, 3, g_, hg).transpose(0, 2, 1, 3).reshape(d, g_, 3 * hg)
    gate_cols = jnp.pad(gate_cols, ((0, 0), (0, 0), (0, NSA_PROJ_PAD - 3 * hg))).reshape(d, g_ * NSA_PROJ_PAD)
    w_pad = jnp.concatenate([w_in[:, :n_main], gate_cols], axis=1).astype(BF16)

    def head_onehot(width):
        lane_head = jnp.arange(width)[:, None] // dk
        return (lane_head == jnp.arange(V7X_LANES)[None, :]).astype(BF16)

    pq, pk = head_onehot(qd), head_onehot(kd)

    def tile_w(wv, reps):
        return jnp.tile(wv, reps).reshape(1, reps * dk)

    slopes = jnp.exp2(-8.0 * (jnp.arange(NSA_HEADS, dtype=F32) + 1) / NSA_HEADS)
    slope_parts = jnp.stack([p.astype(F32) for p in _split_bf16(slopes * LOG2_E, 3)], axis=-1)
    slope_rows = jnp.broadcast_to((slopes * LOG2_E).reshape(g_, hg, 1), (g_, hg, wl))
    q_lanes = jnp.zeros((NSA_HEADS, wl), F32).at[:, dk:dk + 6].set(jnp.concatenate([slope_parts, slope_parts], axis=-1))

    consts = [pq, pq.T, pk, pk.T, tile_w(qk_norm[0], NSA_HEADS), tile_w(qk_norm[2], g_), tile_w(qk_norm[3], g_), q_lanes]
    full2 = lambda arr: pl.BlockSpec(arr.shape, lambda i, j: (0, 0))
    mod_spec = pl.BlockSpec((1, 1, d), lambda i, j: (i, 0, 0))
    grp = lambda dt: jax.ShapeDtypeStruct((b, g_, s, wl), dt)
    grp_spec = pl.BlockSpec((1, g_, tm, wl), lambda i, j: (i, 0, j, 0))
    q5, ks4, vs4, kw4, vw4, kc, vc, gates = pl.pallas_call(
        functools.partial(_nsa_proj_kernel, tm, qd, kd),
        out_shape=[jax.ShapeDtypeStruct((b, g_, hg, s, wl), BF16), grp(BF16), grp(BF16), grp(BF16), grp(BF16),
                   grp(F32), grp(F32), grp(F32)],
        grid=(b, s // tm),
        in_specs=[pl.BlockSpec((1, tm, d), lambda i, j: (i, j, 0)), pl.BlockSpec((1, d), lambda i, j: (0, 0)),
                  mod_spec, mod_spec, full2(w_pad)] + [full2(cst) for cst in consts],
        out_specs=[pl.BlockSpec((1, g_, hg, tm, wl), lambda i, j: (i, 0, 0, j, 0))] + [grp_spec] * 7,
        compiler_params=_params("parallel", "parallel"),
        name="nsa_proj",
    )(x, nw.reshape(1, d), shift[:, None], scale[:, None], w_pad, *consts)

    nsub = s // CMP_STRIDE
    half = CMP_STRIDE * wl
    kc3, vc3 = kc.reshape(b, g_, nsub, half), vc.reshape(b, g_, nsub, half)
    pad_tok = lambda arr: jnp.pad(arr, [(0, 0)] * (arr.ndim - 1) + [(0, wl - dk)])
    pos = pad_tok(cmp_pos).reshape(4, half)
    w1b = jnp.pad(cmp_w1.reshape(2, CMP_LEN, dk, -1), ((0, 0), (0, 0), (0, wl - dk), (0, 0))).reshape(2, 2 * half, -1).astype(BF16)
    w2b = jnp.pad(cmp_w2, ((0, 0), (0, 0), (0, wl - dk))).astype(BF16)
    nk_lanes = jnp.pad(qk_norm[1], (0, wl - dk)).reshape(1, wl)
    cmp_in = pl.BlockSpec((1, 1, nsub, half), lambda i, j: (i, j, 0, 0))
    cmp_out = pl.BlockSpec((1, 1, nsub, wl), lambda i, j: (i, j, 0, 0))
    kcc, vcc = pl.pallas_call(
        functools.partial(_nsa_compress_kernel, half),
        out_shape=[jax.ShapeDtypeStruct((b, g_, nsub, wl), BF16)] * 2,
        grid=(b, g_),
        in_specs=[cmp_in, cmp_in, pl.BlockSpec(pos.shape, lambda i, j: (0, 0)),
                  pl.BlockSpec(w1b.shape, lambda i, j: (0, 0, 0)), pl.BlockSpec(w2b.shape, lambda i, j: (0, 0, 0)),
                  pl.BlockSpec((1, wl), lambda i, j: (0, 0))],
        out_specs=[cmp_out, cmp_out],
        compiler_params=_params("parallel", "parallel"),
        name="nsa_compress",
    )(kc3, vc3, pos, w1b, w2b, nk_lanes)

    nsel = s // SEL_LEN
    kt = min(NSA_KV_TILE, s)
    nkt = s // kt
    seq_kv = lambda n: pl.BlockSpec((1, 1, n, wl), lambda i, j, k, *_: (i, j, 0, 0))

    def tile_specs(qt):
        assert s % qt == 0 and s >= WINDOW + qt and kt % qt == 0
        return (pl.BlockSpec((1, 1, hg, qt, wl), lambda i, j, k, *_: (i, j, 0, k, 0)),
                pl.BlockSpec((1, qt, hg * dk), lambda i, j, k, *_: (i, k, j)),
                pl.BlockSpec((1, 1, qt, wl), lambda i, j, k, *_: (i, j, k, 0)),
                pl.BlockSpec((1, 1, qt, nsel), lambda i, j, k, *_: (i, j, k, 0)))

    qt = NSA_Q_TILE
    q_spec, o_spec, gate_spec, sel_spec = tile_specs(qt)
    oc, sel = pl.pallas_call(
        functools.partial(_nsa_cmp_kernel, qt, nsub, nsel),
        out_shape=[jax.ShapeDtypeStruct((b, s, qd), F32), jax.ShapeDtypeStruct((b, g_, s, nsel), BF16)],
        grid=(b, g_, s // qt),
        in_specs=[q_spec, seq_kv(nsub), seq_kv(nsub), gate_spec],
        out_specs=[o_spec, sel_spec],
        compiler_params=_params("parallel", "parallel", "parallel"),
        name="nsa_cmp_select",
    )(q5, kcc, vcc, gates)

    assert nsel <= wl
    block_onehot = (jnp.arange(s)[:, None] // SEL_LEN == jnp.arange(wl)[None, :]).astype(BF16)

    qt = NSA_ATTN_Q_TILE
    nq = s // qt
    q_spec, o_spec, gate_spec, sel_spec = tile_specs(qt)
    active = sel.reshape(b, g_, nq, qt, nkt, kt // SEL_LEN).max(axis=(3, 5)) > 0
    order = jnp.sort(jnp.where(active, 0, nkt) + jnp.arange(nkt, dtype=jnp.int32), axis=-1)
    tiles = (order % nkt).astype(jnp.int32).reshape(-1)
    counts = active.sum(axis=-1).astype(jnp.int32).reshape(-1)

    o3 = pl.pallas_call(
        functools.partial(_nsa_attn_kernel, qt, kt, nsel, nkt, nq),
        out_shape=jax.ShapeDtypeStruct((b, s, qd), BF16),
        grid_spec=pltpu.PrefetchScalarGridSpec(
            num_scalar_prefetch=2,
            grid=(b, g_, nq),
            in_specs=[q_spec, seq_kv(s), seq_kv(s), seq_kv(s), seq_kv(s),
                      pl.BlockSpec((s, wl), lambda i, j, k, *_: (0, 0)), sel_spec, gate_spec,
                      pl.BlockSpec((1, hg, wl), lambda i, j, k, *_: (j, 0, 0)), o_spec],
            out_specs=o_spec,
        ),
        compiler_params=_params("parallel", "parallel", "arbitrary"),
        name="nsa_attention",
    )(tiles, counts, q5, ks4, vs4, kw4, vw4, block_onehot, sel, gates, slope_rows, oc)
    return _outproj(x.reshape(t, d), o3.reshape(t, qd), gate, w_out, s).reshape(b, s, d)


def kernel(x, c, ada_w, ada_b, norm_mix_w, norm_ffn_w, ffn_w_gate, ffn_w_up, ffn_w_down, nsa_w_in, nsa_cmp_pos, nsa_cmp_w1, nsa_cmp_w2, nsa_qk_norm, nsa_w_out, rwkv_mu, rwkv_w_rkv, rwkv_w0, rwkv_wd_a, rwkv_wd_b, rwkv_a0, rwkv_wa_a, rwkv_wa_b, rwkv_wg_a, rwkv_wg_b, rwkv_k_k, rwkv_k_a, rwkv_r_k, rwkv_ln_w, rwkv_ln_b, rwkv_w_out, pool_w, pool_b, pool_scale):
    b, s, d = x.shape
    depth = ada_w.shape[0]
    mod = _ada_mod(c, ada_w, ada_b)
    for i in range(depth):
        sh_m, sc_m, g_m, sh_f, sc_f, g_f = jnp.split(mod[i], N_MOD, axis=-1)
        kind, j = i % N_MIXERS, i // N_MIXERS
        if kind == 0:
            x = _nsa_layer(x, norm_mix_w[i], sh_m, sc_m, g_m, nsa_w_in[j], nsa_cmp_pos[j], nsa_cmp_w1[j], nsa_cmp_w2[j],
                           nsa_qk_norm[j], nsa_w_out[j])
        elif kind == 1:
            x = _rwkv_layer(x, norm_mix_w[i], sh_m, sc_m, g_m, rwkv_mu[j], rwkv_w_rkv[j], rwkv_w0[j], rwkv_wd_a[j],
                            rwkv_wd_b[j], rwkv_a0[j], rwkv_wa_a[j], rwkv_wa_b[j], rwkv_wg_a[j], rwkv_wg_b[j],
                            rwkv_k_k[j], rwkv_k_a[j], rwkv_r_k[j], rwkv_ln_w[j], rwkv_ln_b[j], rwkv_w_out[j])
        elif kind == 2:
            x = _pool_layer(x, norm_mix_w[i], sh_m, sc_m, g_m, pool_w[j], pool_b[j], pool_scale[j])
        x = _ffn(x.reshape(b * s, d), norm_ffn_w[i], sh_f, sc_f, g_f, ffn_w_gate[i], ffn_w_up[i], ffn_w_down[i], s).reshape(b, s, d)
    return x
```

```python
import functools
import math

import jax
import jax.numpy as jnp
from jax import lax
from jax.experimental import pallas as pl
from jax.experimental.pallas import tpu as pltpu

F32 = jnp.float32
BF16 = jnp.bfloat16

V7X_LANES = 128
V7X_VMEM_LIMIT_BYTES = 56 * 1024 * 1024

RMS_EPS = 1e-6
N_MOD = 6
N_MIXERS = 3

NSA_HEAD_DIM = 64
NSA_KV_GROUPS = 4
NSA_HEADS_PER_GROUP = 4
NSA_HEADS = NSA_KV_GROUPS * NSA_HEADS_PER_GROUP
CMP_LEN = 32
CMP_STRIDE = 16
SEL_LEN = 64
SEL_TOPK = 16
WINDOW = 512
NSA_SCALE = NSA_HEAD_DIM ** -0.5
LOG2_E = math.log2(math.e)
SEL_BIG = 1e9
MASKED_SCORE = -1e30
NSA_UNSELECTED = -(2.0 ** 100)

RWKV_HEAD_DIM = 64
RWKV_GN_EPS = 64e-5
RWKV_CHUNK = 64

POOL_WINDOWS = (2, 4, 8, 16)
POOL_HALO = 16

ROW_TILE = 512
FFN_ROW_TILE = 1024
FFN_COL_CHUNK = 256


def _params(*sem):
    return pltpu.CompilerParams(dimension_semantics=sem, vmem_limit_bytes=V7X_VMEM_LIMIT_BYTES)


def _sigmoid(z):
    return 1.0 / (1.0 + jnp.exp(-z))


def _normmod(x, nw, shift, scale):
    ms = jnp.mean(x * x, axis=-1, keepdims=True)
    return (x * lax.rsqrt(ms + RMS_EPS) * nw) * (1.0 + scale) + shift


def _dot(a, b):
    return jnp.dot(a.astype(BF16), b.astype(BF16), preferred_element_type=F32)


def _dot_nt(a, b, out_dtype=F32):
    out = lax.dot_general(a.astype(BF16), b.astype(BF16), (((1,), (1,)), ((), ())), preferred_element_type=F32)
    return out.astype(out_dtype)


def _dot_tn(a, b):
    return lax.dot_general(a.astype(BF16), b.astype(BF16), (((0,), (0,)), ((), ())), preferred_element_type=F32)


def _split_bf16(x, parts):
    out = []
    for _ in range(parts):
        p = x.astype(BF16)
        out.append(p)
        x = x - p.astype(F32)
    return out


def _mod_kernel(c_ref, w_ref, b_ref, o_ref):
    c = c_ref[...]
    ca = c * _sigmoid(c)
    o_ref[0] = _dot(ca, w_ref[0]) + b_ref[0]


def _ada_mod(c, ada_w, ada_b):
    depth, d, n = ada_w.shape
    b = c.shape[0]
    rows = 8
    tn = 1536
    assert b <= rows and n % tn == 0
    c_pad = jnp.zeros((rows, d), F32).at[:b].set(c)
    out = pl.pallas_call(
        _mod_kernel,
        out_shape=jax.ShapeDtypeStruct((depth, rows, n), F32),
        grid=(depth, n // tn),
        in_specs=[
            pl.BlockSpec((rows, d), lambda i, j: (0, 0)),
            pl.BlockSpec((1, d, tn), lambda i, j: (i, 0, j)),
            pl.BlockSpec((1, 1, tn), lambda i, j: (i, 0, j)),
        ],
        out_specs=pl.BlockSpec((1, rows, tn), lambda i, j: (i, 0, j)),
        compiler_params=_params("parallel", "parallel"),
        name="ada_mod",
    )(c_pad, ada_w, ada_b.reshape(depth, 1, n))
    return out[:, :b]


def _ffn_kernel(x_ref, nw_ref, sh_ref, sc_ref, g_ref, wg_ref, wu_ref, wd_ref, o_ref):
    x = x_ref[...]
    h = _normmod(x, nw_ref[...], sh_ref[0], sc_ref[0]).astype(BF16)
    f = wg_ref.shape[1]
    acts = []
    for c0 in range(0, f, FFN_COL_CHUNK):
        cols = slice(c0, c0 + FFN_COL_CHUNK)
        g = jnp.dot(h, wg_ref[:, cols], preferred_element_type=F32)
        u = jnp.dot(h, wu_ref[:, cols], preferred_element_type=F32)
        acts.append((g * _sigmoid(g) * u).astype(BF16))
    o_ref[...] = x + g_ref[0] * jnp.dot(jnp.concatenate(acts, axis=1), wd_ref[...], preferred_element_type=F32)


def _ffn(x2, nw, shift, scale, gate, wg, wu, wd, seq):
    t, d = x2.shape
    f = wg.shape[1]
    tm = FFN_ROW_TILE
    assert t % tm == 0 and seq % tm == 0 and f % FFN_COL_CHUNK == 0
    tpb = seq // tm
    mod_spec = pl.BlockSpec((1, 1, d), lambda i: (i // tpb, 0, 0))
    resident = lambda shape: pl.BlockSpec(shape, lambda i: (0, 0), pipeline_mode=pl.Buffered(1))
    return pl.pallas_call(
        _ffn_kernel,
        out_shape=jax.ShapeDtypeStruct((t, d), F32),
        grid=(t // tm,),
        in_specs=[
            pl.BlockSpec((tm, d), lambda i: (i, 0)),
            pl.BlockSpec((1, d), lambda i: (0, 0)),
            mod_spec, mod_spec, mod_spec,
            resident((d, f)), resident((d, f)), resident((f, d)),
        ],
        out_specs=pl.BlockSpec((tm, d), lambda i: (i, 0)),
        compiler_params=_params("parallel"),
        name="ffn",
    )(x2, nw.reshape(1, d), shift[:, None], scale[:, None], gate[:, None],
      wg.astype(BF16), wu.astype(BF16), wd.astype(BF16))


def _outproj_kernel(x_ref, z_ref, g_ref, w_ref, o_ref):
    o_ref[...] = x_ref[...] + g_ref[0] * _dot(z_ref[...], w_ref[...])


def _outproj(x2, z2, gate, w, seq):
    t, d = x2.shape
    k = z2.shape[1]
    tm = ROW_TILE
    tpb = seq // tm
    return pl.pallas_call(
        _outproj_kernel,
        out_shape=jax.ShapeDtypeStruct((t, d), F32),
        grid=(t // tm,),
        in_specs=[
            pl.BlockSpec((tm, d), lambda i: (i, 0)),
            pl.BlockSpec((tm, k), lambda i: (i, 0)),
            pl.BlockSpec((1, 1, d), lambda i: (i // tpb, 0, 0)),
            pl.BlockSpec((k, d), lambda i: (0, 0)),
        ],
        out_specs=pl.BlockSpec((tm, d), lambda i: (i, 0)),
        compiler_params=_params("parallel"),
        name="outproj",
    )(x2, z2, gate[:, None], w.astype(BF16))


def _pool_kernel(tm, gd, x_ref, nw_ref, sh_ref, sc_ref, g_ref, pw_ref, pb_ref, ps_ref, o_ref, ext_scr):
    s = pl.program_id(1)
    x = x_ref[0]
    h = _normmod(x, nw_ref[...], sh_ref[0], sc_ref[0])

    @pl.when(s == 0)
    def _():
        ext_scr[0:POOL_HALO, :] = jnp.zeros((POOL_HALO, x.shape[1]), F32)

    ext_scr[POOL_HALO:POOL_HALO + tm, :] = h
    row = s * tm + lax.broadcasted_iota(jnp.int32, (tm, 1), 0)
    ys = []
    for gi, win in enumerate(POOL_WINDOWS):
        lanes = slice(gi * gd, (gi + 1) * gd)
        hg = h[:, lanes]
        acc = hg
        for k in range(1, win):
            acc = acc + ext_scr[POOL_HALO - k:POOL_HALO - k + tm, lanes]
        cnt = jnp.minimum(row + 1, win).astype(F32)
        ys.append(_dot(acc / cnt - hg, pw_ref[gi]))
    y = (jnp.concatenate(ys, axis=-1) + pb_ref[...]) * ps_ref[...]
    o_ref[0] = x + g_ref[0] * y
    ext_scr[0:POOL_HALO, :] = ext_scr[tm:tm + POOL_HALO, :]


def _pool_layer(x, nw, shift, scale, gate, pw, pb, ps):
    b, s, d = x.shape
    tm = ROW_TILE
    gd = d // len(POOL_WINDOWS)
    assert s % tm == 0 and max(POOL_WINDOWS) <= POOL_HALO
    mod_spec = pl.BlockSpec((1, 1, d), lambda i, j: (i, 0, 0))
    vec_spec = pl.BlockSpec((1, d), lambda i, j: (0, 0))
    return pl.pallas_call(
        functools.partial(_pool_kernel, tm, gd),
        out_shape=jax.ShapeDtypeStruct((b, s, d), F32),
        grid=(b, s // tm),
        in_specs=[
            pl.BlockSpec((1, tm, d), lambda i, j: (i, j, 0)),
            vec_spec, mod_spec, mod_spec, mod_spec,
            pl.BlockSpec(pw.shape, lambda i, j: (0, 0, 0)),
            vec_spec, vec_spec,
        ],
        out_specs=pl.BlockSpec((1, tm, d), lambda i, j: (i, j, 0)),
        scratch_shapes=[pltpu.VMEM((tm + POOL_HALO, d), F32)],
        compiler_params=_params("arbitrary", "arbitrary"),
        name="pool_mixer",
    )(x, nw.reshape(1, d), shift[:, None], scale[:, None], gate[:, None],
      pw.astype(BF16), pb.reshape(1, d), ps.reshape(1, d))


def _rwkv_proj_kernel(tm, x_ref, nw_ref, sh_ref, sc_ref, mu_ref, wr_ref, wk_ref, wv_ref,
                      w0_ref, wda_ref, wdb_ref, a0_ref, waa_ref, wab_ref, wga_ref, wgb_ref,
                      r_ref, k_ref, v_ref, lw_ref, a_ref, g_ref, ext_scr):
    s = pl.program_id(1)
    h = _normmod(x_ref[0], nw_ref[...], sh_ref[0], sc_ref[0])

    @pl.when(s == 0)
    def _():
        ext_scr[0:8, :] = jnp.zeros((8, h.shape[1]), F32)

    ext_scr[8:8 + tm, :] = h
    xx = ext_scr[7:7 + tm, :] - h

    def mix(i):
        return h + xx * mu_ref[i:i + 1, :]

    r_ref[0] = _dot(mix(0), wr_ref[...]).astype(r_ref.dtype)
    k_ref[0] = _dot(mix(2), wk_ref[...])
    v_ref[0] = _dot(mix(3), wv_ref[...]).astype(v_ref.dtype)
    dw = w0_ref[...] + _dot(jnp.tanh(_dot(mix(1), wda_ref[...])), wdb_ref[...])
    softplus_neg = jnp.maximum(-dw, 0.0) + jnp.log(1.0 + jnp.exp(-jnp.abs(dw)))
    lw_ref[0] = -jnp.exp(-softplus_neg - 0.5)
    a_ref[0] = _sigmoid(a0_ref[...] + _dot(_dot(mix(4), waa_ref[...]), wab_ref[...]))
    g_ref[0] = _dot(_sigmoid(_dot(mix(5), wga_ref[...])), wgb_ref[...]).astype(g_ref.dtype)
    ext_scr[0:8, :] = ext_scr[tm:tm + 8, :]


def _rwkv_scan_kernel(ct, npair, r_ref, k_ref, v_ref, lw_ref, a_ref, g_ref, kk_ref, ka_ref, rk_ref, lnw_ref, lnb_ref,
                      z_ref, s_scr):
    c, n = RWKV_CHUNK, RWKV_HEAD_DIM
    w = 2 * n
    shift = int(math.log2(n))

    @pl.when(pl.program_id(2) == 0)
    def _():
        s_scr[...] = jnp.zeros_like(s_scr)

    head0 = lax.broadcasted_iota(jnp.int32, (1, w), 1) < n
    ri = lax.broadcasted_iota(jnp.int32, (w, w), 0)
    ci = lax.broadcasted_iota(jnp.int32, (w, w), 1)
    same_head = jnp.where((ri >> shift) == (ci >> shift), 1.0, 0.0).astype(BF16)
    strict, incl, eye = ri > ci, ri >= ci, ri == ci
    rt = lax.broadcasted_iota(jnp.int32, (ct, ct), 0)
    cc = lax.broadcasted_iota(jnp.int32, (ct, ct), 1)
    cshift = int(math.log2(c))
    chunk_tri = jnp.where(((rt >> cshift) == (cc >> cshift)) & (rt >= cc), 1.0, 0.0).astype(BF16)
    nb = ct // c

    def head_sum(xf):
        return sum(jnp.dot(p, same_head, preferred_element_type=F32) for p in _split_bf16(xf, 2))

    def stack(xf):
        x3 = xf.reshape(nb, c, w)
        return jnp.concatenate([jnp.where(head0, x3, 0.0), jnp.where(head0, 0.0, x3)], axis=1)

    def bmm(x, y):
        return jnp.einsum('bij,bjk->bik', x.astype(BF16), y.astype(BF16), preferred_element_type=F32)

    def bmm_nt(x, y):
        return jnp.einsum('bik,bjk->bij', x.astype(BF16), y.astype(BF16), preferred_element_type=F32)

    pairs = []
    stacked = []
    for hp in range(npair):
        ln = slice(hp * w, (hp + 1) * w)
        r, v = r_ref[0, :, ln].astype(F32), v_ref[0, :, ln].astype(F32)
        k, lw, a = k_ref[0, :, ln], lw_ref[0, :, ln], a_ref[0, :, ln]
        kk = k * kk_ref[:, ln]
        kkn = kk * lax.rsqrt(jnp.maximum(head_sum(kk * kk), 1e-24))
        k2 = k * (1.0 + (a - 1.0) * ka_ref[:, ln])
        cum = sum(jnp.dot(chunk_tri, p, preferred_element_type=F32) for p in _split_bf16(lw, 3))
        einv = jnp.exp(-cum)
        pairs.append((r, k2, v))
        stacked.append((stack(-kkn * jnp.exp(cum - lw)), stack(r * jnp.exp(cum)), stack(kkn * a * einv),
                        stack(k2 * einv), stack(v), jnp.exp(cum.reshape(nb, c, w)[:, c - 1:c, :])))
    at, rt_, kb, kq, vs, pc = (jnp.concatenate([st[i] for st in stacked], axis=0) for i in range(6))
    nbt = npair * nb
    kbe, kqe = kb * pc, kq * pc
    tt = bmm_nt(jnp.concatenate([at, rt_], axis=1), jnp.concatenate([kb, kq], axis=1))
    a_m = jnp.where(strict, tt[:, :w, :w], 0.0)
    b_m = jnp.where(strict, tt[:, :w, w:], 0.0)
    ar_m = jnp.where(incl, tt[:, w:, :w], 0.0)
    br_m = jnp.where(incl, tt[:, w:, w:], 0.0)
    rsum = jnp.where(eye, 1.0, 0.0) + a_m
    pw = bmm(a_m, a_m)
    for _ in range(cshift - 2):
        both = bmm(pw, jnp.concatenate([rsum, pw], axis=2))
        rsum = rsum + both[:, :, :w]
        pw = both[:, :, w:]
    tinv = rsum + bmm(pw, rsum)
    w12 = bmm(tinv, jnp.concatenate([at, bmm(b_m, vs)], axis=2))
    xmat = jnp.concatenate([w12, jnp.concatenate([jnp.zeros_like(vs), vs], axis=2)], axis=1)
    kbe_t = jnp.stack([kbe[i].T for i in range(nbt)])
    kqe_t = jnp.stack([kqe[i].T for i in range(nbt)])
    lhs = jnp.concatenate([jnp.concatenate([ar_m, br_m], axis=2), jnp.concatenate([kbe_t, kqe_t], axis=2)], axis=1)
    res = bmm(lhs, xmat)
    g_m = rt_ + res[:, :w, :w]
    y0 = res[:, :w, w:]
    mt = jnp.where(eye, jnp.broadcast_to(pc, (nbt, w, w)), 0.0) + res[:, w:, :w]
    nt = res[:, w:, w:]

    states = [s_scr[hp] for hp in range(npair)]
    ys = [[] for _ in range(npair)]
    for idx in range(nb):
        for hp in range(npair):
            bi = hp * nb + idx
            step = _dot(jnp.concatenate([g_m[bi], mt[bi]], axis=0), states[hp])
            yst = step[:w] + y0[bi]
            states[hp] = step[w:] + nt[bi]
            ys[hp].append(yst[:c] + yst[c:])
    for hp in range(npair):
        ln = slice(hp * w, (hp + 1) * w)
        s_scr[hp] = states[hp]
        y = jnp.concatenate(ys[hp], axis=0)
        r, k2, v = pairs[hp]
        mean = head_sum(y) * (1.0 / n)
        dlt = y - mean
        var = head_sum(dlt * dlt) * (1.0 / n)
        yn = dlt * lax.rsqrt(var + RWKV_GN_EPS) * lnw_ref[:, ln] + lnb_ref[:, ln]
        bonus = head_sum(r * k2 * rk_ref[:, ln]) * v
        z_ref[0, :, ln] = ((yn + bonus) * g_ref[0, :, ln].astype(F32)).astype(z_ref.dtype)


def _rwkv_layer(x, nw, shift, scale, gate, mu, w_rkv, w0, wd_a, wd_b, a0, wa_a, wa_b, wg_a, wg_b,
                k_k, k_a, r_k, ln_w, ln_b, w_out):
    b, s, d = x.shape
    tm = ROW_TILE
    assert s % tm == 0
    mod_spec = pl.BlockSpec((1, 1, d), lambda i, j: (i, 0, 0))
    vec_spec = pl.BlockSpec((1, d), lambda i, j: (0, 0))
    tok_spec = pl.BlockSpec((1, tm, d), lambda i, j: (i, j, 0))

    def full(arr):
        return pl.BlockSpec(arr.shape, lambda i, j: (0,) * arr.ndim)

    weights = [w_rkv[0].astype(BF16), w_rkv[1].astype(BF16), w_rkv[2].astype(BF16),
               w0.reshape(1, d), wd_a.astype(BF16), wd_b.astype(BF16),
               a0.reshape(1, d), wa_a.astype(BF16), wa_b.astype(BF16),
               wg_a.astype(BF16), wg_b.astype(BF16)]
    tok_shape = jax.ShapeDtypeStruct((b, s, d), F32)
    tok_bf16 = jax.ShapeDtypeStruct((b, s, d), BF16)
    r, k, v, lw, a, g = pl.pallas_call(
        functools.partial(_rwkv_proj_kernel, tm),
        out_shape=[tok_bf16, tok_shape, tok_bf16, tok_shape, tok_shape, tok_bf16],
        grid=(b, s // tm),
        in_specs=[tok_spec, vec_spec, mod_spec, mod_spec, full(mu)] + [full(wt) for wt in weights],
        out_specs=[tok_spec] * 6,
        scratch_shapes=[pltpu.VMEM((tm + 8, d), F32)],
        compiler_params=_params("arbitrary", "arbitrary"),
        name="rwkv_proj",
    )(x, nw.reshape(1, d), shift[:, None], scale[:, None], mu, *weights)

    ct = 512
    npair = 4
    pair = 2 * RWKV_HEAD_DIM
    lanes = npair * pair
    assert s % ct == 0 and d % lanes == 0 and pair == V7X_LANES
    seq_spec = pl.BlockSpec((1, ct, lanes), lambda i, p, j: (i, j, p))
    par_spec = pl.BlockSpec((1, lanes), lambda i, p, j: (0, p))
    z = pl.pallas_call(
        functools.partial(_rwkv_scan_kernel, ct, npair),
        out_shape=jax.ShapeDtypeStruct((b, s, d), BF16),
        grid=(b, d // lanes, s // ct),
        in_specs=[seq_spec] * 6 + [par_spec] * 5,
        out_specs=seq_spec,
        scratch_shapes=[pltpu.VMEM((npair, pair, pair), F32)],
        compiler_params=_params("parallel", "parallel", "arbitrary"),
        name="rwkv_scan",
    )(r, k, v, lw, a, g, k_k.reshape(1, d), k_a.reshape(1, d), r_k.reshape(1, d), ln_w.reshape(1, d), ln_b.reshape(1, d))
    return _outproj(x.reshape(b * s, d), z.reshape(b * s, d), gate, w_out, s).reshape(b, s, d)


NSA_Q_TILE = 512
NSA_CMP_SUB_TILE = 128
NSA_ATTN_Q_TILE = 512
NSA_KV_TILE = 512
NSA_PROJ_PAD = 128
NSA_QK_WIDTH = 128


def _head_rms(xf, p_ref, pt_ref, wvec):
    sums = sum(jnp.dot(part, p_ref[...], preferred_element_type=F32) for part in _split_bf16(xf * xf, 2))
    inv = lax.rsqrt(sums * (1.0 / NSA_HEAD_DIM) + RMS_EPS)
    inv_full = sum(jnp.dot(part, pt_ref[...], preferred_element_type=F32) for part in _split_bf16(inv, 3))
    return xf * inv_full * wvec


def _pos_lanes(pos):
    lane = lax.broadcasted_iota(jnp.int32, (1, NSA_QK_WIDTH), 1)
    hi = ((pos >> 6) << 6).astype(F32)
    lo = (pos & 63).astype(F32)
    d = NSA_HEAD_DIM
    ones = jnp.where((lane >= d + 6) & (lane < d + 9), 1.0, 0.0)
    return jnp.where((lane >= d) & (lane < d + 3), hi, jnp.where((lane >= d + 3) & (lane < d + 6), lo, ones))


def _nsa_proj_kernel(tm, qd, kd, x_ref, nw_ref, sh_ref, sc_ref, w_ref, pq_ref, pqt_ref, pk_ref, pkt_ref,
                     nq_ref, nks_ref, nkw_ref, qf_ref,
                     q_ref, ks_ref, vs_ref, kw_ref, vw_ref, kc_ref, vc_ref, gt_ref):
    hg, dk = NSA_HEADS_PER_GROUP, NSA_HEAD_DIM
    h = _normmod(x_ref[0], nw_ref[...], sh_ref[0], sc_ref[0])
    proj = _dot(h, w_ref[...])
    low = lax.broadcasted_iota(jnp.int32, (1, NSA_QK_WIDTH), 1) < dk
    key_lanes = _pos_lanes(pl.program_id(1) * tm + lax.broadcasted_iota(jnp.int32, (tm, 1), 0))

    def heads(x, nheads):
        for pair in range(nheads // 2):
            xp = x[:, pair * 2 * dk:(pair + 1) * 2 * dk]
            yield 2 * pair, xp
            yield 2 * pair + 1, pltpu.roll(xp, dk, 1)

    qn = _head_rms(proj[:, :qd], pq_ref, pqt_ref, nq_ref[...]) * (NSA_SCALE * LOG2_E)
    for hd, xs in heads(qn, NSA_HEADS):
        q_ref[0, hd // hg, hd % hg] = jnp.where(low, xs, qf_ref[hd:hd + 1, :]).astype(BF16)
    o = qd
    for g, xs in heads(proj[:, o:o + kd], NSA_KV_GROUPS):
        kc_ref[0, g] = xs.astype(kc_ref.dtype)
    for g, xs in heads(proj[:, o + kd:o + 2 * kd], NSA_KV_GROUPS):
        vc_ref[0, g] = xs.astype(vc_ref.dtype)
    for g, xs in heads(_head_rms(proj[:, o + 2 * kd:o + 3 * kd], pk_ref, pkt_ref, nks_ref[...]), NSA_KV_GROUPS):
        ks_ref[0, g] = jnp.where(low, xs, key_lanes).astype(BF16)
    for g, xs in heads(proj[:, o + 3 * kd:o + 4 * kd], NSA_KV_GROUPS):
        vs_ref[0, g] = jnp.where(low, xs, 1.0).astype(BF16)
    for g, xs in heads(_head_rms(proj[:, o + 4 * kd:o + 5 * kd], pk_ref, pkt_ref, nkw_ref[...]), NSA_KV_GROUPS):
        kw_ref[0, g] = jnp.where(low, xs, key_lanes).astype(BF16)
    for g, xs in heads(proj[:, o + 5 * kd:o + 6 * kd], NSA_KV_GROUPS):
        vw_ref[0, g] = jnp.where(low, xs, 1.0).astype(BF16)
    o += 6 * kd
    for g in range(NSA_KV_GROUPS):
        gt_ref[0, g] = _sigmoid(proj[:, o + g * NSA_PROJ_PAD:o + (g + 1) * NSA_PROJ_PAD])


def _gelu_tanh(x):
    return 0.5 * x * (1.0 + jnp.tanh(math.sqrt(2.0 / math.pi) * (x + 0.044715 * (x * x * x))))


def _nsa_compress_kernel(half, kc_ref, vc_ref, pos_ref, w1_ref, w2_ref, nk_ref, kco_ref, vco_ref):
    nsub = kc_ref.shape[2]

    def comp(x, i):
        ya = _dot(x + pos_ref[2 * i:2 * i + 1, :], w1_ref[i, 0:half, :])
        yb = _dot(x + pos_ref[2 * i + 1:2 * i + 2, :], w1_ref[i, half:2 * half, :])
        hid = ya + pltpu.roll(yb, nsub - 1, 0)
        return _dot(_gelu_tanh(hid), w2_ref[i])

    kcm = comp(kc_ref[0, 0], 0)
    ms = jnp.sum(kcm * kcm, axis=-1, keepdims=True) * (1.0 / NSA_HEAD_DIM)
    block_end = lax.broadcasted_iota(jnp.int32, (nsub, 1), 0) * CMP_STRIDE + CMP_LEN - 1
    kco_ref[0, 0] = (kcm * lax.rsqrt(ms + RMS_EPS) * nk_ref[...] + _pos_lanes(block_end)).astype(BF16)
    vco_ref[0, 0] = comp(vc_ref[0, 0], 1).astype(BF16)


def _pack_heads(per_head):
    low = lax.broadcasted_iota(jnp.int32, (1, NSA_QK_WIDTH), 1) < NSA_HEAD_DIM
    pairs = [jnp.where(low, per_head[i], pltpu.roll(per_head[i + 1], NSA_HEAD_DIM, 1))
             for i in range(0, len(per_head), 2)]
    return jnp.concatenate(pairs, axis=-1)


def _nsa_cmp_kernel(qt, nc, nsel, q_ref, kc_ref, vc_ref, g_ref, oc_ref, sel_ref):
    hg = NSA_HEADS_PER_GROUP
    q0 = pl.program_id(2) * qt
    qs = NSA_CMP_SUB_TILE
    jo = lax.broadcasted_iota(jnp.int32, (nsel, nc), 0)
    no = lax.broadcasted_iota(jnp.int32, (nsel, nc), 1)
    ratio = SEL_LEN // CMP_STRIDE
    first = jo * ratio - (CMP_LEN // CMP_STRIDE - 1)
    overlap_t = jnp.where((no >= first) & (no < (jo + 1) * ratio) & (no < nc - 1), 1.0, 0.0).astype(BF16)
    n = lax.broadcasted_iota(jnp.int32, (qs, nc), 1)
    imp_parts = []
    for sub in range(qt // qs):
        rows = slice(sub * qs, (sub + 1) * qs)
        q = q_ref[0, 0, :, rows, :].reshape(hg * qs, NSA_QK_WIDTH)
        s3 = _dot_nt(q, kc_ref[0, 0]).reshape(hg, qs, nc)
        t = q0 + sub * qs + lax.broadcasted_iota(jnp.int32, (qs, nc), 0)
        mask = (t >= n * CMP_STRIDE + CMP_LEN - 1)[None]
        sm = jnp.where(mask, s3, MASKED_SCORE)
        e = jnp.exp2(sm - jnp.max(sm, axis=-1, keepdims=True))
        has_key = (q0 + sub * qs + lax.broadcasted_iota(jnp.int32, (1, qs, 1), 1)) >= CMP_LEN - 1
        p = e * jnp.where(has_key, 1.0 / jnp.maximum(jnp.sum(e, axis=-1, keepdims=True), 1e-30), 0.0)
        oc = _dot(p.reshape(hg * qs, nc), vc_ref[0, 0])
        g = g_ref[0, 0, rows, :]
        oc_ref[0, rows, :] = _pack_heads([oc[hd * qs:(hd + 1) * qs] * g[:, hd:hd + 1] for hd in range(hg)])
        psum = p[0] + p[1] + p[2] + p[3]
        imp_parts.append(sum(lax.dot_general(overlap_t, part, (((1,), (1,)), ((), ())), preferred_element_type=F32)
                             for part in _split_bf16(psum, 3)))
    imp_t = jnp.concatenate(imp_parts, axis=1)

    j = lax.broadcasted_iota(jnp.int32, (nsel, qt), 0)
    tt = q0 + lax.broadcasted_iota(jnp.int32, (nsel, qt), 1)
    valid = j * SEL_LEN <= tt
    qb = tt >> int(math.log2(SEL_LEN))
    forced = (j == 0) | (j == qb) | (j == qb - 1)
    score = jnp.where(valid, jnp.where(forced, SEL_BIG, imp_t), -SEL_BIG)
    jf = j.astype(F32)
    picked = -(2.0 ** 120)
    for _ in range(min(SEL_TOPK, nsel)):
        mx = jnp.max(score, axis=0, keepdims=True)
        jmin = jnp.min(jnp.where(score == mx, jf, float(nsel)), axis=0, keepdims=True)
        score = jnp.where(jf == jmin, picked, score)
    sel = jnp.where((score == picked) & valid, 1.0, 0.0)
    sel_ref[0, 0] = sel.T.astype(BF16)


def _nsa_attn_kernel(qt, kt, nsel, nkt, nq, tiles_ref, cnt_ref,
                     q_ref, ks_ref, vs_ref, kw_ref, vw_ref, oh_ref, sel_ref, g_ref, sl_ref, oc_ref, o_ref):
    hg, dk = NSA_HEADS_PER_GROUP, NSA_HEAD_DIM
    rows = hg * qt
    step = (pl.program_id(0) * NSA_KV_GROUPS + pl.program_id(1)) * nq + pl.program_id(2)
    q0 = pl.program_id(2) * qt
    lane = lax.broadcasted_iota(jnp.int32, (1, NSA_QK_WIDTH), 1)
    t_row = (q0 + lax.broadcasted_iota(jnp.int32, (qt, 1), 0)).astype(F32)
    q_heads = []
    for hd in range(hg):
        qh = q_ref[0, 0, hd]
        for i, part in enumerate(_split_bf16(-sl_ref[0, hd:hd + 1, :] * t_row, 3)):
            qh = jnp.where(lane == dk + 6 + i, part, qh)
        q_heads.append(qh)
    q = jnp.concatenate(q_heads, axis=0)
    unsel = (1.0 - sel_ref[0, 0]) * NSA_UNSELECTED
    if nsel < NSA_QK_WIDTH:
        unsel = jnp.concatenate([unsel, jnp.zeros((qt, NSA_QK_WIDTH - nsel), BF16)], axis=1)
    q_sel = jnp.concatenate([q, jnp.concatenate([unsel] * hg, axis=0)], axis=1)

    t_k = q0 + lax.broadcasted_iota(jnp.int32, (qt, kt), 0)
    c_k = lax.broadcasted_iota(jnp.int32, (qt, kt), 1)

    def sel_step(i, carry, diagonal):
        m, acc = carry
        k0 = pl.multiple_of(tiles_ref[step * nkt + i] * kt, kt)
        keys = jnp.concatenate([ks_ref[0, 0, pl.ds(k0, kt), :], oh_ref[pl.ds(k0, kt), :]], axis=1)
        sm = _dot_nt(q_sel, keys, BF16)
        if diagonal:
            causal = (k0 + c_k <= t_k)[None]
            sm = jnp.where(causal, sm.reshape(hg, qt, kt), MASKED_SCORE).reshape(rows, kt)
        m_new = jnp.maximum(m, jnp.max(sm, axis=-1, keepdims=True))
        alpha = jnp.exp2((m - m_new).astype(F32))
        acc = alpha * acc + _dot(jnp.exp2(sm - m_new), vs_ref[0, 0, pl.ds(k0, kt), :])
        return m_new, acc

    init = (jnp.full((rows, 1), MASKED_SCORE, BF16), jnp.zeros((rows, NSA_QK_WIDTH), F32))
    last = cnt_ref[step] - 1
    carry = lax.fori_loop(0, last, functools.partial(sel_step, diagonal=False), init)
    _, acc_s = sel_step(last, carry, diagonal=True)
    o_s = acc_s / jnp.maximum(acc_s[:, dk:dk + 1], 1e-30)

    wk = WINDOW + qt
    ws = pl.multiple_of(jnp.maximum(q0 - WINDOW, 0), qt)
    s3 = _dot_nt(q, kw_ref[0, 0, pl.ds(ws, wk), :], BF16).reshape(hg, qt, wk)
    dist = (q0 + lax.broadcasted_iota(jnp.int32, (qt, wk), 0)) - (ws + lax.broadcasted_iota(jnp.int32, (qt, wk), 1))
    mask = ((dist >= 0) & (dist < WINDOW))[None]
    sm = jnp.where(mask, s3, MASKED_SCORE).reshape(rows, wk)
    e = jnp.exp2(sm - jnp.max(sm, axis=-1, keepdims=True))
    acc_w = _dot(e, vw_ref[0, 0, pl.ds(ws, wk), :])
    o_w = acc_w / jnp.maximum(acc_w[:, dk:dk + 1], 1e-30)

    g = g_ref[0, 0]
    per_head = [g[:, hg + hd:hg + hd + 1] * o_s[hd * qt:(hd + 1) * qt]
                + g[:, 2 * hg + hd:2 * hg + hd + 1] * o_w[hd * qt:(hd + 1) * qt] for hd in range(hg)]
    o_ref[0] = (oc_ref[0] + _pack_heads(per_head)).astype(o_ref.dtype)


def _nsa_layer(x, nw, shift, scale, gate, w_in, cmp_pos, cmp_w1, cmp_w2, qk_norm, w_out):
    b, s, d = x.shape
    t = b * s
    g_, hg, dk = NSA_KV_GROUPS, NSA_HEADS_PER_GROUP, NSA_HEAD_DIM
    qd, kd = NSA_HEADS * dk, g_ * dk
    tm = ROW_TILE
    wl = NSA_QK_WIDTH
    n_main = qd + 6 * kd
    assert w_in.shape[1] == n_main + 3 * NSA_HEADS and s % tm == 0
    gate_cols = w_in[:, n_main:].reshape(d, 3, g_, hg).transpose(0, 2, 1, 3).reshape(d, g_, 3 * hg)
    gate_cols = jnp.pad(gate_cols, ((0, 0), (0, 0), (0, NSA_PROJ_PAD - 3 * hg))).reshape(d, g_ * NSA_PROJ_PAD)
    w_pad = jnp.concatenate([w_in[:, :n_main], gate_cols], axis=1).astype(BF16)

    def head_onehot(width):
        lane_head = jnp.arange(width)[:, None] // dk
        return (lane_head == jnp.arange(V7X_LANES)[None, :]).astype(BF16)

    pq, pk = head_onehot(qd), head_onehot(kd)

    def tile_w(wv, reps):
        return jnp.tile(wv, reps).reshape(1, reps * dk)

    slopes = jnp.exp2(-8.0 * (jnp.arange(NSA_HEADS, dtype=F32) + 1) / NSA_HEADS)
    slope_parts = jnp.stack([p.astype(F32) for p in _split_bf16(slopes * LOG2_E, 3)], axis=-1)
    slope_rows = jnp.broadcast_to((slopes * LOG2_E).reshape(g_, hg, 1), (g_, hg, wl))
    q_lanes = jnp.zeros((NSA_HEADS, wl), F32).at[:, dk:dk + 6].set(jnp.concatenate([slope_parts, slope_parts], axis=-1))

    consts = [pq, pq.T, pk, pk.T, tile_w(qk_norm[0], NSA_HEADS), tile_w(qk_norm[2], g_), tile_w(qk_norm[3], g_), q_lanes]
    full2 = lambda arr: pl.BlockSpec(arr.shape, lambda i, j: (0, 0))
    mod_spec = pl.BlockSpec((1, 1, d), lambda i, j: (i, 0, 0))
    grp = lambda dt: jax.ShapeDtypeStruct((b, g_, s, wl), dt)
    grp_spec = pl.BlockSpec((1, g_, tm, wl), lambda i, j: (i, 0, j, 0))
    q5, ks4, vs4, kw4, vw4, kc, vc, gates = pl.pallas_call(
        functools.partial(_nsa_proj_kernel, tm, qd, kd),
        out_shape=[jax.ShapeDtypeStruct((b, g_, hg, s, wl), BF16), grp(BF16), grp(BF16), grp(BF16), grp(BF16),
                   grp(BF16), grp(BF16), grp(F32)],
        grid=(b, s // tm),
        in_specs=[pl.BlockSpec((1, tm, d), lambda i, j: (i, j, 0)), pl.BlockSpec((1, d), lambda i, j: (0, 0)),
                  mod_spec, mod_spec, full2(w_pad)] + [full2(cst) for cst in consts],
        out_specs=[pl.BlockSpec((1, g_, hg, tm, wl), lambda i, j: (i, 0, 0, j, 0))] + [grp_spec] * 7,
        compiler_params=_params("parallel", "parallel"),
        name="nsa_proj",
    )(x, nw.reshape(1, d), shift[:, None], scale[:, None], w_pad, *consts)

    nsub = s // CMP_STRIDE
    half = CMP_STRIDE * wl
    kc3, vc3 = kc.reshape(b, g_, nsub, half), vc.reshape(b, g_, nsub, half)
    pad_tok = lambda arr: jnp.pad(arr, [(0, 0)] * (arr.ndim - 1) + [(0, wl - dk)])
    pos = pad_tok(cmp_pos).reshape(4, half)
    w1b = jnp.pad(cmp_w1.reshape(2, CMP_LEN, dk, -1), ((0, 0), (0, 0), (0, wl - dk), (0, 0))).reshape(2, 2 * half, -1).astype(BF16)
    w2b = jnp.pad(cmp_w2, ((0, 0), (0, 0), (0, wl - dk))).astype(BF16)
    nk_lanes = jnp.pad(qk_norm[1], (0, wl - dk)).reshape(1, wl)
    cmp_in = pl.BlockSpec((1, 1, nsub, half), lambda i, j: (i, j, 0, 0))
    cmp_out = pl.BlockSpec((1, 1, nsub, wl), lambda i, j: (i, j, 0, 0))
    kcc, vcc = pl.pallas_call(
        functools.partial(_nsa_compress_kernel, half),
        out_shape=[jax.ShapeDtypeStruct((b, g_, nsub, wl), BF16)] * 2,
        grid=(b, g_),
        in_specs=[cmp_in, cmp_in, pl.BlockSpec(pos.shape, lambda i, j: (0, 0)),
                  pl.BlockSpec(w1b.shape, lambda i, j: (0, 0, 0)), pl.BlockSpec(w2b.shape, lambda i, j: (0, 0, 0)),
                  pl.BlockSpec((1, wl), lambda i, j: (0, 0))],
        out_specs=[cmp_out, cmp_out],
        compiler_params=_params("parallel", "parallel"),
        name="nsa_compress",
    )(kc3, vc3, pos, w1b, w2b, nk_lanes)

    nsel = s // SEL_LEN
    kt = min(NSA_KV_TILE, s)
    nkt = s // kt
    seq_kv = lambda n: pl.BlockSpec((1, 1, n, wl), lambda i, j, k, *_: (i, j, 0, 0))

    def tile_specs(qt):
        assert s % qt == 0 and s >= WINDOW + qt and kt % qt == 0
        return (pl.BlockSpec((1, 1, hg, qt, wl), lambda i, j, k, *_: (i, j, 0, k, 0)),
                pl.BlockSpec((1, qt, hg * dk), lambda i, j, k, *_: (i, k, j)),
                pl.BlockSpec((1, 1, qt, wl), lambda i, j, k, *_: (i, j, k, 0)),
                pl.BlockSpec((1, 1, qt, nsel), lambda i, j, k, *_: (i, j, k, 0)))

    qt = NSA_Q_TILE
    q_spec, o_spec, gate_spec, sel_spec = tile_specs(qt)
    oc, sel = pl.pallas_call(
        functools.partial(_nsa_cmp_kernel, qt, nsub, nsel),
        out_shape=[jax.ShapeDtypeStruct((b, s, qd), F32), jax.ShapeDtypeStruct((b, g_, s, nsel), BF16)],
        grid=(b, g_, s // qt),
        in_specs=[q_spec, seq_kv(nsub), seq_kv(nsub), gate_spec],
        out_specs=[o_spec, sel_spec],
        compiler_params=_params("parallel", "parallel", "parallel"),
        name="nsa_cmp_select",
    )(q5, kcc, vcc, gates)

    assert nsel <= wl
    block_onehot = (jnp.arange(s)[:, None] // SEL_LEN == jnp.arange(wl)[None, :]).astype(BF16)

    qt = NSA_ATTN_Q_TILE
    nq = s // qt
    q_spec, o_spec, gate_spec, sel_spec = tile_specs(qt)
    active = sel.reshape(b, g_, nq, qt, nkt, kt // SEL_LEN).max(axis=(3, 5)) > 0
    order = jnp.sort(jnp.where(active, 0, nkt) + jnp.arange(nkt, dtype=jnp.int32), axis=-1)
    tiles = (order % nkt).astype(jnp.int32).reshape(-1)
    counts = active.sum(axis=-1).astype(jnp.int32).reshape(-1)

    o3 = pl.pallas_call(
        functools.partial(_nsa_attn_kernel, qt, kt, nsel, nkt, nq),
        out_shape=jax.ShapeDtypeStruct((b, s, qd), BF16),
        grid_spec=pltpu.PrefetchScalarGridSpec(
            num_scalar_prefetch=2,
            grid=(b, g_, nq),
            in_specs=[q_spec, seq_kv(s), seq_kv(s), seq_kv(s), seq_kv(s),
                      pl.BlockSpec((s, wl), lambda i, j, k, *_: (0, 0)), sel_spec, gate_spec,
                      pl.BlockSpec((1, hg, wl), lambda i, j, k, *_: (j, 0, 0)), o_spec],
            out_specs=o_spec,
        ),
        compiler_params=_params("parallel", "parallel", "arbitrary"),
        name="nsa_attention",
    )(tiles, counts, q5, ks4, vs4, kw4, vw4, block_onehot, sel, gates, slope_rows, oc)
    return _outproj(x.reshape(t, d), o3.reshape(t, qd), gate, w_out, s).reshape(b, s, d)


def kernel(x, c, ada_w, ada_b, norm_mix_w, norm_ffn_w, ffn_w_gate, ffn_w_up, ffn_w_down, nsa_w_in, nsa_cmp_pos, nsa_cmp_w1, nsa_cmp_w2, nsa_qk_norm, nsa_w_out, rwkv_mu, rwkv_w_rkv, rwkv_w0, rwkv_wd_a, rwkv_wd_b, rwkv_a0, rwkv_wa_a, rwkv_wa_b, rwkv_wg_a, rwkv_wg_b, rwkv_k_k, rwkv_k_a, rwkv_r_k, rwkv_ln_w, rwkv_ln_b, rwkv_w_out, pool_w, pool_b, pool_scale):
    b, s, d = x.shape
    depth = ada_w.shape[0]
    mod = _ada_mod(c, ada_w, ada_b)
    for i in range(depth):
        sh_m, sc_m, g_m, sh_f, sc_f, g_f = jnp.split(mod[i], N_MOD, axis=-1)
        kind, j = i % N_MIXERS, i // N_MIXERS
        if kind == 0:
            x = _nsa_layer(x, norm_mix_w[i], sh_m, sc_m, g_m, nsa_w_in[j], nsa_cmp_pos[j], nsa_cmp_w1[j], nsa_cmp_w2[j],
                           nsa_qk_norm[j], nsa_w_out[j])
        elif kind == 1:
            x = _rwkv_layer(x, norm_mix_w[i], sh_m, sc_m, g_m, rwkv_mu[j], rwkv_w_rkv[j], rwkv_w0[j], rwkv_wd_a[j],
                            rwkv_wd_b[j], rwkv_a0[j], rwkv_wa_a[j], rwkv_wa_b[j], rwkv_wg_a[j], rwkv_wg_b[j],
                            rwkv_k_k[j], rwkv_k_a[j], rwkv_r_k[j], rwkv_ln_w[j], rwkv_ln_b[j], rwkv_w_out[j])
        elif kind == 2:
            x = _pool_layer(x, norm_mix_w[i], sh_m, sc_m, g_m, pool_w[j], pool_b[j], pool_scale[j])
        x = _ffn(x.reshape(b * s, d), norm_ffn_w[i], sh_f, sc_f, g_f, ffn_w_gate[i], ffn_w_up[i], ffn_w_down[i], s).reshape(b, s, d)
    return x
```

```python
import functools
import math

import jax
import jax.numpy as jnp
from jax import lax
from jax.experimental import pallas as pl
from jax.experimental.pallas import tpu as pltpu

F32 = jnp.float32
BF16 = jnp.bfloat16

V7X_LANES = 128
V7X_VMEM_LIMIT_BYTES = 56 * 1024 * 1024

RMS_EPS = 1e-6
N_MOD = 6
N_MIXERS = 3

NSA_HEAD_DIM = 64
NSA_KV_GROUPS = 4
NSA_HEADS_PER_GROUP = 4
NSA_HEADS = NSA_KV_GROUPS * NSA_HEADS_PER_GROUP
CMP_LEN = 32
CMP_STRIDE = 16
SEL_LEN = 64
SEL_TOPK = 16
WINDOW = 512
NSA_SCALE = NSA_HEAD_DIM ** -0.5
LOG2_E = math.log2(math.e)
SEL_BIG = 1e9
MASKED_SCORE = -1e30
NSA_UNSELECTED = -(2.0 ** 100)

RWKV_HEAD_DIM = 64
RWKV_GN_EPS = 64e-5
RWKV_CHUNK = 64

POOL_WINDOWS = (2, 4, 8, 16)
POOL_HALO = 16

ROW_TILE = 512
FFN_ROW_TILE = 1024
FFN_COL_CHUNK = 256


def _params(*sem):
    return pltpu.CompilerParams(dimension_semantics=sem, vmem_limit_bytes=V7X_VMEM_LIMIT_BYTES)


def _sigmoid(z):
    return 1.0 / (1.0 + jnp.exp(-z))


def _normmod(x, nw, shift, scale):
    ms = jnp.mean(x * x, axis=-1, keepdims=True)
    return (x * lax.rsqrt(ms + RMS_EPS) * nw) * (1.0 + scale) + shift


def _dot(a, b):
    return jnp.dot(a.astype(BF16), b.astype(BF16), preferred_element_type=F32)


def _dot_nt(a, b, out_dtype=F32):
    out = lax.dot_general(a.astype(BF16), b.astype(BF16), (((1,), (1,)), ((), ())), preferred_element_type=F32)
    return out.astype(out_dtype)


def _split_bf16(x, parts):
    out = []
    for _ in range(parts):
        p = x.astype(BF16)
        out.append(p)
        x = x - p.astype(F32)
    return out


def _mod_kernel(c_ref, w_ref, b_ref, o_ref):
    c = c_ref[...]
    ca = c * _sigmoid(c)
    o_ref[0] = _dot(ca, w_ref[0]) + b_ref[0]


def _ada_mod(c, ada_w, ada_b):
    depth, d, n = ada_w.shape
    b = c.shape[0]
    rows = 8
    tn = 1536
    assert b <= rows and n % tn == 0
    c_pad = jnp.zeros((rows, d), F32).at[:b].set(c)
    out = pl.pallas_call(
        _mod_kernel,
        out_shape=jax.ShapeDtypeStruct((depth, rows, n), F32),
        grid=(depth, n // tn),
        in_specs=[
            pl.BlockSpec((rows, d), lambda i, j: (0, 0)),
            pl.BlockSpec((1, d, tn), lambda i, j: (i, 0, j)),
            pl.BlockSpec((1, 1, tn), lambda i, j: (i, 0, j)),
        ],
        out_specs=pl.BlockSpec((1, rows, tn), lambda i, j: (i, 0, j)),
        compiler_params=_params("parallel", "parallel"),
        name="ada_mod",
    )(c_pad, ada_w, ada_b.reshape(depth, 1, n))
    return out[:, :b]


def _ffn_kernel(x_ref, nw_ref, sh_ref, sc_ref, g_ref, wg_ref, wu_ref, wd_ref, o_ref):
    x = x_ref[...]
    h = _normmod(x, nw_ref[...], sh_ref[0], sc_ref[0]).astype(BF16)
    f = wg_ref.shape[1]
    acts = []
    for c0 in range(0, f, FFN_COL_CHUNK):
        cols = slice(c0, c0 + FFN_COL_CHUNK)
        g = jnp.dot(h, wg_ref[:, cols], preferred_element_type=F32)
        u = jnp.dot(h, wu_ref[:, cols], preferred_element_type=F32)
        acts.append((g * _sigmoid(g) * u).astype(BF16))
    o_ref[...] = x + g_ref[0] * jnp.dot(jnp.concatenate(acts, axis=1), wd_ref[...], preferred_element_type=F32)


def _ffn(x2, nw, shift, scale, gate, wg, wu, wd, seq):
    t, d = x2.shape
    f = wg.shape[1]
    tm = FFN_ROW_TILE
    assert t % tm == 0 and seq % tm == 0 and f % FFN_COL_CHUNK == 0
    tpb = seq // tm
    mod_spec = pl.BlockSpec((1, 1, d), lambda i: (i // tpb, 0, 0))
    resident = lambda shape: pl.BlockSpec(shape, lambda i: (0, 0), pipeline_mode=pl.Buffered(1))
    return pl.pallas_call(
        _ffn_kernel,
        out_shape=jax.ShapeDtypeStruct((t, d), F32),
        grid=(t // tm,),
        in_specs=[
            pl.BlockSpec((tm, d), lambda i: (i, 0)),
            pl.BlockSpec((1, d), lambda i: (0, 0)),
            mod_spec, mod_spec, mod_spec,
            resident((d, f)), resident((d, f)), resident((f, d)),
        ],
        out_specs=pl.BlockSpec((tm, d), lambda i: (i, 0)),
        compiler_params=_params("parallel"),
        name="ffn",
    )(x2, nw.reshape(1, d), shift[:, None], scale[:, None], gate[:, None],
      wg.astype(BF16), wu.astype(BF16), wd.astype(BF16))


def _outproj_kernel(x_ref, z_ref, g_ref, w_ref, o_ref):
    o_ref[...] = x_ref[...] + g_ref[0] * _dot(z_ref[...], w_ref[...])


def _outproj(x2, z2, gate, w, seq):
    t, d = x2.shape
    k = z2.shape[1]
    tm = ROW_TILE
    tpb = seq // tm
    return pl.pallas_call(
        _outproj_kernel,
        out_shape=jax.ShapeDtypeStruct((t, d), F32),
        grid=(t // tm,),
        in_specs=[
            pl.BlockSpec((tm, d), lambda i: (i, 0)),
            pl.BlockSpec((tm, k), lambda i: (i, 0)),
            pl.BlockSpec((1, 1, d), lambda i: (i // tpb, 0, 0)),
            pl.BlockSpec((k, d), lambda i: (0, 0)),
        ],
        out_specs=pl.BlockSpec((tm, d), lambda i: (i, 0)),
        compiler_params=_params("parallel"),
        name="outproj",
    )(x2, z2, gate[:, None], w.astype(BF16))


def _pool_kernel(tm, gd, x_ref, nw_ref, sh_ref, sc_ref, g_ref, pw_ref, pb_ref, ps_ref, o_ref, ext_scr):
    s = pl.program_id(1)
    x = x_ref[0]
    h = _normmod(x, nw_ref[...], sh_ref[0], sc_ref[0])

    @pl.when(s == 0)
    def _():
        ext_scr[0:POOL_HALO, :] = jnp.zeros((POOL_HALO, x.shape[1]), F32)

    ext_scr[POOL_HALO:POOL_HALO + tm, :] = h
    row = s * tm + lax.broadcasted_iota(jnp.int32, (tm, 1), 0)
    ys = []
    for gi, win in enumerate(POOL_WINDOWS):
        lanes = slice(gi * gd, (gi + 1) * gd)
        hg = h[:, lanes]
        acc = hg
        for k in range(1, win):
            acc = acc + ext_scr[POOL_HALO - k:POOL_HALO - k + tm, lanes]
        cnt = jnp.minimum(row + 1, win).astype(F32)
        ys.append(_dot(acc / cnt - hg, pw_ref[gi]))
    y = (jnp.concatenate(ys, axis=-1) + pb_ref[...]) * ps_ref[...]
    o_ref[0] = x + g_ref[0] * y
    ext_scr[0:POOL_HALO, :] = ext_scr[tm:tm + POOL_HALO, :]


def _pool_layer(x, nw, shift, scale, gate, pw, pb, ps):
    b, s, d = x.shape
    tm = ROW_TILE
    gd = d // len(POOL_WINDOWS)
    assert s % tm == 0 and max(POOL_WINDOWS) <= POOL_HALO
    mod_spec = pl.BlockSpec((1, 1, d), lambda i, j: (i, 0, 0))
    vec_spec = pl.BlockSpec((1, d), lambda i, j: (0, 0))
    return pl.pallas_call(
        functools.partial(_pool_kernel, tm, gd),
        out_shape=jax.ShapeDtypeStruct((b, s, d), F32),
        grid=(b, s // tm),
        in_specs=[
            pl.BlockSpec((1, tm, d), lambda i, j: (i, j, 0)),
            vec_spec, mod_spec, mod_spec, mod_spec,
            pl.BlockSpec(pw.shape, lambda i, j: (0, 0, 0)),
            vec_spec, vec_spec,
        ],
        out_specs=pl.BlockSpec((1, tm, d), lambda i, j: (i, j, 0)),
        scratch_shapes=[pltpu.VMEM((tm + POOL_HALO, d), F32)],
        compiler_params=_params("arbitrary", "arbitrary"),
        name="pool_mixer",
    )(x, nw.reshape(1, d), shift[:, None], scale[:, None], gate[:, None],
      pw.astype(BF16), pb.reshape(1, d), ps.reshape(1, d))


def _rwkv_proj_kernel(tm, x_ref, nw_ref, sh_ref, sc_ref, mu_ref, wr_ref, wk_ref, wv_ref,
                      w0_ref, wda_ref, wdb_ref, a0_ref, waa_ref, wab_ref, wga_ref, wgb_ref,
                      r_ref, k_ref, v_ref, lw_ref, a_ref, g_ref, ext_scr):
    s = pl.program_id(1)
    h = _normmod(x_ref[0], nw_ref[...], sh_ref[0], sc_ref[0])

    @pl.when(s == 0)
    def _():
        ext_scr[0:8, :] = jnp.zeros((8, h.shape[1]), F32)

    ext_scr[8:8 + tm, :] = h
    xx = ext_scr[7:7 + tm, :] - h

    def mix(i):
        return h + xx * mu_ref[i:i + 1, :]

    r_ref[0] = _dot(mix(0), wr_ref[...]).astype(r_ref.dtype)
    k_ref[0] = _dot(mix(2), wk_ref[...])
    v_ref[0] = _dot(mix(3), wv_ref[...]).astype(v_ref.dtype)
    dw = w0_ref[...] + _dot(jnp.tanh(_dot(mix(1), wda_ref[...])), wdb_ref[...])
    softplus_neg = jnp.maximum(-dw, 0.0) + jnp.log(1.0 + jnp.exp(-jnp.abs(dw)))
    lw_ref[0] = -jnp.exp(-softplus_neg - 0.5)
    a_ref[0] = _sigmoid(a0_ref[...] + _dot(_dot(mix(4), waa_ref[...]), wab_ref[...]))
    g_ref[0] = _dot(_sigmoid(_dot(mix(5), wga_ref[...])), wgb_ref[...]).astype(g_ref.dtype)
    ext_scr[0:8, :] = ext_scr[tm:tm + 8, :]


def _rwkv_scan_kernel(ct, npair, r_ref, k_ref, v_ref, lw_ref, a_ref, g_ref, kk_ref, ka_ref, rk_ref, lnw_ref, lnb_ref,
                      z_ref, s_scr):
    c, n = RWKV_CHUNK, RWKV_HEAD_DIM
    w = 2 * n
    shift = int(math.log2(n))

    @pl.when(pl.program_id(2) == 0)
    def _():
        s_scr[...] = jnp.zeros_like(s_scr)

    head0 = lax.broadcasted_iota(jnp.int32, (1, w), 1) < n
    ri = lax.broadcasted_iota(jnp.int32, (w, w), 0)
    ci = lax.broadcasted_iota(jnp.int32, (w, w), 1)
    same_head = jnp.where((ri >> shift) == (ci >> shift), 1.0, 0.0).astype(BF16)
    strict, incl, eye = ri > ci, ri >= ci, ri == ci
    rt = lax.broadcasted_iota(jnp.int32, (ct, ct), 0)
    cc = lax.broadcasted_iota(jnp.int32, (ct, ct), 1)
    cshift = int(math.log2(c))
    chunk_tri = jnp.where(((rt >> cshift) == (cc >> cshift)) & (rt >= cc), 1.0, 0.0).astype(BF16)
    nb = ct // c

    def head_sum(xf):
        return sum(jnp.dot(p, same_head, preferred_element_type=F32) for p in _split_bf16(xf, 2))

    def stack(xf):
        x3 = xf.reshape(nb, c, w)
        return jnp.concatenate([jnp.where(head0, x3, 0.0), jnp.where(head0, 0.0, x3)], axis=1)

    def bmm(x, y):
        return jnp.einsum('bij,bjk->bik', x.astype(BF16), y.astype(BF16), preferred_element_type=F32)

    def bmm_nt(x, y):
        return jnp.einsum('bik,bjk->bij', x.astype(BF16), y.astype(BF16), preferred_element_type=F32)

    pairs = []
    stacked = []
    for hp in range(npair):
        ln = slice(hp * w, (hp + 1) * w)
        r, v = r_ref[0, :, ln].astype(F32), v_ref[0, :, ln].astype(F32)
        k, lw, a = k_ref[0, :, ln], lw_ref[0, :, ln], a_ref[0, :, ln]
        kk = k * kk_ref[:, ln]
        kkn = kk * lax.rsqrt(jnp.maximum(head_sum(kk * kk), 1e-24))
        k2 = k * (1.0 + (a - 1.0) * ka_ref[:, ln])
        cum = sum(jnp.dot(chunk_tri, p, preferred_element_type=F32) for p in _split_bf16(lw, 3))
        einv = jnp.exp(-cum)
        pairs.append((r, k2, v))
        stacked.append((stack(-kkn * jnp.exp(cum - lw)), stack(r * jnp.exp(cum)), stack(kkn * a * einv),
                        stack(k2 * einv), stack(v), jnp.exp(cum.reshape(nb, c, w)[:, c - 1:c, :])))
    at, rt_, kb, kq, vs, pc = (jnp.concatenate([st[i] for st in stacked], axis=0) for i in range(6))
    nbt = npair * nb
    kbe, kqe = kb * pc, kq * pc
    tt = bmm_nt(jnp.concatenate([at, rt_], axis=1), jnp.concatenate([kb, kq], axis=1))
    a_m = jnp.where(strict, tt[:, :w, :w], 0.0)
    b_m = jnp.where(strict, tt[:, :w, w:], 0.0)
    ar_m = jnp.where(incl, tt[:, w:, :w], 0.0)
    br_m = jnp.where(incl, tt[:, w:, w:], 0.0)
    rsum = jnp.where(eye, 1.0, 0.0) + a_m
    pw = bmm(a_m, a_m)
    for _ in range(cshift - 2):
        both = bmm(pw, jnp.concatenate([rsum, pw], axis=2))
        rsum = rsum + both[:, :, :w]
        pw = both[:, :, w:]
    tinv = rsum + bmm(pw, rsum)
    w12 = bmm(tinv, jnp.concatenate([at, bmm(b_m, vs)], axis=2))
    xmat = jnp.concatenate([w12, jnp.concatenate([jnp.zeros_like(vs), vs], axis=2)], axis=1)
    kbe_t = jnp.stack([kbe[i].T for i in range(nbt)])
    kqe_t = jnp.stack([kqe[i].T for i in range(nbt)])
    lhs = jnp.concatenate([jnp.concatenate([ar_m, br_m], axis=2), jnp.concatenate([kbe_t, kqe_t], axis=2)], axis=1)
    res = bmm(lhs, xmat)
    g_m = rt_ + res[:, :w, :w]
    y0 = res[:, :w, w:]
    mt = jnp.where(eye, jnp.broadcast_to(pc, (nbt, w, w)), 0.0) + res[:, w:, :w]
    nt = res[:, w:, w:]

    states = [s_scr[hp] for hp in range(npair)]
    ys = [[] for _ in range(npair)]
    for idx in range(nb):
        for hp in range(npair):
            bi = hp * nb + idx
            step = _dot(jnp.concatenate([g_m[bi], mt[bi]], axis=0), states[hp])
            yst = step[:w] + y0[bi]
            states[hp] = step[w:] + nt[bi]
            ys[hp].append(yst[:c] + yst[c:])
    for hp in range(npair):
        ln = slice(hp * w, (hp + 1) * w)
        s_scr[hp] = states[hp]
        y = jnp.concatenate(ys[hp], axis=0)
        r, k2, v = pairs[hp]
        mean = head_sum(y) * (1.0 / n)
        dlt = y - mean
        var = head_sum(dlt * dlt) * (1.0 / n)
        yn = dlt * lax.rsqrt(var + RWKV_GN_EPS) * lnw_ref[:, ln] + lnb_ref[:, ln]
        bonus = head_sum(r * k2 * rk_ref[:, ln]) * v
        z_ref[0, :, ln] = ((yn + bonus) * g_ref[0, :, ln].astype(F32)).astype(z_ref.dtype)


def _rwkv_layer(x, nw, shift, scale, gate, mu, w_rkv, w0, wd_a, wd_b, a0, wa_a, wa_b, wg_a, wg_b,
                k_k, k_a, r_k, ln_w, ln_b, w_out):
    b, s, d = x.shape
    tm = ROW_TILE
    assert s % tm == 0
    mod_spec = pl.BlockSpec((1, 1, d), lambda i, j: (i, 0, 0))
    vec_spec = pl.BlockSpec((1, d), lambda i, j: (0, 0))
    tok_spec = pl.BlockSpec((1, tm, d), lambda i, j: (i, j, 0))

    def full(arr):
        return pl.BlockSpec(arr.shape, lambda i, j: (0,) * arr.ndim)

    weights = [w_rkv[0].astype(BF16), w_rkv[1].astype(BF16), w_rkv[2].astype(BF16),
               w0.reshape(1, d), wd_a.astype(BF16), wd_b.astype(BF16),
               a0.reshape(1, d), wa_a.astype(BF16), wa_b.astype(BF16),
               wg_a.astype(BF16), wg_b.astype(BF16)]
    tok_shape = jax.ShapeDtypeStruct((b, s, d), F32)
    tok_bf16 = jax.ShapeDtypeStruct((b, s, d), BF16)
    r, k, v, lw, a, g = pl.pallas_call(
        functools.partial(_rwkv_proj_kernel, tm),
        out_shape=[tok_bf16, tok_shape, tok_bf16, tok_shape, tok_shape, tok_bf16],
        grid=(b, s // tm),
        in_specs=[tok_spec, vec_spec, mod_spec, mod_spec, full(mu)] + [full(wt) for wt in weights],
        out_specs=[tok_spec] * 6,
        scratch_shapes=[pltpu.VMEM((tm + 8, d), F32)],
        compiler_params=_params("arbitrary", "arbitrary"),
        name="rwkv_proj",
    )(x, nw.reshape(1, d), shift[:, None], scale[:, None], mu, *weights)

    ct = 512
    npair = 4
    pair = 2 * RWKV_HEAD_DIM
    lanes = npair * pair
    assert s % ct == 0 and d % lanes == 0 and pair == V7X_LANES
    seq_spec = pl.BlockSpec((1, ct, lanes), lambda i, p, j: (i, j, p))
    par_spec = pl.BlockSpec((1, lanes), lambda i, p, j: (0, p))
    z = pl.pallas_call(
        functools.partial(_rwkv_scan_kernel, ct, npair),
        out_shape=jax.ShapeDtypeStruct((b, s, d), BF16),
        grid=(b, d // lanes, s // ct),
        in_specs=[seq_spec] * 6 + [par_spec] * 5,
        out_specs=seq_spec,
        scratch_shapes=[pltpu.VMEM((npair, pair, pair), F32)],
        compiler_params=_params("parallel", "parallel", "arbitrary"),
        name="rwkv_scan",
    )(r, k, v, lw, a, g, k_k.reshape(1, d), k_a.reshape(1, d), r_k.reshape(1, d), ln_w.reshape(1, d), ln_b.reshape(1, d))
    return _outproj(x.reshape(b * s, d), z.reshape(b * s, d), gate, w_out, s).reshape(b, s, d)


NSA_Q_TILE = 512
NSA_CMP_SUB_TILE = 128
NSA_ATTN_Q_TILE = 512
NSA_KV_TILE = 512
NSA_PROJ_PAD = 128
NSA_QK_WIDTH = 128


def _head_rms(xf, p_ref, pt_ref, wvec):
    sums = sum(jnp.dot(part, p_ref[...], preferred_element_type=F32) for part in _split_bf16(xf * xf, 2))
    inv = lax.rsqrt(sums * (1.0 / NSA_HEAD_DIM) + RMS_EPS)
    inv_full = sum(jnp.dot(part, pt_ref[...], preferred_element_type=F32) for part in _split_bf16(inv, 3))
    return xf * inv_full * wvec


def _pos_lanes(pos):
    lane = lax.broadcasted_iota(jnp.int32, (1, NSA_QK_WIDTH), 1)
    hi = ((pos >> 6) << 6).astype(F32)
    lo = (pos & 63).astype(F32)
    d = NSA_HEAD_DIM
    ones = jnp.where((lane >= d + 6) & (lane < d + 9), 1.0, 0.0)
    return jnp.where((lane >= d) & (lane < d + 3), hi, jnp.where((lane >= d + 3) & (lane < d + 6), lo, ones))


def _nsa_proj_kernel(tm, qd, kd, x_ref, nw_ref, sh_ref, sc_ref, w_ref, pq_ref, pqt_ref, pk_ref, pkt_ref,
                     nq_ref, nks_ref, nkw_ref, qf_ref, slq_ref,
                     q_ref, ks_ref, vs_ref, kw_ref, vw_ref, kc_ref, vc_ref, gt_ref):
    hg, dk = NSA_HEADS_PER_GROUP, NSA_HEAD_DIM
    h = _normmod(x_ref[0], nw_ref[...], sh_ref[0], sc_ref[0])
    proj = _dot(h, w_ref[...])
    lane = lax.broadcasted_iota(jnp.int32, (1, NSA_QK_WIDTH), 1)
    low = lane < dk
    t_int = pl.program_id(1) * tm + lax.broadcasted_iota(jnp.int32, (tm, 1), 0)
    t_row = t_int.astype(F32)
    key_lanes = _pos_lanes(t_int)

    def heads(x, nheads):
        for pair in range(nheads // 2):
            xp = x[:, pair * 2 * dk:(pair + 1) * 2 * dk]
            yield 2 * pair, xp
            yield 2 * pair + 1, pltpu.roll(xp, dk, 1)

    qn = _head_rms(proj[:, :qd], pq_ref, pqt_ref, nq_ref[...]) * (NSA_SCALE * LOG2_E)
    for hd, xs in heads(qn, NSA_HEADS):
        feat = qf_ref[hd:hd + 1, :]
        for i, part in enumerate(_split_bf16(-slq_ref[hd:hd + 1, :] * t_row, 3)):
            feat = jnp.where(lane == dk + 6 + i, part.astype(F32), feat)
        q_ref[0, hd // hg, hd % hg] = jnp.where(low, xs, feat).astype(BF16)
    o = qd
    for g, xs in heads(proj[:, o:o + kd], NSA_KV_GROUPS):
        kc_ref[0, g] = xs.astype(kc_ref.dtype)
    for g, xs in heads(proj[:, o + kd:o + 2 * kd], NSA_KV_GROUPS):
        vc_ref[0, g] = xs.astype(vc_ref.dtype)
    for g, xs in heads(_head_rms(proj[:, o + 2 * kd:o + 3 * kd], pk_ref, pkt_ref, nks_ref[...]), NSA_KV_GROUPS):
        ks_ref[0, g] = jnp.where(low, xs, key_lanes).astype(BF16)
    for g, xs in heads(proj[:, o + 3 * kd:o + 4 * kd], NSA_KV_GROUPS):
        vs_ref[0, g] = jnp.where(low, xs, 1.0).astype(BF16)
    for g, xs in heads(_head_rms(proj[:, o + 4 * kd:o + 5 * kd], pk_ref, pkt_ref, nkw_ref[...]), NSA_KV_GROUPS):
        kw_ref[0, g] = jnp.where(low, xs, key_lanes).astype(BF16)
    for g, xs in heads(proj[:, o + 5 * kd:o + 6 * kd], NSA_KV_GROUPS):
        vw_ref[0, g] = jnp.where(low, xs, 1.0).astype(BF16)
    o += 6 * kd
    for g in range(NSA_KV_GROUPS):
        gt_ref[0, g] = _sigmoid(proj[:, o + g * NSA_PROJ_PAD:o + (g + 1) * NSA_PROJ_PAD])


def _gelu_tanh(x):
    return 0.5 * x * (1.0 + jnp.tanh(math.sqrt(2.0 / math.pi) * (x + 0.044715 * (x * x * x))))


def _nsa_compress_kernel(half, kc_ref, vc_ref, pos_ref, w1_ref, w2_ref, nk_ref, kco_ref, vco_ref):
    nsub = kc_ref.shape[2]

    def comp(x, i):
        ya = _dot(x + pos_ref[2 * i:2 * i + 1, :], w1_ref[i, 0:half, :])
        yb = _dot(x + pos_ref[2 * i + 1:2 * i + 2, :], w1_ref[i, half:2 * half, :])
        hid = ya + pltpu.roll(yb, nsub - 1, 0)
        return _dot(_gelu_tanh(hid), w2_ref[i])

    kcm = comp(kc_ref[0, 0], 0)
    ms = jnp.sum(kcm * kcm, axis=-1, keepdims=True) * (1.0 / NSA_HEAD_DIM)
    block_end = lax.broadcasted_iota(jnp.int32, (nsub, 1), 0) * CMP_STRIDE + CMP_LEN - 1
    kco_ref[0, 0] = (kcm * lax.rsqrt(ms + RMS_EPS) * nk_ref[...] + _pos_lanes(block_end)).astype(BF16)
    vco_ref[0, 0] = comp(vc_ref[0, 0], 1).astype(BF16)


def _pack_heads(per_head):
    low = lax.broadcasted_iota(jnp.int32, (1, NSA_QK_WIDTH), 1) < NSA_HEAD_DIM
    pairs = [jnp.where(low, per_head[i], pltpu.roll(per_head[i + 1], NSA_HEAD_DIM, 1))
             for i in range(0, len(per_head), 2)]
    return jnp.concatenate(pairs, axis=-1)


def _nsa_cmp_kernel(qt, nc, nsel, q_ref, kc_ref, vc_ref, g_ref, oc_ref, sel_ref):
    hg = NSA_HEADS_PER_GROUP
    q0 = pl.program_id(2) * qt
    qs = NSA_CMP_SUB_TILE
    jo = lax.broadcasted_iota(jnp.int32, (nsel, nc), 0)
    no = lax.broadcasted_iota(jnp.int32, (nsel, nc), 1)
    ratio = SEL_LEN // CMP_STRIDE
    first = jo * ratio - (CMP_LEN // CMP_STRIDE - 1)
    overlap_t = jnp.where((no >= first) & (no < (jo + 1) * ratio) & (no < nc - 1), 1.0, 0.0).astype(BF16)
    n = lax.broadcasted_iota(jnp.int32, (qs, nc), 1)
    imp_parts = []
    for sub in range(qt // qs):
        rows = slice(sub * qs, (sub + 1) * qs)
        q = q_ref[0, 0, :, rows, :].reshape(hg * qs, NSA_QK_WIDTH)
        s3 = _dot_nt(q, kc_ref[0, 0]).reshape(hg, qs, nc)
        t = q0 + sub * qs + lax.broadcasted_iota(jnp.int32, (qs, nc), 0)
        mask = (t >= n * CMP_STRIDE + CMP_LEN - 1)[None]
        sm = jnp.where(mask, s3, MASKED_SCORE)
        e = jnp.exp2(sm - jnp.max(sm, axis=-1, keepdims=True))
        has_key = (q0 + sub * qs + lax.broadcasted_iota(jnp.int32, (1, qs, 1), 1)) >= CMP_LEN - 1
        p = e * jnp.where(has_key, 1.0 / jnp.maximum(jnp.sum(e, axis=-1, keepdims=True), 1e-30), 0.0)
        oc = _dot(p.reshape(hg * qs, nc), vc_ref[0, 0])
        g = g_ref[0, 0, rows, :]
        oc_ref[0, rows, :] = _pack_heads([oc[hd * qs:(hd + 1) * qs] * g[:, hd:hd + 1] for hd in range(hg)])
        psum = p[0] + p[1] + p[2] + p[3]
        imp_parts.append(sum(lax.dot_general(overlap_t, part, (((1,), (1,)), ((), ())), preferred_element_type=F32)
                             for part in _split_bf16(psum, 3)))
    imp_t = jnp.concatenate(imp_parts, axis=1)

    j = lax.broadcasted_iota(jnp.int32, (nsel, qt), 0)
    tt = q0 + lax.broadcasted_iota(jnp.int32, (nsel, qt), 1)
    valid = j * SEL_LEN <= tt
    qb = tt >> int(math.log2(SEL_LEN))
    forced = (j == 0) | (j == qb) | (j == qb - 1)
    score = jnp.where(valid, jnp.where(forced, SEL_BIG, imp_t), -SEL_BIG)
    jf = j.astype(F32)
    picked = -(2.0 ** 120)
    for _ in range(min(SEL_TOPK, nsel)):
        mx = jnp.max(score, axis=0, keepdims=True)
        jmin = jnp.min(jnp.where(score == mx, jf, float(nsel)), axis=0, keepdims=True)
        score = jnp.where(jf == jmin, picked, score)
    sel = jnp.where((score == picked) & valid, 1.0, 0.0)
    sel_ref[0, 0] = sel.T.astype(BF16)


def _nsa_attn_kernel(qt, kt, nsel, nkt, nq, tiles_ref, cnt_ref,
                     q_ref, ks_ref, vs_ref, kw_ref, vw_ref, oh_ref, sel_ref, g_ref, oc_ref, o_ref):
    hg, dk = NSA_HEADS_PER_GROUP, NSA_HEAD_DIM
    rows = hg * qt
    step = (pl.program_id(0) * NSA_KV_GROUPS + pl.program_id(1)) * nq + pl.program_id(2)
    q0 = pl.program_id(2) * qt
    q = q_ref[0, 0].reshape(rows, NSA_QK_WIDTH)
    unsel = (1.0 - sel_ref[0, 0]) * NSA_UNSELECTED
    if nsel < NSA_QK_WIDTH:
        unsel = jnp.concatenate([unsel, jnp.zeros((qt, NSA_QK_WIDTH - nsel), BF16)], axis=1)
    q_sel = jnp.concatenate([q, jnp.concatenate([unsel] * hg, axis=0)], axis=1)

    t_k = q0 + lax.broadcasted_iota(jnp.int32, (qt, kt), 0)
    c_k = lax.broadcasted_iota(jnp.int32, (qt, kt), 1)

    def sel_step(i, carry, diagonal):
        m, acc = carry
        k0 = pl.multiple_of(tiles_ref[step * nkt + i] * kt, kt)
        keys = jnp.concatenate([ks_ref[0, 0, pl.ds(k0, kt), :], oh_ref[pl.ds(k0, kt), :]], axis=1)
        sm = _dot_nt(q_sel, keys, BF16)
        if diagonal:
            causal = (k0 + c_k <= t_k)[None]
            sm = jnp.where(causal, sm.reshape(hg, qt, kt), MASKED_SCORE).reshape(rows, kt)
        m_new = jnp.maximum(m, jnp.max(sm, axis=-1, keepdims=True))
        alpha = jnp.exp2((m - m_new).astype(F32))
        acc = alpha * acc + _dot(jnp.exp2(sm - m_new), vs_ref[0, 0, pl.ds(k0, kt), :])
        return m_new, acc

    init = (jnp.full((rows, 1), MASKED_SCORE, BF16), jnp.zeros((rows, NSA_QK_WIDTH), F32))
    last = cnt_ref[step] - 1
    carry = lax.fori_loop(0, last, functools.partial(sel_step, diagonal=False), init)
    _, acc_s = sel_step(last, carry, diagonal=True)
    o_s = acc_s / jnp.maximum(acc_s[:, dk:dk + 1], 1e-30)

    wk = WINDOW + qt
    ws = pl.multiple_of(jnp.maximum(q0 - WINDOW, 0), qt)
    s3 = _dot_nt(q, kw_ref[0, 0, pl.ds(ws, wk), :], BF16).reshape(hg, qt, wk)
    dist = (q0 + lax.broadcasted_iota(jnp.int32, (qt, wk), 0)) - (ws + lax.broadcasted_iota(jnp.int32, (qt, wk), 1))
    mask = ((dist >= 0) & (dist < WINDOW))[None]
    sm = jnp.where(mask, s3, MASKED_SCORE).reshape(rows, wk)
    e = jnp.exp2(sm - jnp.max(sm, axis=-1, keepdims=True))
    acc_w = _dot(e, vw_ref[0, 0, pl.ds(ws, wk), :])
    o_w = acc_w / jnp.maximum(acc_w[:, dk:dk + 1], 1e-30)

    g = g_ref[0, 0]
    per_head = [g[:, hg + hd:hg + hd + 1] * o_s[hd * qt:(hd + 1) * qt]
                + g[:, 2 * hg + hd:2 * hg + hd + 1] * o_w[hd * qt:(hd + 1) * qt] for hd in range(hg)]
    o_ref[0] = (oc_ref[0] + _pack_heads(per_head)).astype(o_ref.dtype)


def _nsa_layer(x, nw, shift, scale, gate, w_in, cmp_pos, cmp_w1, cmp_w2, qk_norm, w_out):
    b, s, d = x.shape
    t = b * s
    g_, hg, dk = NSA_KV_GROUPS, NSA_HEADS_PER_GROUP, NSA_HEAD_DIM
    qd, kd = NSA_HEADS * dk, g_ * dk
    tm = ROW_TILE
    wl = NSA_QK_WIDTH
    n_main = qd + 6 * kd
    assert w_in.shape[1] == n_main + 3 * NSA_HEADS and s % tm == 0
    gate_cols = w_in[:, n_main:].reshape(d, 3, g_, hg).transpose(0, 2, 1, 3).reshape(d, g_, 3 * hg)
    gate_cols = jnp.pad(gate_cols, ((0, 0), (0, 0), (0, NSA_PROJ_PAD - 3 * hg))).reshape(d, g_ * NSA_PROJ_PAD)
    w_pad = jnp.concatenate([w_in[:, :n_main], gate_cols], axis=1).astype(BF16)

    def head_onehot(width):
        lane_head = jnp.arange(width)[:, None] // dk
        return (lane_head == jnp.arange(V7X_LANES)[None, :]).astype(BF16)

    pq, pk = head_onehot(qd), head_onehot(kd)

    def tile_w(wv, reps):
        return jnp.tile(wv, reps).reshape(1, reps * dk)

    slopes = jnp.exp2(-8.0 * (jnp.arange(NSA_HEADS, dtype=F32) + 1) / NSA_HEADS)
    slope_parts = jnp.stack([p.astype(F32) for p in _split_bf16(slopes * LOG2_E, 3)], axis=-1)
    slope_rows = jnp.broadcast_to((slopes * LOG2_E)[:, None], (NSA_HEADS, wl))
    q_lanes = jnp.zeros((NSA_HEADS, wl), F32).at[:, dk:dk + 6].set(jnp.concatenate([slope_parts, slope_parts], axis=-1))

    consts = [pq, pq.T, pk, pk.T, tile_w(qk_norm[0], NSA_HEADS), tile_w(qk_norm[2], g_), tile_w(qk_norm[3], g_), q_lanes, slope_rows]
    full2 = lambda arr: pl.BlockSpec(arr.shape, lambda i, j: (0, 0))
    mod_spec = pl.BlockSpec((1, 1, d), lambda i, j: (i, 0, 0))
    grp = lambda dt: jax.ShapeDtypeStruct((b, g_, s, wl), dt)
    grp_spec = pl.BlockSpec((1, g_, tm, wl), lambda i, j: (i, 0, j, 0))
    q5, ks4, vs4, kw4, vw4, kc, vc, gates = pl.pallas_call(
        functools.partial(_nsa_proj_kernel, tm, qd, kd),
        out_shape=[jax.ShapeDtypeStruct((b, g_, hg, s, wl), BF16), grp(BF16), grp(BF16), grp(BF16), grp(BF16),
                   grp(BF16), grp(BF16), grp(F32)],
        grid=(b, s // tm),
        in_specs=[pl.BlockSpec((1, tm, d), lambda i, j: (i, j, 0)), pl.BlockSpec((1, d), lambda i, j: (0, 0)),
                  mod_spec, mod_spec, full2(w_pad)] + [full2(cst) for cst in consts],
        out_specs=[pl.BlockSpec((1, g_, hg, tm, wl), lambda i, j: (i, 0, 0, j, 0))] + [grp_spec] * 7,
        compiler_params=_params("parallel", "parallel"),
        name="nsa_proj",
    )(x, nw.reshape(1, d), shift[:, None], scale[:, None], w_pad, *consts)

    nsub = s // CMP_STRIDE
    half = CMP_STRIDE * wl
    kc3, vc3 = kc.reshape(b, g_, nsub, half), vc.reshape(b, g_, nsub, half)
    pad_tok = lambda arr: jnp.pad(arr, [(0, 0)] * (arr.ndim - 1) + [(0, wl - dk)])
    pos = pad_tok(cmp_pos).reshape(4, half)
    w1b = jnp.pad(cmp_w1.reshape(2, CMP_LEN, dk, -1), ((0, 0), (0, 0), (0, wl - dk), (0, 0))).reshape(2, 2 * half, -1).astype(BF16)
    w2b = jnp.pad(cmp_w2, ((0, 0), (0, 0), (0, wl - dk))).astype(BF16)
    nk_lanes = jnp.pad(qk_norm[1], (0, wl - dk)).reshape(1, wl)
    cmp_in = pl.BlockSpec((1, 1, nsub, half), lambda i, j: (i, j, 0, 0))
    cmp_out = pl.BlockSpec((1, 1, nsub, wl), lambda i, j: (i, j, 0, 0))
    kcc, vcc = pl.pallas_call(
        functools.partial(_nsa_compress_kernel, half),
        out_shape=[jax.ShapeDtypeStruct((b, g_, nsub, wl), BF16)] * 2,
        grid=(b, g_),
        in_specs=[cmp_in, cmp_in, pl.BlockSpec(pos.shape, lambda i, j: (0, 0)),
                  pl.BlockSpec(w1b.shape, lambda i, j: (0, 0, 0)), pl.BlockSpec(w2b.shape, lambda i, j: (0, 0, 0)),
                  pl.BlockSpec((1, wl), lambda i, j: (0, 0))],
        out_specs=[cmp_out, cmp_out],
        compiler_params=_params("parallel", "parallel"),
        name="nsa_compress",
    )(kc3, vc3, pos, w1b, w2b, nk_lanes)

    nsel = s // SEL_LEN
    kt = min(NSA_KV_TILE, s)
    nkt = s // kt
    seq_kv = lambda n: pl.BlockSpec((1, 1, n, wl), lambda i, j, k, *_: (i, j, 0, 0))

    def tile_specs(qt):
        assert s % qt == 0 and s >= WINDOW + qt and kt % qt == 0
        return (pl.BlockSpec((1, 1, hg, qt, wl), lambda i, j, k, *_: (i, j, 0, k, 0)),
                pl.BlockSpec((1, qt, hg * dk), lambda i, j, k, *_: (i, k, j)),
                pl.BlockSpec((1, 1, qt, wl), lambda i, j, k, *_: (i, j, k, 0)),
                pl.BlockSpec((1, 1, qt, nsel), lambda i, j, k, *_: (i, j, k, 0)))

    qt = NSA_Q_TILE
    q_spec, o_spec, gate_spec, sel_spec = tile_specs(qt)
    oc, sel = pl.pallas_call(
        functools.partial(_nsa_cmp_kernel, qt, nsub, nsel),
        out_shape=[jax.ShapeDtypeStruct((b, s, qd), F32), jax.ShapeDtypeStruct((b, g_, s, nsel), BF16)],
        grid=(b, g_, s // qt),
        in_specs=[q_spec, seq_kv(nsub), seq_kv(nsub), gate_spec],
        out_specs=[o_spec, sel_spec],
        compiler_params=_params("parallel", "parallel", "parallel"),
        name="nsa_cmp_select",
    )(q5, kcc, vcc, gates)

    assert nsel <= wl
    block_onehot = (jnp.arange(s)[:, None] // SEL_LEN == jnp.arange(wl)[None, :]).astype(BF16)

    qt = NSA_ATTN_Q_TILE
    nq = s // qt
    q_spec, o_spec, gate_spec, sel_spec = tile_specs(qt)
    active = sel.reshape(b, g_, nq, qt, nkt, kt // SEL_LEN).max(axis=(3, 5)) > 0
    order = jnp.sort(jnp.where(active, 0, nkt) + jnp.arange(nkt, dtype=jnp.int32), axis=-1)
    tiles = (order % nkt).astype(jnp.int32).reshape(-1)
    counts = active.sum(axis=-1).astype(jnp.int32).reshape(-1)

    o3 = pl.pallas_call(
        functools.partial(_nsa_attn_kernel, qt, kt, nsel, nkt, nq),
        out_shape=jax.ShapeDtypeStruct((b, s, qd), BF16),
        grid_spec=pltpu.PrefetchScalarGridSpec(
            num_scalar_prefetch=2,
            grid=(b, g_, nq),
            in_specs=[q_spec, seq_kv(s), seq_kv(s), seq_kv(s), seq_kv(s),
                      pl.BlockSpec((s, wl), lambda i, j, k, *_: (0, 0)), sel_spec, gate_spec,
                      o_spec],
            out_specs=o_spec,
        ),
        compiler_params=_params("parallel", "parallel", "arbitrary"),
        name="nsa_attention",
    )(tiles, counts, q5, ks4, vs4, kw4, vw4, block_onehot, sel, gates, oc)
    return _outproj(x.reshape(t, d), o3.reshape(t, qd), gate, w_out, s).reshape(b, s, d)


def kernel(x, c, ada_w, ada_b, norm_mix_w, norm_ffn_w, ffn_w_gate, ffn_w_up, ffn_w_down, nsa_w_in, nsa_cmp_pos, nsa_cmp_w1, nsa_cmp_w2, nsa_qk_norm, nsa_w_out, rwkv_mu, rwkv_w_rkv, rwkv_w0, rwkv_wd_a, rwkv_wd_b, rwkv_a0, rwkv_wa_a, rwkv_wa_b, rwkv_wg_a, rwkv_wg_b, rwkv_k_k, rwkv_k_a, rwkv_r_k, rwkv_ln_w, rwkv_ln_b, rwkv_w_out, pool_w, pool_b, pool_scale):
    b, s, d = x.shape
    depth = ada_w.shape[0]
    mod = _ada_mod(c, ada_w, ada_b)
    for i in range(depth):
        sh_m, sc_m, g_m, sh_f, sc_f, g_f = jnp.split(mod[i], N_MOD, axis=-1)
        kind, j = i % N_MIXERS, i // N_MIXERS
        if kind == 0:
            x = _nsa_layer(x, norm_mix_w[i], sh_m, sc_m, g_m, nsa_w_in[j], nsa_cmp_pos[j], nsa_cmp_w1[j], nsa_cmp_w2[j],
                           nsa_qk_norm[j], nsa_w_out[j])
        elif kind == 1:
            x = _rwkv_layer(x, norm_mix_w[i], sh_m, sc_m, g_m, rwkv_mu[j], rwkv_w_rkv[j], rwkv_w0[j], rwkv_wd_a[j],
                            rwkv_wd_b[j], rwkv_a0[j], rwkv_wa_a[j], rwkv_wa_b[j], rwkv_wg_a[j], rwkv_wg_b[j],
                            rwkv_k_k[j], rwkv_k_a[j], rwkv_r_k[j], rwkv_ln_w[j], rwkv_ln_b[j], rwkv_w_out[j])
        elif kind == 2:
            x = _pool_layer(x, norm_mix_w[i], sh_m, sc_m, g_m, pool_w[j], pool_b[j], pool_scale[j])
        x = _ffn(x.reshape(b * s, d), norm_ffn_w[i], sh_f, sc_f, g_f, ffn_w_gate[i], ffn_w_up[i], ffn_w_down[i], s).reshape(b, s, d)
    return x
```

```python
import functools
import math

import jax
import jax.numpy as jnp
from jax import lax
from jax.experimental import pallas as pl
from jax.experimental.pallas import tpu as pltpu

F32 = jnp.float32
BF16 = jnp.bfloat16

V7X_LANES = 128
V7X_VMEM_LIMIT_BYTES = 56 * 1024 * 1024

RMS_EPS = 1e-6
N_MOD = 6
N_MIXERS = 3

NSA_HEAD_DIM = 64
NSA_KV_GROUPS = 4
NSA_HEADS_PER_GROUP = 4
NSA_HEADS = NSA_KV_GROUPS * NSA_HEADS_PER_GROUP
CMP_LEN = 32
CMP_STRIDE = 16
SEL_LEN = 64
SEL_TOPK = 16
WINDOW = 512
NSA_SCALE = NSA_HEAD_DIM ** -0.5
LOG2_E = math.log2(math.e)
SEL_BIG = 1e9
MASKED_SCORE = -1e30
NSA_UNSELECTED = -(2.0 ** 100)

RWKV_HEAD_DIM = 64
RWKV_GN_EPS = 64e-5
RWKV_CHUNK = 64

POOL_WINDOWS = (2, 4, 8, 16)
POOL_HALO = 16

ROW_TILE = 512
FFN_ROW_TILE = 1024
FFN_COL_CHUNK = 256


def _params(*sem):
    return pltpu.CompilerParams(dimension_semantics=sem, vmem_limit_bytes=V7X_VMEM_LIMIT_BYTES)


def _sigmoid(z):
    return 1.0 / (1.0 + jnp.exp(-z))


def _normmod(x, nw, shift, scale):
    ms = jnp.mean(x * x, axis=-1, keepdims=True)
    return (x * lax.rsqrt(ms + RMS_EPS) * nw) * (1.0 + scale) + shift


def _dot(a, b):
    return jnp.dot(a.astype(BF16), b.astype(BF16), preferred_element_type=F32)


def _dot_nt(a, b, out_dtype=F32):
    out = lax.dot_general(a.astype(BF16), b.astype(BF16), (((1,), (1,)), ((), ())), preferred_element_type=F32)
    return out.astype(out_dtype)


def _split_bf16(x, parts):
    out = []
    for _ in range(parts):
        p = x.astype(BF16)
        out.append(p)
        x = x - p.astype(F32)
    return out


def _mod_kernel(c_ref, w_ref, b_ref, o_ref):
    c = c_ref[...]
    ca = c * _sigmoid(c)
    o_ref[0] = _dot(ca, w_ref[0]) + b_ref[0]


def _ada_mod(c, ada_w, ada_b):
    depth, d, n = ada_w.shape
    b = c.shape[0]
    rows = 8
    tn = 1536
    assert b <= rows and n % tn == 0
    c_pad = jnp.zeros((rows, d), F32).at[:b].set(c)
    out = pl.pallas_call(
        _mod_kernel,
        out_shape=jax.ShapeDtypeStruct((depth, rows, n), F32),
        grid=(depth, n // tn),
        in_specs=[
            pl.BlockSpec((rows, d), lambda i, j: (0, 0)),
            pl.BlockSpec((1, d, tn), lambda i, j: (i, 0, j)),
            pl.BlockSpec((1, 1, tn), lambda i, j: (i, 0, j)),
        ],
        out_specs=pl.BlockSpec((1, rows, tn), lambda i, j: (i, 0, j)),
        compiler_params=_params("parallel", "parallel"),
        name="ada_mod",
    )(c_pad, ada_w, ada_b.reshape(depth, 1, n))
    return out[:, :b]


def _ffn_kernel(x_ref, nw_ref, sh_ref, sc_ref, g_ref, wg_ref, wu_ref, wd_ref, o_ref):
    x = x_ref[...]
    h = _normmod(x, nw_ref[...], sh_ref[0], sc_ref[0]).astype(BF16)
    f = wg_ref.shape[1]
    acts = []
    for c0 in range(0, f, FFN_COL_CHUNK):
        cols = slice(c0, c0 + FFN_COL_CHUNK)
        g = jnp.dot(h, wg_ref[:, cols], preferred_element_type=F32)
        u = jnp.dot(h, wu_ref[:, cols], preferred_element_type=F32)
        acts.append((g * _sigmoid(g) * u).astype(BF16))
    o_ref[...] = x + g_ref[0] * jnp.dot(jnp.concatenate(acts, axis=1), wd_ref[...], preferred_element_type=F32)


def _ffn(x2, nw, shift, scale, gate, wg, wu, wd, seq):
    t, d = x2.shape
    f = wg.shape[1]
    tm = FFN_ROW_TILE
    assert t % tm == 0 and seq % tm == 0 and f % FFN_COL_CHUNK == 0
    tpb = seq // tm
    mod_spec = pl.BlockSpec((1, 1, d), lambda i: (i // tpb, 0, 0))
    resident = lambda shape: pl.BlockSpec(shape, lambda i: (0, 0), pipeline_mode=pl.Buffered(1))
    return pl.pallas_call(
        _ffn_kernel,
        out_shape=jax.ShapeDtypeStruct((t, d), F32),
        grid=(t // tm,),
        in_specs=[
            pl.BlockSpec((tm, d), lambda i: (i, 0)),
            pl.BlockSpec((1, d), lambda i: (0, 0)),
            mod_spec, mod_spec, mod_spec,
            resident((d, f)), resident((d, f)), resident((f, d)),
        ],
        out_specs=pl.BlockSpec((tm, d), lambda i: (i, 0)),
        compiler_params=_params("parallel"),
        name="ffn",
    )(x2, nw.reshape(1, d), shift[:, None], scale[:, None], gate[:, None],
      wg.astype(BF16), wu.astype(BF16), wd.astype(BF16))


def _outproj_kernel(x_ref, z_ref, g_ref, w_ref, o_ref):
    o_ref[...] = x_ref[...] + g_ref[0] * _dot(z_ref[...], w_ref[...])


def _outproj(x2, z2, gate, w, seq):
    t, d = x2.shape
    k = z2.shape[1]
    tm = ROW_TILE
    tpb = seq // tm
    return pl.pallas_call(
        _outproj_kernel,
        out_shape=jax.ShapeDtypeStruct((t, d), F32),
        grid=(t // tm,),
        in_specs=[
            pl.BlockSpec((tm, d), lambda i: (i, 0)),
            pl.BlockSpec((tm, k), lambda i: (i, 0)),
            pl.BlockSpec((1, 1, d), lambda i: (i // tpb, 0, 0)),
            pl.BlockSpec((k, d), lambda i: (0, 0)),
        ],
        out_specs=pl.BlockSpec((tm, d), lambda i: (i, 0)),
        compiler_params=_params("parallel"),
        name="outproj",
    )(x2, z2, gate[:, None], w.astype(BF16))


def _pool_kernel(tm, gd, x_ref, nw_ref, sh_ref, sc_ref, g_ref, pw_ref, pb_ref, ps_ref, o_ref, ext_scr):
    s = pl.program_id(1)
    x = x_ref[0]
    h = _normmod(x, nw_ref[...], sh_ref[0], sc_ref[0])

    @pl.when(s == 0)
    def _():
        ext_scr[0:POOL_HALO, :] = jnp.zeros((POOL_HALO, x.shape[1]), F32)

    ext_scr[POOL_HALO:POOL_HALO + tm, :] = h
    row = s * tm + lax.broadcasted_iota(jnp.int32, (tm, 1), 0)
    ys = []
    for gi, win in enumerate(POOL_WINDOWS):
        lanes = slice(gi * gd, (gi + 1) * gd)
        hg = h[:, lanes]
        acc = hg
        for k in range(1, win):
            acc = acc + ext_scr[POOL_HALO - k:POOL_HALO - k + tm, lanes]
        cnt = jnp.minimum(row + 1, win).astype(F32)
        ys.append(_dot(acc / cnt - hg, pw_ref[gi]))
    y = (jnp.concatenate(ys, axis=-1) + pb_ref[...]) * ps_ref[...]
    o_ref[0] = x + g_ref[0] * y
    ext_scr[0:POOL_HALO, :] = ext_scr[tm:tm + POOL_HALO, :]


def _pool_layer(x, nw, shift, scale, gate, pw, pb, ps):
    b, s, d = x.shape
    tm = ROW_TILE
    gd = d // len(POOL_WINDOWS)
    assert s % tm == 0 and max(POOL_WINDOWS) <= POOL_HALO
    mod_spec = pl.BlockSpec((1, 1, d), lambda i, j: (i, 0, 0))
    vec_spec = pl.BlockSpec((1, d), lambda i, j: (0, 0))
    return pl.pallas_call(
        functools.partial(_pool_kernel, tm, gd),
        out_shape=jax.ShapeDtypeStruct((b, s, d), F32),
        grid=(b, s // tm),
        in_specs=[
            pl.BlockSpec((1, tm, d), lambda i, j: (i, j, 0)),
            vec_spec, mod_spec, mod_spec, mod_spec,
            pl.BlockSpec(pw.shape, lambda i, j: (0, 0, 0)),
            vec_spec, vec_spec,
        ],
        out_specs=pl.BlockSpec((1, tm, d), lambda i, j: (i, j, 0)),
        scratch_shapes=[pltpu.VMEM((tm + POOL_HALO, d), F32)],
        compiler_params=_params("arbitrary", "arbitrary"),
        name="pool_mixer",
    )(x, nw.reshape(1, d), shift[:, None], scale[:, None], gate[:, None],
      pw.astype(BF16), pb.reshape(1, d), ps.reshape(1, d))


def _rwkv_proj_kernel(tm, x_ref, nw_ref, sh_ref, sc_ref, mu_ref, wr_ref, wk_ref, wv_ref,
                      w0_ref, wda_ref, wdb_ref, a0_ref, waa_ref, wab_ref, wga_ref, wgb_ref,
                      r_ref, k_ref, v_ref, lw_ref, a_ref, g_ref, ext_scr):
    s = pl.program_id(1)
    h = _normmod(x_ref[0], nw_ref[...], sh_ref[0], sc_ref[0])

    @pl.when(s == 0)
    def _():
        ext_scr[0:8, :] = jnp.zeros((8, h.shape[1]), F32)

    ext_scr[8:8 + tm, :] = h
    xx = ext_scr[7:7 + tm, :] - h

    def mix(i):
        return h + xx * mu_ref[i:i + 1, :]

    r_ref[0] = _dot(mix(0), wr_ref[...]).astype(r_ref.dtype)
    k_ref[0] = _dot(mix(2), wk_ref[...])
    v_ref[0] = _dot(mix(3), wv_ref[...]).astype(v_ref.dtype)
    dw = w0_ref[...] + _dot(jnp.tanh(_dot(mix(1), wda_ref[...])), wdb_ref[...])
    softplus_neg = jnp.maximum(-dw, 0.0) + jnp.log(1.0 + jnp.exp(-jnp.abs(dw)))
    lw_ref[0] = -jnp.exp(-softplus_neg - 0.5)
    a_ref[0] = _sigmoid(a0_ref[...] + _dot(_dot(mix(4), waa_ref[...]), wab_ref[...]))
    g_ref[0] = _dot(_sigmoid(_dot(mix(5), wga_ref[...])), wgb_ref[...]).astype(g_ref.dtype)
    ext_scr[0:8, :] = ext_scr[tm:tm + 8, :]


def _rwkv_scan_kernel(ct, npair, r_ref, k_ref, v_ref, lw_ref, a_ref, g_ref, kk_ref, ka_ref, rk_ref, lnw_ref, lnb_ref,
                      z_ref, s_scr):
    c, n = RWKV_CHUNK, RWKV_HEAD_DIM
    w = 2 * n
    shift = int(math.log2(n))

    @pl.when(pl.program_id(2) == 0)
    def _():
        s_scr[...] = jnp.zeros_like(s_scr)

    head0 = lax.broadcasted_iota(jnp.int32, (1, w), 1) < n
    ri = lax.broadcasted_iota(jnp.int32, (w, w), 0)
    ci = lax.broadcasted_iota(jnp.int32, (w, w), 1)
    same_head = jnp.where((ri >> shift) == (ci >> shift), 1.0, 0.0).astype(BF16)
    strict, incl, eye = ri > ci, ri >= ci, ri == ci
    rt = lax.broadcasted_iota(jnp.int32, (ct, ct), 0)
    cc = lax.broadcasted_iota(jnp.int32, (ct, ct), 1)
    cshift = int(math.log2(c))
    chunk_tri = jnp.where(((rt >> cshift) == (cc >> cshift)) & (rt >= cc), 1.0, 0.0).astype(BF16)
    nb = ct // c

    def head_sum(xf):
        return sum(jnp.dot(p, same_head, preferred_element_type=F32) for p in _split_bf16(xf, 2))

    def stack(xf):
        x3 = xf.reshape(nb, c, w)
        return jnp.concatenate([jnp.where(head0, x3, 0.0), jnp.where(head0, 0.0, x3)], axis=1)

    def bmm(x, y):
        return jnp.einsum('bij,bjk->bik', x.astype(BF16), y.astype(BF16), preferred_element_type=F32)

    def bmm_nt(x, y):
        return jnp.einsum('bik,bjk->bij', x.astype(BF16), y.astype(BF16), preferred_element_type=F32)

    pairs = []
    stacked = []
    for hp in range(npair):
        ln = slice(hp * w, (hp + 1) * w)
        r, v = r_ref[0, :, ln].astype(F32), v_ref[0, :, ln].astype(F32)
        k, lw, a = k_ref[0, :, ln], lw_ref[0, :, ln], a_ref[0, :, ln]
        kk = k * kk_ref[:, ln]
        kkn = kk * lax.rsqrt(jnp.maximum(head_sum(kk * kk), 1e-24))
        k2 = k * (1.0 + (a - 1.0) * ka_ref[:, ln])
        cum = sum(jnp.dot(chunk_tri, p, preferred_element_type=F32) for p in _split_bf16(lw, 3))
        einv = jnp.exp(-cum)
        pairs.append((r, k2, v))
        stacked.append((stack(-kkn * jnp.exp(cum - lw)), stack(r * jnp.exp(cum)), stack(kkn * a * einv),
                        stack(k2 * einv), stack(v), jnp.exp(cum.reshape(nb, c, w)[:, c - 1:c, :])))
    at, rt_, kb, kq, vs, pc = (jnp.concatenate([st[i] for st in stacked], axis=0) for i in range(6))
    nbt = npair * nb
    kbe, kqe = kb * pc, kq * pc
    tt = bmm_nt(jnp.concatenate([at, rt_], axis=1), jnp.concatenate([kb, kq], axis=1))
    a_m = jnp.where(strict, tt[:, :w, :w], 0.0)
    b_m = jnp.where(strict, tt[:, :w, w:], 0.0)
    ar_m = jnp.where(incl, tt[:, w:, :w], 0.0)
    br_m = jnp.where(incl, tt[:, w:, w:], 0.0)
    rsum = jnp.where(eye, 1.0, 0.0) + a_m
    pw = bmm(a_m, a_m)
    for _ in range(cshift - 2):
        both = bmm(pw, jnp.concatenate([rsum, pw], axis=2))
        rsum = rsum + both[:, :, :w]
        pw = both[:, :, w:]
    tinv = rsum + bmm(pw, rsum)
    w12 = bmm(tinv, jnp.concatenate([at, bmm(b_m, vs)], axis=2))
    xmat = jnp.concatenate([w12, jnp.concatenate([jnp.zeros_like(vs), vs], axis=2)], axis=1)
    kbe_t = jnp.stack([kbe[i].T for i in range(nbt)])
    kqe_t = jnp.stack([kqe[i].T for i in range(nbt)])
    lhs = jnp.concatenate([jnp.concatenate([ar_m, br_m], axis=2), jnp.concatenate([kbe_t, kqe_t], axis=2)], axis=1)
    res = bmm(lhs, xmat)
    g_m = rt_ + res[:, :w, :w]
    y0 = res[:, :w, w:]
    mt = jnp.where(eye, jnp.broadcast_to(pc, (nbt, w, w)), 0.0) + res[:, w:, :w]
    nt = res[:, w:, w:]

    states = [s_scr[hp] for hp in range(npair)]
    ys = [[] for _ in range(npair)]
    for idx in range(nb):
        for hp in range(npair):
            bi = hp * nb + idx
            step = _dot(jnp.concatenate([g_m[bi], mt[bi]], axis=0), states[hp])
            yst = step[:w] + y0[bi]
            states[hp] = step[w:] + nt[bi]
            ys[hp].append(yst[:c] + yst[c:])
    for hp in range(npair):
        ln = slice(hp * w, (hp + 1) * w)
        s_scr[hp] = states[hp]
        y = jnp.concatenate(ys[hp], axis=0)
        r, k2, v = pairs[hp]
        mean = head_sum(y) * (1.0 / n)
        dlt = y - mean
        var = head_sum(dlt * dlt) * (1.0 / n)
        yn = dlt * lax.rsqrt(var + RWKV_GN_EPS) * lnw_ref[:, ln] + lnb_ref[:, ln]
        bonus = head_sum(r * k2 * rk_ref[:, ln]) * v
        z_ref[0, :, ln] = ((yn + bonus) * g_ref[0, :, ln].astype(F32)).astype(z_ref.dtype)


def _rwkv_layer(x, nw, shift, scale, gate, mu, w_rkv, w0, wd_a, wd_b, a0, wa_a, wa_b, wg_a, wg_b,
                k_k, k_a, r_k, ln_w, ln_b, w_out):
    b, s, d = x.shape
    tm = ROW_TILE
    assert s % tm == 0
    mod_spec = pl.BlockSpec((1, 1, d), lambda i, j: (i, 0, 0))
    vec_spec = pl.BlockSpec((1, d), lambda i, j: (0, 0))
    tok_spec = pl.BlockSpec((1, tm, d), lambda i, j: (i, j, 0))

    def full(arr):
        return pl.BlockSpec(arr.shape, lambda i, j: (0,) * arr.ndim)

    weights = [w_rkv[0].astype(BF16), w_rkv[1].astype(BF16), w_rkv[2].astype(BF16),
               w0.reshape(1, d), wd_a.astype(BF16), wd_b.astype(BF16),
               a0.reshape(1, d), wa_a.astype(BF16), wa_b.astype(BF16),
               wg_a.astype(BF16), wg_b.astype(BF16)]
    tok_shape = jax.ShapeDtypeStruct((b, s, d), F32)
    tok_bf16 = jax.ShapeDtypeStruct((b, s, d), BF16)
    r, k, v, lw, a, g = pl.pallas_call(
        functools.partial(_rwkv_proj_kernel, tm),
        out_shape=[tok_bf16, tok_shape, tok_bf16, tok_shape, tok_shape, tok_bf16],
        grid=(b, s // tm),
        in_specs=[tok_spec, vec_spec, mod_spec, mod_spec, full(mu)] + [full(wt) for wt in weights],
        out_specs=[tok_spec] * 6,
        scratch_shapes=[pltpu.VMEM((tm + 8, d), F32)],
        compiler_params=_params("arbitrary", "arbitrary"),
        name="rwkv_proj",
    )(x, nw.reshape(1, d), shift[:, None], scale[:, None], mu, *weights)

    ct = 512
    npair = 4
    pair = 2 * RWKV_HEAD_DIM
    lanes = npair * pair
    assert s % ct == 0 and d % lanes == 0 and pair == V7X_LANES
    seq_spec = pl.BlockSpec((1, ct, lanes), lambda i, p, j: (i, j, p))
    par_spec = pl.BlockSpec((1, lanes), lambda i, p, j: (0, p))
    z = pl.pallas_call(
        functools.partial(_rwkv_scan_kernel, ct, npair),
        out_shape=jax.ShapeDtypeStruct((b, s, d), BF16),
        grid=(b, d // lanes, s // ct),
        in_specs=[seq_spec] * 6 + [par_spec] * 5,
        out_specs=seq_spec,
        scratch_shapes=[pltpu.VMEM((npair, pair, pair), F32)],
        compiler_params=_params("parallel", "parallel", "arbitrary"),
        name="rwkv_scan",
    )(r, k, v, lw, a, g, k_k.reshape(1, d), k_a.reshape(1, d), r_k.reshape(1, d), ln_w.reshape(1, d), ln_b.reshape(1, d))
    return _outproj(x.reshape(b * s, d), z.reshape(b * s, d), gate, w_out, s).reshape(b, s, d)


NSA_Q_TILE = 512
NSA_CMP_SUB_TILE = 128
NSA_ATTN_Q_TILE = 512
NSA_KV_TILE = 512
NSA_PROJ_PAD = 128
NSA_QK_WIDTH = 128


def _head_rms(xf, p_ref, pt_ref, wvec):
    sums = sum(jnp.dot(part, p_ref[...], preferred_element_type=F32) for part in _split_bf16(xf * xf, 2))
    inv = lax.rsqrt(sums * (1.0 / NSA_HEAD_DIM) + RMS_EPS)
    inv_full = sum(jnp.dot(part, pt_ref[...], preferred_element_type=F32) for part in _split_bf16(inv, 3))
    return xf * inv_full * wvec


def _pos_lanes(pos):
    lane = lax.broadcasted_iota(jnp.int32, (1, NSA_QK_WIDTH), 1)
    hi = ((pos >> 6) << 6).astype(F32)
    lo = (pos & 63).astype(F32)
    d = NSA_HEAD_DIM
    ones = jnp.where((lane >= d + 6) & (lane < d + 9), 1.0, 0.0)
    return jnp.where((lane >= d) & (lane < d + 3), hi, jnp.where((lane >= d + 3) & (lane < d + 6), lo, ones))


def _nsa_proj_kernel(tm, qd, kd, x_ref, nw_ref, sh_ref, sc_ref, w_ref, pq_ref, pqt_ref, pk_ref, pkt_ref,
                     nq_ref, nks_ref, nkw_ref, qf_ref, slq_ref,
                     q_ref, ks_ref, vs_ref, kw_ref, vw_ref, kc_ref, vc_ref, gt_ref):
    hg, dk = NSA_HEADS_PER_GROUP, NSA_HEAD_DIM
    h = _normmod(x_ref[0], nw_ref[...], sh_ref[0], sc_ref[0])
    proj = _dot(h, w_ref[...])
    lane = lax.broadcasted_iota(jnp.int32, (1, NSA_QK_WIDTH), 1)
    low = lane < dk
    t_int = pl.program_id(1) * tm + lax.broadcasted_iota(jnp.int32, (tm, 1), 0)
    t_row = t_int.astype(F32)
    key_lanes = _pos_lanes(t_int)

    def heads(x, nheads):
        for pair in range(nheads // 2):
            xp = x[:, pair * 2 * dk:(pair + 1) * 2 * dk]
            yield 2 * pair, xp
            yield 2 * pair + 1, pltpu.roll(xp, dk, 1)

    qn = _head_rms(proj[:, :qd], pq_ref, pqt_ref, nq_ref[...]) * (NSA_SCALE * LOG2_E)
    for hd, xs in heads(qn, NSA_HEADS):
        feat = qf_ref[hd:hd + 1, :]
        for i, part in enumerate(_split_bf16(-slq_ref[hd:hd + 1, :] * t_row, 3)):
            feat = jnp.where(lane == dk + 6 + i, part.astype(F32), feat)
        q_ref[0, hd // hg, hd % hg] = jnp.where(low, xs, feat).astype(BF16)
    o = qd
    for g, xs in heads(proj[:, o:o + kd], NSA_KV_GROUPS):
        kc_ref[0, g] = xs.astype(kc_ref.dtype)
    for g, xs in heads(proj[:, o + kd:o + 2 * kd], NSA_KV_GROUPS):
        vc_ref[0, g] = xs.astype(vc_ref.dtype)
    for g, xs in heads(_head_rms(proj[:, o + 2 * kd:o + 3 * kd], pk_ref, pkt_ref, nks_ref[...]), NSA_KV_GROUPS):
        ks_ref[0, g] = jnp.where(low, xs, key_lanes).astype(BF16)
    for g, xs in heads(proj[:, o + 3 * kd:o + 4 * kd], NSA_KV_GROUPS):
        vs_ref[0, g] = jnp.where(low, xs, 1.0).astype(BF16)
    for g, xs in heads(_head_rms(proj[:, o + 4 * kd:o + 5 * kd], pk_ref, pkt_ref, nkw_ref[...]), NSA_KV_GROUPS):
        kw_ref[0, g] = jnp.where(low, xs, key_lanes).astype(BF16)
    for g, xs in heads(proj[:, o + 5 * kd:o + 6 * kd], NSA_KV_GROUPS):
        vw_ref[0, g] = jnp.where(low, xs, 1.0).astype(BF16)
    o += 6 * kd
    for g in range(NSA_KV_GROUPS):
        gt_ref[0, g] = _sigmoid(proj[:, o + g * NSA_PROJ_PAD:o + (g + 1) * NSA_PROJ_PAD])


def _gelu_tanh(x):
    return 0.5 * x * (1.0 + jnp.tanh(math.sqrt(2.0 / math.pi) * (x + 0.044715 * (x * x * x))))


def _nsa_compress_kernel(half, kc_ref, vc_ref, pos_ref, w1_ref, w2_ref, nk_ref, kco_ref, vco_ref):
    nsub = kc_ref.shape[2]

    def comp(x, i):
        ya = _dot(x + pos_ref[2 * i:2 * i + 1, :], w1_ref[i, 0:half, :])
        yb = _dot(x + pos_ref[2 * i + 1:2 * i + 2, :], w1_ref[i, half:2 * half, :])
        hid = ya + pltpu.roll(yb, nsub - 1, 0)
        return _dot(_gelu_tanh(hid), w2_ref[i])

    kcm = comp(kc_ref[0, 0], 0)
    ms = jnp.sum(kcm * kcm, axis=-1, keepdims=True) * (1.0 / NSA_HEAD_DIM)
    block_end = lax.broadcasted_iota(jnp.int32, (nsub, 1), 0) * CMP_STRIDE + CMP_LEN - 1
    kco_ref[0, 0] = (kcm * lax.rsqrt(ms + RMS_EPS) * nk_ref[...] + _pos_lanes(block_end)).astype(BF16)
    low = lax.broadcasted_iota(jnp.int32, (1, NSA_QK_WIDTH), 1) < NSA_HEAD_DIM
    vco_ref[0, 0] = jnp.where(low, comp(vc_ref[0, 0], 1), 1.0).astype(BF16)


def _pack_heads(per_head):
    low = lax.broadcasted_iota(jnp.int32, (1, NSA_QK_WIDTH), 1) < NSA_HEAD_DIM
    pairs = [jnp.where(low, per_head[i], pltpu.roll(per_head[i + 1], NSA_HEAD_DIM, 1))
             for i in range(0, len(per_head), 2)]
    return jnp.concatenate(pairs, axis=-1)


def _nsa_cmp_kernel(qt, nc, nsel, nq, q_ref, kc_ref, vc_ref, g_ref, oc_ref, sel_ref, imp_scr):
    hg = NSA_HEADS_PER_GROUP
    step = pl.program_id(2)

    @pl.when(step == 0)
    def _():
        imp_scr[...] = jnp.zeros_like(imp_scr)

    imp_prev = imp_scr[...]
    q0_prev = jnp.maximum(step - 1, 0) * qt
    q0 = jnp.minimum(step, nq - 1) * qt
    qs = NSA_CMP_SUB_TILE
    jo = lax.broadcasted_iota(jnp.int32, (nsel, nc), 0)
    no = lax.broadcasted_iota(jnp.int32, (nsel, nc), 1)
    ratio = SEL_LEN // CMP_STRIDE
    first = jo * ratio - (CMP_LEN // CMP_STRIDE - 1)
    overlap_t = jnp.where((no >= first) & (no < (jo + 1) * ratio) & (no < nc - 1), 1.0, 0.0).astype(BF16)
    overlap_ext = jnp.concatenate([overlap_t, jnp.ones((8, nc), BF16)], axis=0)
    n = lax.broadcasted_iota(jnp.int32, (qs, nc), 1)
    dk = NSA_HEAD_DIM

    j = lax.broadcasted_iota(jnp.int32, (nsel, qt), 0)
    tt = q0_prev + lax.broadcasted_iota(jnp.int32, (nsel, qt), 1)
    valid = j * SEL_LEN <= tt
    qb = tt >> int(math.log2(SEL_LEN))
    forced = (j == 0) | (j == qb) | (j == qb - 1)
    score = jnp.where(valid, jnp.where(forced, SEL_BIG, imp_prev), -SEL_BIG)
    jf = j.astype(F32)
    picked = -(2.0 ** 120)
    rounds = min(SEL_TOPK, nsel)
    n_sub = qt // qs

    def topk_rounds(score, count):
        for _ in range(count):
            mx = jnp.max(score, axis=0, keepdims=True)
            jmin = jnp.min(jnp.where(score == mx, jf, float(nsel)), axis=0, keepdims=True)
            score = jnp.where(jf == jmin, picked, score)
        return score

    imp_parts = []
    for sub in range(n_sub):
        rows = slice(sub * qs, (sub + 1) * qs)
        q = q_ref[0, 0, :, rows, :].reshape(hg * qs, NSA_QK_WIDTH)
        s3 = _dot_nt(q, kc_ref[0, 0], BF16).reshape(hg, qs, nc)
        t0 = q0 + sub * qs
        mask = (t0 + lax.broadcasted_iota(jnp.int32, (qs, nc), 0) >= n * CMP_STRIDE + CMP_LEN - 1)[None]
        sm = jnp.where(mask, s3, MASKED_SCORE)
        e = jnp.exp2(sm - jnp.max(sm, axis=-1, keepdims=True))
        acc = _dot(e.reshape(hg * qs, nc), vc_ref[0, 0])
        has_key = t0 + lax.broadcasted_iota(jnp.int32, (qs, 1), 0) >= CMP_LEN - 1
        g = g_ref[0, 0, rows, :]
        per_head = []
        for hd in range(hg):
            a_h = acc[hd * qs:(hd + 1) * qs]
            scale = jnp.where(has_key, g[:, hd:hd + 1] / jnp.maximum(a_h[:, dk:dk + 1], 1e-30), 0.0)
            per_head.append(a_h * scale)
        oc_ref[0, rows, :] = _pack_heads(per_head)
        has_key_t = t0 + lax.broadcasted_iota(jnp.int32, (1, qs), 1) >= CMP_LEN - 1
        imp = jnp.zeros((nsel, qs), F32)
        for hd in range(hg):
            both = lax.dot_general(overlap_ext, e[hd], (((1,), (1,)), ((), ())), preferred_element_type=F32)
            imp = imp + both[:nsel] * jnp.where(has_key_t, 1.0 / jnp.maximum(both[nsel:nsel + 1], 1e-30), 0.0)
        imp_parts.append(imp)
        done = rounds * sub // n_sub
        score = topk_rounds(score, rounds * (sub + 1) // n_sub - done)
    imp_scr[...] = jnp.concatenate(imp_parts, axis=1)
    sel = jnp.where((score == picked) & valid, 1.0, 0.0)
    sel_ref[0, 0] = sel.T.astype(BF16)


def _nsa_attn_kernel(qt, kt, nsel, nkt, nq, tiles_ref, cnt_ref,
                     q_ref, ks_ref, vs_ref, kw_ref, vw_ref, oh_ref, sel_ref, g_ref, oc_ref, o_ref):
    hg, dk = NSA_HEADS_PER_GROUP, NSA_HEAD_DIM
    rows = hg * qt
    step = (pl.program_id(0) * NSA_KV_GROUPS + pl.program_id(1)) * nq + pl.program_id(2)
    q0 = pl.program_id(2) * qt
    q = q_ref[0, 0].reshape(rows, NSA_QK_WIDTH)
    unsel = (1.0 - sel_ref[0, 0]) * NSA_UNSELECTED
    if nsel < NSA_QK_WIDTH:
        unsel = jnp.concatenate([unsel, jnp.zeros((qt, NSA_QK_WIDTH - nsel), BF16)], axis=1)
    q_sel = jnp.concatenate([q, jnp.concatenate([unsel] * hg, axis=0)], axis=1)

    t_k = q0 + lax.broadcasted_iota(jnp.int32, (qt, kt), 0)
    c_k = lax.broadcasted_iota(jnp.int32, (qt, kt), 1)

    def sel_step(i, carry, diagonal):
        m, acc = carry
        k0 = pl.multiple_of(tiles_ref[step * nkt + i] * kt, kt)
        keys = jnp.concatenate([ks_ref[0, 0, pl.ds(k0, kt), :], oh_ref[pl.ds(k0, kt), :]], axis=1)
        sm = _dot_nt(q_sel, keys, BF16)
        if diagonal:
            causal = (k0 + c_k <= t_k)[None]
            sm = jnp.where(causal, sm.reshape(hg, qt, kt), MASKED_SCORE).reshape(rows, kt)
        m_new = jnp.maximum(m, jnp.max(sm, axis=-1, keepdims=True))
        alpha = jnp.exp2((m - m_new).astype(F32))
        acc = alpha * acc + _dot(jnp.exp2(sm - m_new), vs_ref[0, 0, pl.ds(k0, kt), :])
        return m_new, acc

    init = (jnp.full((rows, 1), MASKED_SCORE, BF16), jnp.zeros((rows, NSA_QK_WIDTH), F32))
    last = cnt_ref[step] - 1
    carry = lax.fori_loop(0, last, functools.partial(sel_step, diagonal=False), init)
    _, acc_s = sel_step(last, carry, diagonal=True)
    o_s = acc_s / jnp.maximum(acc_s[:, dk:dk + 1], 1e-30)

    wk = WINDOW + qt
    ws = pl.multiple_of(jnp.maximum(q0 - WINDOW, 0), qt)
    s3 = _dot_nt(q, kw_ref[0, 0, pl.ds(ws, wk), :], BF16).reshape(hg, qt, wk)
    dist = (q0 + lax.broadcasted_iota(jnp.int32, (qt, wk), 0)) - (ws + lax.broadcasted_iota(jnp.int32, (qt, wk), 1))
    mask = ((dist >= 0) & (dist < WINDOW))[None]
    sm = jnp.where(mask, s3, MASKED_SCORE).reshape(rows, wk)
    e = jnp.exp2(sm - jnp.max(sm, axis=-1, keepdims=True))
    acc_w = _dot(e, vw_ref[0, 0, pl.ds(ws, wk), :])
    o_w = acc_w / jnp.maximum(acc_w[:, dk:dk + 1], 1e-30)

    g = g_ref[0, 0]
    per_head = [g[:, hg + hd:hg + hd + 1] * o_s[hd * qt:(hd + 1) * qt]
                + g[:, 2 * hg + hd:2 * hg + hd + 1] * o_w[hd * qt:(hd + 1) * qt] for hd in range(hg)]
    o_ref[0] = (oc_ref[0] + _pack_heads(per_head)).astype(o_ref.dtype)


def _nsa_layer(x, nw, shift, scale, gate, w_in, cmp_pos, cmp_w1, cmp_w2, qk_norm, w_out):
    b, s, d = x.shape
    t = b * s
    g_, hg, dk = NSA_KV_GROUPS, NSA_HEADS_PER_GROUP, NSA_HEAD_DIM
    qd, kd = NSA_HEADS * dk, g_ * dk
    tm = ROW_TILE
    wl = NSA_QK_WIDTH
    n_main = qd + 6 * kd
    assert w_in.shape[1] == n_main + 3 * NSA_HEADS and s % tm == 0
    gate_cols = w_in[:, n_main:].reshape(d, 3, g_, hg).transpose(0, 2, 1, 3).reshape(d, g_, 3 * hg)
    gate_cols = jnp.pad(gate_cols, ((0, 0), (0, 0), (0, NSA_PROJ_PAD - 3 * hg))).reshape(d, g_ * NSA_PROJ_PAD)
    w_pad = jnp.concatenate([w_in[:, :n_main], gate_cols], axis=1).astype(BF16)

    def head_onehot(width):
        lane_head = jnp.arange(width)[:, None] // dk
        return (lane_head == jnp.arange(V7X_LANES)[None, :]).astype(BF16)

    pq, pk = head_onehot(qd), head_onehot(kd)

    def tile_w(wv, reps):
        return jnp.tile(wv, reps).reshape(1, reps * dk)

    slopes = jnp.exp2(-8.0 * (jnp.arange(NSA_HEADS, dtype=F32) + 1) / NSA_HEADS)
    slope_parts = jnp.stack([p.astype(F32) for p in _split_bf16(slopes * LOG2_E, 3)], axis=-1)
    slope_rows = jnp.broadcast_to((slopes * LOG2_E)[:, None], (NSA_HEADS, wl))
    q_lanes = jnp.zeros((NSA_HEADS, wl), F32).at[:, dk:dk + 6].set(jnp.concatenate([slope_parts, slope_parts], axis=-1))

    consts = [pq, pq.T, pk, pk.T, tile_w(qk_norm[0], NSA_HEADS), tile_w(qk_norm[2], g_), tile_w(qk_norm[3], g_), q_lanes, slope_rows]
    full2 = lambda arr: pl.BlockSpec(arr.shape, lambda i, j: (0, 0))
    mod_spec = pl.BlockSpec((1, 1, d), lambda i, j: (i, 0, 0))
    grp = lambda dt: jax.ShapeDtypeStruct((b, g_, s, wl), dt)
    grp_spec = pl.BlockSpec((1, g_, tm, wl), lambda i, j: (i, 0, j, 0))
    q5, ks4, vs4, kw4, vw4, kc, vc, gates = pl.pallas_call(
        functools.partial(_nsa_proj_kernel, tm, qd, kd),
        out_shape=[jax.ShapeDtypeStruct((b, g_, hg, s, wl), BF16), grp(BF16), grp(BF16), grp(BF16), grp(BF16),
                   grp(BF16), grp(BF16), grp(F32)],
        grid=(b, s // tm),
        in_specs=[pl.BlockSpec((1, tm, d), lambda i, j: (i, j, 0)), pl.BlockSpec((1, d), lambda i, j: (0, 0)),
                  mod_spec, mod_spec, full2(w_pad)] + [full2(cst) for cst in consts],
        out_specs=[pl.BlockSpec((1, g_, hg, tm, wl), lambda i, j: (i, 0, 0, j, 0))] + [grp_spec] * 7,
        compiler_params=_params("parallel", "parallel"),
        name="nsa_proj",
    )(x, nw.reshape(1, d), shift[:, None], scale[:, None], w_pad, *consts)

    nsub = s // CMP_STRIDE
    half = CMP_STRIDE * wl
    kc3, vc3 = kc.reshape(b, g_, nsub, half), vc.reshape(b, g_, nsub, half)
    pad_tok = lambda arr: jnp.pad(arr, [(0, 0)] * (arr.ndim - 1) + [(0, wl - dk)])
    pos = pad_tok(cmp_pos).reshape(4, half)
    w1b = jnp.pad(cmp_w1.reshape(2, CMP_LEN, dk, -1), ((0, 0), (0, 0), (0, wl - dk), (0, 0))).reshape(2, 2 * half, -1).astype(BF16)
    w2b = jnp.pad(cmp_w2, ((0, 0), (0, 0), (0, wl - dk))).astype(BF16)
    nk_lanes = jnp.pad(qk_norm[1], (0, wl - dk)).reshape(1, wl)
    cmp_in = pl.BlockSpec((1, 1, nsub, half), lambda i, j: (i, j, 0, 0))
    cmp_out = pl.BlockSpec((1, 1, nsub, wl), lambda i, j: (i, j, 0, 0))
    kcc, vcc = pl.pallas_call(
        functools.partial(_nsa_compress_kernel, half),
        out_shape=[jax.ShapeDtypeStruct((b, g_, nsub, wl), BF16)] * 2,
        grid=(b, g_),
        in_specs=[cmp_in, cmp_in, pl.BlockSpec(pos.shape, lambda i, j: (0, 0)),
                  pl.BlockSpec(w1b.shape, lambda i, j: (0, 0, 0)), pl.BlockSpec(w2b.shape, lambda i, j: (0, 0, 0)),
                  pl.BlockSpec((1, wl), lambda i, j: (0, 0))],
        out_specs=[cmp_out, cmp_out],
        compiler_params=_params("parallel", "parallel"),
        name="nsa_compress",
    )(kc3, vc3, pos, w1b, w2b, nk_lanes)

    nsel = s // SEL_LEN
    kt = min(NSA_KV_TILE, s)
    nkt = s // kt
    seq_kv = lambda n: pl.BlockSpec((1, 1, n, wl), lambda i, j, k, *_: (i, j, 0, 0))

    def tile_specs(qt):
        assert s % qt == 0 and s >= WINDOW + qt and kt % qt == 0
        return (pl.BlockSpec((1, 1, hg, qt, wl), lambda i, j, k, *_: (i, j, 0, k, 0)),
                pl.BlockSpec((1, qt, hg * dk), lambda i, j, k, *_: (i, k, j)),
                pl.BlockSpec((1, 1, qt, wl), lambda i, j, k, *_: (i, j, k, 0)),
                pl.BlockSpec((1, 1, qt, nsel), lambda i, j, k, *_: (i, j, k, 0)))

    qt = NSA_Q_TILE
    nq = s // qt
    assert s % qt == 0
    cur = lambda k: jnp.minimum(k, nq - 1)
    prev = lambda k: jnp.maximum(k - 1, 0)
    oc, sel = pl.pallas_call(
        functools.partial(_nsa_cmp_kernel, qt, nsub, nsel, nq),
        out_shape=[jax.ShapeDtypeStruct((b, s, qd), F32), jax.ShapeDtypeStruct((b, g_, s, nsel), BF16)],
        grid=(b, g_, nq + 1),
        in_specs=[pl.BlockSpec((1, 1, hg, qt, wl), lambda i, j, k: (i, j, 0, cur(k), 0)),
                  seq_kv(nsub), seq_kv(nsub),
                  pl.BlockSpec((1, 1, qt, wl), lambda i, j, k: (i, j, cur(k), 0))],
        out_specs=[pl.BlockSpec((1, qt, hg * dk), lambda i, j, k: (i, cur(k), j)),
                   pl.BlockSpec((1, 1, qt, nsel), lambda i, j, k: (i, j, prev(k), 0))],
        scratch_shapes=[pltpu.VMEM((nsel, qt), F32)],
        compiler_params=_params("parallel", "parallel", "arbitrary"),
        name="nsa_cmp_select",
    )(q5, kcc, vcc, gates)

    assert nsel <= wl
    block_onehot = (jnp.arange(s)[:, None] // SEL_LEN == jnp.arange(wl)[None, :]).astype(BF16)

    qt = NSA_ATTN_Q_TILE
    nq = s // qt
    q_spec, o_spec, gate_spec, sel_spec = tile_specs(qt)
    active = sel.reshape(b, g_, nq, qt, nkt, kt // SEL_LEN).max(axis=(3, 5)) > 0
    order = jnp.sort(jnp.where(active, 0, nkt) + jnp.arange(nkt, dtype=jnp.int32), axis=-1)
    tiles = (order % nkt).astype(jnp.int32).reshape(-1)
    counts = active.sum(axis=-1).astype(jnp.int32).reshape(-1)

    o3 = pl.pallas_call(
        functools.partial(_nsa_attn_kernel, qt, kt, nsel, nkt, nq),
        out_shape=jax.ShapeDtypeStruct((b, s, qd), BF16),
        grid_spec=pltpu.PrefetchScalarGridSpec(
            num_scalar_prefetch=2,
            grid=(b, g_, nq),
            in_specs=[q_spec, seq_kv(s), seq_kv(s), seq_kv(s), seq_kv(s),
                      pl.BlockSpec((s, wl), lambda i, j, k, *_: (0, 0)), sel_spec, gate_spec,
                      o_spec],
            out_specs=o_spec,
        ),
        compiler_params=_params("parallel", "parallel", "arbitrary"),
        name="nsa_attention",
    )(tiles, counts, q5, ks4, vs4, kw4, vw4, block_onehot, sel, gates, oc)
    return _outproj(x.reshape(t, d), o3.reshape(t, qd), gate, w_out, s).reshape(b, s, d)


def kernel(x, c, ada_w, ada_b, norm_mix_w, norm_ffn_w, ffn_w_gate, ffn_w_up, ffn_w_down, nsa_w_in, nsa_cmp_pos, nsa_cmp_w1, nsa_cmp_w2, nsa_qk_norm, nsa_w_out, rwkv_mu, rwkv_w_rkv, rwkv_w0, rwkv_wd_a, rwkv_wd_b, rwkv_a0, rwkv_wa_a, rwkv_wa_b, rwkv_wg_a, rwkv_wg_b, rwkv_k_k, rwkv_k_a, rwkv_r_k, rwkv_ln_w, rwkv_ln_b, rwkv_w_out, pool_w, pool_b, pool_scale):
    b, s, d = x.shape
    depth = ada_w.shape[0]
    mod = _ada_mod(c, ada_w, ada_b)
    for i in range(depth):
        sh_m, sc_m, g_m, sh_f, sc_f, g_f = jnp.split(mod[i], N_MOD, axis=-1)
        kind, j = i % N_MIXERS, i // N_MIXERS
        if kind == 0:
            x = _nsa_layer(x, norm_mix_w[i], sh_m, sc_m, g_m, nsa_w_in[j], nsa_cmp_pos[j], nsa_cmp_w1[j], nsa_cmp_w2[j],
                           nsa_qk_norm[j], nsa_w_out[j])
        elif kind == 1:
            x = _rwkv_layer(x, norm_mix_w[i], sh_m, sc_m, g_m, rwkv_mu[j], rwkv_w_rkv[j], rwkv_w0[j], rwkv_wd_a[j],
                            rwkv_wd_b[j], rwkv_a0[j], rwkv_wa_a[j], rwkv_wa_b[j], rwkv_wg_a[j], rwkv_wg_b[j],
                            rwkv_k_k[j], rwkv_k_a[j], rwkv_r_k[j], rwkv_ln_w[j], rwkv_ln_b[j], rwkv_w_out[j])
        elif kind == 2:
            x = _pool_layer(x, norm_mix_w[i], sh_m, sc_m, g_m, pool_w[j], pool_b[j], pool_scale[j])
        x = _ffn(x.reshape(b * s, d), norm_ffn_w[i], sh_f, sc_f, g_f, ffn_w_gate[i], ffn_w_up[i], ffn_w_down[i], s).reshape(b, s, d)
    return x
```

```python
import functools
import math

import jax
import jax.numpy as jnp
from jax import lax
from jax.experimental import pallas as pl
from jax.experimental.pallas import tpu as pltpu

F32 = jnp.float32
BF16 = jnp.bfloat16

V7X_LANES = 128
V7X_VMEM_LIMIT_BYTES = 56 * 1024 * 1024

RMS_EPS = 1e-6
N_MOD = 6
N_MIXERS = 3

NSA_HEAD_DIM = 64
NSA_KV_GROUPS = 4
NSA_HEADS_PER_GROUP = 4
NSA_HEADS = NSA_KV_GROUPS * NSA_HEADS_PER_GROUP
CMP_LEN = 32
CMP_STRIDE = 16
SEL_LEN = 64
SEL_TOPK = 16
WINDOW = 512
NSA_SCALE = NSA_HEAD_DIM ** -0.5
LOG2_E = math.log2(math.e)
SEL_BIG = 1e9
MASKED_SCORE = -1e30
NSA_UNSELECTED = -(2.0 ** 100)

RWKV_HEAD_DIM = 64
RWKV_GN_EPS = 64e-5
RWKV_CHUNK = 64

POOL_WINDOWS = (2, 4, 8, 16)
POOL_HALO = 16

ROW_TILE = 512
FFN_ROW_TILE = 1024
FFN_COL_CHUNK = 256


def _params(*sem):
    return pltpu.CompilerParams(dimension_semantics=sem, vmem_limit_bytes=V7X_VMEM_LIMIT_BYTES)


def _sigmoid(z):
    return 1.0 / (1.0 + jnp.exp(-z))


def _normmod(x, nw, shift, scale):
    ms = jnp.mean(x * x, axis=-1, keepdims=True)
    return (x * lax.rsqrt(ms + RMS_EPS) * nw) * (1.0 + scale) + shift


def _dot(a, b):
    return jnp.dot(a.astype(BF16), b.astype(BF16), preferred_element_type=F32)


def _dot_nt(a, b, out_dtype=F32):
    out = lax.dot_general(a.astype(BF16), b.astype(BF16), (((1,), (1,)), ((), ())), preferred_element_type=F32)
    return out.astype(out_dtype)


def _split_bf16(x, parts):
    out = []
    for _ in range(parts):
        p = x.astype(BF16)
        out.append(p)
        x = x - p.astype(F32)
    return out


def _mod_kernel(c_ref, w_ref, b_ref, o_ref):
    c = c_ref[...]
    ca = c * _sigmoid(c)
    o_ref[0] = _dot(ca, w_ref[0]) + b_ref[0]


def _ada_mod(c, ada_w, ada_b):
    depth, d, n = ada_w.shape
    b = c.shape[0]
    rows = 8
    tn = 1536
    assert b <= rows and n % tn == 0
    c_pad = jnp.zeros((rows, d), F32).at[:b].set(c)
    out = pl.pallas_call(
        _mod_kernel,
        out_shape=jax.ShapeDtypeStruct((depth, rows, n), F32),
        grid=(depth, n // tn),
        in_specs=[
            pl.BlockSpec((rows, d), lambda i, j: (0, 0)),
            pl.BlockSpec((1, d, tn), lambda i, j: (i, 0, j)),
            pl.BlockSpec((1, 1, tn), lambda i, j: (i, 0, j)),
        ],
        out_specs=pl.BlockSpec((1, rows, tn), lambda i, j: (i, 0, j)),
        compiler_params=_params("parallel", "parallel"),
        name="ada_mod",
    )(c_pad, ada_w, ada_b.reshape(depth, 1, n))
    return out[:, :b]


def _ffn_kernel(x_ref, nw_ref, sh_ref, sc_ref, g_ref, wg_ref, wu_ref, wd_ref, o_ref):
    x = x_ref[...]
    h = _normmod(x, nw_ref[...], sh_ref[0], sc_ref[0]).astype(BF16)
    f = wg_ref.shape[1]
    acts = []
    for c0 in range(0, f, FFN_COL_CHUNK):
        cols = slice(c0, c0 + FFN_COL_CHUNK)
        g = jnp.dot(h, wg_ref[:, cols], preferred_element_type=F32)
        u = jnp.dot(h, wu_ref[:, cols], preferred_element_type=F32)
        acts.append((g * _sigmoid(g) * u).astype(BF16))
    o_ref[...] = x + g_ref[0] * jnp.dot(jnp.concatenate(acts, axis=1), wd_ref[...], preferred_element_type=F32)


def _ffn(x2, nw, shift, scale, gate, wg, wu, wd, seq):
    t, d = x2.shape
    f = wg.shape[1]
    tm = FFN_ROW_TILE
    assert t % tm == 0 and seq % tm == 0 and f % FFN_COL_CHUNK == 0
    tpb = seq // tm
    mod_spec = pl.BlockSpec((1, 1, d), lambda i: (i // tpb, 0, 0))
    resident = lambda shape: pl.BlockSpec(shape, lambda i: (0, 0), pipeline_mode=pl.Buffered(1))
    return pl.pallas_call(
        _ffn_kernel,
        out_shape=jax.ShapeDtypeStruct((t, d), F32),
        grid=(t // tm,),
        in_specs=[
            pl.BlockSpec((tm, d), lambda i: (i, 0)),
            pl.BlockSpec((1, d), lambda i: (0, 0)),
            mod_spec, mod_spec, mod_spec,
            resident((d, f)), resident((d, f)), resident((f, d)),
        ],
        out_specs=pl.BlockSpec((tm, d), lambda i: (i, 0)),
        compiler_params=_params("parallel"),
        name="ffn",
    )(x2, nw.reshape(1, d), shift[:, None], scale[:, None], gate[:, None],
      wg.astype(BF16), wu.astype(BF16), wd.astype(BF16))


def _outproj_kernel(x_ref, z_ref, g_ref, w_ref, o_ref):
    o_ref[...] = x_ref[...] + g_ref[0] * _dot(z_ref[...], w_ref[...])


def _outproj(x2, z2, gate, w, seq):
    t, d = x2.shape
    k = z2.shape[1]
    tm = ROW_TILE
    tpb = seq // tm
    return pl.pallas_call(
        _outproj_kernel,
        out_shape=jax.ShapeDtypeStruct((t, d), F32),
        grid=(t // tm,),
        in_specs=[
            pl.BlockSpec((tm, d), lambda i: (i, 0)),
            pl.BlockSpec((tm, k), lambda i: (i, 0)),
            pl.BlockSpec((1, 1, d), lambda i: (i // tpb, 0, 0)),
            pl.BlockSpec((k, d), lambda i: (0, 0)),
        ],
        out_specs=pl.BlockSpec((tm, d), lambda i: (i, 0)),
        compiler_params=_params("parallel"),
        name="outproj",
    )(x2, z2, gate[:, None], w.astype(BF16))


def _pool_kernel(tm, gd, x_ref, nw_ref, sh_ref, sc_ref, g_ref, pw_ref, pb_ref, ps_ref, o_ref, ext_scr):
    s = pl.program_id(1)
    x = x_ref[0]
    h = _normmod(x, nw_ref[...], sh_ref[0], sc_ref[0])

    @pl.when(s == 0)
    def _():
        ext_scr[0:POOL_HALO, :] = jnp.zeros((POOL_HALO, x.shape[1]), F32)

    ext_scr[POOL_HALO:POOL_HALO + tm, :] = h
    row = s * tm + lax.broadcasted_iota(jnp.int32, (tm, 1), 0)
    ys = []
    for gi, win in enumerate(POOL_WINDOWS):
        lanes = slice(gi * gd, (gi + 1) * gd)
        hg = h[:, lanes]
        acc = hg
        for k in range(1, win):
            acc = acc + ext_scr[POOL_HALO - k:POOL_HALO - k + tm, lanes]
        cnt = jnp.minimum(row + 1, win).astype(F32)
        ys.append(_dot(acc / cnt - hg, pw_ref[gi]))
    y = (jnp.concatenate(ys, axis=-1) + pb_ref[...]) * ps_ref[...]
    o_ref[0] = x + g_ref[0] * y
    ext_scr[0:POOL_HALO, :] = ext_scr[tm:tm + POOL_HALO, :]


def _pool_layer(x, nw, shift, scale, gate, pw, pb, ps):
    b, s, d = x.shape
    tm = ROW_TILE
    gd = d // len(POOL_WINDOWS)
    assert s % tm == 0 and max(POOL_WINDOWS) <= POOL_HALO
    mod_spec = pl.BlockSpec((1, 1, d), lambda i, j: (i, 0, 0))
    vec_spec = pl.BlockSpec((1, d), lambda i, j: (0, 0))
    return pl.pallas_call(
        functools.partial(_pool_kernel, tm, gd),
        out_shape=jax.ShapeDtypeStruct((b, s, d), F32),
        grid=(b, s // tm),
        in_specs=[
            pl.BlockSpec((1, tm, d), lambda i, j: (i, j, 0)),
            vec_spec, mod_spec, mod_spec, mod_spec,
            pl.BlockSpec(pw.shape, lambda i, j: (0, 0, 0)),
            vec_spec, vec_spec,
        ],
        out_specs=pl.BlockSpec((1, tm, d), lambda i, j: (i, j, 0)),
        scratch_shapes=[pltpu.VMEM((tm + POOL_HALO, d), F32)],
        compiler_params=_params("arbitrary", "arbitrary"),
        name="pool_mixer",
    )(x, nw.reshape(1, d), shift[:, None], scale[:, None], gate[:, None],
      pw.astype(BF16), pb.reshape(1, d), ps.reshape(1, d))


def _rwkv_proj_kernel(tm, x_ref, nw_ref, sh_ref, sc_ref, mu_ref, wr_ref, wk_ref, wv_ref,
                      w0_ref, wda_ref, wdb_ref, a0_ref, waa_ref, wab_ref, wga_ref, wgb_ref,
                      r_ref, k_ref, v_ref, lw_ref, a_ref, g_ref, ext_scr):
    s = pl.program_id(1)
    h = _normmod(x_ref[0], nw_ref[...], sh_ref[0], sc_ref[0])

    @pl.when(s == 0)
    def _():
        ext_scr[0:8, :] = jnp.zeros((8, h.shape[1]), F32)

    ext_scr[8:8 + tm, :] = h
    xx = ext_scr[7:7 + tm, :] - h

    def mix(i):
        return h + xx * mu_ref[i:i + 1, :]

    r_ref[0] = _dot(mix(0), wr_ref[...]).astype(r_ref.dtype)
    k_ref[0] = _dot(mix(2), wk_ref[...])
    v_ref[0] = _dot(mix(3), wv_ref[...]).astype(v_ref.dtype)
    dw = w0_ref[...] + _dot(jnp.tanh(_dot(mix(1), wda_ref[...])), wdb_ref[...])
    softplus_neg = jnp.maximum(-dw, 0.0) + jnp.log(1.0 + jnp.exp(-jnp.abs(dw)))
    lw_ref[0] = -jnp.exp(-softplus_neg - 0.5)
    a_ref[0] = _sigmoid(a0_ref[...] + _dot(_dot(mix(4), waa_ref[...]), wab_ref[...]))
    g_ref[0] = _dot(_sigmoid(_dot(mix(5), wga_ref[...])), wgb_ref[...]).astype(g_ref.dtype)
    ext_scr[0:8, :] = ext_scr[tm:tm + 8, :]


def _rwkv_scan_kernel(ct, npair, r_ref, k_ref, v_ref, lw_ref, a_ref, g_ref, kk_ref, ka_ref, rk_ref, lnw_ref, lnb_ref,
                      z_ref, s_scr):
    c, n = RWKV_CHUNK, RWKV_HEAD_DIM
    w = 2 * n
    shift = int(math.log2(n))

    @pl.when(pl.program_id(2) == 0)
    def _():
        s_scr[...] = jnp.zeros_like(s_scr)

    head0 = lax.broadcasted_iota(jnp.int32, (1, w), 1) < n
    ri = lax.broadcasted_iota(jnp.int32, (w, w), 0)
    ci = lax.broadcasted_iota(jnp.int32, (w, w), 1)
    same_head = jnp.where((ri >> shift) == (ci >> shift), 1.0, 0.0).astype(BF16)
    strict, incl, eye = ri > ci, ri >= ci, ri == ci
    rt = lax.broadcasted_iota(jnp.int32, (ct, ct), 0)
    cc = lax.broadcasted_iota(jnp.int32, (ct, ct), 1)
    cshift = int(math.log2(c))
    chunk_tri = jnp.where(((rt >> cshift) == (cc >> cshift)) & (rt >= cc), 1.0, 0.0).astype(BF16)
    nb = ct // c

    def head_sum(xf):
        return sum(jnp.dot(p, same_head, preferred_element_type=F32) for p in _split_bf16(xf, 2))

    def stack(xf):
        x3 = xf.reshape(nb, c, w)
        return jnp.concatenate([jnp.where(head0, x3, 0.0), jnp.where(head0, 0.0, x3)], axis=1)

    def bmm(x, y):
        return jnp.einsum('bij,bjk->bik', x.astype(BF16), y.astype(BF16), preferred_element_type=F32)

    def bmm_nt(x, y):
        return jnp.einsum('bik,bjk->bij', x.astype(BF16), y.astype(BF16), preferred_element_type=F32)

    pairs = []
    stacked = []
    for hp in range(npair):
        ln = slice(hp * w, (hp + 1) * w)
        r, v = r_ref[0, :, ln].astype(F32), v_ref[0, :, ln].astype(F32)
        k, lw, a = k_ref[0, :, ln], lw_ref[0, :, ln], a_ref[0, :, ln]
        kk = k * kk_ref[:, ln]
        kkn = kk * lax.rsqrt(jnp.maximum(head_sum(kk * kk), 1e-24))
        k2 = k * (1.0 + (a - 1.0) * ka_ref[:, ln])
        cum = sum(jnp.dot(chunk_tri, p, preferred_element_type=F32) for p in _split_bf16(lw, 3))
        einv = jnp.exp(-cum)
        pairs.append((r, k2, v))
        stacked.append((stack(-kkn * jnp.exp(cum - lw)), stack(r * jnp.exp(cum)), stack(kkn * a * einv),
                        stack(k2 * einv), stack(v), jnp.exp(cum.reshape(nb, c, w)[:, c - 1:c, :])))
    at, rt_, kb, kq, vs, pc = (jnp.concatenate([st[i] for st in stacked], axis=0) for i in range(6))
    nbt = npair * nb
    kbe, kqe = kb * pc, kq * pc
    tt = bmm_nt(jnp.concatenate([at, rt_], axis=1), jnp.concatenate([kb, kq], axis=1))
    a_m = jnp.where(strict, tt[:, :w, :w], 0.0)
    b_m = jnp.where(strict, tt[:, :w, w:], 0.0)
    ar_m = jnp.where(incl, tt[:, w:, :w], 0.0)
    br_m = jnp.where(incl, tt[:, w:, w:], 0.0)
    rsum = jnp.where(eye, 1.0, 0.0) + a_m
    pw = bmm(a_m, a_m)
    for _ in range(cshift - 2):
        both = bmm(pw, jnp.concatenate([rsum, pw], axis=2))
        rsum = rsum + both[:, :, :w]
        pw = both[:, :, w:]
    tinv = rsum + bmm(pw, rsum)
    w12 = bmm(tinv, jnp.concatenate([at, bmm(b_m, vs)], axis=2))
    xmat = jnp.concatenate([w12, jnp.concatenate([jnp.zeros_like(vs), vs], axis=2)], axis=1)
    kbe_t = jnp.stack([kbe[i].T for i in range(nbt)])
    kqe_t = jnp.stack([kqe[i].T for i in range(nbt)])
    lhs = jnp.concatenate([jnp.concatenate([ar_m, br_m], axis=2), jnp.concatenate([kbe_t, kqe_t], axis=2)], axis=1)
    res = bmm(lhs, xmat)
    g_m = rt_ + res[:, :w, :w]
    y0 = res[:, :w, w:]
    mt = jnp.where(eye, jnp.broadcast_to(pc, (nbt, w, w)), 0.0) + res[:, w:, :w]
    nt = res[:, w:, w:]

    states = [s_scr[hp] for hp in range(npair)]
    ys = [[] for _ in range(npair)]
    for idx in range(nb):
        for hp in range(npair):
            bi = hp * nb + idx
            step = _dot(jnp.concatenate([g_m[bi], mt[bi]], axis=0), states[hp])
            yst = step[:w] + y0[bi]
            states[hp] = step[w:] + nt[bi]
            ys[hp].append(yst[:c] + yst[c:])
    for hp in range(npair):
        ln = slice(hp * w, (hp + 1) * w)
        s_scr[hp] = states[hp]
        y = jnp.concatenate(ys[hp], axis=0)
        r, k2, v = pairs[hp]
        mean = head_sum(y) * (1.0 / n)
        dlt = y - mean
        var = head_sum(dlt * dlt) * (1.0 / n)
        yn = dlt * lax.rsqrt(var + RWKV_GN_EPS) * lnw_ref[:, ln] + lnb_ref[:, ln]
        bonus = head_sum(r * k2 * rk_ref[:, ln]) * v
        z_ref[0, :, ln] = ((yn + bonus) * g_ref[0, :, ln].astype(F32)).astype(z_ref.dtype)


def _rwkv_layer(x, nw, shift, scale, gate, mu, w_rkv, w0, wd_a, wd_b, a0, wa_a, wa_b, wg_a, wg_b,
                k_k, k_a, r_k, ln_w, ln_b, w_out):
    b, s, d = x.shape
    tm = ROW_TILE
    assert s % tm == 0
    mod_spec = pl.BlockSpec((1, 1, d), lambda i, j: (i, 0, 0))
    vec_spec = pl.BlockSpec((1, d), lambda i, j: (0, 0))
    tok_spec = pl.BlockSpec((1, tm, d), lambda i, j: (i, j, 0))

    def full(arr):
        return pl.BlockSpec(arr.shape, lambda i, j: (0,) * arr.ndim)

    weights = [w_rkv[0].astype(BF16), w_rkv[1].astype(BF16), w_rkv[2].astype(BF16),
               w0.reshape(1, d), wd_a.astype(BF16), wd_b.astype(BF16),
               a0.reshape(1, d), wa_a.astype(BF16), wa_b.astype(BF16),
               wg_a.astype(BF16), wg_b.astype(BF16)]
    tok_shape = jax.ShapeDtypeStruct((b, s, d), F32)
    tok_bf16 = jax.ShapeDtypeStruct((b, s, d), BF16)
    r, k, v, lw, a, g = pl.pallas_call(
        functools.partial(_rwkv_proj_kernel, tm),
        out_shape=[tok_bf16, tok_shape, tok_bf16, tok_shape, tok_shape, tok_bf16],
        grid=(b, s // tm),
        in_specs=[tok_spec, vec_spec, mod_spec, mod_spec, full(mu)] + [full(wt) for wt in weights],
        out_specs=[tok_spec] * 6,
        scratch_shapes=[pltpu.VMEM((tm + 8, d), F32)],
        compiler_params=_params("arbitrary", "arbitrary"),
        name="rwkv_proj",
    )(x, nw.reshape(1, d), shift[:, None], scale[:, None], mu, *weights)

    ct = 512
    npair = 4
    pair = 2 * RWKV_HEAD_DIM
    lanes = npair * pair
    assert s % ct == 0 and d % lanes == 0 and pair == V7X_LANES
    seq_spec = pl.BlockSpec((1, ct, lanes), lambda i, p, j: (i, j, p))
    par_spec = pl.BlockSpec((1, lanes), lambda i, p, j: (0, p))
    z = pl.pallas_call(
        functools.partial(_rwkv_scan_kernel, ct, npair),
        out_shape=jax.ShapeDtypeStruct((b, s, d), BF16),
        grid=(b, d // lanes, s // ct),
        in_specs=[seq_spec] * 6 + [par_spec] * 5,
        out_specs=seq_spec,
        scratch_shapes=[pltpu.VMEM((npair, pair, pair), F32)],
        compiler_params=_params("parallel", "parallel", "arbitrary"),
        name="rwkv_scan",
    )(r, k, v, lw, a, g, k_k.reshape(1, d), k_a.reshape(1, d), r_k.reshape(1, d), ln_w.reshape(1, d), ln_b.reshape(1, d))
    return _outproj(x.reshape(b * s, d), z.reshape(b * s, d), gate, w_out, s).reshape(b, s, d)


NSA_Q_TILE = 1024
NSA_CMP_SUB_TILE = 128
NSA_ATTN_Q_TILE = 512
NSA_KV_TILE = 512
NSA_PROJ_PAD = 128
NSA_QK_WIDTH = 128


def _head_rms(xf, p_ref, pt_ref, wvec):
    sums = sum(jnp.dot(part, p_ref[...], preferred_element_type=F32) for part in _split_bf16(xf * xf, 2))
    inv = lax.rsqrt(sums * (1.0 / NSA_HEAD_DIM) + RMS_EPS)
    inv_full = sum(jnp.dot(part, pt_ref[...], preferred_element_type=F32) for part in _split_bf16(inv, 3))
    return xf * inv_full * wvec


def _pos_lanes(pos):
    lane = lax.broadcasted_iota(jnp.int32, (1, NSA_QK_WIDTH), 1)
    hi = ((pos >> 6) << 6).astype(F32)
    lo = (pos & 63).astype(F32)
    d = NSA_HEAD_DIM
    ones = jnp.where((lane >= d + 6) & (lane < d + 9), 1.0, 0.0)
    return jnp.where((lane >= d) & (lane < d + 3), hi, jnp.where((lane >= d + 3) & (lane < d + 6), lo, ones))


def _nsa_proj_kernel(tm, qd, kd, x_ref, nw_ref, sh_ref, sc_ref, w_ref, pq_ref, pqt_ref, pk_ref, pkt_ref,
                     nq_ref, nks_ref, nkw_ref, qf_ref, slq_ref,
                     q_ref, ks_ref, vs_ref, kw_ref, vw_ref, kc_ref, vc_ref, gt_ref):
    hg, dk = NSA_HEADS_PER_GROUP, NSA_HEAD_DIM
    h = _normmod(x_ref[0], nw_ref[...], sh_ref[0], sc_ref[0])
    proj = _dot(h, w_ref[...])
    lane = lax.broadcasted_iota(jnp.int32, (1, NSA_QK_WIDTH), 1)
    low = lane < dk
    t_int = pl.program_id(1) * tm + lax.broadcasted_iota(jnp.int32, (tm, 1), 0)
    t_row = t_int.astype(F32)
    key_lanes = _pos_lanes(t_int)

    def heads(x, nheads):
        for pair in range(nheads // 2):
            xp = x[:, pair * 2 * dk:(pair + 1) * 2 * dk]
            yield 2 * pair, xp
            yield 2 * pair + 1, pltpu.roll(xp, dk, 1)

    qn = _head_rms(proj[:, :qd], pq_ref, pqt_ref, nq_ref[...]) * (NSA_SCALE * LOG2_E)
    for hd, xs in heads(qn, NSA_HEADS):
        feat = qf_ref[hd:hd + 1, :]
        for i, part in enumerate(_split_bf16(-slq_ref[hd:hd + 1, :] * t_row, 3)):
            feat = jnp.where(lane == dk + 6 + i, part.astype(F32), feat)
        q_ref[0, hd // hg, hd % hg] = jnp.where(low, xs, feat).astype(BF16)
    o = qd
    for g, xs in heads(proj[:, o:o + kd], NSA_KV_GROUPS):
        kc_ref[0, g] = xs.astype(kc_ref.dtype)
    for g, xs in heads(proj[:, o + kd:o + 2 * kd], NSA_KV_GROUPS):
        vc_ref[0, g] = xs.astype(vc_ref.dtype)
    for g, xs in heads(_head_rms(proj[:, o + 2 * kd:o + 3 * kd], pk_ref, pkt_ref, nks_ref[...]), NSA_KV_GROUPS):
        ks_ref[0, g] = jnp.where(low, xs, key_lanes).astype(BF16)
    for g, xs in heads(proj[:, o + 3 * kd:o + 4 * kd], NSA_KV_GROUPS):
        vs_ref[0, g] = jnp.where(low, xs, 1.0).astype(BF16)
    for g, xs in heads(_head_rms(proj[:, o + 4 * kd:o + 5 * kd], pk_ref, pkt_ref, nkw_ref[...]), NSA_KV_GROUPS):
        kw_ref[0, g] = jnp.where(low, xs, key_lanes).astype(BF16)
    for g, xs in heads(proj[:, o + 5 * kd:o + 6 * kd], NSA_KV_GROUPS):
        vw_ref[0, g] = jnp.where(low, xs, 1.0).astype(BF16)
    o += 6 * kd
    for g in range(NSA_KV_GROUPS):
        gt_ref[0, g] = _sigmoid(proj[:, o + g * NSA_PROJ_PAD:o + (g + 1) * NSA_PROJ_PAD])


def _gelu_tanh(x):
    return 0.5 * x * (1.0 + jnp.tanh(math.sqrt(2.0 / math.pi) * (x + 0.044715 * (x * x * x))))


def _nsa_compress_kernel(half, kc_ref, vc_ref, pos_ref, w1_ref, w2_ref, nk_ref, kco_ref, vco_ref):
    nsub = kc_ref.shape[2]

    def comp(x, i):
        ya = _dot(x + pos_ref[2 * i:2 * i + 1, :], w1_ref[i, 0:half, :])
        yb = _dot(x + pos_ref[2 * i + 1:2 * i + 2, :], w1_ref[i, half:2 * half, :])
        hid = ya + pltpu.roll(yb, nsub - 1, 0)
        return _dot(_gelu_tanh(hid), w2_ref[i])

    kcm = comp(kc_ref[0, 0], 0)
    ms = jnp.sum(kcm * kcm, axis=-1, keepdims=True) * (1.0 / NSA_HEAD_DIM)
    block_end = lax.broadcasted_iota(jnp.int32, (nsub, 1), 0) * CMP_STRIDE + CMP_LEN - 1
    kco_ref[0, 0] = (kcm * lax.rsqrt(ms + RMS_EPS) * nk_ref[...] + _pos_lanes(block_end)).astype(BF16)
    vco_ref[0, 0] = comp(vc_ref[0, 0], 1).astype(BF16)


def _pack_heads(per_head):
    low = lax.broadcasted_iota(jnp.int32, (1, NSA_QK_WIDTH), 1) < NSA_HEAD_DIM
    pairs = [jnp.where(low, per_head[i], pltpu.roll(per_head[i + 1], NSA_HEAD_DIM, 1))
             for i in range(0, len(per_head), 2)]
    return jnp.concatenate(pairs, axis=-1)


def _nsa_cmp_kernel(qt, nc, nsel, q_ref, kc_ref, vc_ref, g_ref, oc_ref, sel_ref):
    hg = NSA_HEADS_PER_GROUP
    q0 = pl.program_id(2) * qt
    qs = NSA_CMP_SUB_TILE
    jo = lax.broadcasted_iota(jnp.int32, (nsel, nc), 0)
    no = lax.broadcasted_iota(jnp.int32, (nsel, nc), 1)
    ratio = SEL_LEN // CMP_STRIDE
    first = jo * ratio - (CMP_LEN // CMP_STRIDE - 1)
    overlap_t = jnp.where((no >= first) & (no < (jo + 1) * ratio) & (no < nc - 1), 1.0, 0.0).astype(BF16)
    n = lax.broadcasted_iota(jnp.int32, (qs, nc), 1)
    imp_parts = []
    for sub in range(qt // qs):
        rows = slice(sub * qs, (sub + 1) * qs)
        q = q_ref[0, 0, :, rows, :].reshape(hg * qs, NSA_QK_WIDTH)
        s3 = _dot_nt(q, kc_ref[0, 0]).reshape(hg, qs, nc)
        t = q0 + sub * qs + lax.broadcasted_iota(jnp.int32, (qs, nc), 0)
        mask = (t >= n * CMP_STRIDE + CMP_LEN - 1)[None]
        sm = jnp.where(mask, s3, MASKED_SCORE)
        e = jnp.exp2(sm - jnp.max(sm, axis=-1, keepdims=True))
        has_key = (q0 + sub * qs + lax.broadcasted_iota(jnp.int32, (1, qs, 1), 1)) >= CMP_LEN - 1
        p = e * jnp.where(has_key, 1.0 / jnp.maximum(jnp.sum(e, axis=-1, keepdims=True), 1e-30), 0.0)
        oc = _dot(p.reshape(hg * qs, nc), vc_ref[0, 0])
        g = g_ref[0, 0, rows, :]
        oc_ref[0, rows, :] = _pack_heads([oc[hd * qs:(hd + 1) * qs] * g[:, hd:hd + 1] for hd in range(hg)])
        psum = p[0] + p[1] + p[2] + p[3]
        imp_parts.append(sum(lax.dot_general(overlap_t, part, (((1,), (1,)), ((), ())), preferred_element_type=F32)
                             for part in _split_bf16(psum, 3)))
    imp_t = jnp.concatenate(imp_parts, axis=1)

    j = lax.broadcasted_iota(jnp.int32, (nsel, qt), 0)
    tt = q0 + lax.broadcasted_iota(jnp.int32, (nsel, qt), 1)
    valid = j * SEL_LEN <= tt
    qb = tt >> int(math.log2(SEL_LEN))
    forced = (j == 0) | (j == qb) | (j == qb - 1)
    score = jnp.where(valid, jnp.where(forced, SEL_BIG, imp_t), -SEL_BIG)
    jf = j.astype(F32)
    picked = -(2.0 ** 120)
    for _ in range(min(SEL_TOPK, nsel)):
        mx = jnp.max(score, axis=0, keepdims=True)
        jmin = jnp.min(jnp.where(score == mx, jf, float(nsel)), axis=0, keepdims=True)
        score = jnp.where(jf == jmin, picked, score)
    sel = jnp.where((score == picked) & valid, 1.0, 0.0)
    sel_ref[0, 0] = sel.T.astype(BF16)


def _nsa_attn_kernel(qt, kt, nsel, nkt, nq, tiles_ref, cnt_ref,
                     q_ref, ks_ref, vs_ref, kw_ref, vw_ref, oh_ref, sel_ref, g_ref, oc_ref, o_ref):
    hg, dk = NSA_HEADS_PER_GROUP, NSA_HEAD_DIM
    rows = hg * qt
    step = (pl.program_id(0) * NSA_KV_GROUPS + pl.program_id(1)) * nq + pl.program_id(2)
    q0 = pl.program_id(2) * qt
    q = q_ref[0, 0].reshape(rows, NSA_QK_WIDTH)
    unsel = (1.0 - sel_ref[0, 0]) * NSA_UNSELECTED
    if nsel < NSA_QK_WIDTH:
        unsel = jnp.concatenate([unsel, jnp.zeros((qt, NSA_QK_WIDTH - nsel), BF16)], axis=1)
    q_sel = jnp.concatenate([q, jnp.concatenate([unsel] * hg, axis=0)], axis=1)

    t_k = q0 + lax.broadcasted_iota(jnp.int32, (qt, kt), 0)
    c_k = lax.broadcasted_iota(jnp.int32, (qt, kt), 1)

    def sel_step(i, carry, diagonal):
        m, acc = carry
        k0 = pl.multiple_of(tiles_ref[step * nkt + i] * kt, kt)
        keys = jnp.concatenate([ks_ref[0, 0, pl.ds(k0, kt), :], oh_ref[pl.ds(k0, kt), :]], axis=1)
        sm = _dot_nt(q_sel, keys, BF16)
        if diagonal:
            causal = (k0 + c_k <= t_k)[None]
            sm = jnp.where(causal, sm.reshape(hg, qt, kt), MASKED_SCORE).reshape(rows, kt)
        m_new = jnp.maximum(m, jnp.max(sm, axis=-1, keepdims=True))
        alpha = jnp.exp2((m - m_new).astype(F32))
        acc = alpha * acc + _dot(jnp.exp2(sm - m_new), vs_ref[0, 0, pl.ds(k0, kt), :])
        return m_new, acc

    init = (jnp.full((rows, 1), MASKED_SCORE, BF16), jnp.zeros((rows, NSA_QK_WIDTH), F32))
    last = cnt_ref[step] - 1
    carry = lax.fori_loop(0, last, functools.partial(sel_step, diagonal=False), init)
    _, acc_s = sel_step(last, carry, diagonal=True)
    o_s = acc_s / jnp.maximum(acc_s[:, dk:dk + 1], 1e-30)

    wk = WINDOW + qt
    ws = pl.multiple_of(jnp.maximum(q0 - WINDOW, 0), qt)
    s3 = _dot_nt(q, kw_ref[0, 0, pl.ds(ws, wk), :], BF16).reshape(hg, qt, wk)
    dist = (q0 + lax.broadcasted_iota(jnp.int32, (qt, wk), 0)) - (ws + lax.broadcasted_iota(jnp.int32, (qt, wk), 1))
    mask = ((dist >= 0) & (dist < WINDOW))[None]
    sm = jnp.where(mask, s3, MASKED_SCORE).reshape(rows, wk)
    e = jnp.exp2(sm - jnp.max(sm, axis=-1, keepdims=True))
    acc_w = _dot(e, vw_ref[0, 0, pl.ds(ws, wk), :])
    o_w = acc_w / jnp.maximum(acc_w[:, dk:dk + 1], 1e-30)

    g = g_ref[0, 0]
    per_head = [g[:, hg + hd:hg + hd + 1] * o_s[hd * qt:(hd + 1) * qt]
                + g[:, 2 * hg + hd:2 * hg + hd + 1] * o_w[hd * qt:(hd + 1) * qt] for hd in range(hg)]
    o_ref[0] = (oc_ref[0] + _pack_heads(per_head)).astype(o_ref.dtype)


def _nsa_layer(x, nw, shift, scale, gate, w_in, cmp_pos, cmp_w1, cmp_w2, qk_norm, w_out):
    b, s, d = x.shape
    t = b * s
    g_, hg, dk = NSA_KV_GROUPS, NSA_HEADS_PER_GROUP, NSA_HEAD_DIM
    qd, kd = NSA_HEADS * dk, g_ * dk
    tm = ROW_TILE
    wl = NSA_QK_WIDTH
    n_main = qd + 6 * kd
    assert w_in.shape[1] == n_main + 3 * NSA_HEADS and s % tm == 0
    gate_cols = w_in[:, n_main:].reshape(d, 3, g_, hg).transpose(0, 2, 1, 3).reshape(d, g_, 3 * hg)
    gate_cols = jnp.pad(gate_cols, ((0, 0), (0, 0), (0, NSA_PROJ_PAD - 3 * hg))).reshape(d, g_ * NSA_PROJ_PAD)
    w_pad = jnp.concatenate([w_in[:, :n_main], gate_cols], axis=1).astype(BF16)

    def head_onehot(width):
        lane_head = jnp.arange(width)[:, None] // dk
        return (lane_head == jnp.arange(V7X_LANES)[None, :]).astype(BF16)

    pq, pk = head_onehot(qd), head_onehot(kd)

    def tile_w(wv, reps):
        return jnp.tile(wv, reps).reshape(1, reps * dk)

    slopes = jnp.exp2(-8.0 * (jnp.arange(NSA_HEADS, dtype=F32) + 1) / NSA_HEADS)
    slope_parts = jnp.stack([p.astype(F32) for p in _split_bf16(slopes * LOG2_E, 3)], axis=-1)
    slope_rows = jnp.broadcast_to((slopes * LOG2_E)[:, None], (NSA_HEADS, wl))
    q_lanes = jnp.zeros((NSA_HEADS, wl), F32).at[:, dk:dk + 6].set(jnp.concatenate([slope_parts, slope_parts], axis=-1))

    consts = [pq, pq.T, pk, pk.T, tile_w(qk_norm[0], NSA_HEADS), tile_w(qk_norm[2], g_), tile_w(qk_norm[3], g_), q_lanes, slope_rows]
    full2 = lambda arr: pl.BlockSpec(arr.shape, lambda i, j: (0, 0))
    mod_spec = pl.BlockSpec((1, 1, d), lambda i, j: (i, 0, 0))
    grp = lambda dt: jax.ShapeDtypeStruct((b, g_, s, wl), dt)
    grp_spec = pl.BlockSpec((1, g_, tm, wl), lambda i, j: (i, 0, j, 0))
    q5, ks4, vs4, kw4, vw4, kc, vc, gates = pl.pallas_call(
        functools.partial(_nsa_proj_kernel, tm, qd, kd),
        out_shape=[jax.ShapeDtypeStruct((b, g_, hg, s, wl), BF16), grp(BF16), grp(BF16), grp(BF16), grp(BF16),
                   grp(BF16), grp(BF16), grp(F32)],
        grid=(b, s // tm),
        in_specs=[pl.BlockSpec((1, tm, d), lambda i, j: (i, j, 0)), pl.BlockSpec((1, d), lambda i, j: (0, 0)),
                  mod_spec, mod_spec, full2(w_pad)] + [full2(cst) for cst in consts],
        out_specs=[pl.BlockSpec((1, g_, hg, tm, wl), lambda i, j: (i, 0, 0, j, 0))] + [grp_spec] * 7,
        compiler_params=_params("parallel", "parallel"),
        name="nsa_proj",
    )(x, nw.reshape(1, d), shift[:, None], scale[:, None], w_pad, *consts)

    nsub = s // CMP_STRIDE
    half = CMP_STRIDE * wl
    kc3, vc3 = kc.reshape(b, g_, nsub, half), vc.reshape(b, g_, nsub, half)
    pad_tok = lambda arr: jnp.pad(arr, [(0, 0)] * (arr.ndim - 1) + [(0, wl - dk)])
    pos = pad_tok(cmp_pos).reshape(4, half)
    w1b = jnp.pad(cmp_w1.reshape(2, CMP_LEN, dk, -1), ((0, 0), (0, 0), (0, wl - dk), (0, 0))).reshape(2, 2 * half, -1).astype(BF16)
    w2b = jnp.pad(cmp_w2, ((0, 0), (0, 0), (0, wl - dk))).astype(BF16)
    nk_lanes = jnp.pad(qk_norm[1], (0, wl - dk)).reshape(1, wl)
    cmp_in = pl.BlockSpec((1, 1, nsub, half), lambda i, j: (i, j, 0, 0))
    cmp_out = pl.BlockSpec((1, 1, nsub, wl), lambda i, j: (i, j, 0, 0))
    kcc, vcc = pl.pallas_call(
        functools.partial(_nsa_compress_kernel, half),
        out_shape=[jax.ShapeDtypeStruct((b, g_, nsub, wl), BF16)] * 2,
        grid=(b, g_),
        in_specs=[cmp_in, cmp_in, pl.BlockSpec(pos.shape, lambda i, j: (0, 0)),
                  pl.BlockSpec(w1b.shape, lambda i, j: (0, 0, 0)), pl.BlockSpec(w2b.shape, lambda i, j: (0, 0, 0)),
                  pl.BlockSpec((1, wl), lambda i, j: (0, 0))],
        out_specs=[cmp_out, cmp_out],
        compiler_params=_params("parallel", "parallel"),
        name="nsa_compress",
    )(kc3, vc3, pos, w1b, w2b, nk_lanes)

    nsel = s // SEL_LEN
    kt = min(NSA_KV_TILE, s)
    nkt = s // kt
    seq_kv = lambda n: pl.BlockSpec((1, 1, n, wl), lambda i, j, k, *_: (i, j, 0, 0))

    def tile_specs(qt):
        assert s % qt == 0
        return (pl.BlockSpec((1, 1, hg, qt, wl), lambda i, j, k, *_: (i, j, 0, k, 0)),
                pl.BlockSpec((1, qt, hg * dk), lambda i, j, k, *_: (i, k, j)),
                pl.BlockSpec((1, 1, qt, wl), lambda i, j, k, *_: (i, j, k, 0)),
                pl.BlockSpec((1, 1, qt, nsel), lambda i, j, k, *_: (i, j, k, 0)))

    qt = NSA_Q_TILE
    q_spec, o_spec, gate_spec, sel_spec = tile_specs(qt)
    oc, sel = pl.pallas_call(
        functools.partial(_nsa_cmp_kernel, qt, nsub, nsel),
        out_shape=[jax.ShapeDtypeStruct((b, s, qd), F32), jax.ShapeDtypeStruct((b, g_, s, nsel), BF16)],
        grid=(b, g_, s // qt),
        in_specs=[q_spec, seq_kv(nsub), seq_kv(nsub), gate_spec],
        out_specs=[o_spec, sel_spec],
        compiler_params=_params("parallel", "parallel", "parallel"),
        name="nsa_cmp_select",
    )(q5, kcc, vcc, gates)

    assert nsel <= wl
    block_onehot = (jnp.arange(s)[:, None] // SEL_LEN == jnp.arange(wl)[None, :]).astype(BF16)

    qt = NSA_ATTN_Q_TILE
    nq = s // qt
    assert s >= WINDOW + qt and kt % qt == 0
    q_spec, o_spec, gate_spec, sel_spec = tile_specs(qt)
    active = sel.reshape(b, g_, nq, qt, nkt, kt // SEL_LEN).max(axis=(3, 5)) > 0
    order = jnp.sort(jnp.where(active, 0, nkt) + jnp.arange(nkt, dtype=jnp.int32), axis=-1)
    tiles = (order % nkt).astype(jnp.int32).reshape(-1)
    counts = active.sum(axis=-1).astype(jnp.int32).reshape(-1)

    o3 = pl.pallas_call(
        functools.partial(_nsa_attn_kernel, qt, kt, nsel, nkt, nq),
        out_shape=jax.ShapeDtypeStruct((b, s, qd), BF16),
        grid_spec=pltpu.PrefetchScalarGridSpec(
            num_scalar_prefetch=2,
            grid=(b, g_, nq),
            in_specs=[q_spec, seq_kv(s), seq_kv(s), seq_kv(s), seq_kv(s),
                      pl.BlockSpec((s, wl), lambda i, j, k, *_: (0, 0)), sel_spec, gate_spec,
                      o_spec],
            out_specs=o_spec,
        ),
        compiler_params=_params("parallel", "parallel", "arbitrary"),
        name="nsa_attention",
    )(tiles, counts, q5, ks4, vs4, kw4, vw4, block_onehot, sel, gates, oc)
    return _outproj(x.reshape(t, d), o3.reshape(t, qd), gate, w_out, s).reshape(b, s, d)


def kernel(x, c, ada_w, ada_b, norm_mix_w, norm_ffn_w, ffn_w_gate, ffn_w_up, ffn_w_down, nsa_w_in, nsa_cmp_pos, nsa_cmp_w1, nsa_cmp_w2, nsa_qk_norm, nsa_w_out, rwkv_mu, rwkv_w_rkv, rwkv_w0, rwkv_wd_a, rwkv_wd_b, rwkv_a0, rwkv_wa_a, rwkv_wa_b, rwkv_wg_a, rwkv_wg_b, rwkv_k_k, rwkv_k_a, rwkv_r_k, rwkv_ln_w, rwkv_ln_b, rwkv_w_out, pool_w, pool_b, pool_scale):
    b, s, d = x.shape
    depth = ada_w.shape[0]
    mod = _ada_mod(c, ada_w, ada_b)
    for i in range(depth):
        sh_m, sc_m, g_m, sh_f, sc_f, g_f = jnp.split(mod[i], N_MOD, axis=-1)
        kind, j = i % N_MIXERS, i // N_MIXERS
        if kind == 0:
            x = _nsa_layer(x, norm_mix_w[i], sh_m, sc_m, g_m, nsa_w_in[j], nsa_cmp_pos[j], nsa_cmp_w1[j], nsa_cmp_w2[j],
                           nsa_qk_norm[j], nsa_w_out[j])
        elif kind == 1:
            x = _rwkv_layer(x, norm_mix_w[i], sh_m, sc_m, g_m, rwkv_mu[j], rwkv_w_rkv[j], rwkv_w0[j], rwkv_wd_a[j],
                            rwkv_wd_b[j], rwkv_a0[j], rwkv_wa_a[j], rwkv_wa_b[j], rwkv_wg_a[j], rwkv_wg_b[j],
                            rwkv_k_k[j], rwkv_k_a[j], rwkv_r_k[j], rwkv_ln_w[j], rwkv_ln_b[j], rwkv_w_out[j])
        elif kind == 2:
            x = _pool_layer(x, norm_mix_w[i], sh_m, sc_m, g_m, pool_w[j], pool_b[j], pool_scale[j])
        x = _ffn(x.reshape(b * s, d), norm_ffn_w[i], sh_f, sc_f, g_f, ffn_w_gate[i], ffn_w_up[i], ffn_w_down[i], s).reshape(b, s, d)
    return x
```

```python
import functools
import math

import jax
import jax.numpy as jnp
from jax import lax
from jax.experimental import pallas as pl
from jax.experimental.pallas import tpu as pltpu

F32 = jnp.float32
BF16 = jnp.bfloat16

V7X_LANES = 128
V7X_VMEM_LIMIT_BYTES = 56 * 1024 * 1024

RMS_EPS = 1e-6
N_MOD = 6
N_MIXERS = 3

NSA_HEAD_DIM = 64
NSA_KV_GROUPS = 4
NSA_HEADS_PER_GROUP = 4
NSA_HEADS = NSA_KV_GROUPS * NSA_HEADS_PER_GROUP
CMP_LEN = 32
CMP_STRIDE = 16
SEL_LEN = 64
SEL_TOPK = 16
WINDOW = 512
NSA_SCALE = NSA_HEAD_DIM ** -0.5
LOG2_E = math.log2(math.e)
SEL_BIG = 1e9
MASKED_SCORE = -1e30
NSA_UNSELECTED = -(2.0 ** 100)

RWKV_HEAD_DIM = 64
RWKV_GN_EPS = 64e-5
RWKV_CHUNK = 64

POOL_WINDOWS = (2, 4, 8, 16)
POOL_HALO = 16

ROW_TILE = 512
FFN_ROW_TILE = 1024
FFN_COL_CHUNK = 256


def _params(*sem):
    return pltpu.CompilerParams(dimension_semantics=sem, vmem_limit_bytes=V7X_VMEM_LIMIT_BYTES)


def _sigmoid(z):
    return 1.0 / (1.0 + jnp.exp(-z))


def _normmod(x, nw, shift, scale):
    ms = jnp.mean(x * x, axis=-1, keepdims=True)
    return (x * lax.rsqrt(ms + RMS_EPS) * nw) * (1.0 + scale) + shift


def _dot(a, b):
    return jnp.dot(a.astype(BF16), b.astype(BF16), preferred_element_type=F32)


def _dot_nt(a, b, out_dtype=F32):
    out = lax.dot_general(a.astype(BF16), b.astype(BF16), (((1,), (1,)), ((), ())), preferred_element_type=F32)
    return out.astype(out_dtype)


def _split_bf16(x, parts):
    out = []
    for _ in range(parts):
        p = x.astype(BF16)
        out.append(p)
        x = x - p.astype(F32)
    return out


def _mod_kernel(c_ref, w_ref, b_ref, o_ref):
    c = c_ref[...]
    ca = c * _sigmoid(c)
    o_ref[0] = _dot(ca, w_ref[0]) + b_ref[0]


def _ada_mod(c, ada_w, ada_b):
    depth, d, n = ada_w.shape
    b = c.shape[0]
    rows = 8
    tn = 1536
    assert b <= rows and n % tn == 0
    c_pad = jnp.zeros((rows, d), F32).at[:b].set(c)
    out = pl.pallas_call(
        _mod_kernel,
        out_shape=jax.ShapeDtypeStruct((depth, rows, n), F32),
        grid=(depth, n // tn),
        in_specs=[
            pl.BlockSpec((rows, d), lambda i, j: (0, 0)),
            pl.BlockSpec((1, d, tn), lambda i, j: (i, 0, j)),
            pl.BlockSpec((1, 1, tn), lambda i, j: (i, 0, j)),
        ],
        out_specs=pl.BlockSpec((1, rows, tn), lambda i, j: (i, 0, j)),
        compiler_params=_params("parallel", "parallel"),
        name="ada_mod",
    )(c_pad, ada_w, ada_b.reshape(depth, 1, n))
    return out[:, :b]


def _ffn_kernel(x_ref, nw_ref, sh_ref, sc_ref, g_ref, wg_ref, wu_ref, wd_ref, o_ref):
    x = x_ref[...]
    h = _normmod(x, nw_ref[...], sh_ref[0], sc_ref[0]).astype(BF16)
    f = wg_ref.shape[1]
    acts = []
    for c0 in range(0, f, FFN_COL_CHUNK):
        cols = slice(c0, c0 + FFN_COL_CHUNK)
        g = jnp.dot(h, wg_ref[:, cols], preferred_element_type=F32)
        u = jnp.dot(h, wu_ref[:, cols], preferred_element_type=F32)
        acts.append((g * _sigmoid(g) * u).astype(BF16))
    o_ref[...] = x + g_ref[0] * jnp.dot(jnp.concatenate(acts, axis=1), wd_ref[...], preferred_element_type=F32)


def _ffn(x2, nw, shift, scale, gate, wg, wu, wd, seq):
    t, d = x2.shape
    f = wg.shape[1]
    tm = FFN_ROW_TILE
    assert t % tm == 0 and seq % tm == 0 and f % FFN_COL_CHUNK == 0
    tpb = seq // tm
    mod_spec = pl.BlockSpec((1, 1, d), lambda i: (i // tpb, 0, 0))
    resident = lambda shape: pl.BlockSpec(shape, lambda i: (0, 0), pipeline_mode=pl.Buffered(1))
    return pl.pallas_call(
        _ffn_kernel,
        out_shape=jax.ShapeDtypeStruct((t, d), F32),
        grid=(t // tm,),
        in_specs=[
            pl.BlockSpec((tm, d), lambda i: (i, 0)),
            pl.BlockSpec((1, d), lambda i: (0, 0)),
            mod_spec, mod_spec, mod_spec,
            resident((d, f)), resident((d, f)), resident((f, d)),
        ],
        out_specs=pl.BlockSpec((tm, d), lambda i: (i, 0)),
        compiler_params=_params("parallel"),
        name="ffn",
    )(x2, nw.reshape(1, d), shift[:, None], scale[:, None], gate[:, None],
      wg.astype(BF16), wu.astype(BF16), wd.astype(BF16))


def _outproj_kernel(x_ref, z_ref, g_ref, w_ref, o_ref):
    o_ref[...] = x_ref[...] + g_ref[0] * _dot(z_ref[...], w_ref[...])


def _outproj(x2, z2, gate, w, seq):
    t, d = x2.shape
    k = z2.shape[1]
    tm = ROW_TILE
    tpb = seq // tm
    return pl.pallas_call(
        _outproj_kernel,
        out_shape=jax.ShapeDtypeStruct((t, d), F32),
        grid=(t // tm,),
        in_specs=[
            pl.BlockSpec((tm, d), lambda i: (i, 0)),
            pl.BlockSpec((tm, k), lambda i: (i, 0)),
            pl.BlockSpec((1, 1, d), lambda i: (i // tpb, 0, 0)),
            pl.BlockSpec((k, d), lambda i: (0, 0)),
        ],
        out_specs=pl.BlockSpec((tm, d), lambda i: (i, 0)),
        compiler_params=_params("parallel"),
        name="outproj",
    )(x2, z2, gate[:, None], w.astype(BF16))


def _pool_kernel(tm, gd, x_ref, nw_ref, sh_ref, sc_ref, g_ref, pw_ref, pb_ref, ps_ref, o_ref, ext_scr):
    s = pl.program_id(1)
    x = x_ref[0]
    h = _normmod(x, nw_ref[...], sh_ref[0], sc_ref[0])

    @pl.when(s == 0)
    def _():
        ext_scr[0:POOL_HALO, :] = jnp.zeros((POOL_HALO, x.shape[1]), F32)

    ext_scr[POOL_HALO:POOL_HALO + tm, :] = h
    row = s * tm + lax.broadcasted_iota(jnp.int32, (tm, 1), 0)
    ys = []
    for gi, win in enumerate(POOL_WINDOWS):
        lanes = slice(gi * gd, (gi + 1) * gd)
        hg = h[:, lanes]
        acc = hg
        for k in range(1, win):
            acc = acc + ext_scr[POOL_HALO - k:POOL_HALO - k + tm, lanes]
        cnt = jnp.minimum(row + 1, win).astype(F32)
        ys.append(_dot(acc / cnt - hg, pw_ref[gi]))
    y = (jnp.concatenate(ys, axis=-1) + pb_ref[...]) * ps_ref[...]
    o_ref[0] = x + g_ref[0] * y
    ext_scr[0:POOL_HALO, :] = ext_scr[tm:tm + POOL_HALO, :]


def _pool_layer(x, nw, shift, scale, gate, pw, pb, ps):
    b, s, d = x.shape
    tm = ROW_TILE
    gd = d // len(POOL_WINDOWS)
    assert s % tm == 0 and max(POOL_WINDOWS) <= POOL_HALO
    mod_spec = pl.BlockSpec((1, 1, d), lambda i, j: (i, 0, 0))
    vec_spec = pl.BlockSpec((1, d), lambda i, j: (0, 0))
    return pl.pallas_call(
        functools.partial(_pool_kernel, tm, gd),
        out_shape=jax.ShapeDtypeStruct((b, s, d), F32),
        grid=(b, s // tm),
        in_specs=[
            pl.BlockSpec((1, tm, d), lambda i, j: (i, j, 0)),
            vec_spec, mod_spec, mod_spec, mod_spec,
            pl.BlockSpec(pw.shape, lambda i, j: (0, 0, 0)),
            vec_spec, vec_spec,
        ],
        out_specs=pl.BlockSpec((1, tm, d), lambda i, j: (i, j, 0)),
        scratch_shapes=[pltpu.VMEM((tm + POOL_HALO, d), F32)],
        compiler_params=_params("arbitrary", "arbitrary"),
        name="pool_mixer",
    )(x, nw.reshape(1, d), shift[:, None], scale[:, None], gate[:, None],
      pw.astype(BF16), pb.reshape(1, d), ps.reshape(1, d))


def _rwkv_proj_kernel(tm, x_ref, nw_ref, sh_ref, sc_ref, mu_ref, wr_ref, wk_ref, wv_ref,
                      w0_ref, wda_ref, wdb_ref, a0_ref, waa_ref, wab_ref, wga_ref, wgb_ref,
                      r_ref, k_ref, v_ref, lw_ref, a_ref, g_ref, ext_scr):
    s = pl.program_id(1)
    h = _normmod(x_ref[0], nw_ref[...], sh_ref[0], sc_ref[0])

    @pl.when(s == 0)
    def _():
        ext_scr[0:8, :] = jnp.zeros((8, h.shape[1]), F32)

    ext_scr[8:8 + tm, :] = h
    xx = ext_scr[7:7 + tm, :] - h

    def mix(i):
        return h + xx * mu_ref[i:i + 1, :]

    r_ref[0] = _dot(mix(0), wr_ref[...]).astype(r_ref.dtype)
    k_ref[0] = _dot(mix(2), wk_ref[...])
    v_ref[0] = _dot(mix(3), wv_ref[...]).astype(v_ref.dtype)
    dw = w0_ref[...] + _dot(jnp.tanh(_dot(mix(1), wda_ref[...])), wdb_ref[...])
    softplus_neg = jnp.maximum(-dw, 0.0) + jnp.log(1.0 + jnp.exp(-jnp.abs(dw)))
    lw_ref[0] = -jnp.exp(-softplus_neg - 0.5)
    a_ref[0] = _sigmoid(a0_ref[...] + _dot(_dot(mix(4), waa_ref[...]), wab_ref[...]))
    g_ref[0] = _dot(_sigmoid(_dot(mix(5), wga_ref[...])), wgb_ref[...]).astype(g_ref.dtype)
    ext_scr[0:8, :] = ext_scr[tm:tm + 8, :]


def _rwkv_scan_kernel(ct, npair, r_ref, k_ref, v_ref, lw_ref, a_ref, g_ref, kk_ref, ka_ref, rk_ref, lnw_ref, lnb_ref,
                      z_ref, s_scr):
    c, n = RWKV_CHUNK, RWKV_HEAD_DIM
    w = 2 * n
    shift = int(math.log2(n))

    @pl.when(pl.program_id(2) == 0)
    def _():
        s_scr[...] = jnp.zeros_like(s_scr)

    head0 = lax.broadcasted_iota(jnp.int32, (1, w), 1) < n
    ri = lax.broadcasted_iota(jnp.int32, (w, w), 0)
    ci = lax.broadcasted_iota(jnp.int32, (w, w), 1)
    same_head = jnp.where((ri >> shift) == (ci >> shift), 1.0, 0.0).astype(BF16)
    strict, incl, eye = ri > ci, ri >= ci, ri == ci
    rt = lax.broadcasted_iota(jnp.int32, (ct, ct), 0)
    cc = lax.broadcasted_iota(jnp.int32, (ct, ct), 1)
    cshift = int(math.log2(c))
    chunk_tri = jnp.where(((rt >> cshift) == (cc >> cshift)) & (rt >= cc), 1.0, 0.0).astype(BF16)
    nb = ct // c

    def head_sum(xf):
        return sum(jnp.dot(p, same_head, preferred_element_type=F32) for p in _split_bf16(xf, 2))

    def stack(xf):
        x3 = xf.reshape(nb, c, w)
        return jnp.concatenate([jnp.where(head0, x3, 0.0), jnp.where(head0, 0.0, x3)], axis=1)

    def bmm(x, y):
        return jnp.einsum('bij,bjk->bik', x.astype(BF16), y.astype(BF16), preferred_element_type=F32)

    def bmm_nt(x, y):
        return jnp.einsum('bik,bjk->bij', x.astype(BF16), y.astype(BF16), preferred_element_type=F32)

    pairs = []
    stacked = []
    for hp in range(npair):
        ln = slice(hp * w, (hp + 1) * w)
        r, v = r_ref[0, :, ln].astype(F32), v_ref[0, :, ln].astype(F32)
        k, lw, a = k_ref[0, :, ln], lw_ref[0, :, ln], a_ref[0, :, ln]
        kk = k * kk_ref[:, ln]
        kkn = kk * lax.rsqrt(jnp.maximum(head_sum(kk * kk), 1e-24))
        k2 = k * (1.0 + (a - 1.0) * ka_ref[:, ln])
        cum = sum(jnp.dot(chunk_tri, p, preferred_element_type=F32) for p in _split_bf16(lw, 3))
        einv = jnp.exp(-cum)
        pairs.append((r, k2, v))
        stacked.append((stack(-kkn * jnp.exp(cum - lw)), stack(r * jnp.exp(cum)), stack(kkn * a * einv),
                        stack(k2 * einv), stack(v), jnp.exp(cum.reshape(nb, c, w)[:, c - 1:c, :])))
    at, rt_, kb, kq, vs, pc = (jnp.concatenate([st[i] for st in stacked], axis=0) for i in range(6))
    nbt = npair * nb
    kbe, kqe = kb * pc, kq * pc
    tt = bmm_nt(jnp.concatenate([at, rt_], axis=1), jnp.concatenate([kb, kq], axis=1))
    a_m = jnp.where(strict, tt[:, :w, :w], 0.0)
    b_m = jnp.where(strict, tt[:, :w, w:], 0.0)
    ar_m = jnp.where(incl, tt[:, w:, :w], 0.0)
    br_m = jnp.where(incl, tt[:, w:, w:], 0.0)
    rsum = jnp.where(eye, 1.0, 0.0) + a_m
    pw = bmm(a_m, a_m)
    for _ in range(cshift - 2):
        both = bmm(pw, jnp.concatenate([rsum, pw], axis=2))
        rsum = rsum + both[:, :, :w]
        pw = both[:, :, w:]
    tinv = rsum + bmm(pw, rsum)
    w12 = bmm(tinv, jnp.concatenate([at, bmm(b_m, vs)], axis=2))
    xmat = jnp.concatenate([w12, jnp.concatenate([jnp.zeros_like(vs), vs], axis=2)], axis=1)
    kbe_t = jnp.stack([kbe[i].T for i in range(nbt)])
    kqe_t = jnp.stack([kqe[i].T for i in range(nbt)])
    lhs = jnp.concatenate([jnp.concatenate([ar_m, br_m], axis=2), jnp.concatenate([kbe_t, kqe_t], axis=2)], axis=1)
    res = bmm(lhs, xmat)
    g_m = rt_ + res[:, :w, :w]
    y0 = res[:, :w, w:]
    mt = jnp.where(eye, jnp.broadcast_to(pc, (nbt, w, w)), 0.0) + res[:, w:, :w]
    nt = res[:, w:, w:]

    states = [s_scr[hp] for hp in range(npair)]
    ys = [[] for _ in range(npair)]
    for idx in range(nb):
        for hp in range(npair):
            bi = hp * nb + idx
            step = _dot(jnp.concatenate([g_m[bi], mt[bi]], axis=0), states[hp])
            yst = step[:w] + y0[bi]
            states[hp] = step[w:] + nt[bi]
            ys[hp].append(yst[:c] + yst[c:])
    for hp in range(npair):
        ln = slice(hp * w, (hp + 1) * w)
        s_scr[hp] = states[hp]
        y = jnp.concatenate(ys[hp], axis=0)
        r, k2, v = pairs[hp]
        mean = head_sum(y) * (1.0 / n)
        dlt = y - mean
        var = head_sum(dlt * dlt) * (1.0 / n)
        yn = dlt * lax.rsqrt(var + RWKV_GN_EPS) * lnw_ref[:, ln] + lnb_ref[:, ln]
        bonus = head_sum(r * k2 * rk_ref[:, ln]) * v
        z_ref[0, :, ln] = ((yn + bonus) * g_ref[0, :, ln].astype(F32)).astype(z_ref.dtype)


def _rwkv_layer(x, nw, shift, scale, gate, mu, w_rkv, w0, wd_a, wd_b, a0, wa_a, wa_b, wg_a, wg_b,
                k_k, k_a, r_k, ln_w, ln_b, w_out):
    b, s, d = x.shape
    tm = ROW_TILE
    assert s % tm == 0
    mod_spec = pl.BlockSpec((1, 1, d), lambda i, j: (i, 0, 0))
    vec_spec = pl.BlockSpec((1, d), lambda i, j: (0, 0))
    tok_spec = pl.BlockSpec((1, tm, d), lambda i, j: (i, j, 0))

    def full(arr):
        return pl.BlockSpec(arr.shape, lambda i, j: (0,) * arr.ndim)

    weights = [w_rkv[0].astype(BF16), w_rkv[1].astype(BF16), w_rkv[2].astype(BF16),
               w0.reshape(1, d), wd_a.astype(BF16), wd_b.astype(BF16),
               a0.reshape(1, d), wa_a.astype(BF16), wa_b.astype(BF16),
               wg_a.astype(BF16), wg_b.astype(BF16)]
    tok_shape = jax.ShapeDtypeStruct((b, s, d), F32)
    tok_bf16 = jax.ShapeDtypeStruct((b, s, d), BF16)
    r, k, v, lw, a, g = pl.pallas_call(
        functools.partial(_rwkv_proj_kernel, tm),
        out_shape=[tok_bf16, tok_shape, tok_bf16, tok_shape, tok_shape, tok_bf16],
        grid=(b, s // tm),
        in_specs=[tok_spec, vec_spec, mod_spec, mod_spec, full(mu)] + [full(wt) for wt in weights],
        out_specs=[tok_spec] * 6,
        scratch_shapes=[pltpu.VMEM((tm + 8, d), F32)],
        compiler_params=_params("arbitrary", "arbitrary"),
        name="rwkv_proj",
    )(x, nw.reshape(1, d), shift[:, None], scale[:, None], mu, *weights)

    ct = 512
    npair = 4
    pair = 2 * RWKV_HEAD_DIM
    lanes = npair * pair
    assert s % ct == 0 and d % lanes == 0 and pair == V7X_LANES
    seq_spec = pl.BlockSpec((1, ct, lanes), lambda i, p, j: (i, j, p))
    par_spec = pl.BlockSpec((1, lanes), lambda i, p, j: (0, p))
    z = pl.pallas_call(
        functools.partial(_rwkv_scan_kernel, ct, npair),
        out_shape=jax.ShapeDtypeStruct((b, s, d), BF16),
        grid=(b, d // lanes, s // ct),
        in_specs=[seq_spec] * 6 + [par_spec] * 5,
        out_specs=seq_spec,
        scratch_shapes=[pltpu.VMEM((npair, pair, pair), F32)],
        compiler_params=_params("parallel", "parallel", "arbitrary"),
        name="rwkv_scan",
    )(r, k, v, lw, a, g, k_k.reshape(1, d), k_a.reshape(1, d), r_k.reshape(1, d), ln_w.reshape(1, d), ln_b.reshape(1, d))
    return _outproj(x.reshape(b * s, d), z.reshape(b * s, d), gate, w_out, s).reshape(b, s, d)


NSA_Q_TILE = 1024
NSA_CMP_SUB_TILE = 256
NSA_ATTN_Q_TILE = 512
NSA_KV_TILE = 512
NSA_PROJ_PAD = 128
NSA_QK_WIDTH = 128


def _head_rms(xf, p_ref, pt_ref, wvec):
    sums = sum(jnp.dot(part, p_ref[...], preferred_element_type=F32) for part in _split_bf16(xf * xf, 2))
    inv = lax.rsqrt(sums * (1.0 / NSA_HEAD_DIM) + RMS_EPS)
    inv_full = sum(jnp.dot(part, pt_ref[...], preferred_element_type=F32) for part in _split_bf16(inv, 3))
    return xf * inv_full * wvec


def _pos_lanes(pos):
    lane = lax.broadcasted_iota(jnp.int32, (1, NSA_QK_WIDTH), 1)
    hi = ((pos >> 6) << 6).astype(F32)
    lo = (pos & 63).astype(F32)
    d = NSA_HEAD_DIM
    ones = jnp.where((lane >= d + 6) & (lane < d + 9), 1.0, 0.0)
    return jnp.where((lane >= d) & (lane < d + 3), hi, jnp.where((lane >= d + 3) & (lane < d + 6), lo, ones))


def _nsa_proj_kernel(tm, qd, kd, x_ref, nw_ref, sh_ref, sc_ref, w_ref, pq_ref, pqt_ref, pk_ref, pkt_ref,
                     nq_ref, nks_ref, nkw_ref, qf_ref, slq_ref,
                     q_ref, ks_ref, vs_ref, kw_ref, vw_ref, kc_ref, vc_ref, gt_ref):
    hg, dk = NSA_HEADS_PER_GROUP, NSA_HEAD_DIM
    h = _normmod(x_ref[0], nw_ref[...], sh_ref[0], sc_ref[0])
    proj = _dot(h, w_ref[...])
    lane = lax.broadcasted_iota(jnp.int32, (1, NSA_QK_WIDTH), 1)
    low = lane < dk
    t_int = pl.program_id(1) * tm + lax.broadcasted_iota(jnp.int32, (tm, 1), 0)
    t_row = t_int.astype(F32)
    key_lanes = _pos_lanes(t_int)

    def heads(x, nheads):
        for pair in range(nheads // 2):
            xp = x[:, pair * 2 * dk:(pair + 1) * 2 * dk]
            yield 2 * pair, xp
            yield 2 * pair + 1, pltpu.roll(xp, dk, 1)

    qn = _head_rms(proj[:, :qd], pq_ref, pqt_ref, nq_ref[...]) * (NSA_SCALE * LOG2_E)
    for hd, xs in heads(qn, NSA_HEADS):
        feat = qf_ref[hd:hd + 1, :]
        for i, part in enumerate(_split_bf16(-slq_ref[hd:hd + 1, :] * t_row, 3)):
            feat = jnp.where(lane == dk + 6 + i, part.astype(F32), feat)
        q_ref[0, hd // hg, hd % hg] = jnp.where(low, xs, feat).astype(BF16)
    o = qd
    for g, xs in heads(proj[:, o:o + kd], NSA_KV_GROUPS):
        kc_ref[0, g] = xs.astype(kc_ref.dtype)
    for g, xs in heads(proj[:, o + kd:o + 2 * kd], NSA_KV_GROUPS):
        vc_ref[0, g] = xs.astype(vc_ref.dtype)
    for g, xs in heads(_head_rms(proj[:, o + 2 * kd:o + 3 * kd], pk_ref, pkt_ref, nks_ref[...]), NSA_KV_GROUPS):
        ks_ref[0, g] = jnp.where(low, xs, key_lanes).astype(BF16)
    for g, xs in heads(proj[:, o + 3 * kd:o + 4 * kd], NSA_KV_GROUPS):
        vs_ref[0, g] = jnp.where(low, xs, 1.0).astype(BF16)
    for g, xs in heads(_head_rms(proj[:, o + 4 * kd:o + 5 * kd], pk_ref, pkt_ref, nkw_ref[...]), NSA_KV_GROUPS):
        kw_ref[0, g] = jnp.where(low, xs, key_lanes).astype(BF16)
    for g, xs in heads(proj[:, o + 5 * kd:o + 6 * kd], NSA_KV_GROUPS):
        vw_ref[0, g] = jnp.where(low, xs, 1.0).astype(BF16)
    o += 6 * kd
    for g in range(NSA_KV_GROUPS):
        gt_ref[0, g] = _sigmoid(proj[:, o + g * NSA_PROJ_PAD:o + (g + 1) * NSA_PROJ_PAD])


def _gelu_tanh(x):
    return 0.5 * x * (1.0 + jnp.tanh(math.sqrt(2.0 / math.pi) * (x + 0.044715 * (x * x * x))))


def _nsa_compress_kernel(half, kc_ref, vc_ref, pos_ref, w1_ref, w2_ref, nk_ref, kco_ref, vco_ref):
    nsub = kc_ref.shape[2]

    def comp(x, i):
        ya = _dot(x + pos_ref[2 * i:2 * i + 1, :], w1_ref[i, 0:half, :])
        yb = _dot(x + pos_ref[2 * i + 1:2 * i + 2, :], w1_ref[i, half:2 * half, :])
        hid = ya + pltpu.roll(yb, nsub - 1, 0)
        return _dot(_gelu_tanh(hid), w2_ref[i])

    kcm = comp(kc_ref[0, 0], 0)
    ms = jnp.sum(kcm * kcm, axis=-1, keepdims=True) * (1.0 / NSA_HEAD_DIM)
    block_end = lax.broadcasted_iota(jnp.int32, (nsub, 1), 0) * CMP_STRIDE + CMP_LEN - 1
    kco_ref[0, 0] = (kcm * lax.rsqrt(ms + RMS_EPS) * nk_ref[...] + _pos_lanes(block_end)).astype(BF16)
    vco_ref[0, 0] = comp(vc_ref[0, 0], 1).astype(BF16)


def _pack_heads(per_head):
    low = lax.broadcasted_iota(jnp.int32, (1, NSA_QK_WIDTH), 1) < NSA_HEAD_DIM
    pairs = [jnp.where(low, per_head[i], pltpu.roll(per_head[i + 1], NSA_HEAD_DIM, 1))
             for i in range(0, len(per_head), 2)]
    return jnp.concatenate(pairs, axis=-1)


def _nsa_cmp_kernel(qt, nc, nsel, q_ref, kc_ref, vc_ref, g_ref, oc_ref, sel_ref):
    hg = NSA_HEADS_PER_GROUP
    q0 = pl.program_id(2) * qt
    qs = NSA_CMP_SUB_TILE
    jo = lax.broadcasted_iota(jnp.int32, (nsel, nc), 0)
    no = lax.broadcasted_iota(jnp.int32, (nsel, nc), 1)
    ratio = SEL_LEN // CMP_STRIDE
    first = jo * ratio - (CMP_LEN // CMP_STRIDE - 1)
    overlap_t = jnp.where((no >= first) & (no < (jo + 1) * ratio) & (no < nc - 1), 1.0, 0.0).astype(BF16)
    n = lax.broadcasted_iota(jnp.int32, (qs, nc), 1)
    imp_parts = []
    for sub in range(qt // qs):
        rows = slice(sub * qs, (sub + 1) * qs)
        q = q_ref[0, 0, :, rows, :].reshape(hg * qs, NSA_QK_WIDTH)
        s3 = _dot_nt(q, kc_ref[0, 0]).reshape(hg, qs, nc)
        t = q0 + sub * qs + lax.broadcasted_iota(jnp.int32, (qs, nc), 0)
        mask = (t >= n * CMP_STRIDE + CMP_LEN - 1)[None]
        sm = jnp.where(mask, s3, MASKED_SCORE)
        e = jnp.exp2(sm - jnp.max(sm, axis=-1, keepdims=True))
        has_key = (q0 + sub * qs + lax.broadcasted_iota(jnp.int32, (1, qs, 1), 1)) >= CMP_LEN - 1
        p = e * jnp.where(has_key, 1.0 / jnp.maximum(jnp.sum(e, axis=-1, keepdims=True), 1e-30), 0.0)
        oc = _dot(p.reshape(hg * qs, nc), vc_ref[0, 0])
        g = g_ref[0, 0, rows, :]
        oc_ref[0, rows, :] = _pack_heads([oc[hd * qs:(hd + 1) * qs] * g[:, hd:hd + 1] for hd in range(hg)])
        psum = p[0] + p[1] + p[2] + p[3]
        imp_parts.append(sum(lax.dot_general(overlap_t, part, (((1,), (1,)), ((), ())), preferred_element_type=F32)
                             for part in _split_bf16(psum, 3)))
    imp_t = jnp.concatenate(imp_parts, axis=1)

    j = lax.broadcasted_iota(jnp.int32, (nsel, qt), 0)
    tt = q0 + lax.broadcasted_iota(jnp.int32, (nsel, qt), 1)
    valid = j * SEL_LEN <= tt
    qb = tt >> int(math.log2(SEL_LEN))
    forced = (j == 0) | (j == qb) | (j == qb - 1)
    score = jnp.where(valid, jnp.where(forced, SEL_BIG, imp_t), -SEL_BIG)
    jf = j.astype(F32)
    picked = -(2.0 ** 120)
    for _ in range(min(SEL_TOPK, nsel)):
        mx = jnp.max(score, axis=0, keepdims=True)
        jmin = jnp.min(jnp.where(score == mx, jf, float(nsel)), axis=0, keepdims=True)
        score = jnp.where(jf == jmin, picked, score)
    sel = jnp.where((score == picked) & valid, 1.0, 0.0)
    sel_ref[0, 0] = sel.T.astype(BF16)


def _nsa_attn_kernel(qt, kt, nsel, nkt, nq, tiles_ref, cnt_ref,
                     q_ref, ks_ref, vs_ref, kw_ref, vw_ref, oh_ref, sel_ref, g_ref, oc_ref, o_ref):
    hg, dk = NSA_HEADS_PER_GROUP, NSA_HEAD_DIM
    rows = hg * qt
    step = (pl.program_id(0) * NSA_KV_GROUPS + pl.program_id(1)) * nq + pl.program_id(2)
    q0 = pl.program_id(2) * qt
    q = q_ref[0, 0].reshape(rows, NSA_QK_WIDTH)
    unsel = (1.0 - sel_ref[0, 0]) * NSA_UNSELECTED
    if nsel < NSA_QK_WIDTH:
        unsel = jnp.concatenate([unsel, jnp.zeros((qt, NSA_QK_WIDTH - nsel), BF16)], axis=1)
    q_sel = jnp.concatenate([q, jnp.concatenate([unsel] * hg, axis=0)], axis=1)

    t_k = q0 + lax.broadcasted_iota(jnp.int32, (qt, kt), 0)
    c_k = lax.broadcasted_iota(jnp.int32, (qt, kt), 1)

    def sel_step(i, carry, diagonal):
        m, acc = carry
        k0 = pl.multiple_of(tiles_ref[step * nkt + i] * kt, kt)
        keys = jnp.concatenate([ks_ref[0, 0, pl.ds(k0, kt), :], oh_ref[pl.ds(k0, kt), :]], axis=1)
        sm = _dot_nt(q_sel, keys, BF16)
        if diagonal:
            causal = (k0 + c_k <= t_k)[None]
            sm = jnp.where(causal, sm.reshape(hg, qt, kt), MASKED_SCORE).reshape(rows, kt)
        m_new = jnp.maximum(m, jnp.max(sm, axis=-1, keepdims=True))
        alpha = jnp.exp2((m - m_new).astype(F32))
        acc = alpha * acc + _dot(jnp.exp2(sm - m_new), vs_ref[0, 0, pl.ds(k0, kt), :])
        return m_new, acc

    init = (jnp.full((rows, 1), MASKED_SCORE, BF16), jnp.zeros((rows, NSA_QK_WIDTH), F32))
    last = cnt_ref[step] - 1
    carry = lax.fori_loop(0, last, functools.partial(sel_step, diagonal=False), init)
    _, acc_s = sel_step(last, carry, diagonal=True)
    o_s = acc_s / jnp.maximum(acc_s[:, dk:dk + 1], 1e-30)

    wk = WINDOW + qt
    ws = pl.multiple_of(jnp.maximum(q0 - WINDOW, 0), qt)
    s3 = _dot_nt(q, kw_ref[0, 0, pl.ds(ws, wk), :], BF16).reshape(hg, qt, wk)
    dist = (q0 + lax.broadcasted_iota(jnp.int32, (qt, wk), 0)) - (ws + lax.broadcasted_iota(jnp.int32, (qt, wk), 1))
    mask = ((dist >= 0) & (dist < WINDOW))[None]
    sm = jnp.where(mask, s3, MASKED_SCORE).reshape(rows, wk)
    e = jnp.exp2(sm - jnp.max(sm, axis=-1, keepdims=True))
    acc_w = _dot(e, vw_ref[0, 0, pl.ds(ws, wk), :])
    o_w = acc_w / jnp.maximum(acc_w[:, dk:dk + 1], 1e-30)

    g = g_ref[0, 0]
    per_head = [g[:, hg + hd:hg + hd + 1] * o_s[hd * qt:(hd + 1) * qt]
                + g[:, 2 * hg + hd:2 * hg + hd + 1] * o_w[hd * qt:(hd + 1) * qt] for hd in range(hg)]
    o_ref[0] = (oc_ref[0] + _pack_heads(per_head)).astype(o_ref.dtype)


def _nsa_layer(x, nw, shift, scale, gate, w_in, cmp_pos, cmp_w1, cmp_w2, qk_norm, w_out):
    b, s, d = x.shape
    t = b * s
    g_, hg, dk = NSA_KV_GROUPS, NSA_HEADS_PER_GROUP, NSA_HEAD_DIM
    qd, kd = NSA_HEADS * dk, g_ * dk
    tm = ROW_TILE
    wl = NSA_QK_WIDTH
    n_main = qd + 6 * kd
    assert w_in.shape[1] == n_main + 3 * NSA_HEADS and s % tm == 0
    gate_cols = w_in[:, n_main:].reshape(d, 3, g_, hg).transpose(0, 2, 1, 3).reshape(d, g_, 3 * hg)
    gate_cols = jnp.pad(gate_cols, ((0, 0), (0, 0), (0, NSA_PROJ_PAD - 3 * hg))).reshape(d, g_ * NSA_PROJ_PAD)
    w_pad = jnp.concatenate([w_in[:, :n_main], gate_cols], axis=1).astype(BF16)

    def head_onehot(width):
        lane_head = jnp.arange(width)[:, None] // dk
        return (lane_head == jnp.arange(V7X_LANES)[None, :]).astype(BF16)

    pq, pk = head_onehot(qd), head_onehot(kd)

    def tile_w(wv, reps):
        return jnp.tile(wv, reps).reshape(1, reps * dk)

    slopes = jnp.exp2(-8.0 * (jnp.arange(NSA_HEADS, dtype=F32) + 1) / NSA_HEADS)
    slope_parts = jnp.stack([p.astype(F32) for p in _split_bf16(slopes * LOG2_E, 3)], axis=-1)
    slope_rows = jnp.broadcast_to((slopes * LOG2_E)[:, None], (NSA_HEADS, wl))
    q_lanes = jnp.zeros((NSA_HEADS, wl), F32).at[:, dk:dk + 6].set(jnp.concatenate([slope_parts, slope_parts], axis=-1))

    consts = [pq, pq.T, pk, pk.T, tile_w(qk_norm[0], NSA_HEADS), tile_w(qk_norm[2], g_), tile_w(qk_norm[3], g_), q_lanes, slope_rows]
    full2 = lambda arr: pl.BlockSpec(arr.shape, lambda i, j: (0, 0))
    mod_spec = pl.BlockSpec((1, 1, d), lambda i, j: (i, 0, 0))
    grp = lambda dt: jax.ShapeDtypeStruct((b, g_, s, wl), dt)
    grp_spec = pl.BlockSpec((1, g_, tm, wl), lambda i, j: (i, 0, j, 0))
    q5, ks4, vs4, kw4, vw4, kc, vc, gates = pl.pallas_call(
        functools.partial(_nsa_proj_kernel, tm, qd, kd),
        out_shape=[jax.ShapeDtypeStruct((b, g_, hg, s, wl), BF16), grp(BF16), grp(BF16), grp(BF16), grp(BF16),
                   grp(BF16), grp(BF16), grp(F32)],
        grid=(b, s // tm),
        in_specs=[pl.BlockSpec((1, tm, d), lambda i, j: (i, j, 0)), pl.BlockSpec((1, d), lambda i, j: (0, 0)),
                  mod_spec, mod_spec, full2(w_pad)] + [full2(cst) for cst in consts],
        out_specs=[pl.BlockSpec((1, g_, hg, tm, wl), lambda i, j: (i, 0, 0, j, 0))] + [grp_spec] * 7,
        compiler_params=_params("parallel", "parallel"),
        name="nsa_proj",
    )(x, nw.reshape(1, d), shift[:, None], scale[:, None], w_pad, *consts)

    nsub = s // CMP_STRIDE
    half = CMP_STRIDE * wl
    kc3, vc3 = kc.reshape(b, g_, nsub, half), vc.reshape(b, g_, nsub, half)
    pad_tok = lambda arr: jnp.pad(arr, [(0, 0)] * (arr.ndim - 1) + [(0, wl - dk)])
    pos = pad_tok(cmp_pos).reshape(4, half)
    w1b = jnp.pad(cmp_w1.reshape(2, CMP_LEN, dk, -1), ((0, 0), (0, 0), (0, wl - dk), (0, 0))).reshape(2, 2 * half, -1).astype(BF16)
    w2b = jnp.pad(cmp_w2, ((0, 0), (0, 0), (0, wl - dk))).astype(BF16)
    nk_lanes = jnp.pad(qk_norm[1], (0, wl - dk)).reshape(1, wl)
    cmp_in = pl.BlockSpec((1, 1, nsub, half), lambda i, j: (i, j, 0, 0))
    cmp_out = pl.BlockSpec((1, 1, nsub, wl), lambda i, j: (i, j, 0, 0))
    kcc, vcc = pl.pallas_call(
        functools.partial(_nsa_compress_kernel, half),
        out_shape=[jax.ShapeDtypeStruct((b, g_, nsub, wl), BF16)] * 2,
        grid=(b, g_),
        in_specs=[cmp_in, cmp_in, pl.BlockSpec(pos.shape, lambda i, j: (0, 0)),
                  pl.BlockSpec(w1b.shape, lambda i, j: (0, 0, 0)), pl.BlockSpec(w2b.shape, lambda i, j: (0, 0, 0)),
                  pl.BlockSpec((1, wl), lambda i, j: (0, 0))],
        out_specs=[cmp_out, cmp_out],
        compiler_params=_params("parallel", "parallel"),
        name="nsa_compress",
    )(kc3, vc3, pos, w1b, w2b, nk_lanes)

    nsel = s // SEL_LEN
    kt = min(NSA_KV_TILE, s)
    nkt = s // kt
    seq_kv = lambda n: pl.BlockSpec((1, 1, n, wl), lambda i, j, k, *_: (i, j, 0, 0))

    def tile_specs(qt):
        assert s % qt == 0
        return (pl.BlockSpec((1, 1, hg, qt, wl), lambda i, j, k, *_: (i, j, 0, k, 0)),
                pl.BlockSpec((1, qt, hg * dk), lambda i, j, k, *_: (i, k, j)),
                pl.BlockSpec((1, 1, qt, wl), lambda i, j, k, *_: (i, j, k, 0)),
                pl.BlockSpec((1, 1, qt, nsel), lambda i, j, k, *_: (i, j, k, 0)))

    qt = NSA_Q_TILE
    q_spec, o_spec, gate_spec, sel_spec = tile_specs(qt)
    oc, sel = pl.pallas_call(
        functools.partial(_nsa_cmp_kernel, qt, nsub, nsel),
        out_shape=[jax.ShapeDtypeStruct((b, s, qd), F32), jax.ShapeDtypeStruct((b, g_, s, nsel), BF16)],
        grid=(b, g_, s // qt),
        in_specs=[q_spec, seq_kv(nsub), seq_kv(nsub), gate_spec],
        out_specs=[o_spec, sel_spec],
        compiler_params=_params("parallel", "parallel", "parallel"),
        name="nsa_cmp_select",
    )(q5, kcc, vcc, gates)

    assert nsel <= wl
    block_onehot = (jnp.arange(s)[:, None] // SEL_LEN == jnp.arange(wl)[None, :]).astype(BF16)

    qt = NSA_ATTN_Q_TILE
    nq = s // qt
    assert s >= WINDOW + qt and kt % qt == 0
    q_spec, o_spec, gate_spec, sel_spec = tile_specs(qt)
    active = sel.reshape(b, g_, nq, qt, nkt, kt // SEL_LEN).max(axis=(3, 5)) > 0
    order = jnp.sort(jnp.where(active, 0, nkt) + jnp.arange(nkt, dtype=jnp.int32), axis=-1)
    tiles = (order % nkt).astype(jnp.int32).reshape(-1)
    counts = active.sum(axis=-1).astype(jnp.int32).reshape(-1)

    o3 = pl.pallas_call(
        functools.partial(_nsa_attn_kernel, qt, kt, nsel, nkt, nq),
        out_shape=jax.ShapeDtypeStruct((b, s, qd), BF16),
        grid_spec=pltpu.PrefetchScalarGridSpec(
            num_scalar_prefetch=2,
            grid=(b, g_, nq),
            in_specs=[q_spec, seq_kv(s), seq_kv(s), seq_kv(s), seq_kv(s),
                      pl.BlockSpec((s, wl), lambda i, j, k, *_: (0, 0)), sel_spec, gate_spec,
                      o_spec],
            out_specs=o_spec,
        ),
        compiler_params=_params("parallel", "parallel", "arbitrary"),
        name="nsa_attention",
    )(tiles, counts, q5, ks4, vs4, kw4, vw4, block_onehot, sel, gates, oc)
    return _outproj(x.reshape(t, d), o3.reshape(t, qd), gate, w_out, s).reshape(b, s, d)


def kernel(x, c, ada_w, ada_b, norm_mix_w, norm_ffn_w, ffn_w_gate, ffn_w_up, ffn_w_down, nsa_w_in, nsa_cmp_pos, nsa_cmp_w1, nsa_cmp_w2, nsa_qk_norm, nsa_w_out, rwkv_mu, rwkv_w_rkv, rwkv_w0, rwkv_wd_a, rwkv_wd_b, rwkv_a0, rwkv_wa_a, rwkv_wa_b, rwkv_wg_a, rwkv_wg_b, rwkv_k_k, rwkv_k_a, rwkv_r_k, rwkv_ln_w, rwkv_ln_b, rwkv_w_out, pool_w, pool_b, pool_scale):
    b, s, d = x.shape
    depth = ada_w.shape[0]
    mod = _ada_mod(c, ada_w, ada_b)
    for i in range(depth):
        sh_m, sc_m, g_m, sh_f, sc_f, g_f = jnp.split(mod[i], N_MOD, axis=-1)
        kind, j = i % N_MIXERS, i // N_MIXERS
        if kind == 0:
            x = _nsa_layer(x, norm_mix_w[i], sh_m, sc_m, g_m, nsa_w_in[j], nsa_cmp_pos[j], nsa_cmp_w1[j], nsa_cmp_w2[j],
                           nsa_qk_norm[j], nsa_w_out[j])
        elif kind == 1:
            x = _rwkv_layer(x, norm_mix_w[i], sh_m, sc_m, g_m, rwkv_mu[j], rwkv_w_rkv[j], rwkv_w0[j], rwkv_wd_a[j],
                            rwkv_wd_b[j], rwkv_a0[j], rwkv_wa_a[j], rwkv_wa_b[j], rwkv_wg_a[j], rwkv_wg_b[j],
                            rwkv_k_k[j], rwkv_k_a[j], rwkv_r_k[j], rwkv_ln_w[j], rwkv_ln_b[j], rwkv_w_out[j])
        elif kind == 2:
            x = _pool_layer(x, norm_mix_w[i], sh_m, sc_m, g_m, pool_w[j], pool_b[j], pool_scale[j])
        x = _ffn(x.reshape(b * s, d), norm_ffn_w[i], sh_f, sc_f, g_f, ffn_w_gate[i], ffn_w_up[i], ffn_w_down[i], s).reshape(b, s, d)
    return x
```
